```python
import math
import jax, jax.numpy as jnp
from jax import lax
import numpy as np

D_MODEL = 1024
BATCH = 4
SEQ = 8192
DEPTH = 2
DEC_BATCH = 8
DEC_SEQ = 32
PAST_LEN = 1024

CHUNK = 64
Q_BLOCK = 128
HEAD_DIM = 64
H_A = 4
H_B = 4
H_C = 4
H_X = 4
C_PREV = 8
C_PAST = C_PREV * CHUNK
REL_CLIP = 128
T5_BUCKETS = 32
T5_MAX_DIST = 128
N_MEM = 256
D_FF = 4 * D_MODEL
WA = H_A * 2 * HEAD_DIM
WB = H_B * HEAD_DIM
WC = H_C * HEAD_DIM
WX = H_X * HEAD_DIM
N_BRANCH = 3
IN_SPLITS = (WA, WA, WA, WB, WB, WB, WC, WC, WC, N_BRANCH * D_MODEL)
IN_COLS = sum(IN_SPLITS)
EPS = 1e-6

kernel_name = 'hybrid_chunk_streaming_encoder_step'


def rmsnorm(x, g):
    xf = x.astype(jnp.float32)
    y = xf * lax.rsqrt(jnp.mean(xf * xf, axis=-1, keepdims=True) + EPS)
    return (y * g.astype(jnp.float32)).astype(x.dtype)


def half_ffn(x, g, wg, wu, wd):
    h = rmsnorm(x, g)
    return 0.5 * ((jax.nn.silu(h @ wg) * (h @ wu)) @ wd)


def lambda_init(layer):
    return 0.8 - 0.6 * math.exp(-0.3 * layer)


def diff_lambda(lq1, lk1, lq2, lk2, layer):
    f = lambda a, b: jnp.exp(jnp.sum(a.astype(jnp.float32) * b.astype(jnp.float32)))
    return f(lq1, lk1) - f(lq2, lk2) + lambda_init(layer)


def t5_bucket(rel):
    half = T5_BUCKETS // 2
    max_exact = half // 2
    n = jnp.abs(rel)
    nf = jnp.maximum(n, 1).astype(jnp.float32)
    large = max_exact + (jnp.log(nf / max_exact) / math.log(T5_MAX_DIST / max_exact) * (half - max_exact)).astype(jnp.int32)
    large = jnp.minimum(large, half - 1)
    return jnp.where(rel > 0, half, 0) + jnp.where(n < max_exact, n, large)


def diff_attend(q, k, v, qpos, kpos, t5_bias, lam):
    bias = jnp.transpose(t5_bias[t5_bucket(kpos[None, :] - qpos[:, None])], (2, 0, 1)).astype(jnp.float32)
    mask = (kpos[None, :] // CHUNK) <= (qpos[:, None] // CHUNK)
    s = jnp.einsum('bqhmd,bkhmd->bmhqk', q, k).astype(jnp.float32) * (HEAD_DIM ** -0.5) + bias
    p = jax.nn.softmax(jnp.where(mask, s, -jnp.inf), axis=-1)
    a = p[:, 0] - lam * p[:, 1]
    return jnp.einsum('bhqk,bkhe->bqhe', a.astype(v.dtype), v)


def stick_attend(q, k, v, qpos, kpos):
    z = jnp.einsum('bqhd,bkhd->bhqk', q, k).astype(jnp.float32) * (HEAD_DIM ** -0.5)
    valid = kpos[None, :] < qpos[:, None]
    log_1m = jnp.where(valid, jax.nn.log_sigmoid(-z), 0.0)
    after = lax.cumsum(log_1m, axis=3, reverse=True) - log_1m
    w = jnp.where(valid, jnp.exp(jax.nn.log_sigmoid(z) + after), 0.0)
    return jnp.einsum('bhqk,bkhd->bqhd', w.astype(v.dtype), v)


def band_attend(q, k, v, qpos, kpos, rel_table):
    rel = kpos[:, None, :] - qpos[:, :, None]
    bias = jnp.moveaxis(rel_table[:, jnp.clip(rel, -REL_CLIP, REL_CLIP) + REL_CLIP], 0, 1).astype(jnp.float32)
    qc = qpos[:, :, None] // CHUNK
    kc = kpos[:, None, :] // CHUNK
    mask = (kpos[:, None, :] >= 0) & (kc <= qc) & (kc >= qc - C_PREV)
    s = jnp.einsum('bcqhd,bckhd->bchqk', q, k).astype(jnp.float32) * (HEAD_DIM ** -0.5) + bias
    p = jax.nn.softmax(jnp.where(mask[:, None], s, -jnp.inf), axis=-1)
    return jnp.einsum('bchqk,bckhd->bcqhd', p.astype(v.dtype), v)


def band_gather(t):
    b, s, h, d = t.shape
    nc = s // CHUNK
    tp = jnp.pad(t.reshape(b, nc, CHUNK, h, d), ((0, 0), (C_PREV, 0), (0, 0), (0, 0), (0, 0)))
    return jnp.concatenate([tp[:, j:j + nc] for j in range(C_PREV + 1)], axis=2)


def map_query_blocks(attend, q):
    b, s = q.shape[0], q.shape[1]
    nb = s // Q_BLOCK
    qb = jnp.moveaxis(q.reshape((b, nb, Q_BLOCK) + q.shape[2:]), 1, 0)

    def body(args):
        qi, bi = args
        return attend(qi, bi * Q_BLOCK + jnp.arange(Q_BLOCK))

    o = lax.map(body, (qb, jnp.arange(nb)))
    return jnp.moveaxis(o, 0, 1).reshape((b, s) + o.shape[3:])


def mixer_project(x, g, w_in, a_qn, a_kn, c_qn, c_kn):
    h = rmsnorm(x, g)
    lead = h.shape[:-1]
    cuts = [int(c) for c in np.cumsum(IN_SPLITS)[:-1]]
    qa, ka, va, qb, kb, vb, qc, kc, vc, gz = jnp.split(h @ w_in, cuts, axis=-1)
    qa = rmsnorm(qa.reshape(lead + (H_A, 2, HEAD_DIM)), a_qn)
    ka = rmsnorm(ka.reshape(lead + (H_A, 2, HEAD_DIM)), a_kn)
    va = va.reshape(lead + (H_A, 2 * HEAD_DIM))
    qb = qb.reshape(lead + (H_B, HEAD_DIM))
    kb = kb.reshape(lead + (H_B, HEAD_DIM))
    vb = vb.reshape(lead + (H_B, HEAD_DIM))
    qc = rmsnorm(qc.reshape(lead + (H_C, HEAD_DIM)), c_qn)
    kc = rmsnorm(kc.reshape(lead + (H_C, HEAD_DIM)), c_kn)
    vc = vc.reshape(lead + (H_C, HEAD_DIM))
    gates = jax.nn.sigmoid(gz.reshape(lead + (N_BRANCH, D_MODEL)))
    return qa, ka, va, qb, kb, vb, qc, kc, vc, gates


def mixer_merge(oa, ob, oc, gates, subln, layer, w_br_a, w_br_b, w_br_c, w_out):
    lead = oa.shape[:-2]
    oa = rmsnorm(oa, subln) * (1.0 - lambda_init(layer))
    ba = oa.reshape(lead + (WA,)) @ w_br_a
    bb = ob.reshape(lead + (WB,)) @ w_br_b
    bc = oc.reshape(lead + (WC,)) @ w_br_c
    merged = gates[..., 0, :] * ba + gates[..., 1, :] * bb + gates[..., 2, :] * bc
    return merged @ w_out


def memory_kv(mem, g, w_kv, kn):
    lead = mem.shape[:-1]
    mk, mv = jnp.split(rmsnorm(mem, g) @ w_kv, 2, axis=-1)
    return rmsnorm(mk.reshape(lead + (H_X, HEAD_DIM)), kn), mv.reshape(lead + (H_X, HEAD_DIM))


def cross_attend(x, mk, mv, g, w_q, qn, w_o):
    lead = x.shape[:-1]
    q = rmsnorm((rmsnorm(x, g) @ w_q).reshape(lead + (H_X, HEAD_DIM)), qn)
    s = jnp.einsum('bqhd,bmhd->bhqm', q, mk).astype(jnp.float32) * (HEAD_DIM ** -0.5)
    p = jax.nn.softmax(s, axis=-1)
    o = jnp.einsum('bhqm,bmhd->bqhd', p.astype(mv.dtype), mv)
    return o.reshape(lead + (WX,)) @ w_o


def setup_inputs(seed: int = 0) -> dict:
    key = jax.random.key(seed)
    ks = iter(jax.random.split(key, 64))
    nrm = lambda shape, scale: jax.random.normal(next(ks), shape, jnp.float32) * scale
    gain = lambda shape: 1.0 + 0.05 * jax.random.normal(next(ks), shape, jnp.float32)
    c_buf = min(C_PAST, PAST_LEN)
    L = DEPTH
    return {
        'x_prompt': nrm((BATCH, SEQ, D_MODEL), 1.0),
        'x_sample': nrm((DEC_BATCH, DEC_SEQ, D_MODEL), 1.0),
        'mem_prompt': nrm((BATCH, N_MEM, D_MODEL), 1.0),
        'cache_a_k': nrm((L, DEC_BATCH, PAST_LEN, H_A, 2 * HEAD_DIM), 1.0),
        'cache_a_v': nrm((L, DEC_BATCH, PAST_LEN, H_A, 2 * HEAD_DIM), 1.0),
        'cache_b_k': nrm((L, DEC_BATCH, PAST_LEN, H_B, HEAD_DIM), 1.0),
        'cache_b_v': nrm((L, DEC_BATCH, PAST_LEN, H_B, HEAD_DIM), 1.0),
        'cache_c_k': nrm((L, DEC_BATCH, c_buf, H_C, HEAD_DIM), 1.0),
        'cache_c_v': nrm((L, DEC_BATCH, c_buf, H_C, HEAD_DIM), 1.0),
        'cache_mem_k': nrm((L, DEC_BATCH, N_MEM, H_X, HEAD_DIM), 1.0),
        'cache_mem_v': nrm((L, DEC_BATCH, N_MEM, H_X, HEAD_DIM), 1.0),
        't5_bias': nrm((T5_BUCKETS, H_A), 0.5),
        'ffn1_norm': gain((L, D_MODEL)),
        'ffn1_wg': nrm((L, D_MODEL, D_FF), D_MODEL ** -0.5),
        'ffn1_wu': nrm((L, D_MODEL, D_FF), D_MODEL ** -0.5),
        'ffn1_wd': nrm((L, D_FF, D_MODEL), D_FF ** -0.5),
        'mix_norm': gain((L, D_MODEL)),
        'w_in': nrm((L, D_MODEL, IN_COLS), D_MODEL ** -0.5),
        'a_qnorm': gain((L, HEAD_DIM)),
        'a_knorm': gain((L, HEAD_DIM)),
        'a_lq1': nrm((L, HEAD_DIM), 0.1),
        'a_lk1': nrm((L, HEAD_DIM), 0.1),
        'a_lq2': nrm((L, HEAD_DIM), 0.1),
        'a_lk2': nrm((L, HEAD_DIM), 0.1),
        'a_subln': gain((L, 2 * HEAD_DIM)),
        'c_qnorm': gain((L, HEAD_DIM)),
        'c_knorm': gain((L, HEAD_DIM)),
        'c_rel_bias': nrm((L, H_C, 2 * REL_CLIP + 1), 0.5),
        'w_br_a': nrm((L, WA, D_MODEL), WA ** -0.5),
        'w_br_b': nrm((L, WB, D_MODEL), WB ** -0.5),
        'w_br_c': nrm((L, WC, D_MODEL), WC ** -0.5),
        'w_out': nrm((L, D_MODEL, D_MODEL), D_MODEL ** -0.5),
        'x_norm': gain((L, D_MODEL)),
        'mem_norm': gain((L, D_MODEL)),
        'x_wq': nrm((L, D_MODEL, WX), D_MODEL ** -0.5),
        'x_wkv': nrm((L, D_MODEL, 2 * WX), D_MODEL ** -0.5),
        'x_qnorm': gain((L, HEAD_DIM)),
        'x_knorm': gain((L, HEAD_DIM)),
        'x_wo': nrm((L, WX, D_MODEL), WX ** -0.5),
        'ffn2_norm': gain((L, D_MODEL)),
        'ffn2_wg': nrm((L, D_MODEL, D_FF), D_MODEL ** -0.5),
        'ffn2_wu': nrm((L, D_MODEL, D_FF), D_MODEL ** -0.5),
        'ffn2_wd': nrm((L, D_FF, D_MODEL), D_FF ** -0.5),
    }


def reference(x_prompt, x_sample, mem_prompt, cache_a_k, cache_a_v, cache_b_k, cache_b_v,
              cache_c_k, cache_c_v, cache_mem_k, cache_mem_v, t5_bias,
              ffn1_norm, ffn1_wg, ffn1_wu, ffn1_wd, mix_norm, w_in,
              a_qnorm, a_knorm, a_lq1, a_lk1, a_lq2, a_lk2, a_subln,
              c_qnorm, c_knorm, c_rel_bias, w_br_a, w_br_b, w_br_c, w_out,
              x_norm, mem_norm, x_wq, x_wkv, x_qnorm, x_knorm, x_wo,
              ffn2_norm, ffn2_wg, ffn2_wu, ffn2_wd):
    xp = x_prompt
    bp, sp = xp.shape[0], xp.shape[1]
    nc = sp // CHUNK
    pos_p = jnp.arange(sp)
    qpos_band = pos_p.reshape(nc, CHUNK)
    kpos_band = (jnp.arange(nc)[:, None] - C_PREV) * CHUNK + jnp.arange((C_PREV + 1) * CHUNK)[None, :]
    w_keep = min(C_PAST, sp)
    ak_p, av_p, bk_p, bv_p, ck_p, cv_p, mk_p, mv_p = [], [], [], [], [], [], [], []
    for i in range(DEPTH):
        xp = xp + half_ffn(xp, ffn1_norm[i], ffn1_wg[i], ffn1_wu[i], ffn1_wd[i])
        qa, ka, va, qb, kb, vb, qc, kc, vc, gates = mixer_project(
            xp, mix_norm[i], w_in[i], a_qnorm[i], a_knorm[i], c_qnorm[i], c_knorm[i])
        lam = diff_lambda(a_lq1[i], a_lk1[i], a_lq2[i], a_lk2[i], i)
        oa = map_query_blocks(lambda qi, qpos: diff_attend(qi, ka, va, qpos, pos_p, t5_bias, lam), qa)
        ob = map_query_blocks(lambda qi, qpos: stick_attend(qi, kb, vb, qpos, pos_p), qb)
        oc = band_attend(qc.reshape(bp, nc, CHUNK, H_C, HEAD_DIM), band_gather(kc), band_gather(vc),
                         qpos_band, kpos_band, c_rel_bias[i]).reshape(bp, sp, H_C, HEAD_DIM)
        xp = xp + mixer_merge(oa, ob, oc, gates, a_subln[i], i, w_br_a[i], w_br_b[i], w_br_c[i], w_out[i])
        mk, mv = memory_kv(mem_prompt, mem_norm[i], x_wkv[i], x_knorm[i])
        xp = xp + cross_attend(xp, mk, mv, x_norm[i], x_wq[i], x_qnorm[i], x_wo[i])
        xp = xp + half_ffn(xp, ffn2_norm[i], ffn2_wg[i], ffn2_wu[i], ffn2_wd[i])
        ak_p.append(ka.reshape(bp, sp, H_A, 2 * HEAD_DIM))
        av_p.append(va)
        bk_p.append(kb)
        bv_p.append(vb)
        ck_p.append(kc[:, sp - w_keep:])
        cv_p.append(vc[:, sp - w_keep:])
        mk_p.append(mk)
        mv_p.append(mv)

    xs = x_sample
    bs, ns = xs.shape[0], xs.shape[1]
    past = cache_a_k.shape[2]
    w_buf = cache_c_k.shape[2]
    qpos_s = past + jnp.arange(ns)
    kpos_s = jnp.arange(past + ns)
    kpos_sb = (past - w_buf + jnp.arange(w_buf + ns))[None, :]
    ak_s, av_s, bk_s, bv_s, ck_s, cv_s = [], [], [], [], [], []
    for i in range(DEPTH):
        xs = xs + half_ffn(xs, ffn1_norm[i], ffn1_wg[i], ffn1_wu[i], ffn1_wd[i])
        qa, ka, va, qb, kb, vb, qc, kc, vc, gates = mixer_project(
            xs, mix_norm[i], w_in[i], a_qnorm[i], a_knorm[i], c_qnorm[i], c_knorm[i])
        lam = diff_lambda(a_lq1[i], a_lk1[i], a_lq2[i], a_lk2[i], i)
        ka_all = jnp.concatenate([cache_a_k[i].reshape(bs, past, H_A, 2, HEAD_DIM), ka], axis=1)
        va_all = jnp.concatenate([cache_a_v[i], va], axis=1)
        oa = diff_attend(qa, ka_all, va_all, qpos_s, kpos_s, t5_bias, lam)
        kb_all = jnp.concatenate([cache_b_k[i], kb], axis=1)
        vb_all = jnp.concatenate([cache_b_v[i], vb], axis=1)
        ob = stick_attend(qb, kb_all, vb_all, qpos_s, kpos_s)
        kc_all = jnp.concatenate([cache_c_k[i], kc], axis=1)
        vc_all = jnp.concatenate([cache_c_v[i], vc], axis=1)
        oc = band_attend(qc[:, None], kc_all[:, None], vc_all[:, None], qpos_s[None, :], kpos_sb, c_rel_bias[i])[:, 0]
        xs = xs + mixer_merge(oa, ob, oc, gates, a_subln[i], i, w_br_a[i], w_br_b[i], w_br_c[i], w_out[i])
        xs = xs + cross_attend(xs, cache_mem_k[i], cache_mem_v[i], x_norm[i], x_wq[i], x_qnorm[i], x_wo[i])
        xs = xs + half_ffn(xs, ffn2_norm[i], ffn2_wg[i], ffn2_wu[i], ffn2_wd[i])
        ak_s.append(ka.reshape(bs, ns, H_A, 2 * HEAD_DIM))
        av_s.append(va)
        bk_s.append(kb)
        bv_s.append(vb)
        ck_s.append(kc_all[:, ns:])
        cv_s.append(vc_all[:, ns:])

    return (xp, xs,
            jnp.stack(ak_p), jnp.stack(av_p), jnp.stack(bk_p), jnp.stack(bv_p),
            jnp.stack(ck_p), jnp.stack(cv_p), jnp.stack(mk_p), jnp.stack(mv_p),
            jnp.stack(ak_s), jnp.stack(av_s), jnp.stack(bk_s), jnp.stack(bv_s),
            jnp.stack(ck_s), jnp.stack(cv_s))
```

```python
import functools
import math

import numpy as np
import jax
import jax.numpy as jnp
from jax import lax
from jax.experimental import pallas as pl
from jax.experimental.pallas import tpu as pltpu

F32 = jnp.float32
BF16 = jnp.bfloat16

EPS = 1e-6
HEAD_DIM = 64
CHUNK = 64
C_PREV = 8
REL_CLIP = 128
T5_BUCKETS = 32
T5_MAX_DIST = 128
LANES = 128
NEG = -1e30
STICK_SKIP = -110.0
V7X_VMEM_LIMIT_BYTES = 56 * 1024 * 1024

ATTN_TQ = 256
ATTN_TK = 256
BAND_W = (C_PREV + 4) * CHUNK


def _cparams(sem):
    return pltpu.CompilerParams(dimension_semantics=sem, vmem_limit_bytes=V7X_VMEM_LIMIT_BYTES)


def _rms_rows(x, g):
    return x * lax.rsqrt(jnp.mean(x * x, axis=-1, keepdims=True) + EPS) * g


def _dot(a, b):
    return jnp.dot(a, b, preferred_element_type=F32)


def _dot_nt(a, b):
    return lax.dot_general(a, b, (((1,), (1,)), ((), ())), preferred_element_type=F32)


def _keep_lanes(q, lo, hi):
    lane = lax.broadcasted_iota(jnp.int32, q.shape, 1)
    return jnp.where((lane >= lo) & (lane < hi), q.astype(F32), 0.0).astype(BF16)


def _group_rms(y, gmat):
    parts = []
    for c in range(y.shape[1] // LANES):
        yc = y[:, c * LANES:(c + 1) * LANES]
        ms = _dot((yc * yc).astype(BF16), gmat)
        parts.append(yc * lax.rsqrt(ms + EPS))
    return parts[0] if len(parts) == 1 else jnp.concatenate(parts, axis=1)


def _group_mean_matrix():
    g = np.kron(np.eye(LANES // HEAD_DIM), np.ones((HEAD_DIM, HEAD_DIM))) / HEAD_DIM
    return jnp.asarray(g, BF16)


def _ffn_kernel(x_ref, g_ref, wg_ref, wu_ref, wd_ref, o_ref, h_scr, acc_scr):
    j = pl.program_id(1)

    @pl.when(j == 0)
    def _():
        h_scr[...] = _rms_rows(x_ref[...], g_ref[...]).astype(BF16)
        acc_scr[...] = jnp.zeros_like(acc_scr)

    h = h_scr[...]
    a = _dot(h, wg_ref[...])
    u = _dot(h, wu_ref[...])
    t = a * jax.nn.sigmoid(a) * u
    acc_scr[...] += _dot(t.astype(BF16), wd_ref[...])

    @pl.when(j == pl.num_programs(1) - 1)
    def _():
        o_ref[...] = x_ref[...] + 0.5 * acc_scr[...]


def _ffn(x, g, wg, wu, wd, tm, tf=512):
    n, d = x.shape
    dff = wg.shape[1]
    return pl.pallas_call(
        _ffn_kernel,
        grid=(n // tm, dff // tf),
        in_specs=[
            pl.BlockSpec((tm, d), lambda i, j: (i, 0)),
            pl.BlockSpec((1, d), lambda i, j: (0, 0)),
            pl.BlockSpec((d, tf), lambda i, j: (0, j)),
            pl.BlockSpec((d, tf), lambda i, j: (0, j)),
            pl.BlockSpec((tf, d), lambda i, j: (j, 0)),
        ],
        out_specs=pl.BlockSpec((tm, d), lambda i, j: (i, 0)),
        out_shape=jax.ShapeDtypeStruct((n, d), F32),
        scratch_shapes=[pltpu.VMEM((tm, d), BF16), pltpu.VMEM((tm, d), F32)],
        compiler_params=_cparams(("parallel", "arbitrary")),
        name="ffn",
    )(x, g.reshape(1, d), wg, wu, wd)


def _proj_kernel(x_ref, g_ref, w_ref, gain_ref, gmat_ref, *out_refs, segs):
    h = _rms_rows(x_ref[...], g_ref[...]).astype(BF16)
    off = 0
    for (width, normed, gained), o_ref in zip(segs, out_refs):
        y = _dot(h, w_ref[:, off:off + width])
        if normed:
            y = _group_rms(y, gmat_ref[...])
        if gained:
            y = y * gain_ref[:, off:off + width]
        o_ref[...] = y.astype(o_ref.dtype)
        off += width


def _proj(x, g, w, gain, segs, dtypes, tm):
    n, d = x.shape
    wtot = w.shape[1]
    widths = [s[0] for s in segs]
    return pl.pallas_call(
        functools.partial(_proj_kernel, segs=tuple(segs)),
        grid=(n // tm,),
        in_specs=[
            pl.BlockSpec((tm, d), lambda i: (i, 0)),
            pl.BlockSpec((1, d), lambda i: (0, 0)),
            pl.BlockSpec((d, wtot), lambda i: (0, 0)),
            pl.BlockSpec((1, wtot), lambda i: (0, 0)),
            pl.BlockSpec((LANES, LANES), lambda i: (0, 0)),
        ],
        out_specs=[pl.BlockSpec((tm, wd), lambda i: (i, 0)) for wd in widths],
        out_shape=[jax.ShapeDtypeStruct((n, wd), dt) for wd, dt in zip(widths, dtypes)],
        compiler_params=_cparams(("parallel",)),
        name="proj",
    )(x, g.reshape(1, d), w, gain, _group_mean_matrix())


def _softmax_step(qm, kb, vb, bias, m_ref, l_ref, acc_ref):
    s = _dot_nt(qm, kb) + bias
    m_old = m_ref[...]
    m_new = jnp.maximum(m_old, jnp.max(s, axis=1, keepdims=True))
    alpha = jnp.exp(m_old - m_new)
    p = jnp.exp(s - m_new)
    l_ref[...] = alpha * l_ref[...] + jnp.sum(p, axis=1, keepdims=True)
    acc_ref[...] = alpha * acc_ref[...] + _dot(p.astype(BF16), vb)
    m_ref[...] = m_new


def _cast_rows(src_ref, dst_ref, rows, dst_off=0, step=512):
    def body(i, c):
        r = pl.multiple_of(i * step, step)
        dst_ref[pl.ds(dst_off + r, step), :] = src_ref[0, pl.ds(r, step), :].astype(BF16)
        return c
    lax.fori_loop(0, rows // step, body, 0)


def _attn_a_kernel(lam_ref, cfar_ref, q_ref, k_ref, v_ref, bias_ref, o_ref,
                   kbf, vbf, m1, l1, a1, m2, l2, a2):
    h = pl.program_id(1)
    qi = pl.program_id(2)
    seq = kbf.shape[0]
    tq = q_ref.shape[1]
    tk = ATTN_TK

    @pl.when(qi == 0)
    def _():
        _cast_rows(k_ref, kbf, seq)
        _cast_rows(v_ref, vbf, seq)

    q = q_ref[0]
    q1 = _keep_lanes(q, 0, HEAD_DIM)
    q2 = _keep_lanes(q, HEAD_DIM, LANES)
    for m_ref, l_ref, a_ref in ((m1, l1, a1), (m2, l2, a2)):
        m_ref[...] = jnp.full_like(m_ref, NEG)
        l_ref[...] = jnp.zeros_like(l_ref)
        a_ref[...] = jnp.zeros_like(a_ref)

    def block(kstart, bias):
        kb = kbf[pl.ds(kstart, tk), :]
        vb = vbf[pl.ds(kstart, tk), :]
        _softmax_step(q1, kb, vb, bias, m1, l1, a1)
        _softmax_step(q2, kb, vb, bias, m2, l2, a2)

    block(pl.multiple_of(qi * tq, tq), bias_ref[0, 0])

    @pl.when(qi >= 1)
    def _():
        block(pl.multiple_of((qi - 1) * tq, tq), bias_ref[0, 1])

    cfar = cfar_ref[h]

    def far(j, c):
        block(pl.multiple_of(j * tk, tk), cfar)
        return c
    lax.fori_loop(0, qi - 1, far, 0)

    o_ref[0] = a1[...] / l1[...] - lam_ref[0] * (a2[...] / l2[...])


def _attn_a(lam, cfar, qa, ka, va, bias):
    b, s, w = qa.shape
    nh = w // LANES
    tq = ATTN_TQ
    smem = pl.BlockSpec(memory_space=pltpu.SMEM)
    return pl.pallas_call(
        _attn_a_kernel,
        grid=(b, nh, s // tq),
        in_specs=[
            smem, smem,
            pl.BlockSpec((1, tq, LANES), lambda bi, h, qi: (bi, qi, h)),
            pl.BlockSpec((1, s, LANES), lambda bi, h, qi: (bi, 0, h)),
            pl.BlockSpec((1, s, LANES), lambda bi, h, qi: (bi, 0, h)),
            pl.BlockSpec((1, 2, tq, ATTN_TK), lambda bi, h, qi: (h, 0, 0, 0)),
        ],
        out_specs=pl.BlockSpec((1, tq, LANES), lambda bi, h, qi: (bi, qi, h)),
        out_shape=jax.ShapeDtypeStruct((b, s, w), F32),
        scratch_shapes=[
            pltpu.VMEM((s, LANES), BF16), pltpu.VMEM((s, LANES), BF16),
            pltpu.VMEM((tq, 1), F32), pltpu.VMEM((tq, 1), F32), pltpu.VMEM((tq, LANES), F32),
            pltpu.VMEM((tq, 1), F32), pltpu.VMEM((tq, 1), F32), pltpu.VMEM((tq, LANES), F32),
        ],
        compiler_params=_cparams(("parallel", "parallel", "arbitrary")),
        name="mixer_a",
    )(lam, cfar, qa, ka, va, bias)


def _stick_block(z, valid, umat, carry):
    sp = jnp.maximum(z, 0.0) + jnp.log(1.0 + jnp.exp(-jnp.abs(z)))
    log1m = -sp
    if valid is not None:
        log1m = jnp.where(valid, log1m, 0.0)
    hi = log1m.astype(BF16)
    lo = (log1m - hi.astype(F32)).astype(BF16)
    after = _dot(hi, umat) + _dot(lo, umat)
    w = jnp.exp(z - sp + after + carry)
    if valid is not None:
        w = jnp.where(valid, w, 0.0)
    return w, jnp.sum(log1m, axis=1, keepdims=True)


def _attn_b_kernel(q_ref, k_ref, v_ref, u_ref, o_ref, kbf, vbf, c_scr, acc_scr):
    qi = pl.program_id(2)
    seq = kbf.shape[0]
    tq = q_ref.shape[1]
    tk = ATTN_TK

    @pl.when(qi == 0)
    def _():
        _cast_rows(k_ref, kbf, seq)
        _cast_rows(v_ref, vbf, seq)

    q = q_ref[0]
    lane = lax.broadcasted_iota(jnp.int32, q.shape, 1)
    row = lax.broadcasted_iota(jnp.int32, (tq, tk), 0)
    col = lax.broadcasted_iota(jnp.int32, (tq, tk), 1)
    umat = u_ref[...]
    outs = []
    for hh in range(LANES // HEAD_DIM):
        qm = _keep_lanes(q, hh * HEAD_DIM, (hh + 1) * HEAD_DIM)

        kstart = pl.multiple_of(qi * tq, tq)
        z = _dot_nt(qm, kbf[pl.ds(kstart, tk), :])
        w, rs = _stick_block(z, col < row, umat, 0.0)
        acc_scr[...] = _dot(w.astype(BF16), vbf[pl.ds(kstart, tk), :])
        c_scr[...] = rs

        def cond(st):
            j, cmax = st
            return (j >= 0) & (cmax > STICK_SKIP)

        def body(st):
            j, _ = st
            ks = pl.multiple_of(j * tk, tk)
            zz = _dot_nt(qm, kbf[pl.ds(ks, tk), :])
            carry = c_scr[...]
            ww, rr = _stick_block(zz, None, umat, carry)
            acc_scr[...] += _dot(ww.astype(BF16), vbf[pl.ds(ks, tk), :])
            cnew = carry + rr
            c_scr[...] = cnew
            return j - 1, jnp.max(cnew)

        lax.while_loop(cond, body, (qi - 1, jnp.max(rs)))
        outs.append(acc_scr[...])
    o_ref[0] = jnp.where(lane < HEAD_DIM, outs[0], outs[1]).astype(o_ref.dtype)


def _strict_lower(n):
    return jnp.asarray(np.tril(np.ones((n, n)), -1), BF16)


def _attn_b(qb, kb, vb):
    b, s, w = qb.shape
    tq = ATTN_TQ
    return pl.pallas_call(
        _attn_b_kernel,
        grid=(b, w // LANES, s // tq),
        in_specs=[
            pl.BlockSpec((1, tq, LANES), lambda bi, h, qi: (bi, qi, h)),
            pl.BlockSpec((1, s, LANES), lambda bi, h, qi: (bi, 0, h)),
            pl.BlockSpec((1, s, LANES), lambda bi, h, qi: (bi, 0, h)),
            pl.BlockSpec((ATTN_TK, ATTN_TK), lambda bi, h, qi: (0, 0)),
        ],
        out_specs=pl.BlockSpec((1, tq, LANES), lambda bi, h, qi: (bi, qi, h)),
        out_shape=jax.ShapeDtypeStruct((b, s, w), BF16),
        scratch_shapes=[
            pltpu.VMEM((s, LANES), BF16), pltpu.VMEM((s, LANES), BF16),
            pltpu.VMEM((tq, 1), F32), pltpu.VMEM((tq, LANES), F32),
        ],
        compiler_params=_cparams(("parallel", "parallel", "arbitrary")),
        name="mixer_b",
    )(qb, kb, vb, _strict_lower(ATTN_TK))


def _attn_c_kernel(q_ref, k_ref, v_ref, bias_ref, o_ref, kbf, vbf):
    qi = pl.program_id(2)
    seq = k_ref.shape[1]
    tq = q_ref.shape[1]
    pad = C_PREV * CHUNK

    @pl.when(qi == 0)
    def _():
        kbf[0:pad, :] = jnp.zeros((pad, LANES), BF16)
        vbf[0:pad, :] = jnp.zeros((pad, LANES), BF16)
        _cast_rows(k_ref, kbf, seq, dst_off=pad)
        _cast_rows(v_ref, vbf, seq, dst_off=pad)

    q = q_ref[0]
    lane = lax.broadcasted_iota(jnp.int32, q.shape, 1)
    wstart = pl.multiple_of(qi * tq, tq)
    kw = kbf[pl.ds(wstart, BAND_W), :]
    vw = vbf[pl.ds(wstart, BAND_W), :]
    col = lax.broadcasted_iota(jnp.int32, (tq, BAND_W), 1)
    in_seq = col >= pad - qi * tq
    outs = []
    for hh in range(LANES // HEAD_DIM):
        qm = _keep_lanes(q, hh * HEAD_DIM, (hh + 1) * HEAD_DIM)
        s = jnp.where(in_seq, _dot_nt(qm, kw) + bias_ref[hh], NEG)
        m = jnp.max(s, axis=1, keepdims=True)
        p = jnp.exp(s - m)
        l = jnp.sum(p, axis=1, keepdims=True)
        outs.append(_dot(p.astype(BF16), vw) / l)
    o_ref[0] = jnp.where(lane < HEAD_DIM, outs[0], outs[1]).astype(o_ref.dtype)


def _attn_c(qc, kc, vc, bias):
    b, s, w = qc.shape
    tq = ATTN_TQ
    pad = C_PREV * CHUNK
    hp = LANES // HEAD_DIM
    return pl.pallas_call(
        _attn_c_kernel,
        grid=(b, w // LANES, s // tq),
        in_specs=[
            pl.BlockSpec((1, tq, LANES), lambda bi, h, qi: (bi, qi, h)),
            pl.BlockSpec((1, s, LANES), lambda bi, h, qi: (bi, 0, h)),
            pl.BlockSpec((1, s, LANES), lambda bi, h, qi: (bi, 0, h)),
            pl.BlockSpec((hp, tq, BAND_W), lambda bi, h, qi: (h, 0, 0)),
        ],
        out_specs=pl.BlockSpec((1, tq, LANES), lambda bi, h, qi: (bi, qi, h)),
        out_shape=jax.ShapeDtypeStruct((b, s, w), BF16),
        scratch_shapes=[pltpu.VMEM((s + pad, LANES), BF16), pltpu.VMEM((s + pad, LANES), BF16)],
        compiler_params=_cparams(("parallel", "parallel", "arbitrary")),
        name="mixer_c",
    )(qc, kc, vc, bias)


def _merge_kernel(x_ref, oa_ref, ob_ref, oc_ref, g_ref, wgate_ref, sub_ref,
                  wa_ref, wb_ref, wc_ref, wout_ref, o_ref):
    x = x_ref[...]
    d = x.shape[1]
    h = _rms_rows(x, g_ref[...]).astype(BF16)
    oa = oa_ref[...]
    parts = []
    for c in range(oa.shape[1] // LANES):
        oc_ = oa[:, c * LANES:(c + 1) * LANES]
        parts.append(oc_ * lax.rsqrt(jnp.mean(oc_ * oc_, axis=-1, keepdims=True) + EPS))
    oan = (jnp.concatenate(parts, axis=1) * sub_ref[...]).astype(BF16)
    merged = jax.nn.sigmoid(_dot(h, wgate_ref[:, 0:d])) * _dot(oan, wa_ref[...])
    merged += jax.nn.sigmoid(_dot(h, wgate_ref[:, d:2 * d])) * _dot(ob_ref[...], wb_ref[...])
    merged += jax.nn.sigmoid(_dot(h, wgate_ref[:, 2 * d:3 * d])) * _dot(oc_ref[...], wc_ref[...])
    o_ref[...] = x + _dot(merged.astype(BF16), wout_ref[...])


def _merge(x, oa, ob, oc, g, wgate, sub, wa, wb, wc, wout, tm):
    n, d = x.shape
    full = lambda a: pl.BlockSpec(a.shape, lambda i: (0,) * a.ndim)
    rows = lambda a: pl.BlockSpec((tm, a.shape[1]), lambda i: (i, 0))
    g = g.reshape(1, d)
    return pl.pallas_call(
        _merge_kernel,
        grid=(n // tm,),
        in_specs=[rows(x), rows(oa), rows(ob), rows(oc), full(g), full(wgate), full(sub),
                  full(wa), full(wb), full(wc), full(wout)],
        out_specs=rows(x),
        out_shape=jax.ShapeDtypeStruct((n, d), F32),
        compiler_params=_cparams(("parallel",)),
        name="merge",
    )(x, oa, ob, oc, g, wgate, sub, wa, wb, wc, wout)


def _cross_kernel(x_ref, g_ref, wq_ref, gain_ref, gmat_ref, mk_ref, mv_ref, wo_ref, o_ref):
    x = x_ref[0]
    h = _rms_rows(x, g_ref[...]).astype(BF16)
    q = (_group_rms(_dot(h, wq_ref[...]), gmat_ref[...]) * gain_ref[...]).astype(BF16)
    mk = mk_ref[0]
    mv = mv_ref[0]
    lane = lax.broadcasted_iota(jnp.int32, q.shape, 1)
    o = jnp.zeros(q.shape, F32)
    for hh in range(q.shape[1] // HEAD_DIM):
        in_head = (lane >= hh * HEAD_DIM) & (lane < (hh + 1) * HEAD_DIM)
        s = _dot_nt(_keep_lanes(q, hh * HEAD_DIM, (hh + 1) * HEAD_DIM), mk)
        p = jnp.exp(s - jnp.max(s, axis=1, keepdims=True))
        l = jnp.sum(p, axis=1, keepdims=True)
        o = jnp.where(in_head, _dot(p.astype(BF16), mv) / l, o)
    o_ref[0] = x + _dot(o.astype(BF16), wo_ref[...])


def _cross(x, g, wq, gain, mk, mv, wo, tm):
    b, s, d = x.shape
    full = lambda a: pl.BlockSpec(a.shape, lambda bi, i: (0,) * a.ndim)
    g = g.reshape(1, d)
    gmat = _group_mean_matrix()
    return pl.pallas_call(
        _cross_kernel,
        grid=(b, s // tm),
        in_specs=[
            pl.BlockSpec((1, tm, d), lambda bi, i: (bi, i, 0)),
            full(g), full(wq), full(gain), full(gmat),
            pl.BlockSpec((1,) + mk.shape[1:], lambda bi, i: (bi, 0, 0)),
            pl.BlockSpec((1,) + mv.shape[1:], lambda bi, i: (bi, 0, 0)),
            full(wo),
        ],
        out_specs=pl.BlockSpec((1, tm, d), lambda bi, i: (bi, i, 0)),
        out_shape=jax.ShapeDtypeStruct((b, s, d), F32),
        compiler_params=_cparams(("parallel", "parallel")),
        name="cross",
    )(x, g, wq, gain, gmat, mk, mv, wo)


def _sample_attn_kernel(lam_ref,
                        qa_ref, kan_ref, van_ref, kac_ref, vac_ref, bac_ref, ban_ref,
                        qb_ref, kbn_ref, vbn_ref, kbc_ref, vbc_ref, ubig_ref, usmall_ref,
                        qc_ref, kcn_ref, vcn_ref, kcc_ref, vcc_ref, bcc_ref, bcn_ref,
                        oa_ref, ob_ref, oc_ref):
    lam = lam_ref[0]
    ns = qa_ref.shape[1]
    bf = lambda r: r[0].astype(BF16)

    def heads(q, width):
        for hh in range(LANES // width):
            yield hh, None, _keep_lanes(q, hh * width, (hh + 1) * width)

    def softmax2(s_c, s_n):
        m = jnp.maximum(jnp.max(s_c, axis=1, keepdims=True), jnp.max(s_n, axis=1, keepdims=True))
        p_c = jnp.exp(s_c - m)
        p_n = jnp.exp(s_n - m)
        inv = 1.0 / (jnp.sum(p_c, axis=1, keepdims=True) + jnp.sum(p_n, axis=1, keepdims=True))
        return p_c * inv, p_n * inv

    for h in range(qa_ref.shape[2] // LANES):
        sl = slice(h * LANES, (h + 1) * LANES)
        q = qa_ref[0, :, sl]
        k_c = kac_ref[0, :, sl].astype(BF16)
        k_n = kan_ref[0, :, sl].astype(BF16)
        maps = []
        for _, _, qm in heads(q, HEAD_DIM):
            maps.append(softmax2(_dot_nt(qm, k_c) + bac_ref[h], _dot_nt(qm, k_n) + ban_ref[h]))
        a_c = (maps[0][0] - lam * maps[1][0]).astype(BF16)
        a_n = (maps[0][1] - lam * maps[1][1]).astype(BF16)
        oa_ref[0, :, sl] = (_dot(a_c, vac_ref[0, :, sl].astype(BF16))
                            + _dot(a_n, van_ref[0, :, sl].astype(BF16)))

    past = kbc_ref.shape[1]
    tk = ubig_ref.shape[0]
    row = lax.broadcasted_iota(jnp.int32, (ns, ns), 0)
    col = lax.broadcasted_iota(jnp.int32, (ns, ns), 1)
    for pr in range(qb_ref.shape[2] // LANES):
        sl = slice(pr * LANES, (pr + 1) * LANES)
        q = qb_ref[0, :, sl]
        k_n = kbn_ref[0, :, sl].astype(BF16)
        v_n = vbn_ref[0, :, sl].astype(BF16)
        outs = []
        for _, _, qm in heads(q, HEAD_DIM):
            w, carry = _stick_block(_dot_nt(qm, k_n), col < row, usmall_ref[...], 0.0)
            acc = _dot(w.astype(BF16), v_n)
            for j in range(past // tk - 1, -1, -1):
                k_c = kbc_ref[0, j * tk:(j + 1) * tk, sl].astype(BF16)
                v_c = vbc_ref[0, j * tk:(j + 1) * tk, sl].astype(BF16)
                w, rs = _stick_block(_dot_nt(qm, k_c), None, ubig_ref[...], carry)
                acc += _dot(w.astype(BF16), v_c)
                carry = carry + rs
            outs.append(acc)
        lane = lax.broadcasted_iota(jnp.int32, q.shape, 1)
        ob_ref[0, :, sl] = jnp.where(lane < HEAD_DIM, outs[0], outs[1]).astype(ob_ref.dtype)

    for pr in range(qc_ref.shape[2] // LANES):
        sl = slice(pr * LANES, (pr + 1) * LANES)
        q = qc_ref[0, :, sl]
        k_c = kcc_ref[0, :, sl].astype(BF16)
        k_n = kcn_ref[0, :, sl].astype(BF16)
        v_c = vcc_ref[0, :, sl].astype(BF16)
        v_n = vcn_ref[0, :, sl].astype(BF16)
        outs = []
        for hh, _, qm in heads(q, HEAD_DIM):
            hd = pr * (LANES // HEAD_DIM) + hh
            p_c, p_n = softmax2(_dot_nt(qm, k_c) + bcc_ref[hd], _dot_nt(qm, k_n) + bcn_ref[hd])
            outs.append(_dot(p_c.astype(BF16), v_c) + _dot(p_n.astype(BF16), v_n))
        lane = lax.broadcasted_iota(jnp.int32, q.shape, 1)
        oc_ref[0, :, sl] = jnp.where(lane < HEAD_DIM, outs[0], outs[1]).astype(oc_ref.dtype)


def _sample_attn(lam, qa, ka, va, cak, cav, bac, ban, qb, kb, vb, cbk, cbv,
                 qc, kc, vc, cck, ccv, bcc, bcn):
    b, ns, _ = qa.shape
    per_b = lambda a: pl.BlockSpec((1,) + a.shape[1:], lambda bi: (bi,) + (0,) * (a.ndim - 1))
    full = lambda a: pl.BlockSpec(a.shape, lambda bi: (0,) * a.ndim)
    ubig = _strict_lower(ATTN_TK)
    usmall = _strict_lower(ns)
    args = [qa, ka, va, cak, cav, bac, ban, qb, kb, vb, cbk, cbv, ubig, usmall,
            qc, kc, vc, cck, ccv, bcc, bcn]
    specs = [per_b(qa), per_b(ka), per_b(va), per_b(cak), per_b(cav), full(bac), full(ban),
             per_b(qb), per_b(kb), per_b(vb), per_b(cbk), per_b(cbv), full(ubig), full(usmall),
             per_b(qc), per_b(kc), per_b(vc), per_b(cck), per_b(ccv), full(bcc), full(bcn)]
    return pl.pallas_call(
        _sample_attn_kernel,
        grid=(b,),
        in_specs=[pl.BlockSpec(memory_space=pltpu.SMEM)] + specs,
        out_specs=[per_b(qa), per_b(qb), per_b(qc)],
        out_shape=[jax.ShapeDtypeStruct(qa.shape, F32), jax.ShapeDtypeStruct(qb.shape, BF16),
                   jax.ShapeDtypeStruct(qc.shape, BF16)],
        compiler_params=_cparams(("parallel",)),
        name="sample_mixers",
    )(lam, *args)


def _t5_bucket_np(rel):
    half = T5_BUCKETS // 2
    max_exact = half // 2
    n = np.abs(rel)
    nf = np.maximum(n, 1).astype(np.float64)
    large = max_exact + (np.log(nf / max_exact) / math.log(T5_MAX_DIST / max_exact)
                         * (half - max_exact)).astype(np.int64)
    large = np.minimum(large, half - 1)
    return np.where(rel > 0, half, 0) + np.where(n < max_exact, n, large)


def _t5_bias_table(t5_bias, qpos, kpos):
    rel = kpos[None, :] - qpos[:, None]
    bias = jnp.transpose(t5_bias[_t5_bucket_np(rel)], (2, 0, 1)).astype(F32)
    mask = (kpos[None, :] // CHUNK) <= (qpos[:, None] // CHUNK)
    return jnp.where(jnp.asarray(mask)[None], bias, NEG)


def _band_bias_table(rel_table, qpos, kpos):
    rel = kpos[None, :] - qpos[:, None]
    bias = rel_table[:, np.clip(rel, -REL_CLIP, REL_CLIP) + REL_CLIP].astype(F32)
    qc = qpos[:, None] // CHUNK
    kc = kpos[None, :] // CHUNK
    mask = (kpos[None, :] >= 0) & (kc <= qc) & (kc >= qc - C_PREV)
    return jnp.where(jnp.asarray(mask)[None], bias, NEG)


def _lambda_init(layer):
    return 0.8 - 0.6 * math.exp(-0.3 * layer)


def kernel(x_prompt, x_sample, mem_prompt, cache_a_k, cache_a_v, cache_b_k, cache_b_v, cache_c_k, cache_c_v, cache_mem_k, cache_mem_v, t5_bias, ffn1_norm, ffn1_wg, ffn1_wu, ffn1_wd, mix_norm, w_in, a_qnorm, a_knorm, a_lq1, a_lk1, a_lq2, a_lk2, a_subln, c_qnorm, c_knorm, c_rel_bias, w_br_a, w_br_b, w_br_c, w_out, x_norm, mem_norm, x_wq, x_wkv, x_qnorm, x_knorm, x_wo, ffn2_norm, ffn2_wg, ffn2_wu, ffn2_wd):
    bp, sp, d = x_prompt.shape
    bs, ns, _ = x_sample.shape
    depth = w_in.shape[0]
    past = cache_a_k.shape[2]
    w_buf = cache_c_k.shape[2]
    n_mem = mem_prompt.shape[1]
    wa = cache_a_k.shape[3] * cache_a_k.shape[4]
    wb = cache_b_k.shape[3] * cache_b_k.shape[4]
    wc = cache_c_k.shape[3] * cache_c_k.shape[4]
    wx = cache_mem_k.shape[3] * cache_mem_k.shape[4]
    h_a = cache_a_k.shape[3]
    n_qkv = 3 * (wa + wb + wc)
    w_keep = min(C_PREV * CHUNK, sp)
    scale = HEAD_DIM ** -0.5
    assert sp % ATTN_TQ == 0 and ATTN_TQ == ATTN_TK and ATTN_TQ == 4 * CHUNK and sp >= w_keep
    assert past % ATTN_TK == 0 and T5_MAX_DIST <= ATTN_TK

    tile = lambda g, reps: jnp.tile(g.astype(F32), reps)
    bf = lambda a: a.astype(BF16)

    loc = np.arange(ATTN_TQ)
    a_bias = jnp.stack([_t5_bias_table(t5_bias, loc + ATTN_TQ, loc + ATTN_TQ),
                        _t5_bias_table(t5_bias, loc + ATTN_TQ, loc)], axis=1)
    a_far = t5_bias[T5_BUCKETS // 2 - 1].astype(F32)
    qpos_s = past + np.arange(ns)
    a_bias_sc = _t5_bias_table(t5_bias, qpos_s, np.arange(past))
    a_bias_sn = _t5_bias_table(t5_bias, qpos_s, qpos_s)
    kpos_sb = past - w_buf + np.arange(w_buf + ns)

    xp = x_prompt.reshape(bp * sp, d)
    xs = x_sample.reshape(bs * ns, d)
    mem = mem_prompt.reshape(bp * n_mem, d)
    outs = {k: [] for k in ("ak_p", "av_p", "bk_p", "bv_p", "ck_p", "cv_p", "mk_p", "mv_p",
                            "ak_s", "av_s", "bk_s", "bv_s", "ck_s", "cv_s")}
    qkv_segs = [(wa, True, True), (wa, True, True), (wa, False, False),
                (wb, False, True), (wb, False, False), (wb, False, False),
                (wc, True, True), (wc, True, True), (wc, False, False)]
    qkv_dtypes = [BF16, F32, F32, BF16, F32, F32, BF16, F32, F32]

    for i in range(depth):
        wg1, wu1, wd1 = bf(ffn1_wg[i]), bf(ffn1_wu[i]), bf(ffn1_wd[i])
        wg2, wu2, wd2 = bf(ffn2_wg[i]), bf(ffn2_wu[i]), bf(ffn2_wd[i])
        w_qkv, w_gate = bf(w_in[i][:, :n_qkv]), bf(w_in[i][:, n_qkv:])
        qkv_gain = jnp.concatenate([
            tile(a_qnorm[i], wa // HEAD_DIM) * scale, tile(a_knorm[i], wa // HEAD_DIM), jnp.ones((wa,), F32),
            jnp.full((wb,), scale, F32), jnp.ones((2 * wb,), F32),
            tile(c_qnorm[i], wc // HEAD_DIM) * scale, tile(c_knorm[i], wc // HEAD_DIM), jnp.ones((wc,), F32),
        ]).reshape(1, n_qkv)
        sub = (tile(a_subln[i], h_a) * (1.0 - _lambda_init(i))).reshape(1, wa)
        wbr = bf(w_br_a[i]), bf(w_br_b[i]), bf(w_br_c[i])
        wout = bf(w_out[i])
        wq, wo = bf(x_wq[i]), bf(x_wo[i])
        xq_gain = (tile(x_qnorm[i], wx // HEAD_DIM) * scale).reshape(1, wx)
        dot64 = lambda a, b: jnp.exp(jnp.sum(a.astype(F32) * b.astype(F32)))
        lam = (dot64(a_lq1[i], a_lk1[i]) - dot64(a_lq2[i], a_lk2[i]) + _lambda_init(i)).reshape(1)
        c_bias_p = _band_bias_table(c_rel_bias[i], C_PREV * CHUNK + loc, np.arange(BAND_W))
        c_bias_sc = _band_bias_table(c_rel_bias[i], qpos_s, kpos_sb[:w_buf])
        c_bias_sn = _band_bias_table(c_rel_bias[i], qpos_s, kpos_sb[w_buf:])

        xp = _ffn(xp, ffn1_norm[i], wg1, wu1, wd1, tm=1024)
        qa, ka, va, qb, kb, vb, qc, kc, vc = _proj(xp, mix_norm[i], w_qkv, qkv_gain, qkv_segs, qkv_dtypes, tm=512)
        r3 = lambda a: a.reshape(bp, sp, a.shape[-1])
        oa = _attn_a(lam, a_far, r3(qa), r3(ka), r3(va), a_bias)
        ob = _attn_b(r3(qb), r3(kb), r3(vb))
        oc = _attn_c(r3(qc), r3(kc), r3(vc), c_bias_p)
        xp = _merge(xp, oa.reshape(-1, wa), ob.reshape(-1, wb), oc.reshape(-1, wc), mix_norm[i], w_gate, sub,
                    *wbr, wout, tm=512)
        mk, mv = _proj(mem, mem_norm[i], bf(x_wkv[i]),
                       jnp.concatenate([tile(x_knorm[i], wx // HEAD_DIM), jnp.ones((wx,), F32)]).reshape(1, 2 * wx),
                       [(wx, True, True), (wx, False, False)], [F32, F32], tm=n_mem)
        mk3, mv3 = mk.reshape(bp, n_mem, wx), mv.reshape(bp, n_mem, wx)
        xp = _cross(xp.reshape(bp, sp, d), x_norm[i], wq, xq_gain, bf(mk3), bf(mv3), wo, tm=512).reshape(-1, d)
        xp = _ffn(xp, ffn2_norm[i], wg2, wu2, wd2, tm=1024)
        outs["ak_p"].append(ka.reshape(bp, sp, h_a, -1))
        outs["av_p"].append(va.reshape(bp, sp, h_a, -1))
        outs["bk_p"].append(kb.reshape(bp, sp, -1, HEAD_DIM))
        outs["bv_p"].append(vb.reshape(bp, sp, -1, HEAD_DIM))
        outs["ck_p"].append(r3(kc)[:, sp - w_keep:].reshape(bp, w_keep, -1, HEAD_DIM))
        outs["cv_p"].append(r3(vc)[:, sp - w_keep:].reshape(bp, w_keep, -1, HEAD_DIM))
        outs["mk_p"].append(mk3.reshape(bp, n_mem, -1, HEAD_DIM))
        outs["mv_p"].append(mv3.reshape(bp, n_mem, -1, HEAD_DIM))

        xs = _ffn(xs, ffn1_norm[i], wg1, wu1, wd1, tm=bs * ns)
        qa, ka, va, qb, kb, vb, qc, kc, vc = _proj(xs, mix_norm[i], w_qkv, qkv_gain, qkv_segs, qkv_dtypes, tm=bs * ns)
        s3 = lambda a: a.reshape(bs, ns, a.shape[-1])
        c3 = lambda a: a.reshape(bs, a.shape[1], -1)
        oa, ob, oc = _sample_attn(
            lam, s3(qa), s3(ka), s3(va), c3(cache_a_k[i]), c3(cache_a_v[i]), a_bias_sc, a_bias_sn,
            s3(qb), s3(kb), s3(vb), c3(cache_b_k[i]), c3(cache_b_v[i]),
            s3(qc), s3(kc), s3(vc), c3(cache_c_k[i]), c3(cache_c_v[i]), c_bias_sc, c_bias_sn)
        xs = _merge(xs, oa.reshape(-1, wa), ob.reshape(-1, wb), oc.reshape(-1, wc), mix_norm[i], w_gate, sub,
                    *wbr, wout, tm=bs * ns)
        xs = _cross(xs.reshape(bs, ns, d), x_norm[i], wq, xq_gain, bf(c3(cache_mem_k[i])), bf(c3(cache_mem_v[i])),
                    wo, tm=ns).reshape(-1, d)
        xs = _ffn(xs, ffn2_norm[i], wg2, wu2, wd2, tm=bs * ns)
        outs["ak_s"].append(ka.reshape(bs, ns, h_a, -1))
        outs["av_s"].append(va.reshape(bs, ns, h_a, -1))
        outs["bk_s"].append(kb.reshape(bs, ns, -1, HEAD_DIM))
        outs["bv_s"].append(vb.reshape(bs, ns, -1, HEAD_DIM))
        outs["ck_s"].append(jnp.concatenate([cache_c_k[i], kc.reshape(bs, ns, -1, HEAD_DIM)], axis=1)[:, ns:])
        outs["cv_s"].append(jnp.concatenate([cache_c_v[i], vc.reshape(bs, ns, -1, HEAD_DIM)], axis=1)[:, ns:])

    st = lambda k: jnp.stack(outs[k])
    return (xp.reshape(bp, sp, d), xs.reshape(bs, ns, d),
            st("ak_p"), st("av_p"), st("bk_p"), st("bv_p"), st("ck_p"), st("cv_p"), st("mk_p"), st("mv_p"),
            st("ak_s"), st("av_s"), st("bk_s"), st("bv_s"), st("ck_s"), st("cv_s"))
```

```python
import functools
import math

import numpy as np
import jax
import jax.numpy as jnp
from jax import lax
from jax.experimental import pallas as pl
from jax.experimental.pallas import tpu as pltpu

F32 = jnp.float32
BF16 = jnp.bfloat16

EPS = 1e-6
HEAD_DIM = 64
CHUNK = 64
C_PREV = 8
REL_CLIP = 128
T5_BUCKETS = 32
T5_MAX_DIST = 128
LANES = 128
NEG = -1e30
STICK_SKIP = -110.0
V7X_VMEM_LIMIT_BYTES = 56 * 1024 * 1024

MIXER_A_TILE = 512
ATTN_TQ = 256
ATTN_TK = 256
BAND_W = (C_PREV + 4) * CHUNK


def _cparams(sem):
    return pltpu.CompilerParams(dimension_semantics=sem, vmem_limit_bytes=V7X_VMEM_LIMIT_BYTES)


def _rms_rows(x, g):
    return x * lax.rsqrt(jnp.mean(x * x, axis=-1, keepdims=True) + EPS) * g


def _dot(a, b):
    return jnp.dot(a, b, preferred_element_type=F32)


def _dot_nt(a, b):
    return lax.dot_general(a, b, (((1,), (1,)), ((), ())), preferred_element_type=F32)


def _keep_lanes(q, lo, hi):
    lane = lax.broadcasted_iota(jnp.int32, q.shape, 1)
    return jnp.where((lane >= lo) & (lane < hi), q.astype(F32), 0.0).astype(BF16)


def _group_rms(y, gmat):
    parts = []
    for c in range(y.shape[1] // LANES):
        yc = y[:, c * LANES:(c + 1) * LANES]
        ms = _dot((yc * yc).astype(BF16), gmat)
        parts.append(yc * lax.rsqrt(ms + EPS))
    return parts[0] if len(parts) == 1 else jnp.concatenate(parts, axis=1)


def _group_mean_matrix():
    g = np.kron(np.eye(LANES // HEAD_DIM), np.ones((HEAD_DIM, HEAD_DIM))) / HEAD_DIM
    return jnp.asarray(g, BF16)


def _ffn_kernel(x_ref, g_ref, wg_ref, wu_ref, wd_ref, o_ref, h_scr, acc_scr):
    j = pl.program_id(1)

    @pl.when(j == 0)
    def _():
        h_scr[...] = _rms_rows(x_ref[...], g_ref[...]).astype(BF16)
        acc_scr[...] = jnp.zeros_like(acc_scr)

    h = h_scr[...]
    a = _dot(h, wg_ref[...])
    u = _dot(h, wu_ref[...])
    t = a * jax.nn.sigmoid(a) * u
    acc_scr[...] += _dot(t.astype(BF16), wd_ref[...])

    @pl.when(j == pl.num_programs(1) - 1)
    def _():
        o_ref[...] = x_ref[...] + 0.5 * acc_scr[...]


def _ffn(x, g, wg, wu, wd, tm, tf=512):
    n, d = x.shape
    dff = wg.shape[1]
    return pl.pallas_call(
        _ffn_kernel,
        grid=(n // tm, dff // tf),
        in_specs=[
            pl.BlockSpec((tm, d), lambda i, j: (i, 0)),
            pl.BlockSpec((1, d), lambda i, j: (0, 0)),
            pl.BlockSpec((d, tf), lambda i, j: (0, j)),
            pl.BlockSpec((d, tf), lambda i, j: (0, j)),
            pl.BlockSpec((tf, d), lambda i, j: (j, 0)),
        ],
        out_specs=pl.BlockSpec((tm, d), lambda i, j: (i, 0)),
        out_shape=jax.ShapeDtypeStruct((n, d), F32),
        scratch_shapes=[pltpu.VMEM((tm, d), BF16), pltpu.VMEM((tm, d), F32)],
        compiler_params=_cparams(("parallel", "arbitrary")),
        name="ffn",
    )(x, g.reshape(1, d), wg, wu, wd)


def _proj_kernel(x_ref, g_ref, w_ref, gain_ref, gmat_ref, *out_refs, segs):
    h = _rms_rows(x_ref[...], g_ref[...]).astype(BF16)
    off = 0
    for (width, normed, gained), o_ref in zip(segs, out_refs):
        y = _dot(h, w_ref[:, off:off + width])
        if normed:
            y = _group_rms(y, gmat_ref[...])
        if gained:
            y = y * gain_ref[:, off:off + width]
        o_ref[...] = y.astype(o_ref.dtype)
        off += width


def _proj(x, g, w, gain, segs, dtypes, tm):
    n, d = x.shape
    wtot = w.shape[1]
    widths = [s[0] for s in segs]
    return pl.pallas_call(
        functools.partial(_proj_kernel, segs=tuple(segs)),
        grid=(n // tm,),
        in_specs=[
            pl.BlockSpec((tm, d), lambda i: (i, 0)),
            pl.BlockSpec((1, d), lambda i: (0, 0)),
            pl.BlockSpec((d, wtot), lambda i: (0, 0)),
            pl.BlockSpec((1, wtot), lambda i: (0, 0)),
            pl.BlockSpec((LANES, LANES), lambda i: (0, 0)),
        ],
        out_specs=[pl.BlockSpec((tm, wd), lambda i: (i, 0)) for wd in widths],
        out_shape=[jax.ShapeDtypeStruct((n, wd), dt) for wd, dt in zip(widths, dtypes)],
        compiler_params=_cparams(("parallel",)),
        name="proj",
    )(x, g.reshape(1, d), w, gain, _group_mean_matrix())


def _softmax_step_km(s, shift, vt, m_ref, l_ref, acc_ref):
    m_old = m_ref[...]
    m_new = jnp.maximum(m_old, jnp.max(s, axis=0, keepdims=True) + shift)
    alpha = jnp.exp(m_old - m_new)
    p = jnp.exp(s - (m_new - shift))
    l_ref[...] = alpha * l_ref[...] + jnp.sum(p, axis=0, keepdims=True)
    acc_ref[...] = alpha * acc_ref[...] + _dot(vt, p.astype(BF16))
    m_ref[...] = m_new


def _cast_rows(src_ref, dst_ref, rows, dst_off=0, step=512):
    def body(i, c):
        r = pl.multiple_of(i * step, step)
        dst_ref[pl.ds(dst_off + r, step), :] = src_ref[0, pl.ds(r, step), :].astype(BF16)
        return c
    lax.fori_loop(0, rows // step, body, 0)


def _attn_a_kernel(lam_ref, cfar_ref, q_ref, k_ref, v_ref, bias_ref, o_ref,
                   kbf, vtb, m1, l1, a1, m2, l2, a2):
    h = pl.program_id(1)
    qi = pl.program_id(2)
    seq = kbf.shape[0]
    tq = q_ref.shape[1]
    tk = vtb.shape[2]

    @pl.when(qi == 0)
    def _():
        _cast_rows(k_ref, kbf, seq)

        def vt_body(j, c):
            r = pl.multiple_of(j * tk, tk)
            vtb[j] = v_ref[0, pl.ds(r, tk), :].T.astype(BF16)
            return c
        lax.fori_loop(0, seq // tk, vt_body, 0)

    qt = q_ref[0].astype(F32).T
    sub = lax.broadcasted_iota(jnp.int32, qt.shape, 0)
    q1t = jnp.where(sub < HEAD_DIM, qt, 0.0).astype(BF16)
    q2t = jnp.where(sub >= HEAD_DIM, qt, 0.0).astype(BF16)
    maps = ((q1t, m1, l1, a1), (q2t, m2, l2, a2))
    for _, m_ref, l_ref, a_ref in maps:
        m_ref[...] = jnp.full_like(m_ref, NEG)
        l_ref[...] = jnp.zeros_like(l_ref)
        a_ref[...] = jnp.zeros_like(a_ref)

    def block(j, bias_tile, shift):
        kb = kbf[pl.ds(pl.multiple_of(j * tk, tk), tk), :]
        vt = vtb[j]
        for qmt, m_ref, l_ref, a_ref in maps:
            s = _dot(kb, qmt)
            if bias_tile is not None:
                s = s + bias_tile
            _softmax_step_km(s, shift, vt, m_ref, l_ref, a_ref)

    block(qi, bias_ref[0, 0], 0.0)

    @pl.when(qi >= 1)
    def _():
        block(qi - 1, bias_ref[0, 1], 0.0)

    cfar = cfar_ref[h]

    def far(j, c):
        block(j, None, cfar)
        return c
    lax.fori_loop(0, qi - 1, far, 0)

    out_t = a1[...] / l1[...] - lam_ref[0] * (a2[...] / l2[...])
    o_ref[0] = out_t.T


def _attn_a(lam, cfar, qa, ka, va, bias):
    b, s, w = qa.shape
    nh = w // LANES
    tq = tk = MIXER_A_TILE
    smem = pl.BlockSpec(memory_space=pltpu.SMEM)
    return pl.pallas_call(
        _attn_a_kernel,
        grid=(b, nh, s // tq),
        in_specs=[
            smem, smem,
            pl.BlockSpec((1, tq, LANES), lambda bi, h, qi: (bi, qi, h)),
            pl.BlockSpec((1, s, LANES), lambda bi, h, qi: (bi, 0, h)),
            pl.BlockSpec((1, s, LANES), lambda bi, h, qi: (bi, 0, h)),
            pl.BlockSpec((1, 2, tk, tq), lambda bi, h, qi: (h, 0, 0, 0)),
        ],
        out_specs=pl.BlockSpec((1, tq, LANES), lambda bi, h, qi: (bi, qi, h)),
        out_shape=jax.ShapeDtypeStruct((b, s, w), F32),
        scratch_shapes=[
            pltpu.VMEM((s, LANES), BF16), pltpu.VMEM((s // tk, LANES, tk), BF16),
            pltpu.VMEM((1, tq), F32), pltpu.VMEM((1, tq), F32), pltpu.VMEM((LANES, tq), F32),
            pltpu.VMEM((1, tq), F32), pltpu.VMEM((1, tq), F32), pltpu.VMEM((LANES, tq), F32),
        ],
        compiler_params=_cparams(("parallel", "parallel", "arbitrary")),
        name="mixer_a",
    )(lam, cfar, qa, ka, va, bias)


def _stick_block(z, valid, umat, carry):
    sp = jnp.maximum(z, 0.0) + jnp.log(1.0 + jnp.exp(-jnp.abs(z)))
    log1m = -sp
    if valid is not None:
        log1m = jnp.where(valid, log1m, 0.0)
    hi = log1m.astype(BF16)
    lo = (log1m - hi.astype(F32)).astype(BF16)
    after = _dot(hi, umat) + _dot(lo, umat)
    w = jnp.exp(z - sp + after + carry)
    if valid is not None:
        w = jnp.where(valid, w, 0.0)
    return w, jnp.sum(log1m, axis=1, keepdims=True)


def _attn_b_kernel(q_ref, k_ref, v_ref, u_ref, o_ref, kbf, vbf, c_scr, acc_scr):
    qi = pl.program_id(2)
    seq = kbf.shape[0]
    tq = q_ref.shape[1]
    tk = ATTN_TK

    @pl.when(qi == 0)
    def _():
        _cast_rows(k_ref, kbf, seq)
        _cast_rows(v_ref, vbf, seq)

    q = q_ref[0]
    lane = lax.broadcasted_iota(jnp.int32, q.shape, 1)
    row = lax.broadcasted_iota(jnp.int32, (tq, tk), 0)
    col = lax.broadcasted_iota(jnp.int32, (tq, tk), 1)
    umat = u_ref[...]
    outs = []
    for hh in range(LANES // HEAD_DIM):
        qm = _keep_lanes(q, hh * HEAD_DIM, (hh + 1) * HEAD_DIM)

        kstart = pl.multiple_of(qi * tq, tq)
        z = _dot_nt(qm, kbf[pl.ds(kstart, tk), :])
        w, rs = _stick_block(z, col < row, umat, 0.0)
        acc_scr[...] = _dot(w.astype(BF16), vbf[pl.ds(kstart, tk), :])
        c_scr[...] = rs

        def cond(st):
            j, cmax = st
            return (j >= 0) & (cmax > STICK_SKIP)

        def body(st):
            j, _ = st
            ks = pl.multiple_of(j * tk, tk)
            zz = _dot_nt(qm, kbf[pl.ds(ks, tk), :])
            carry = c_scr[...]
            ww, rr = _stick_block(zz, None, umat, carry)
            acc_scr[...] += _dot(ww.astype(BF16), vbf[pl.ds(ks, tk), :])
            cnew = carry + rr
            c_scr[...] = cnew
            return j - 1, jnp.max(cnew)

        lax.while_loop(cond, body, (qi - 1, jnp.max(rs)))
        outs.append(acc_scr[...])
    o_ref[0] = jnp.where(lane < HEAD_DIM, outs[0], outs[1]).astype(o_ref.dtype)


def _strict_lower(n):
    return jnp.asarray(np.tril(np.ones((n, n)), -1), BF16)


def _attn_b(qb, kb, vb):
    b, s, w = qb.shape
    tq = ATTN_TQ
    return pl.pallas_call(
        _attn_b_kernel,
        grid=(b, w // LANES, s // tq),
        in_specs=[
            pl.BlockSpec((1, tq, LANES), lambda bi, h, qi: (bi, qi, h)),
            pl.BlockSpec((1, s, LANES), lambda bi, h, qi: (bi, 0, h)),
            pl.BlockSpec((1, s, LANES), lambda bi, h, qi: (bi, 0, h)),
            pl.BlockSpec((ATTN_TK, ATTN_TK), lambda bi, h, qi: (0, 0)),
        ],
        out_specs=pl.BlockSpec((1, tq, LANES), lambda bi, h, qi: (bi, qi, h)),
        out_shape=jax.ShapeDtypeStruct((b, s, w), BF16),
        scratch_shapes=[
            pltpu.VMEM((s, LANES), BF16), pltpu.VMEM((s, LANES), BF16),
            pltpu.VMEM((tq, 1), F32), pltpu.VMEM((tq, LANES), F32),
        ],
        compiler_params=_cparams(("parallel", "parallel", "arbitrary")),
        name="mixer_b",
    )(qb, kb, vb, _strict_lower(ATTN_TK))


def _attn_c_kernel(q_ref, k_ref, v_ref, bias_ref, o_ref, kbf, vbf):
    qi = pl.program_id(2)
    seq = k_ref.shape[1]
    tq = q_ref.shape[1]
    pad = C_PREV * CHUNK

    @pl.when(qi == 0)
    def _():
        kbf[0:pad, :] = jnp.zeros((pad, LANES), BF16)
        vbf[0:pad, :] = jnp.zeros((pad, LANES), BF16)
        _cast_rows(k_ref, kbf, seq, dst_off=pad)
        _cast_rows(v_ref, vbf, seq, dst_off=pad)

    q = q_ref[0]
    lane = lax.broadcasted_iota(jnp.int32, q.shape, 1)
    wstart = pl.multiple_of(qi * tq, tq)
    kw = kbf[pl.ds(wstart, BAND_W), :]
    vw = vbf[pl.ds(wstart, BAND_W), :]
    col = lax.broadcasted_iota(jnp.int32, (tq, BAND_W), 1)
    in_seq = col >= pad - qi * tq
    outs = []
    for hh in range(LANES // HEAD_DIM):
        qm = _keep_lanes(q, hh * HEAD_DIM, (hh + 1) * HEAD_DIM)
        s = jnp.where(in_seq, _dot_nt(qm, kw) + bias_ref[hh], NEG)
        m = jnp.max(s, axis=1, keepdims=True)
        p = jnp.exp(s - m)
        l = jnp.sum(p, axis=1, keepdims=True)
        outs.append(_dot(p.astype(BF16), vw) / l)
    o_ref[0] = jnp.where(lane < HEAD_DIM, outs[0], outs[1]).astype(o_ref.dtype)


def _attn_c(qc, kc, vc, bias):
    b, s, w = qc.shape
    tq = ATTN_TQ
    pad = C_PREV * CHUNK
    hp = LANES // HEAD_DIM
    return pl.pallas_call(
        _attn_c_kernel,
        grid=(b, w // LANES, s // tq),
        in_specs=[
            pl.BlockSpec((1, tq, LANES), lambda bi, h, qi: (bi, qi, h)),
            pl.BlockSpec((1, s, LANES), lambda bi, h, qi: (bi, 0, h)),
            pl.BlockSpec((1, s, LANES), lambda bi, h, qi: (bi, 0, h)),
            pl.BlockSpec((hp, tq, BAND_W), lambda bi, h, qi: (h, 0, 0)),
        ],
        out_specs=pl.BlockSpec((1, tq, LANES), lambda bi, h, qi: (bi, qi, h)),
        out_shape=jax.ShapeDtypeStruct((b, s, w), BF16),
        scratch_shapes=[pltpu.VMEM((s + pad, LANES), BF16), pltpu.VMEM((s + pad, LANES), BF16)],
        compiler_params=_cparams(("parallel", "parallel", "arbitrary")),
        name="mixer_c",
    )(qc, kc, vc, bias)


def _merge_kernel(x_ref, oa_ref, ob_ref, oc_ref, g_ref, wgate_ref, sub_ref,
                  wa_ref, wb_ref, wc_ref, wout_ref, o_ref):
    x = x_ref[...]
    d = x.shape[1]
    h = _rms_rows(x, g_ref[...]).astype(BF16)
    oa = oa_ref[...]
    parts = []
    for c in range(oa.shape[1] // LANES):
        oc_ = oa[:, c * LANES:(c + 1) * LANES]
        parts.append(oc_ * lax.rsqrt(jnp.mean(oc_ * oc_, axis=-1, keepdims=True) + EPS))
    oan = (jnp.concatenate(parts, axis=1) * sub_ref[...]).astype(BF16)
    merged = jax.nn.sigmoid(_dot(h, wgate_ref[:, 0:d])) * _dot(oan, wa_ref[...])
    merged += jax.nn.sigmoid(_dot(h, wgate_ref[:, d:2 * d])) * _dot(ob_ref[...], wb_ref[...])
    merged += jax.nn.sigmoid(_dot(h, wgate_ref[:, 2 * d:3 * d])) * _dot(oc_ref[...], wc_ref[...])
    o_ref[...] = x + _dot(merged.astype(BF16), wout_ref[...])


def _merge(x, oa, ob, oc, g, wgate, sub, wa, wb, wc, wout, tm):
    n, d = x.shape
    full = lambda a: pl.BlockSpec(a.shape, lambda i: (0,) * a.ndim)
    rows = lambda a: pl.BlockSpec((tm, a.shape[1]), lambda i: (i, 0))
    g = g.reshape(1, d)
    return pl.pallas_call(
        _merge_kernel,
        grid=(n // tm,),
        in_specs=[rows(x), rows(oa), rows(ob), rows(oc), full(g), full(wgate), full(sub),
                  full(wa), full(wb), full(wc), full(wout)],
        out_specs=rows(x),
        out_shape=jax.ShapeDtypeStruct((n, d), F32),
        compiler_params=_cparams(("parallel",)),
        name="merge",
    )(x, oa, ob, oc, g, wgate, sub, wa, wb, wc, wout)


def _cross_kernel(x_ref, g_ref, wq_ref, gain_ref, gmat_ref, mk_ref, mv_ref, wo_ref, o_ref):
    x = x_ref[0]
    h = _rms_rows(x, g_ref[...]).astype(BF16)
    q = (_group_rms(_dot(h, wq_ref[...]), gmat_ref[...]) * gain_ref[...]).astype(BF16)
    mk = mk_ref[0]
    mv = mv_ref[0]
    lane = lax.broadcasted_iota(jnp.int32, q.shape, 1)
    o = jnp.zeros(q.shape, F32)
    for hh in range(q.shape[1] // HEAD_DIM):
        in_head = (lane >= hh * HEAD_DIM) & (lane < (hh + 1) * HEAD_DIM)
        s = _dot_nt(_keep_lanes(q, hh * HEAD_DIM, (hh + 1) * HEAD_DIM), mk)
        p = jnp.exp(s - jnp.max(s, axis=1, keepdims=True))
        l = jnp.sum(p, axis=1, keepdims=True)
        o = jnp.where(in_head, _dot(p.astype(BF16), mv) / l, o)
    o_ref[0] = x + _dot(o.astype(BF16), wo_ref[...])


def _cross(x, g, wq, gain, mk, mv, wo, tm):
    b, s, d = x.shape
    full = lambda a: pl.BlockSpec(a.shape, lambda bi, i: (0,) * a.ndim)
    g = g.reshape(1, d)
    gmat = _group_mean_matrix()
    return pl.pallas_call(
        _cross_kernel,
        grid=(b, s // tm),
        in_specs=[
            pl.BlockSpec((1, tm, d), lambda bi, i: (bi, i, 0)),
            full(g), full(wq), full(gain), full(gmat),
            pl.BlockSpec((1,) + mk.shape[1:], lambda bi, i: (bi, 0, 0)),
            pl.BlockSpec((1,) + mv.shape[1:], lambda bi, i: (bi, 0, 0)),
            full(wo),
        ],
        out_specs=pl.BlockSpec((1, tm, d), lambda bi, i: (bi, i, 0)),
        out_shape=jax.ShapeDtypeStruct((b, s, d), F32),
        compiler_params=_cparams(("parallel", "parallel")),
        name="cross",
    )(x, g, wq, gain, gmat, mk, mv, wo)


def _sample_attn_kernel(lam_ref,
                        qa_ref, kan_ref, van_ref, kac_ref, vac_ref, bac_ref, ban_ref,
                        qb_ref, kbn_ref, vbn_ref, kbc_ref, vbc_ref, ubig_ref, usmall_ref,
                        qc_ref, kcn_ref, vcn_ref, kcc_ref, vcc_ref, bcc_ref, bcn_ref,
                        oa_ref, ob_ref, oc_ref):
    lam = lam_ref[0]
    ns = qa_ref.shape[1]
    bf = lambda r: r[0].astype(BF16)

    def heads(q, width):
        for hh in range(LANES // width):
            yield hh, None, _keep_lanes(q, hh * width, (hh + 1) * width)

    def softmax2(s_c, s_n):
        m = jnp.maximum(jnp.max(s_c, axis=1, keepdims=True), jnp.max(s_n, axis=1, keepdims=True))
        p_c = jnp.exp(s_c - m)
        p_n = jnp.exp(s_n - m)
        inv = 1.0 / (jnp.sum(p_c, axis=1, keepdims=True) + jnp.sum(p_n, axis=1, keepdims=True))
        return p_c * inv, p_n * inv

    for h in range(qa_ref.shape[2] // LANES):
        sl = slice(h * LANES, (h + 1) * LANES)
        q = qa_ref[0, :, sl]
        k_c = kac_ref[0, :, sl].astype(BF16)
        k_n = kan_ref[0, :, sl].astype(BF16)
        maps = []
        for _, _, qm in heads(q, HEAD_DIM):
            maps.append(softmax2(_dot_nt(qm, k_c) + bac_ref[h], _dot_nt(qm, k_n) + ban_ref[h]))
        a_c = (maps[0][0] - lam * maps[1][0]).astype(BF16)
        a_n = (maps[0][1] - lam * maps[1][1]).astype(BF16)
        oa_ref[0, :, sl] = (_dot(a_c, vac_ref[0, :, sl].astype(BF16))
                            + _dot(a_n, van_ref[0, :, sl].astype(BF16)))

    past = kbc_ref.shape[1]
    tk = ubig_ref.shape[0]
    row = lax.broadcasted_iota(jnp.int32, (ns, ns), 0)
    col = lax.broadcasted_iota(jnp.int32, (ns, ns), 1)
    for pr in range(qb_ref.shape[2] // LANES):
        sl = slice(pr * LANES, (pr + 1) * LANES)
        q = qb_ref[0, :, sl]
        k_n = kbn_ref[0, :, sl].astype(BF16)
        v_n = vbn_ref[0, :, sl].astype(BF16)
        outs = []
        for _, _, qm in heads(q, HEAD_DIM):
            w, carry = _stick_block(_dot_nt(qm, k_n), col < row, usmall_ref[...], 0.0)
            acc = _dot(w.astype(BF16), v_n)
            for j in range(past // tk - 1, -1, -1):
                k_c = kbc_ref[0, j * tk:(j + 1) * tk, sl].astype(BF16)
                v_c = vbc_ref[0, j * tk:(j + 1) * tk, sl].astype(BF16)
                w, rs = _stick_block(_dot_nt(qm, k_c), None, ubig_ref[...], carry)
                acc += _dot(w.astype(BF16), v_c)
                carry = carry + rs
            outs.append(acc)
        lane = lax.broadcasted_iota(jnp.int32, q.shape, 1)
        ob_ref[0, :, sl] = jnp.where(lane < HEAD_DIM, outs[0], outs[1]).astype(ob_ref.dtype)

    for pr in range(qc_ref.shape[2] // LANES):
        sl = slice(pr * LANES, (pr + 1) * LANES)
        q = qc_ref[0, :, sl]
        k_c = kcc_ref[0, :, sl].astype(BF16)
        k_n = kcn_ref[0, :, sl].astype(BF16)
        v_c = vcc_ref[0, :, sl].astype(BF16)
        v_n = vcn_ref[0, :, sl].astype(BF16)
        outs = []
        for hh, _, qm in heads(q, HEAD_DIM):
            hd = pr * (LANES // HEAD_DIM) + hh
            p_c, p_n = softmax2(_dot_nt(qm, k_c) + bcc_ref[hd], _dot_nt(qm, k_n) + bcn_ref[hd])
            outs.append(_dot(p_c.astype(BF16), v_c) + _dot(p_n.astype(BF16), v_n))
        lane = lax.broadcasted_iota(jnp.int32, q.shape, 1)
        oc_ref[0, :, sl] = jnp.where(lane < HEAD_DIM, outs[0], outs[1]).astype(oc_ref.dtype)


def _sample_attn(lam, qa, ka, va, cak, cav, bac, ban, qb, kb, vb, cbk, cbv,
                 qc, kc, vc, cck, ccv, bcc, bcn):
    b, ns, _ = qa.shape
    per_b = lambda a: pl.BlockSpec((1,) + a.shape[1:], lambda bi: (bi,) + (0,) * (a.ndim - 1))
    full = lambda a: pl.BlockSpec(a.shape, lambda bi: (0,) * a.ndim)
    ubig = _strict_lower(ATTN_TK)
    usmall = _strict_lower(ns)
    args = [qa, ka, va, cak, cav, bac, ban, qb, kb, vb, cbk, cbv, ubig, usmall,
            qc, kc, vc, cck, ccv, bcc, bcn]
    specs = [per_b(qa), per_b(ka), per_b(va), per_b(cak), per_b(cav), full(bac), full(ban),
             per_b(qb), per_b(kb), per_b(vb), per_b(cbk), per_b(cbv), full(ubig), full(usmall),
             per_b(qc), per_b(kc), per_b(vc), per_b(cck), per_b(ccv), full(bcc), full(bcn)]
    return pl.pallas_call(
        _sample_attn_kernel,
        grid=(b,),
        in_specs=[pl.BlockSpec(memory_space=pltpu.SMEM)] + specs,
        out_specs=[per_b(qa), per_b(qb), per_b(qc)],
        out_shape=[jax.ShapeDtypeStruct(qa.shape, F32), jax.ShapeDtypeStruct(qb.shape, BF16),
                   jax.ShapeDtypeStruct(qc.shape, BF16)],
        compiler_params=_cparams(("parallel",)),
        name="sample_mixers",
    )(lam, *args)


def _t5_bucket_np(rel):
    half = T5_BUCKETS // 2
    max_exact = half // 2
    n = np.abs(rel)
    nf = np.maximum(n, 1).astype(np.float64)
    large = max_exact + (np.log(nf / max_exact) / math.log(T5_MAX_DIST / max_exact)
                         * (half - max_exact)).astype(np.int64)
    large = np.minimum(large, half - 1)
    return np.where(rel > 0, half, 0) + np.where(n < max_exact, n, large)


def _toeplitz(lookup, n_rows, n_cols):
    period = n_rows + n_cols
    slot = np.arange(period)
    diff = np.minimum((slot + n_rows - 1) % period - (n_rows - 1), n_cols - 1)
    vec = lookup(diff).astype(F32)
    flat = jnp.tile(vec, (1, n_rows))[:, :n_rows * (period - 1)]
    return flat.reshape(vec.shape[0], n_rows, period - 1)[:, :, :n_cols]


def _t5_bias_table(t5_bias, qpos, kpos, key_major=False):
    t5_rows = lambda rel: t5_bias[_t5_bucket_np(rel)].T
    mask = (kpos[None, :] // CHUNK) <= (qpos[:, None] // CHUNK)
    if key_major:
        table = _toeplitz(lambda dd: t5_rows(kpos[0] - qpos[0] - dd), len(kpos), len(qpos))
        mask = mask.T
    else:
        table = _toeplitz(lambda dd: t5_rows(kpos[0] - qpos[0] + dd), len(qpos), len(kpos))
    return jnp.where(jnp.asarray(mask)[None], table, NEG)


def _band_bias_table(rel_table, qpos, kpos):
    lookup = lambda dd: rel_table[:, np.clip(kpos[0] - qpos[0] + dd, -REL_CLIP, REL_CLIP) + REL_CLIP]
    table = _toeplitz(lookup, len(qpos), len(kpos))
    qc = qpos[:, None] // CHUNK
    kc = kpos[None, :] // CHUNK
    mask = (kpos[None, :] >= 0) & (kc <= qc) & (kc >= qc - C_PREV)
    return jnp.where(jnp.asarray(mask)[None], table, NEG)


def _lambda_init(layer):
    return 0.8 - 0.6 * math.exp(-0.3 * layer)


def kernel(x_prompt, x_sample, mem_prompt, cache_a_k, cache_a_v, cache_b_k, cache_b_v, cache_c_k, cache_c_v, cache_mem_k, cache_mem_v, t5_bias, ffn1_norm, ffn1_wg, ffn1_wu, ffn1_wd, mix_norm, w_in, a_qnorm, a_knorm, a_lq1, a_lk1, a_lq2, a_lk2, a_subln, c_qnorm, c_knorm, c_rel_bias, w_br_a, w_br_b, w_br_c, w_out, x_norm, mem_norm, x_wq, x_wkv, x_qnorm, x_knorm, x_wo, ffn2_norm, ffn2_wg, ffn2_wu, ffn2_wd):
    bp, sp, d = x_prompt.shape
    bs, ns, _ = x_sample.shape
    depth = w_in.shape[0]
    past = cache_a_k.shape[2]
    w_buf = cache_c_k.shape[2]
    n_mem = mem_prompt.shape[1]
    wa = cache_a_k.shape[3] * cache_a_k.shape[4]
    wb = cache_b_k.shape[3] * cache_b_k.shape[4]
    wc = cache_c_k.shape[3] * cache_c_k.shape[4]
    wx = cache_mem_k.shape[3] * cache_mem_k.shape[4]
    h_a = cache_a_k.shape[3]
    n_qkv = 3 * (wa + wb + wc)
    w_keep = min(C_PREV * CHUNK, sp)
    scale = HEAD_DIM ** -0.5
    assert sp % ATTN_TQ == 0 and ATTN_TQ == ATTN_TK and ATTN_TQ == 4 * CHUNK and sp >= w_keep
    assert past % ATTN_TK == 0
    assert sp % MIXER_A_TILE == 0 and MIXER_A_TILE % CHUNK == 0 and T5_MAX_DIST <= MIXER_A_TILE

    tile = lambda g, reps: jnp.tile(g.astype(F32), reps)
    bf = lambda a: a.astype(BF16)

    loc_a = np.arange(MIXER_A_TILE)
    loc = np.arange(ATTN_TQ)
    a_bias = jnp.stack([_t5_bias_table(t5_bias, loc_a + MIXER_A_TILE, loc_a + MIXER_A_TILE, key_major=True),
                        _t5_bias_table(t5_bias, loc_a + MIXER_A_TILE, loc_a, key_major=True)], axis=1)
    a_far = t5_bias[T5_BUCKETS // 2 - 1].astype(F32)
    qpos_s = past + np.arange(ns)
    a_bias_sc = _t5_bias_table(t5_bias, qpos_s, np.arange(past))
    a_bias_sn = _t5_bias_table(t5_bias, qpos_s, qpos_s)
    kpos_sb = past - w_buf + np.arange(w_buf + ns)

    xp = x_prompt.reshape(bp * sp, d)
    xs = x_sample.reshape(bs * ns, d)
    mem = mem_prompt.reshape(bp * n_mem, d)
    outs = {k: [] for k in ("ak_p", "av_p", "bk_p", "bv_p", "ck_p", "cv_p", "mk_p", "mv_p",
                            "ak_s", "av_s", "bk_s", "bv_s", "ck_s", "cv_s")}
    qkv_segs = [(wa, True, True), (wa, True, True), (wa, False, False),
                (wb, False, True), (wb, False, False), (wb, False, False),
                (wc, True, True), (wc, True, True), (wc, False, False)]
    qkv_dtypes = [BF16, F32, F32, BF16, F32, F32, BF16, F32, F32]

    for i in range(depth):
        wg1, wu1, wd1 = bf(ffn1_wg[i]), bf(ffn1_wu[i]), bf(ffn1_wd[i])
        wg2, wu2, wd2 = bf(ffn2_wg[i]), bf(ffn2_wu[i]), bf(ffn2_wd[i])
        w_qkv, w_gate = bf(w_in[i][:, :n_qkv]), bf(w_in[i][:, n_qkv:])
        qkv_gain = jnp.concatenate([
            tile(a_qnorm[i], wa // HEAD_DIM) * scale, tile(a_knorm[i], wa // HEAD_DIM), jnp.ones((wa,), F32),
            jnp.full((wb,), scale, F32), jnp.ones((2 * wb,), F32),
            tile(c_qnorm[i], wc // HEAD_DIM) * scale, tile(c_knorm[i], wc // HEAD_DIM), jnp.ones((wc,), F32),
        ]).reshape(1, n_qkv)
        sub = (tile(a_subln[i], h_a) * (1.0 - _lambda_init(i))).reshape(1, wa)
        wbr = bf(w_br_a[i]), bf(w_br_b[i]), bf(w_br_c[i])
        wout = bf(w_out[i])
        wq, wo = bf(x_wq[i]), bf(x_wo[i])
        xq_gain = (tile(x_qnorm[i], wx // HEAD_DIM) * scale).reshape(1, wx)
        dot64 = lambda a, b: jnp.exp(jnp.sum(a.astype(F32) * b.astype(F32)))
        lam = (dot64(a_lq1[i], a_lk1[i]) - dot64(a_lq2[i], a_lk2[i]) + _lambda_init(i)).reshape(1)
        c_bias_p = _band_bias_table(c_rel_bias[i], C_PREV * CHUNK + loc, np.arange(BAND_W))
        c_bias_sc = _band_bias_table(c_rel_bias[i], qpos_s, kpos_sb[:w_buf])
        c_bias_sn = _band_bias_table(c_rel_bias[i], qpos_s, kpos_sb[w_buf:])

        xp = _ffn(xp, ffn1_norm[i], wg1, wu1, wd1, tm=1024)
        qa, ka, va, qb, kb, vb, qc, kc, vc = _proj(xp, mix_norm[i], w_qkv, qkv_gain, qkv_segs, qkv_dtypes, tm=512)
        r3 = lambda a: a.reshape(bp, sp, a.shape[-1])
        oa = _attn_a(lam, a_far, r3(qa), r3(ka), r3(va), a_bias)
        ob = _attn_b(r3(qb), r3(kb), r3(vb))
        oc = _attn_c(r3(qc), r3(kc), r3(vc), c_bias_p)
        xp = _merge(xp, oa.reshape(-1, wa), ob.reshape(-1, wb), oc.reshape(-1, wc), mix_norm[i], w_gate, sub,
                    *wbr, wout, tm=512)
        mk, mv = _proj(mem, mem_norm[i], bf(x_wkv[i]),
                       jnp.concatenate([tile(x_knorm[i], wx // HEAD_DIM), jnp.ones((wx,), F32)]).reshape(1, 2 * wx),
                       [(wx, True, True), (wx, False, False)], [F32, F32], tm=n_mem)
        mk3, mv3 = mk.reshape(bp, n_mem, wx), mv.reshape(bp, n_mem, wx)
        xp = _cross(xp.reshape(bp, sp, d), x_norm[i], wq, xq_gain, bf(mk3), bf(mv3), wo, tm=512).reshape(-1, d)
        xp = _ffn(xp, ffn2_norm[i], wg2, wu2, wd2, tm=1024)
        outs["ak_p"].append(ka.reshape(bp, sp, h_a, -1))
        outs["av_p"].append(va.reshape(bp, sp, h_a, -1))
        outs["bk_p"].append(kb.reshape(bp, sp, -1, HEAD_DIM))
        outs["bv_p"].append(vb.reshape(bp, sp, -1, HEAD_DIM))
        outs["ck_p"].append(r3(kc)[:, sp - w_keep:].reshape(bp, w_keep, -1, HEAD_DIM))
        outs["cv_p"].append(r3(vc)[:, sp - w_keep:].reshape(bp, w_keep, -1, HEAD_DIM))
        outs["mk_p"].append(mk3.reshape(bp, n_mem, -1, HEAD_DIM))
        outs["mv_p"].append(mv3.reshape(bp, n_mem, -1, HEAD_DIM))

        xs = _ffn(xs, ffn1_norm[i], wg1, wu1, wd1, tm=bs * ns)
        qa, ka, va, qb, kb, vb, qc, kc, vc = _proj(xs, mix_norm[i], w_qkv, qkv_gain, qkv_segs, qkv_dtypes, tm=bs * ns)
        s3 = lambda a: a.reshape(bs, ns, a.shape[-1])
        c3 = lambda a: a.reshape(bs, a.shape[1], -1)
        oa, ob, oc = _sample_attn(
            lam, s3(qa), s3(ka), s3(va), c3(cache_a_k[i]), c3(cache_a_v[i]), a_bias_sc, a_bias_sn,
            s3(qb), s3(kb), s3(vb), c3(cache_b_k[i]), c3(cache_b_v[i]),
            s3(qc), s3(kc), s3(vc), c3(cache_c_k[i]), c3(cache_c_v[i]), c_bias_sc, c_bias_sn)
        xs = _merge(xs, oa.reshape(-1, wa), ob.reshape(-1, wb), oc.reshape(-1, wc), mix_norm[i], w_gate, sub,
                    *wbr, wout, tm=bs * ns)
        xs = _cross(xs.reshape(bs, ns, d), x_norm[i], wq, xq_gain, bf(c3(cache_mem_k[i])), bf(c3(cache_mem_v[i])),
                    wo, tm=ns).reshape(-1, d)
        xs = _ffn(xs, ffn2_norm[i], wg2, wu2, wd2, tm=bs * ns)
        outs["ak_s"].append(ka.reshape(bs, ns, h_a, -1))
        outs["av_s"].append(va.reshape(bs, ns, h_a, -1))
        outs["bk_s"].append(kb.reshape(bs, ns, -1, HEAD_DIM))
        outs["bv_s"].append(vb.reshape(bs, ns, -1, HEAD_DIM))
        outs["ck_s"].append(jnp.concatenate([cache_c_k[i], kc.reshape(bs, ns, -1, HEAD_DIM)], axis=1)[:, ns:])
        outs["cv_s"].append(jnp.concatenate([cache_c_v[i], vc.reshape(bs, ns, -1, HEAD_DIM)], axis=1)[:, ns:])

    st = lambda k: jnp.stack(outs[k])
    return (xp.reshape(bp, sp, d), xs.reshape(bs, ns, d),
            st("ak_p"), st("av_p"), st("bk_p"), st("bv_p"), st("ck_p"), st("cv_p"), st("mk_p"), st("mv_p"),
            st("ak_s"), st("av_s"), st("bk_s"), st("bv_s"), st("ck_s"), st("cv_s"))
```

```python
import functools
import math

import numpy as np
import jax
import jax.numpy as jnp
from jax import lax
from jax.experimental import pallas as pl
from jax.experimental.pallas import tpu as pltpu

F32 = jnp.float32
BF16 = jnp.bfloat16

EPS = 1e-6
HEAD_DIM = 64
CHUNK = 64
C_PREV = 8
REL_CLIP = 128
T5_BUCKETS = 32
T5_MAX_DIST = 128
LANES = 128
BF16_SUBLANES = 16
NEG = -1e30
LOG2E = math.log2(math.e)
STICK_SKIP = -110.0
V7X_VMEM_LIMIT_BYTES = 56 * 1024 * 1024

MIXER_A_TILE = 512
ATTN_TQ = 256
ATTN_TK = 256
BAND_W = (C_PREV + 4) * CHUNK


def _cparams(sem):
    return pltpu.CompilerParams(dimension_semantics=sem, vmem_limit_bytes=V7X_VMEM_LIMIT_BYTES)


def _rms_rows(x, g):
    return x * lax.rsqrt(jnp.mean(x * x, axis=-1, keepdims=True) + EPS) * g


def _dot(a, b):
    return jnp.dot(a, b, preferred_element_type=F32)


def _dot_nt(a, b):
    return lax.dot_general(a, b, (((1,), (1,)), ((), ())), preferred_element_type=F32)


def _keep_lanes(q, lo, hi):
    lane = lax.broadcasted_iota(jnp.int32, q.shape, 1)
    return jnp.where((lane >= lo) & (lane < hi), q.astype(F32), 0.0).astype(BF16)


def _group_rms(y, gmat):
    parts = []
    for c in range(y.shape[1] // LANES):
        yc = y[:, c * LANES:(c + 1) * LANES]
        ms = _dot((yc * yc).astype(BF16), gmat)
        parts.append(yc * lax.rsqrt(ms + EPS))
    return parts[0] if len(parts) == 1 else jnp.concatenate(parts, axis=1)


def _group_mean_matrix():
    g = np.kron(np.eye(LANES // HEAD_DIM), np.ones((HEAD_DIM, HEAD_DIM))) / HEAD_DIM
    return jnp.asarray(g, BF16)


def _ffn_kernel(x_ref, g_ref, wg_ref, wu_ref, wd_ref, o_ref, h_scr, acc_scr):
    j = pl.program_id(1)

    @pl.when(j == 0)
    def _():
        h_scr[...] = _rms_rows(x_ref[...], g_ref[...]).astype(BF16)
        acc_scr[...] = jnp.zeros_like(acc_scr)

    h = h_scr[...]
    a = _dot(h, wg_ref[...])
    u = _dot(h, wu_ref[...])
    t = a * jax.nn.sigmoid(a) * u
    acc_scr[...] += _dot(t.astype(BF16), wd_ref[...])

    @pl.when(j == pl.num_programs(1) - 1)
    def _():
        o_ref[...] = x_ref[...] + 0.5 * acc_scr[...]


def _ffn(x, g, wg, wu, wd, tm, tf=512):
    n, d = x.shape
    dff = wg.shape[1]
    return pl.pallas_call(
        _ffn_kernel,
        grid=(n // tm, dff // tf),
        in_specs=[
            pl.BlockSpec((tm, d), lambda i, j: (i, 0)),
            pl.BlockSpec((1, d), lambda i, j: (0, 0)),
            pl.BlockSpec((d, tf), lambda i, j: (0, j)),
            pl.BlockSpec((d, tf), lambda i, j: (0, j)),
            pl.BlockSpec((tf, d), lambda i, j: (j, 0)),
        ],
        out_specs=pl.BlockSpec((tm, d), lambda i, j: (i, 0)),
        out_shape=jax.ShapeDtypeStruct((n, d), F32),
        scratch_shapes=[pltpu.VMEM((tm, d), BF16), pltpu.VMEM((tm, d), F32)],
        compiler_params=_cparams(("parallel", "arbitrary")),
        name="ffn",
    )(x, g.reshape(1, d), wg, wu, wd)


def _proj_kernel(x_ref, g_ref, w_ref, gain_ref, gmat_ref, *out_refs, segs):
    h = _rms_rows(x_ref[...], g_ref[...]).astype(BF16)
    off = 0
    for (width, normed, gained), o_ref in zip(segs, out_refs):
        y = _dot(h, w_ref[:, off:off + width])
        if normed:
            y = _group_rms(y, gmat_ref[...])
        if gained:
            y = y * gain_ref[:, off:off + width]
        o_ref[...] = y.astype(o_ref.dtype)
        off += width


def _proj(x, g, w, gain, segs, dtypes, tm):
    n, d = x.shape
    wtot = w.shape[1]
    widths = [s[0] for s in segs]
    return pl.pallas_call(
        functools.partial(_proj_kernel, segs=tuple(segs)),
        grid=(n // tm,),
        in_specs=[
            pl.BlockSpec((tm, d), lambda i: (i, 0)),
            pl.BlockSpec((1, d), lambda i: (0, 0)),
            pl.BlockSpec((d, wtot), lambda i: (0, 0)),
            pl.BlockSpec((1, wtot), lambda i: (0, 0)),
            pl.BlockSpec((LANES, LANES), lambda i: (0, 0)),
        ],
        out_specs=[pl.BlockSpec((tm, wd), lambda i: (i, 0)) for wd in widths],
        out_shape=[jax.ShapeDtypeStruct((n, wd), dt) for wd, dt in zip(widths, dtypes)],
        compiler_params=_cparams(("parallel",)),
        name="proj",
    )(x, g.reshape(1, d), w, gain, _group_mean_matrix())


def _softmax_step_km(s, vt, m_ref, acc_ref):
    m_old = m_ref[...]
    m_new = jnp.maximum(m_old, jnp.max(s, axis=0, keepdims=True))
    alpha = jnp.exp2(m_old - m_new)
    p = jnp.exp2(s - m_new)
    acc_ref[...] = alpha * acc_ref[...] + _dot(vt, p.astype(BF16))
    m_ref[...] = m_new


def _cast_rows(src_ref, dst_ref, rows, dst_off=0, step=512):
    def body(i, c):
        r = pl.multiple_of(i * step, step)
        dst_ref[pl.ds(dst_off + r, step), :] = src_ref[0, pl.ds(r, step), :].astype(BF16)
        return c
    lax.fori_loop(0, rows // step, body, 0)


def _attn_a_kernel(lam_ref, q_ref, k_ref, v_ref, bias_ref, o_ref,
                   kbf, vtb, s_scr, m1, a1, m2, a2):
    qi = pl.program_id(2)
    seq = kbf.shape[0]
    tq = q_ref.shape[1]
    tk = vtb.shape[2]
    dv = v_ref.shape[2]

    @pl.when(qi == 0)
    def _():
        _cast_rows(k_ref, kbf, seq)

        def vt_body(j, c):
            r = pl.multiple_of(j * tk, tk)
            vtb[j, 0:dv, :] = v_ref[0, pl.ds(r, tk), :].T.astype(BF16)
            vtb[j, dv:, :] = jnp.ones((vtb.shape[1] - dv, tk), BF16)
            return c
        lax.fori_loop(0, seq // tk, vt_body, 0)

    qt = q_ref[0].astype(F32).T
    sub = lax.broadcasted_iota(jnp.int32, qt.shape, 0)
    q1t = jnp.where(sub < HEAD_DIM, qt, 0.0).astype(BF16)
    q2t = jnp.where(sub >= HEAD_DIM, qt, 0.0).astype(BF16)
    states = ((m1, a1), (m2, a2))
    for m_ref, a_ref in states:
        m_ref[...] = jnp.full_like(m_ref, NEG)
        a_ref[...] = jnp.zeros_like(a_ref)

    n_blocks = qi + 1

    def scores(t, buf):
        kb = kbf[pl.ds(pl.multiple_of((qi - t) * tk, tk), tk), :]
        s_scr[buf, 0] = _dot(kb, q1t)
        s_scr[buf, 1] = _dot(kb, q2t)

    def update(t, buf):
        bias = bias_ref[0, jnp.minimum(t, bias_ref.shape[1] - 1)]
        vt = vtb[qi - t]
        for mi, (m_ref, a_ref) in enumerate(states):
            _softmax_step_km(s_scr[buf, mi] + bias, vt, m_ref, a_ref)

    odd = n_blocks & 1

    @pl.when(odd == 1)
    def _():
        scores(0, 0)
        update(0, 0)

    n_pairs = lax.shift_right_logical(n_blocks, 1)

    @pl.when(n_pairs > 0)
    def _():
        scores(odd, 0)

    def pair(p, c):
        t0 = odd + 2 * p
        scores(t0 + 1, 1)
        update(t0, 0)
        scores(jnp.minimum(t0 + 2, qi), 0)
        update(t0 + 1, 1)
        return c
    lax.fori_loop(0, n_pairs, pair, 0)

    out_t = (a1[0:dv, :] / a1[dv:dv + 1, :]
             - lam_ref[0] * (a2[0:dv, :] / a2[dv:dv + 1, :]))
    o_ref[0] = out_t.T


def _attn_a(lam, qa, ka, va, bias):
    b, s, w = qa.shape
    nh = w // LANES
    tq = tk = MIXER_A_TILE
    vrows = LANES + BF16_SUBLANES
    return pl.pallas_call(
        _attn_a_kernel,
        grid=(b, nh, s // tq),
        in_specs=[
            pl.BlockSpec(memory_space=pltpu.SMEM),
            pl.BlockSpec((1, tq, LANES), lambda bi, h, qi: (bi, qi, h)),
            pl.BlockSpec((1, s, LANES), lambda bi, h, qi: (bi, 0, h)),
            pl.BlockSpec((1, s, LANES), lambda bi, h, qi: (bi, 0, h)),
            pl.BlockSpec((1,) + bias.shape[1:], lambda bi, h, qi: (h, 0, 0, 0)),
        ],
        out_specs=pl.BlockSpec((1, tq, LANES), lambda bi, h, qi: (bi, qi, h)),
        out_shape=jax.ShapeDtypeStruct((b, s, w), F32),
        scratch_shapes=[
            pltpu.VMEM((s, LANES), BF16), pltpu.VMEM((s // tk, vrows, tk), BF16),
            pltpu.VMEM((2, 2, tk, tq), F32),
            pltpu.VMEM((1, tq), F32), pltpu.VMEM((vrows, tq), F32),
            pltpu.VMEM((1, tq), F32), pltpu.VMEM((vrows, tq), F32),
        ],
        compiler_params=_cparams(("parallel", "parallel", "arbitrary")),
        name="mixer_a",
    )(lam, qa, ka, va, bias)


def _stick_block(z, valid, umat, carry):
    sp = jnp.maximum(z, 0.0) + jnp.log(1.0 + jnp.exp(-jnp.abs(z)))
    log1m = -sp
    if valid is not None:
        log1m = jnp.where(valid, log1m, 0.0)
    hi = log1m.astype(BF16)
    lo = (log1m - hi.astype(F32)).astype(BF16)
    after = _dot(hi, umat) + _dot(lo, umat)
    w = jnp.exp(z - sp + after + carry)
    if valid is not None:
        w = jnp.where(valid, w, 0.0)
    return w, jnp.sum(log1m, axis=1, keepdims=True)


def _attn_b_kernel(q_ref, k_ref, v_ref, u_ref, o_ref, kbf, vbf, c_scr, acc_scr):
    qi = pl.program_id(2)
    seq = kbf.shape[0]
    tq = q_ref.shape[1]
    tk = ATTN_TK

    @pl.when(qi == 0)
    def _():
        _cast_rows(k_ref, kbf, seq)
        _cast_rows(v_ref, vbf, seq)

    q = q_ref[0]
    lane = lax.broadcasted_iota(jnp.int32, q.shape, 1)
    row = lax.broadcasted_iota(jnp.int32, (tq, tk), 0)
    col = lax.broadcasted_iota(jnp.int32, (tq, tk), 1)
    umat = u_ref[...]
    outs = []
    for hh in range(LANES // HEAD_DIM):
        qm = _keep_lanes(q, hh * HEAD_DIM, (hh + 1) * HEAD_DIM)

        kstart = pl.multiple_of(qi * tq, tq)
        z = _dot_nt(qm, kbf[pl.ds(kstart, tk), :])
        w, rs = _stick_block(z, col < row, umat, 0.0)
        acc_scr[...] = _dot(w.astype(BF16), vbf[pl.ds(kstart, tk), :])
        c_scr[...] = rs

        def cond(st):
            j, cmax = st
            return (j >= 0) & (cmax > STICK_SKIP)

        def body(st):
            j, _ = st
            ks = pl.multiple_of(j * tk, tk)
            zz = _dot_nt(qm, kbf[pl.ds(ks, tk), :])
            carry = c_scr[...]
            ww, rr = _stick_block(zz, None, umat, carry)
            acc_scr[...] += _dot(ww.astype(BF16), vbf[pl.ds(ks, tk), :])
            cnew = carry + rr
            c_scr[...] = cnew
            return j - 1, jnp.max(cnew)

        lax.while_loop(cond, body, (qi - 1, jnp.max(rs)))
        outs.append(acc_scr[...])
    o_ref[0] = jnp.where(lane < HEAD_DIM, outs[0], outs[1]).astype(o_ref.dtype)


def _strict_lower(n):
    return jnp.asarray(np.tril(np.ones((n, n)), -1), BF16)


def _attn_b(qb, kb, vb):
    b, s, w = qb.shape
    tq = ATTN_TQ
    return pl.pallas_call(
        _attn_b_kernel,
        grid=(b, w // LANES, s // tq),
        in_specs=[
            pl.BlockSpec((1, tq, LANES), lambda bi, h, qi: (bi, qi, h)),
            pl.BlockSpec((1, s, LANES), lambda bi, h, qi: (bi, 0, h)),
            pl.BlockSpec((1, s, LANES), lambda bi, h, qi: (bi, 0, h)),
            pl.BlockSpec((ATTN_TK, ATTN_TK), lambda bi, h, qi: (0, 0)),
        ],
        out_specs=pl.BlockSpec((1, tq, LANES), lambda bi, h, qi: (bi, qi, h)),
        out_shape=jax.ShapeDtypeStruct((b, s, w), BF16),
        scratch_shapes=[
            pltpu.VMEM((s, LANES), BF16), pltpu.VMEM((s, LANES), BF16),
            pltpu.VMEM((tq, 1), F32), pltpu.VMEM((tq, LANES), F32),
        ],
        compiler_params=_cparams(("parallel", "parallel", "arbitrary")),
        name="mixer_b",
    )(qb, kb, vb, _strict_lower(ATTN_TK))


def _attn_c_kernel(q_ref, k_ref, v_ref, bias_ref, o_ref, kbf, vbf):
    qi = pl.program_id(2)
    seq = k_ref.shape[1]
    tq = q_ref.shape[1]
    pad = C_PREV * CHUNK

    @pl.when(qi == 0)
    def _():
        kbf[0:pad, :] = jnp.zeros((pad, LANES), BF16)
        vbf[0:pad, :] = jnp.zeros((pad, LANES), BF16)
        _cast_rows(k_ref, kbf, seq, dst_off=pad)
        _cast_rows(v_ref, vbf, seq, dst_off=pad)

    q = q_ref[0]
    lane = lax.broadcasted_iota(jnp.int32, q.shape, 1)
    wstart = pl.multiple_of(qi * tq, tq)
    kw = kbf[pl.ds(wstart, BAND_W), :]
    vw = vbf[pl.ds(wstart, BAND_W), :]
    col = lax.broadcasted_iota(jnp.int32, (tq, BAND_W), 1)
    in_seq = col >= pad - qi * tq
    outs = []
    for hh in range(LANES // HEAD_DIM):
        qm = _keep_lanes(q, hh * HEAD_DIM, (hh + 1) * HEAD_DIM)
        s = jnp.where(in_seq, _dot_nt(qm, kw) + bias_ref[hh], NEG)
        m = jnp.max(s, axis=1, keepdims=True)
        p = jnp.exp(s - m)
        l = jnp.sum(p, axis=1, keepdims=True)
        outs.append(_dot(p.astype(BF16), vw) / l)
    o_ref[0] = jnp.where(lane < HEAD_DIM, outs[0], outs[1]).astype(o_ref.dtype)


def _attn_c(qc, kc, vc, bias):
    b, s, w = qc.shape
    tq = ATTN_TQ
    pad = C_PREV * CHUNK
    hp = LANES // HEAD_DIM
    return pl.pallas_call(
        _attn_c_kernel,
        grid=(b, w // LANES, s // tq),
        in_specs=[
            pl.BlockSpec((1, tq, LANES), lambda bi, h, qi: (bi, qi, h)),
            pl.BlockSpec((1, s, LANES), lambda bi, h, qi: (bi, 0, h)),
            pl.BlockSpec((1, s, LANES), lambda bi, h, qi: (bi, 0, h)),
            pl.BlockSpec((hp, tq, BAND_W), lambda bi, h, qi: (h, 0, 0)),
        ],
        out_specs=pl.BlockSpec((1, tq, LANES), lambda bi, h, qi: (bi, qi, h)),
        out_shape=jax.ShapeDtypeStruct((b, s, w), BF16),
        scratch_shapes=[pltpu.VMEM((s + pad, LANES), BF16), pltpu.VMEM((s + pad, LANES), BF16)],
        compiler_params=_cparams(("parallel", "parallel", "arbitrary")),
        name="mixer_c",
    )(qc, kc, vc, bias)


def _merge_kernel(x_ref, oa_ref, ob_ref, oc_ref, g_ref, wgate_ref, sub_ref,
                  wa_ref, wb_ref, wc_ref, wout_ref, o_ref):
    x = x_ref[...]
    d = x.shape[1]
    h = _rms_rows(x, g_ref[...]).astype(BF16)
    oa = oa_ref[...]
    parts = []
    for c in range(oa.shape[1] // LANES):
        oc_ = oa[:, c * LANES:(c + 1) * LANES]
        parts.append(oc_ * lax.rsqrt(jnp.mean(oc_ * oc_, axis=-1, keepdims=True) + EPS))
    oan = (jnp.concatenate(parts, axis=1) * sub_ref[...]).astype(BF16)
    merged = jax.nn.sigmoid(_dot(h, wgate_ref[:, 0:d])) * _dot(oan, wa_ref[...])
    merged += jax.nn.sigmoid(_dot(h, wgate_ref[:, d:2 * d])) * _dot(ob_ref[...], wb_ref[...])
    merged += jax.nn.sigmoid(_dot(h, wgate_ref[:, 2 * d:3 * d])) * _dot(oc_ref[...], wc_ref[...])
    o_ref[...] = x + _dot(merged.astype(BF16), wout_ref[...])


def _merge(x, oa, ob, oc, g, wgate, sub, wa, wb, wc, wout, tm):
    n, d = x.shape
    full = lambda a: pl.BlockSpec(a.shape, lambda i: (0,) * a.ndim)
    rows = lambda a: pl.BlockSpec((tm, a.shape[1]), lambda i: (i, 0))
    g = g.reshape(1, d)
    return pl.pallas_call(
        _merge_kernel,
        grid=(n // tm,),
        in_specs=[rows(x), rows(oa), rows(ob), rows(oc), full(g), full(wgate), full(sub),
                  full(wa), full(wb), full(wc), full(wout)],
        out_specs=rows(x),
        out_shape=jax.ShapeDtypeStruct((n, d), F32),
        compiler_params=_cparams(("parallel",)),
        name="merge",
    )(x, oa, ob, oc, g, wgate, sub, wa, wb, wc, wout)


def _cross_kernel(x_ref, g_ref, wq_ref, gain_ref, gmat_ref, mk_ref, mv_ref, wo_ref, o_ref):
    x = x_ref[0]
    h = _rms_rows(x, g_ref[...]).astype(BF16)
    q = (_group_rms(_dot(h, wq_ref[...]), gmat_ref[...]) * gain_ref[...]).astype(BF16)
    mk = mk_ref[0]
    mv = mv_ref[0]
    lane = lax.broadcasted_iota(jnp.int32, q.shape, 1)
    o = jnp.zeros(q.shape, F32)
    for hh in range(q.shape[1] // HEAD_DIM):
        in_head = (lane >= hh * HEAD_DIM) & (lane < (hh + 1) * HEAD_DIM)
        s = _dot_nt(_keep_lanes(q, hh * HEAD_DIM, (hh + 1) * HEAD_DIM), mk)
        p = jnp.exp(s - jnp.max(s, axis=1, keepdims=True))
        l = jnp.sum(p, axis=1, keepdims=True)
        o = jnp.where(in_head, _dot(p.astype(BF16), mv) / l, o)
    o_ref[0] = x + _dot(o.astype(BF16), wo_ref[...])


def _cross(x, g, wq, gain, mk, mv, wo, tm):
    b, s, d = x.shape
    full = lambda a: pl.BlockSpec(a.shape, lambda bi, i: (0,) * a.ndim)
    g = g.reshape(1, d)
    gmat = _group_mean_matrix()
    return pl.pallas_call(
        _cross_kernel,
        grid=(b, s // tm),
        in_specs=[
            pl.BlockSpec((1, tm, d), lambda bi, i: (bi, i, 0)),
            full(g), full(wq), full(gain), full(gmat),
            pl.BlockSpec((1,) + mk.shape[1:], lambda bi, i: (bi, 0, 0)),
            pl.BlockSpec((1,) + mv.shape[1:], lambda bi, i: (bi, 0, 0)),
            full(wo),
        ],
        out_specs=pl.BlockSpec((1, tm, d), lambda bi, i: (bi, i, 0)),
        out_shape=jax.ShapeDtypeStruct((b, s, d), F32),
        compiler_params=_cparams(("parallel", "parallel")),
        name="cross",
    )(x, g, wq, gain, gmat, mk, mv, wo)


def _sample_attn_kernel(lam_ref,
                        qa_ref, kan_ref, van_ref, kac_ref, vac_ref, bac_ref, ban_ref,
                        qb_ref, kbn_ref, vbn_ref, kbc_ref, vbc_ref, ubig_ref, usmall_ref,
                        qc_ref, kcn_ref, vcn_ref, kcc_ref, vcc_ref, bcc_ref, bcn_ref,
                        oa_ref, ob_ref, oc_ref):
    lam = lam_ref[0]
    ns = qa_ref.shape[1]
    bf = lambda r: r[0].astype(BF16)

    def heads(q, width):
        for hh in range(LANES // width):
            yield hh, None, _keep_lanes(q, hh * width, (hh + 1) * width)

    def softmax2(s_c, s_n, exp=jnp.exp):
        m = jnp.maximum(jnp.max(s_c, axis=1, keepdims=True), jnp.max(s_n, axis=1, keepdims=True))
        p_c = exp(s_c - m)
        p_n = exp(s_n - m)
        inv = 1.0 / (jnp.sum(p_c, axis=1, keepdims=True) + jnp.sum(p_n, axis=1, keepdims=True))
        return p_c * inv, p_n * inv

    for h in range(qa_ref.shape[2] // LANES):
        sl = slice(h * LANES, (h + 1) * LANES)
        q = qa_ref[0, :, sl]
        k_c = kac_ref[0, :, sl].astype(BF16)
        k_n = kan_ref[0, :, sl].astype(BF16)
        maps = []
        for _, _, qm in heads(q, HEAD_DIM):
            maps.append(softmax2(_dot_nt(qm, k_c) + bac_ref[h], _dot_nt(qm, k_n) + ban_ref[h], exp=jnp.exp2))
        a_c = (maps[0][0] - lam * maps[1][0]).astype(BF16)
        a_n = (maps[0][1] - lam * maps[1][1]).astype(BF16)
        oa_ref[0, :, sl] = (_dot(a_c, vac_ref[0, :, sl].astype(BF16))
                            + _dot(a_n, van_ref[0, :, sl].astype(BF16)))

    past = kbc_ref.shape[1]
    tk = ubig_ref.shape[0]
    row = lax.broadcasted_iota(jnp.int32, (ns, ns), 0)
    col = lax.broadcasted_iota(jnp.int32, (ns, ns), 1)
    for pr in range(qb_ref.shape[2] // LANES):
        sl = slice(pr * LANES, (pr + 1) * LANES)
        q = qb_ref[0, :, sl]
        k_n = kbn_ref[0, :, sl].astype(BF16)
        v_n = vbn_ref[0, :, sl].astype(BF16)
        outs = []
        for _, _, qm in heads(q, HEAD_DIM):
            w, carry = _stick_block(_dot_nt(qm, k_n), col < row, usmall_ref[...], 0.0)
            acc = _dot(w.astype(BF16), v_n)
            for j in range(past // tk - 1, -1, -1):
                k_c = kbc_ref[0, j * tk:(j + 1) * tk, sl].astype(BF16)
                v_c = vbc_ref[0, j * tk:(j + 1) * tk, sl].astype(BF16)
                w, rs = _stick_block(_dot_nt(qm, k_c), None, ubig_ref[...], carry)
                acc += _dot(w.astype(BF16), v_c)
                carry = carry + rs
            outs.append(acc)
        lane = lax.broadcasted_iota(jnp.int32, q.shape, 1)
        ob_ref[0, :, sl] = jnp.where(lane < HEAD_DIM, outs[0], outs[1]).astype(ob_ref.dtype)

    for pr in range(qc_ref.shape[2] // LANES):
        sl = slice(pr * LANES, (pr + 1) * LANES)
        q = qc_ref[0, :, sl]
        k_c = kcc_ref[0, :, sl].astype(BF16)
        k_n = kcn_ref[0, :, sl].astype(BF16)
        v_c = vcc_ref[0, :, sl].astype(BF16)
        v_n = vcn_ref[0, :, sl].astype(BF16)
        outs = []
        for hh, _, qm in heads(q, HEAD_DIM):
            hd = pr * (LANES // HEAD_DIM) + hh
            p_c, p_n = softmax2(_dot_nt(qm, k_c) + bcc_ref[hd], _dot_nt(qm, k_n) + bcn_ref[hd])
            outs.append(_dot(p_c.astype(BF16), v_c) + _dot(p_n.astype(BF16), v_n))
        lane = lax.broadcasted_iota(jnp.int32, q.shape, 1)
        oc_ref[0, :, sl] = jnp.where(lane < HEAD_DIM, outs[0], outs[1]).astype(oc_ref.dtype)


def _sample_attn(lam, qa, ka, va, cak, cav, bac, ban, qb, kb, vb, cbk, cbv,
                 qc, kc, vc, cck, ccv, bcc, bcn):
    b, ns, _ = qa.shape
    per_b = lambda a: pl.BlockSpec((1,) + a.shape[1:], lambda bi: (bi,) + (0,) * (a.ndim - 1))
    full = lambda a: pl.BlockSpec(a.shape, lambda bi: (0,) * a.ndim)
    ubig = _strict_lower(ATTN_TK)
    usmall = _strict_lower(ns)
    args = [qa, ka, va, cak, cav, bac, ban, qb, kb, vb, cbk, cbv, ubig, usmall,
            qc, kc, vc, cck, ccv, bcc, bcn]
    specs = [per_b(qa), per_b(ka), per_b(va), per_b(cak), per_b(cav), full(bac), full(ban),
             per_b(qb), per_b(kb), per_b(vb), per_b(cbk), per_b(cbv), full(ubig), full(usmall),
             per_b(qc), per_b(kc), per_b(vc), per_b(cck), per_b(ccv), full(bcc), full(bcn)]
    return pl.pallas_call(
        _sample_attn_kernel,
        grid=(b,),
        in_specs=[pl.BlockSpec(memory_space=pltpu.SMEM)] + specs,
        out_specs=[per_b(qa), per_b(qb), per_b(qc)],
        out_shape=[jax.ShapeDtypeStruct(qa.shape, F32), jax.ShapeDtypeStruct(qb.shape, BF16),
                   jax.ShapeDtypeStruct(qc.shape, BF16)],
        compiler_params=_cparams(("parallel",)),
        name="sample_mixers",
    )(lam, *args)


def _t5_bucket_np(rel):
    half = T5_BUCKETS // 2
    max_exact = half // 2
    n = np.abs(rel)
    nf = np.maximum(n, 1).astype(np.float64)
    large = max_exact + (np.log(nf / max_exact) / math.log(T5_MAX_DIST / max_exact)
                         * (half - max_exact)).astype(np.int64)
    large = np.minimum(large, half - 1)
    return np.where(rel > 0, half, 0) + np.where(n < max_exact, n, large)


def _toeplitz(lookup, n_rows, n_cols):
    period = n_rows + n_cols
    slot = np.arange(period)
    diff = np.minimum((slot + n_rows - 1) % period - (n_rows - 1), n_cols - 1)
    vec = lookup(diff).astype(F32)
    flat = jnp.tile(vec, (1, n_rows))[:, :n_rows * (period - 1)]
    return flat.reshape(vec.shape[0], n_rows, period - 1)[:, :, :n_cols]


def _t5_bias_table(t5_bias, qpos, kpos, key_major=False):
    t5_rows = lambda rel: t5_bias[_t5_bucket_np(rel)].T
    mask = (kpos[None, :] // CHUNK) <= (qpos[:, None] // CHUNK)
    if key_major:
        table = _toeplitz(lambda dd: t5_rows(kpos[0] - qpos[0] - dd), len(kpos), len(qpos))
        mask = mask.T
    else:
        table = _toeplitz(lambda dd: t5_rows(kpos[0] - qpos[0] + dd), len(qpos), len(kpos))
    return jnp.where(jnp.asarray(mask)[None], table, NEG)


def _band_bias_table(rel_table, qpos, kpos):
    lookup = lambda dd: rel_table[:, np.clip(kpos[0] - qpos[0] + dd, -REL_CLIP, REL_CLIP) + REL_CLIP]
    table = _toeplitz(lookup, len(qpos), len(kpos))
    qc = qpos[:, None] // CHUNK
    kc = kpos[None, :] // CHUNK
    mask = (kpos[None, :] >= 0) & (kc <= qc) & (kc >= qc - C_PREV)
    return jnp.where(jnp.asarray(mask)[None], table, NEG)


def _lambda_init(layer):
    return 0.8 - 0.6 * math.exp(-0.3 * layer)


def kernel(x_prompt, x_sample, mem_prompt, cache_a_k, cache_a_v, cache_b_k, cache_b_v, cache_c_k, cache_c_v, cache_mem_k, cache_mem_v, t5_bias, ffn1_norm, ffn1_wg, ffn1_wu, ffn1_wd, mix_norm, w_in, a_qnorm, a_knorm, a_lq1, a_lk1, a_lq2, a_lk2, a_subln, c_qnorm, c_knorm, c_rel_bias, w_br_a, w_br_b, w_br_c, w_out, x_norm, mem_norm, x_wq, x_wkv, x_qnorm, x_knorm, x_wo, ffn2_norm, ffn2_wg, ffn2_wu, ffn2_wd):
    bp, sp, d = x_prompt.shape
    bs, ns, _ = x_sample.shape
    depth = w_in.shape[0]
    past = cache_a_k.shape[2]
    w_buf = cache_c_k.shape[2]
    n_mem = mem_prompt.shape[1]
    wa = cache_a_k.shape[3] * cache_a_k.shape[4]
    wb = cache_b_k.shape[3] * cache_b_k.shape[4]
    wc = cache_c_k.shape[3] * cache_c_k.shape[4]
    wx = cache_mem_k.shape[3] * cache_mem_k.shape[4]
    h_a = cache_a_k.shape[3]
    n_qkv = 3 * (wa + wb + wc)
    w_keep = min(C_PREV * CHUNK, sp)
    scale = HEAD_DIM ** -0.5
    assert sp % ATTN_TQ == 0 and ATTN_TQ == ATTN_TK and ATTN_TQ == 4 * CHUNK and sp >= w_keep
    assert past % ATTN_TK == 0
    assert sp % MIXER_A_TILE == 0 and MIXER_A_TILE % CHUNK == 0 and T5_MAX_DIST <= MIXER_A_TILE

    tile = lambda g, reps: jnp.tile(g.astype(F32), reps)
    bf = lambda a: a.astype(BF16)

    loc_a = np.arange(MIXER_A_TILE)
    loc = np.arange(ATTN_TQ)
    a_far = t5_bias[T5_BUCKETS // 2 - 1].astype(F32)
    a_bias = LOG2E * jnp.stack([
        _t5_bias_table(t5_bias, loc_a + MIXER_A_TILE, loc_a + MIXER_A_TILE, key_major=True),
        _t5_bias_table(t5_bias, loc_a + MIXER_A_TILE, loc_a, key_major=True),
        jnp.broadcast_to(a_far[:, None, None], (h_a, MIXER_A_TILE, MIXER_A_TILE))], axis=1)
    qpos_s = past + np.arange(ns)
    a_bias_sc = LOG2E * _t5_bias_table(t5_bias, qpos_s, np.arange(past))
    a_bias_sn = LOG2E * _t5_bias_table(t5_bias, qpos_s, qpos_s)
    kpos_sb = past - w_buf + np.arange(w_buf + ns)

    xp = x_prompt.reshape(bp * sp, d)
    xs = x_sample.reshape(bs * ns, d)
    mem = mem_prompt.reshape(bp * n_mem, d)
    outs = {k: [] for k in ("ak_p", "av_p", "bk_p", "bv_p", "ck_p", "cv_p", "mk_p", "mv_p",
                            "ak_s", "av_s", "bk_s", "bv_s", "ck_s", "cv_s")}
    qkv_segs = [(wa, True, True), (wa, True, True), (wa, False, False),
                (wb, False, True), (wb, False, False), (wb, False, False),
                (wc, True, True), (wc, True, True), (wc, False, False)]
    qkv_dtypes = [BF16, F32, F32, BF16, F32, F32, BF16, F32, F32]

    for i in range(depth):
        wg1, wu1, wd1 = bf(ffn1_wg[i]), bf(ffn1_wu[i]), bf(ffn1_wd[i])
        wg2, wu2, wd2 = bf(ffn2_wg[i]), bf(ffn2_wu[i]), bf(ffn2_wd[i])
        w_qkv, w_gate = bf(w_in[i][:, :n_qkv]), bf(w_in[i][:, n_qkv:])
        qkv_gain = jnp.concatenate([
            tile(a_qnorm[i], wa // HEAD_DIM) * (scale * LOG2E), tile(a_knorm[i], wa // HEAD_DIM), jnp.ones((wa,), F32),
            jnp.full((wb,), scale, F32), jnp.ones((2 * wb,), F32),
            tile(c_qnorm[i], wc // HEAD_DIM) * scale, tile(c_knorm[i], wc // HEAD_DIM), jnp.ones((wc,), F32),
        ]).reshape(1, n_qkv)
        sub = (tile(a_subln[i], h_a) * (1.0 - _lambda_init(i))).reshape(1, wa)
        wbr = bf(w_br_a[i]), bf(w_br_b[i]), bf(w_br_c[i])
        wout = bf(w_out[i])
        wq, wo = bf(x_wq[i]), bf(x_wo[i])
        xq_gain = (tile(x_qnorm[i], wx // HEAD_DIM) * scale).reshape(1, wx)
        dot64 = lambda a, b: jnp.exp(jnp.sum(a.astype(F32) * b.astype(F32)))
        lam = (dot64(a_lq1[i], a_lk1[i]) - dot64(a_lq2[i], a_lk2[i]) + _lambda_init(i)).reshape(1)
        c_bias_p = _band_bias_table(c_rel_bias[i], C_PREV * CHUNK + loc, np.arange(BAND_W))
        c_bias_sc = _band_bias_table(c_rel_bias[i], qpos_s, kpos_sb[:w_buf])
        c_bias_sn = _band_bias_table(c_rel_bias[i], qpos_s, kpos_sb[w_buf:])

        xp = _ffn(xp, ffn1_norm[i], wg1, wu1, wd1, tm=1024)
        qa, ka, va, qb, kb, vb, qc, kc, vc = _proj(xp, mix_norm[i], w_qkv, qkv_gain, qkv_segs, qkv_dtypes, tm=512)
        r3 = lambda a: a.reshape(bp, sp, a.shape[-1])
        oa = _attn_a(lam, r3(qa), r3(ka), r3(va), a_bias)
        ob = _attn_b(r3(qb), r3(kb), r3(vb))
        oc = _attn_c(r3(qc), r3(kc), r3(vc), c_bias_p)
        xp = _merge(xp, oa.reshape(-1, wa), ob.reshape(-1, wb), oc.reshape(-1, wc), mix_norm[i], w_gate, sub,
                    *wbr, wout, tm=512)
        mk, mv = _proj(mem, mem_norm[i], bf(x_wkv[i]),
                       jnp.concatenate([tile(x_knorm[i], wx // HEAD_DIM), jnp.ones((wx,), F32)]).reshape(1, 2 * wx),
                       [(wx, True, True), (wx, False, False)], [F32, F32], tm=n_mem)
        mk3, mv3 = mk.reshape(bp, n_mem, wx), mv.reshape(bp, n_mem, wx)
        xp = _cross(xp.reshape(bp, sp, d), x_norm[i], wq, xq_gain, bf(mk3), bf(mv3), wo, tm=512).reshape(-1, d)
        xp = _ffn(xp, ffn2_norm[i], wg2, wu2, wd2, tm=1024)
        outs["ak_p"].append(ka.reshape(bp, sp, h_a, -1))
        outs["av_p"].append(va.reshape(bp, sp, h_a, -1))
        outs["bk_p"].append(kb.reshape(bp, sp, -1, HEAD_DIM))
        outs["bv_p"].append(vb.reshape(bp, sp, -1, HEAD_DIM))
        outs["ck_p"].append(r3(kc)[:, sp - w_keep:].reshape(bp, w_keep, -1, HEAD_DIM))
        outs["cv_p"].append(r3(vc)[:, sp - w_keep:].reshape(bp, w_keep, -1, HEAD_DIM))
        outs["mk_p"].append(mk3.reshape(bp, n_mem, -1, HEAD_DIM))
        outs["mv_p"].append(mv3.reshape(bp, n_mem, -1, HEAD_DIM))

        xs = _ffn(xs, ffn1_norm[i], wg1, wu1, wd1, tm=bs * ns)
        qa, ka, va, qb, kb, vb, qc, kc, vc = _proj(xs, mix_norm[i], w_qkv, qkv_gain, qkv_segs, qkv_dtypes, tm=bs * ns)
        s3 = lambda a: a.reshape(bs, ns, a.shape[-1])
        c3 = lambda a: a.reshape(bs, a.shape[1], -1)
        oa, ob, oc = _sample_attn(
            lam, s3(qa), s3(ka), s3(va), c3(cache_a_k[i]), c3(cache_a_v[i]), a_bias_sc, a_bias_sn,
            s3(qb), s3(kb), s3(vb), c3(cache_b_k[i]), c3(cache_b_v[i]),
            s3(qc), s3(kc), s3(vc), c3(cache_c_k[i]), c3(cache_c_v[i]), c_bias_sc, c_bias_sn)
        xs = _merge(xs, oa.reshape(-1, wa), ob.reshape(-1, wb), oc.reshape(-1, wc), mix_norm[i], w_gate, sub,
                    *wbr, wout, tm=bs * ns)
        xs = _cross(xs.reshape(bs, ns, d), x_norm[i], wq, xq_gain, bf(c3(cache_mem_k[i])), bf(c3(cache_mem_v[i])),
                    wo, tm=ns).reshape(-1, d)
        xs = _ffn(xs, ffn2_norm[i], wg2, wu2, wd2, tm=bs * ns)
        outs["ak_s"].append(ka.reshape(bs, ns, h_a, -1))
        outs["av_s"].append(va.reshape(bs, ns, h_a, -1))
        outs["bk_s"].append(kb.reshape(bs, ns, -1, HEAD_DIM))
        outs["bv_s"].append(vb.reshape(bs, ns, -1, HEAD_DIM))
        outs["ck_s"].append(jnp.concatenate([cache_c_k[i], kc.reshape(bs, ns, -1, HEAD_DIM)], axis=1)[:, ns:])
        outs["cv_s"].append(jnp.concatenate([cache_c_v[i], vc.reshape(bs, ns, -1, HEAD_DIM)], axis=1)[:, ns:])

    st = lambda k: jnp.stack(outs[k])
    return (xp.reshape(bp, sp, d), xs.reshape(bs, ns, d),
            st("ak_p"), st("av_p"), st("bk_p"), st("bv_p"), st("ck_p"), st("cv_p"), st("mk_p"), st("mv_p"),
            st("ak_s"), st("av_s"), st("bk_s"), st("bv_s"), st("ck_s"), st("cv_s"))
```

```python
import functools
import math

import numpy as np
import jax
import jax.numpy as jnp
from jax import lax
from jax.experimental import pallas as pl
from jax.experimental.pallas import tpu as pltpu

F32 = jnp.float32
BF16 = jnp.bfloat16

EPS = 1e-6
HEAD_DIM = 64
CHUNK = 64
C_PREV = 8
REL_CLIP = 128
T5_BUCKETS = 32
T5_MAX_DIST = 128
LANES = 128
BF16_SUBLANES = 16
NEG = -1e30
LOG2E = math.log2(math.e)
BOUND_SLACK = 1.001
BOUNDED_EXP2_SPAN = 100.0
STICK_SKIP = -110.0
V7X_VMEM_LIMIT_BYTES = 56 * 1024 * 1024

MIXER_A_TILE = 512
ATTN_TQ = 256
ATTN_TK = 256
BAND_W = (C_PREV + 4) * CHUNK


def _cparams(sem):
    return pltpu.CompilerParams(dimension_semantics=sem, vmem_limit_bytes=V7X_VMEM_LIMIT_BYTES)


def _rms_rows(x, g):
    return x * lax.rsqrt(jnp.mean(x * x, axis=-1, keepdims=True) + EPS) * g


def _dot(a, b):
    return jnp.dot(a, b, preferred_element_type=F32)


def _dot_nt(a, b):
    return lax.dot_general(a, b, (((1,), (1,)), ((), ())), preferred_element_type=F32)


def _keep_lanes(q, lo, hi):
    lane = lax.broadcasted_iota(jnp.int32, q.shape, 1)
    return jnp.where((lane >= lo) & (lane < hi), q.astype(F32), 0.0).astype(BF16)


def _group_rms(y, gmat):
    parts = []
    for c in range(y.shape[1] // LANES):
        yc = y[:, c * LANES:(c + 1) * LANES]
        ms = _dot((yc * yc).astype(BF16), gmat)
        parts.append(yc * lax.rsqrt(ms + EPS))
    return parts[0] if len(parts) == 1 else jnp.concatenate(parts, axis=1)


def _group_mean_matrix():
    g = np.kron(np.eye(LANES // HEAD_DIM), np.ones((HEAD_DIM, HEAD_DIM))) / HEAD_DIM
    return jnp.asarray(g, BF16)


def _ffn_kernel(x_ref, g_ref, wg_ref, wu_ref, wd_ref, o_ref, h_scr, acc_scr):
    j = pl.program_id(1)

    @pl.when(j == 0)
    def _():
        h_scr[...] = _rms_rows(x_ref[...], g_ref[...]).astype(BF16)
        acc_scr[...] = jnp.zeros_like(acc_scr)

    h = h_scr[...]
    a = _dot(h, wg_ref[...])
    u = _dot(h, wu_ref[...])
    t = a * jax.nn.sigmoid(a) * u
    acc_scr[...] += _dot(t.astype(BF16), wd_ref[...])

    @pl.when(j == pl.num_programs(1) - 1)
    def _():
        o_ref[...] = x_ref[...] + 0.5 * acc_scr[...]


def _ffn(x, g, wg, wu, wd, tm, tf=512):
    n, d = x.shape
    dff = wg.shape[1]
    return pl.pallas_call(
        _ffn_kernel,
        grid=(n // tm, dff // tf),
        in_specs=[
            pl.BlockSpec((tm, d), lambda i, j: (i, 0)),
            pl.BlockSpec((1, d), lambda i, j: (0, 0)),
            pl.BlockSpec((d, tf), lambda i, j: (0, j)),
            pl.BlockSpec((d, tf), lambda i, j: (0, j)),
            pl.BlockSpec((tf, d), lambda i, j: (j, 0)),
        ],
        out_specs=pl.BlockSpec((tm, d), lambda i, j: (i, 0)),
        out_shape=jax.ShapeDtypeStruct((n, d), F32),
        scratch_shapes=[pltpu.VMEM((tm, d), BF16), pltpu.VMEM((tm, d), F32)],
        compiler_params=_cparams(("parallel", "arbitrary")),
        name="ffn",
    )(x, g.reshape(1, d), wg, wu, wd)


def _proj_kernel(x_ref, g_ref, w_ref, gain_ref, gmat_ref, *out_refs, segs):
    h = _rms_rows(x_ref[...], g_ref[...]).astype(BF16)
    off = 0
    for (width, normed, gained), o_ref in zip(segs, out_refs):
        y = _dot(h, w_ref[:, off:off + width])
        if normed:
            y = _group_rms(y, gmat_ref[...])
        if gained:
            y = y * gain_ref[:, off:off + width]
        o_ref[...] = y.astype(o_ref.dtype)
        off += width


def _proj(x, g, w, gain, segs, dtypes, tm):
    n, d = x.shape
    wtot = w.shape[1]
    widths = [s[0] for s in segs]
    return pl.pallas_call(
        functools.partial(_proj_kernel, segs=tuple(segs)),
        grid=(n // tm,),
        in_specs=[
            pl.BlockSpec((tm, d), lambda i: (i, 0)),
            pl.BlockSpec((1, d), lambda i: (0, 0)),
            pl.BlockSpec((d, wtot), lambda i: (0, 0)),
            pl.BlockSpec((1, wtot), lambda i: (0, 0)),
            pl.BlockSpec((LANES, LANES), lambda i: (0, 0)),
        ],
        out_specs=[pl.BlockSpec((tm, wd), lambda i: (i, 0)) for wd in widths],
        out_shape=[jax.ShapeDtypeStruct((n, wd), dt) for wd, dt in zip(widths, dtypes)],
        compiler_params=_cparams(("parallel",)),
        name="proj",
    )(x, g.reshape(1, d), w, gain, _group_mean_matrix())


def _softmax_step_km(s, vt, m_ref, acc_ref):
    m_old = m_ref[...]
    m_new = jnp.maximum(m_old, jnp.max(s, axis=0, keepdims=True))
    alpha = jnp.exp2(m_old - m_new)
    p = jnp.exp2(s - m_new)
    acc_ref[...] = alpha * acc_ref[...] + _dot(vt, p.astype(BF16))
    m_ref[...] = m_new


def _cast_rows(src_ref, dst_ref, rows, dst_off=0, step=512):
    def body(i, c):
        r = pl.multiple_of(i * step, step)
        dst_ref[pl.ds(dst_off + r, step), :] = src_ref[0, pl.ds(r, step), :].astype(BF16)
        return c
    lax.fori_loop(0, rows // step, body, 0)


def _attn_a_kernel(lam_ref, brange_ref, q_ref, k_ref, v_ref, bias_ref, o_ref,
                   kbf, vtb, s_scr, knorm, m1, a1, m2, a2):
    h = pl.program_id(1)
    qi = pl.program_id(2)
    seq = kbf.shape[0]
    tq = q_ref.shape[1]
    tk = vtb.shape[2]
    dv = v_ref.shape[2]

    @pl.when(qi == 0)
    def _():
        lane = lax.broadcasted_iota(jnp.int32, (tk, LANES), 1)

        def prep(j, c):
            r = pl.multiple_of(j * tk, tk)
            kb = k_ref[0, pl.ds(r, tk), :].astype(BF16)
            kbf[pl.ds(r, tk), :] = kb
            vtb[j, 0:dv, :] = v_ref[0, pl.ds(r, tk), :].T.astype(BF16)
            vtb[j, dv:, :] = jnp.ones((vtb.shape[1] - dv, tk), BF16)
            ksq = kb.astype(F32) * kb.astype(F32)
            n1 = jnp.max(jnp.sum(jnp.where(lane < HEAD_DIM, ksq, 0.0), axis=1, keepdims=True))
            n2 = jnp.max(jnp.sum(jnp.where(lane >= HEAD_DIM, ksq, 0.0), axis=1, keepdims=True))
            return jnp.maximum(c[0], n1), jnp.maximum(c[1], n2)
        n1, n2 = lax.fori_loop(0, seq // tk, prep, (jnp.float32(0.0), jnp.float32(0.0)))
        knorm[0] = n1
        knorm[1] = n2

    qt = q_ref[0].astype(F32).T
    sub = lax.broadcasted_iota(jnp.int32, qt.shape, 0)
    q1t = jnp.where(sub < HEAD_DIM, qt, 0.0).astype(BF16)
    q2t = jnp.where(sub >= HEAD_DIM, qt, 0.0).astype(BF16)
    states = ((m1, a1), (m2, a2))
    for m_ref, a_ref in states:
        a_ref[...] = jnp.zeros_like(a_ref)

    far_bias, bias_max, bias_min = brange_ref[3 * h], brange_ref[3 * h + 1], brange_ref[3 * h + 2]
    qsq = qt * qt
    bound1 = jnp.sqrt(jnp.sum(jnp.where(sub < HEAD_DIM, qsq, 0.0), axis=0, keepdims=True) * knorm[0])
    bound2 = jnp.sqrt(jnp.sum(jnp.where(sub >= HEAD_DIM, qsq, 0.0), axis=0, keepdims=True) * knorm[1])
    bounds = (bound1 * BOUND_SLACK + bias_max, bound2 * BOUND_SLACK + bias_max)
    spread = 2.0 * BOUND_SLACK * jnp.maximum(jnp.max(bound1), jnp.max(bound2)) + (bias_max - bias_min)
    bounded = spread <= BOUNDED_EXP2_SPAN

    @pl.when(bounded)
    def _():
        qmaps = ((q1t, bounds[0], a1), (q2t, bounds[1], a2))

        def weights(j, adds):
            kb = kbf[pl.ds(pl.multiple_of(j * tk, tk), tk), :]
            return [jnp.exp2(_dot(kb, qmt) + add).astype(BF16) for (qmt, _, _), add in zip(qmaps, adds)]

        def special(j, tile):
            ps = weights(j, [tile - shift for _, shift, _ in qmaps])
            for p, (_, _, a_ref) in zip(ps, qmaps):
                a_ref[...] += _dot(vtb[j], p)

        special(qi, bias_ref[0, 0])

        @pl.when(qi >= 1)
        def _():
            special(qi - 1, bias_ref[0, 1])

        far_adds = [far_bias - shift for _, shift, _ in qmaps]
        n_far = jnp.maximum(qi - 1, 0)
        odd = n_far & 1

        @pl.when(odd == 1)
        def _():
            for p, (_, _, a_ref) in zip(weights(0, far_adds), qmaps):
                a_ref[...] += _dot(vtb[0], p)

        def pair(i, c):
            j = odd + 2 * i
            pa = weights(j, far_adds)
            pb = weights(j + 1, far_adds)
            for p0, p1, (_, _, a_ref) in zip(pa, pb, qmaps):
                a_ref[...] += _dot(vtb[j], p0) + _dot(vtb[j + 1], p1)
            return c
        lax.fori_loop(0, lax.shift_right_logical(n_far, 1), pair, 0)

    @pl.when(jnp.logical_not(bounded))
    def _():
        for m_ref, _ in states:
            m_ref[...] = jnp.full_like(m_ref, NEG)
        n_blocks = qi + 1

        def scores(t, buf):
            kb = kbf[pl.ds(pl.multiple_of((qi - t) * tk, tk), tk), :]
            s_scr[buf, 0] = _dot(kb, q1t)
            s_scr[buf, 1] = _dot(kb, q2t)

        def update(t, buf):
            bias = bias_ref[0, jnp.minimum(t, bias_ref.shape[1] - 1)]
            vt = vtb[qi - t]
            for mi, (m_ref, a_ref) in enumerate(states):
                _softmax_step_km(s_scr[buf, mi] + bias, vt, m_ref, a_ref)

        odd = n_blocks & 1

        @pl.when(odd == 1)
        def _():
            scores(0, 0)
            update(0, 0)

        n_pairs = lax.shift_right_logical(n_blocks, 1)

        @pl.when(n_pairs > 0)
        def _():
            scores(odd, 0)

        def pair(p, c):
            t0 = odd + 2 * p
            scores(t0 + 1, 1)
            update(t0, 0)
            scores(jnp.minimum(t0 + 2, qi), 0)
            update(t0 + 1, 1)
            return c
        lax.fori_loop(0, n_pairs, pair, 0)

    out_t = (a1[0:dv, :] / a1[dv:dv + 1, :]
             - lam_ref[0] * (a2[0:dv, :] / a2[dv:dv + 1, :]))
    o_ref[0] = out_t.T


def _attn_a(lam, brange, qa, ka, va, bias):
    b, s, w = qa.shape
    nh = w // LANES
    tq = tk = MIXER_A_TILE
    vrows = LANES + BF16_SUBLANES
    return pl.pallas_call(
        _attn_a_kernel,
        grid=(b, nh, s // tq),
        in_specs=[
            pl.BlockSpec(memory_space=pltpu.SMEM),
            pl.BlockSpec(memory_space=pltpu.SMEM),
            pl.BlockSpec((1, tq, LANES), lambda bi, h, qi: (bi, qi, h)),
            pl.BlockSpec((1, s, LANES), lambda bi, h, qi: (bi, 0, h)),
            pl.BlockSpec((1, s, LANES), lambda bi, h, qi: (bi, 0, h)),
            pl.BlockSpec((1,) + bias.shape[1:], lambda bi, h, qi: (h, 0, 0, 0)),
        ],
        out_specs=pl.BlockSpec((1, tq, LANES), lambda bi, h, qi: (bi, qi, h)),
        out_shape=jax.ShapeDtypeStruct((b, s, w), F32),
        scratch_shapes=[
            pltpu.VMEM((s, LANES), BF16), pltpu.VMEM((s // tk, vrows, tk), BF16),
            pltpu.VMEM((2, 2, tk, tq), F32), pltpu.SMEM((2,), F32),
            pltpu.VMEM((1, tq), F32), pltpu.VMEM((vrows, tq), F32),
            pltpu.VMEM((1, tq), F32), pltpu.VMEM((vrows, tq), F32),
        ],
        compiler_params=_cparams(("parallel", "parallel", "arbitrary")),
        name="mixer_a",
    )(lam, brange, qa, ka, va, bias)


def _stick_block(z, valid, umat, carry):
    sp = jnp.maximum(z, 0.0) + jnp.log(1.0 + jnp.exp(-jnp.abs(z)))
    log1m = -sp
    if valid is not None:
        log1m = jnp.where(valid, log1m, 0.0)
    hi = log1m.astype(BF16)
    lo = (log1m - hi.astype(F32)).astype(BF16)
    after = _dot(hi, umat) + _dot(lo, umat)
    w = jnp.exp(z - sp + after + carry)
    if valid is not None:
        w = jnp.where(valid, w, 0.0)
    return w, jnp.sum(log1m, axis=1, keepdims=True)


def _attn_b_kernel(q_ref, k_ref, v_ref, u_ref, o_ref, kbf, vbf, c_scr, acc_scr):
    qi = pl.program_id(2)
    seq = kbf.shape[0]
    tq = q_ref.shape[1]
    tk = ATTN_TK

    @pl.when(qi == 0)
    def _():
        _cast_rows(k_ref, kbf, seq)
        _cast_rows(v_ref, vbf, seq)

    q = q_ref[0]
    lane = lax.broadcasted_iota(jnp.int32, q.shape, 1)
    row = lax.broadcasted_iota(jnp.int32, (tq, tk), 0)
    col = lax.broadcasted_iota(jnp.int32, (tq, tk), 1)
    umat = u_ref[...]
    outs = []
    for hh in range(LANES // HEAD_DIM):
        qm = _keep_lanes(q, hh * HEAD_DIM, (hh + 1) * HEAD_DIM)

        kstart = pl.multiple_of(qi * tq, tq)
        z = _dot_nt(qm, kbf[pl.ds(kstart, tk), :])
        w, rs = _stick_block(z, col < row, umat, 0.0)
        acc_scr[...] = _dot(w.astype(BF16), vbf[pl.ds(kstart, tk), :])
        c_scr[...] = rs

        def cond(st):
            j, cmax = st
            return (j >= 0) & (cmax > STICK_SKIP)

        def body(st):
            j, _ = st
            ks = pl.multiple_of(j * tk, tk)
            zz = _dot_nt(qm, kbf[pl.ds(ks, tk), :])
            carry = c_scr[...]
            ww, rr = _stick_block(zz, None, umat, carry)
            acc_scr[...] += _dot(ww.astype(BF16), vbf[pl.ds(ks, tk), :])
            cnew = carry + rr
            c_scr[...] = cnew
            return j - 1, jnp.max(cnew)

        lax.while_loop(cond, body, (qi - 1, jnp.max(rs)))
        outs.append(acc_scr[...])
    o_ref[0] = jnp.where(lane < HEAD_DIM, outs[0], outs[1]).astype(o_ref.dtype)


def _strict_lower(n):
    return jnp.asarray(np.tril(np.ones((n, n)), -1), BF16)


def _attn_b(qb, kb, vb):
    b, s, w = qb.shape
    tq = ATTN_TQ
    return pl.pallas_call(
        _attn_b_kernel,
        grid=(b, w // LANES, s // tq),
        in_specs=[
            pl.BlockSpec((1, tq, LANES), lambda bi, h, qi: (bi, qi, h)),
            pl.BlockSpec((1, s, LANES), lambda bi, h, qi: (bi, 0, h)),
            pl.BlockSpec((1, s, LANES), lambda bi, h, qi: (bi, 0, h)),
            pl.BlockSpec((ATTN_TK, ATTN_TK), lambda bi, h, qi: (0, 0)),
        ],
        out_specs=pl.BlockSpec((1, tq, LANES), lambda bi, h, qi: (bi, qi, h)),
        out_shape=jax.ShapeDtypeStruct((b, s, w), BF16),
        scratch_shapes=[
            pltpu.VMEM((s, LANES), BF16), pltpu.VMEM((s, LANES), BF16),
            pltpu.VMEM((tq, 1), F32), pltpu.VMEM((tq, LANES), F32),
        ],
        compiler_params=_cparams(("parallel", "parallel", "arbitrary")),
        name="mixer_b",
    )(qb, kb, vb, _strict_lower(ATTN_TK))


def _attn_c_kernel(q_ref, k_ref, v_ref, bias_ref, o_ref, kbf, vbf):
    qi = pl.program_id(2)
    seq = k_ref.shape[1]
    tq = q_ref.shape[1]
    pad = C_PREV * CHUNK

    @pl.when(qi == 0)
    def _():
        kbf[0:pad, :] = jnp.zeros((pad, LANES), BF16)
        vbf[0:pad, :] = jnp.zeros((pad, LANES), BF16)
        _cast_rows(k_ref, kbf, seq, dst_off=pad)
        _cast_rows(v_ref, vbf, seq, dst_off=pad)

    q = q_ref[0]
    lane = lax.broadcasted_iota(jnp.int32, q.shape, 1)
    wstart = pl.multiple_of(qi * tq, tq)
    kw = kbf[pl.ds(wstart, BAND_W), :]
    vw = vbf[pl.ds(wstart, BAND_W), :]
    col = lax.broadcasted_iota(jnp.int32, (tq, BAND_W), 1)
    in_seq = col >= pad - qi * tq
    outs = []
    for hh in range(LANES // HEAD_DIM):
        qm = _keep_lanes(q, hh * HEAD_DIM, (hh + 1) * HEAD_DIM)
        s = jnp.where(in_seq, _dot_nt(qm, kw) + bias_ref[hh], NEG)
        m = jnp.max(s, axis=1, keepdims=True)
        p = jnp.exp(s - m)
        l = jnp.sum(p, axis=1, keepdims=True)
        outs.append(_dot(p.astype(BF16), vw) / l)
    o_ref[0] = jnp.where(lane < HEAD_DIM, outs[0], outs[1]).astype(o_ref.dtype)


def _attn_c(qc, kc, vc, bias):
    b, s, w = qc.shape
    tq = ATTN_TQ
    pad = C_PREV * CHUNK
    hp = LANES // HEAD_DIM
    return pl.pallas_call(
        _attn_c_kernel,
        grid=(b, w // LANES, s // tq),
        in_specs=[
            pl.BlockSpec((1, tq, LANES), lambda bi, h, qi: (bi, qi, h)),
            pl.BlockSpec((1, s, LANES), lambda bi, h, qi: (bi, 0, h)),
            pl.BlockSpec((1, s, LANES), lambda bi, h, qi: (bi, 0, h)),
            pl.BlockSpec((hp, tq, BAND_W), lambda bi, h, qi: (h, 0, 0)),
        ],
        out_specs=pl.BlockSpec((1, tq, LANES), lambda bi, h, qi: (bi, qi, h)),
        out_shape=jax.ShapeDtypeStruct((b, s, w), BF16),
        scratch_shapes=[pltpu.VMEM((s + pad, LANES), BF16), pltpu.VMEM((s + pad, LANES), BF16)],
        compiler_params=_cparams(("parallel", "parallel", "arbitrary")),
        name="mixer_c",
    )(qc, kc, vc, bias)


def _merge_kernel(x_ref, oa_ref, ob_ref, oc_ref, g_ref, wgate_ref, sub_ref,
                  wa_ref, wb_ref, wc_ref, wout_ref, o_ref):
    x = x_ref[...]
    d = x.shape[1]
    h = _rms_rows(x, g_ref[...]).astype(BF16)
    oa = oa_ref[...]
    parts = []
    for c in range(oa.shape[1] // LANES):
        oc_ = oa[:, c * LANES:(c + 1) * LANES]
        parts.append(oc_ * lax.rsqrt(jnp.mean(oc_ * oc_, axis=-1, keepdims=True) + EPS))
    oan = (jnp.concatenate(parts, axis=1) * sub_ref[...]).astype(BF16)
    merged = jax.nn.sigmoid(_dot(h, wgate_ref[:, 0:d])) * _dot(oan, wa_ref[...])
    merged += jax.nn.sigmoid(_dot(h, wgate_ref[:, d:2 * d])) * _dot(ob_ref[...], wb_ref[...])
    merged += jax.nn.sigmoid(_dot(h, wgate_ref[:, 2 * d:3 * d])) * _dot(oc_ref[...], wc_ref[...])
    o_ref[...] = x + _dot(merged.astype(BF16), wout_ref[...])


def _merge(x, oa, ob, oc, g, wgate, sub, wa, wb, wc, wout, tm):
    n, d = x.shape
    full = lambda a: pl.BlockSpec(a.shape, lambda i: (0,) * a.ndim)
    rows = lambda a: pl.BlockSpec((tm, a.shape[1]), lambda i: (i, 0))
    g = g.reshape(1, d)
    return pl.pallas_call(
        _merge_kernel,
        grid=(n // tm,),
        in_specs=[rows(x), rows(oa), rows(ob), rows(oc), full(g), full(wgate), full(sub),
                  full(wa), full(wb), full(wc), full(wout)],
        out_specs=rows(x),
        out_shape=jax.ShapeDtypeStruct((n, d), F32),
        compiler_params=_cparams(("parallel",)),
        name="merge",
    )(x, oa, ob, oc, g, wgate, sub, wa, wb, wc, wout)


def _cross_kernel(x_ref, g_ref, wq_ref, gain_ref, gmat_ref, mk_ref, mv_ref, wo_ref, o_ref):
    x = x_ref[0]
    h = _rms_rows(x, g_ref[...]).astype(BF16)
    q = (_group_rms(_dot(h, wq_ref[...]), gmat_ref[...]) * gain_ref[...]).astype(BF16)
    mk = mk_ref[0]
    mv = mv_ref[0]
    lane = lax.broadcasted_iota(jnp.int32, q.shape, 1)
    o = jnp.zeros(q.shape, F32)
    for hh in range(q.shape[1] // HEAD_DIM):
        in_head = (lane >= hh * HEAD_DIM) & (lane < (hh + 1) * HEAD_DIM)
        s = _dot_nt(_keep_lanes(q, hh * HEAD_DIM, (hh + 1) * HEAD_DIM), mk)
        p = jnp.exp(s - jnp.max(s, axis=1, keepdims=True))
        l = jnp.sum(p, axis=1, keepdims=True)
        o = jnp.where(in_head, _dot(p.astype(BF16), mv) / l, o)
    o_ref[0] = x + _dot(o.astype(BF16), wo_ref[...])


def _cross(x, g, wq, gain, mk, mv, wo, tm):
    b, s, d = x.shape
    full = lambda a: pl.BlockSpec(a.shape, lambda bi, i: (0,) * a.ndim)
    g = g.reshape(1, d)
    gmat = _group_mean_matrix()
    return pl.pallas_call(
        _cross_kernel,
        grid=(b, s // tm),
        in_specs=[
            pl.BlockSpec((1, tm, d), lambda bi, i: (bi, i, 0)),
            full(g), full(wq), full(gain), full(gmat),
            pl.BlockSpec((1,) + mk.shape[1:], lambda bi, i: (bi, 0, 0)),
            pl.BlockSpec((1,) + mv.shape[1:], lambda bi, i: (bi, 0, 0)),
            full(wo),
        ],
        out_specs=pl.BlockSpec((1, tm, d), lambda bi, i: (bi, i, 0)),
        out_shape=jax.ShapeDtypeStruct((b, s, d), F32),
        compiler_params=_cparams(("parallel", "parallel")),
        name="cross",
    )(x, g, wq, gain, gmat, mk, mv, wo)


def _sample_attn_kernel(lam_ref,
                        qa_ref, kan_ref, van_ref, kac_ref, vac_ref, bac_ref, ban_ref,
                        qb_ref, kbn_ref, vbn_ref, kbc_ref, vbc_ref, ubig_ref, usmall_ref,
                        qc_ref, kcn_ref, vcn_ref, kcc_ref, vcc_ref, bcc_ref, bcn_ref,
                        oa_ref, ob_ref, oc_ref):
    lam = lam_ref[0]
    ns = qa_ref.shape[1]
    bf = lambda r: r[0].astype(BF16)

    def heads(q, width):
        for hh in range(LANES // width):
            yield hh, None, _keep_lanes(q, hh * width, (hh + 1) * width)

    def softmax2(s_c, s_n, exp=jnp.exp):
        m = jnp.maximum(jnp.max(s_c, axis=1, keepdims=True), jnp.max(s_n, axis=1, keepdims=True))
        p_c = exp(s_c - m)
        p_n = exp(s_n - m)
        inv = 1.0 / (jnp.sum(p_c, axis=1, keepdims=True) + jnp.sum(p_n, axis=1, keepdims=True))
        return p_c * inv, p_n * inv

    for h in range(qa_ref.shape[2] // LANES):
        sl = slice(h * LANES, (h + 1) * LANES)
        q = qa_ref[0, :, sl]
        k_c = kac_ref[0, :, sl].astype(BF16)
        k_n = kan_ref[0, :, sl].astype(BF16)
        maps = []
        for _, _, qm in heads(q, HEAD_DIM):
            maps.append(softmax2(_dot_nt(qm, k_c) + bac_ref[h], _dot_nt(qm, k_n) + ban_ref[h], exp=jnp.exp2))
        a_c = (maps[0][0] - lam * maps[1][0]).astype(BF16)
        a_n = (maps[0][1] - lam * maps[1][1]).astype(BF16)
        oa_ref[0, :, sl] = (_dot(a_c, vac_ref[0, :, sl].astype(BF16))
                            + _dot(a_n, van_ref[0, :, sl].astype(BF16)))

    past = kbc_ref.shape[1]
    tk = ubig_ref.shape[0]
    row = lax.broadcasted_iota(jnp.int32, (ns, ns), 0)
    col = lax.broadcasted_iota(jnp.int32, (ns, ns), 1)
    for pr in range(qb_ref.shape[2] // LANES):
        sl = slice(pr * LANES, (pr + 1) * LANES)
        q = qb_ref[0, :, sl]
        k_n = kbn_ref[0, :, sl].astype(BF16)
        v_n = vbn_ref[0, :, sl].astype(BF16)
        outs = []
        for _, _, qm in heads(q, HEAD_DIM):
            w, carry = _stick_block(_dot_nt(qm, k_n), col < row, usmall_ref[...], 0.0)
            acc = _dot(w.astype(BF16), v_n)
            for j in range(past // tk - 1, -1, -1):
                k_c = kbc_ref[0, j * tk:(j + 1) * tk, sl].astype(BF16)
                v_c = vbc_ref[0, j * tk:(j + 1) * tk, sl].astype(BF16)
                w, rs = _stick_block(_dot_nt(qm, k_c), None, ubig_ref[...], carry)
                acc += _dot(w.astype(BF16), v_c)
                carry = carry + rs
            outs.append(acc)
        lane = lax.broadcasted_iota(jnp.int32, q.shape, 1)
        ob_ref[0, :, sl] = jnp.where(lane < HEAD_DIM, outs[0], outs[1]).astype(ob_ref.dtype)

    for pr in range(qc_ref.shape[2] // LANES):
        sl = slice(pr * LANES, (pr + 1) * LANES)
        q = qc_ref[0, :, sl]
        k_c = kcc_ref[0, :, sl].astype(BF16)
        k_n = kcn_ref[0, :, sl].astype(BF16)
        v_c = vcc_ref[0, :, sl].astype(BF16)
        v_n = vcn_ref[0, :, sl].astype(BF16)
        outs = []
        for hh, _, qm in heads(q, HEAD_DIM):
            hd = pr * (LANES // HEAD_DIM) + hh
            p_c, p_n = softmax2(_dot_nt(qm, k_c) + bcc_ref[hd], _dot_nt(qm, k_n) + bcn_ref[hd])
            outs.append(_dot(p_c.astype(BF16), v_c) + _dot(p_n.astype(BF16), v_n))
        lane = lax.broadcasted_iota(jnp.int32, q.shape, 1)
        oc_ref[0, :, sl] = jnp.where(lane < HEAD_DIM, outs[0], outs[1]).astype(oc_ref.dtype)


def _sample_attn(lam, qa, ka, va, cak, cav, bac, ban, qb, kb, vb, cbk, cbv,
                 qc, kc, vc, cck, ccv, bcc, bcn):
    b, ns, _ = qa.shape
    per_b = lambda a: pl.BlockSpec((1,) + a.shape[1:], lambda bi: (bi,) + (0,) * (a.ndim - 1))
    full = lambda a: pl.BlockSpec(a.shape, lambda bi: (0,) * a.ndim)
    ubig = _strict_lower(ATTN_TK)
    usmall = _strict_lower(ns)
    args = [qa, ka, va, cak, cav, bac, ban, qb, kb, vb, cbk, cbv, ubig, usmall,
            qc, kc, vc, cck, ccv, bcc, bcn]
    specs = [per_b(qa), per_b(ka), per_b(va), per_b(cak), per_b(cav), full(bac), full(ban),
             per_b(qb), per_b(kb), per_b(vb), per_b(cbk), per_b(cbv), full(ubig), full(usmall),
             per_b(qc), per_b(kc), per_b(vc), per_b(cck), per_b(ccv), full(bcc), full(bcn)]
    return pl.pallas_call(
        _sample_attn_kernel,
        grid=(b,),
        in_specs=[pl.BlockSpec(memory_space=pltpu.SMEM)] + specs,
        out_specs=[per_b(qa), per_b(qb), per_b(qc)],
        out_shape=[jax.ShapeDtypeStruct(qa.shape, F32), jax.ShapeDtypeStruct(qb.shape, BF16),
                   jax.ShapeDtypeStruct(qc.shape, BF16)],
        compiler_params=_cparams(("parallel",)),
        name="sample_mixers",
    )(lam, *args)


def _t5_bucket_np(rel):
    half = T5_BUCKETS // 2
    max_exact = half // 2
    n = np.abs(rel)
    nf = np.maximum(n, 1).astype(np.float64)
    large = max_exact + (np.log(nf / max_exact) / math.log(T5_MAX_DIST / max_exact)
                         * (half - max_exact)).astype(np.int64)
    large = np.minimum(large, half - 1)
    return np.where(rel > 0, half, 0) + np.where(n < max_exact, n, large)


def _toeplitz(lookup, n_rows, n_cols):
    period = n_rows + n_cols
    slot = np.arange(period)
    diff = np.minimum((slot + n_rows - 1) % period - (n_rows - 1), n_cols - 1)
    vec = lookup(diff).astype(F32)
    flat = jnp.tile(vec, (1, n_rows))[:, :n_rows * (period - 1)]
    return flat.reshape(vec.shape[0], n_rows, period - 1)[:, :, :n_cols]


def _t5_bias_table(t5_bias, qpos, kpos, key_major=False):
    t5_rows = lambda rel: t5_bias[_t5_bucket_np(rel)].T
    mask = (kpos[None, :] // CHUNK) <= (qpos[:, None] // CHUNK)
    if key_major:
        table = _toeplitz(lambda dd: t5_rows(kpos[0] - qpos[0] - dd), len(kpos), len(qpos))
        mask = mask.T
    else:
        table = _toeplitz(lambda dd: t5_rows(kpos[0] - qpos[0] + dd), len(qpos), len(kpos))
    return jnp.where(jnp.asarray(mask)[None], table, NEG)


def _band_bias_table(rel_table, qpos, kpos):
    lookup = lambda dd: rel_table[:, np.clip(kpos[0] - qpos[0] + dd, -REL_CLIP, REL_CLIP) + REL_CLIP]
    table = _toeplitz(lookup, len(qpos), len(kpos))
    qc = qpos[:, None] // CHUNK
    kc = kpos[None, :] // CHUNK
    mask = (kpos[None, :] >= 0) & (kc <= qc) & (kc >= qc - C_PREV)
    return jnp.where(jnp.asarray(mask)[None], table, NEG)


def _lambda_init(layer):
    return 0.8 - 0.6 * math.exp(-0.3 * layer)


def kernel(x_prompt, x_sample, mem_prompt, cache_a_k, cache_a_v, cache_b_k, cache_b_v, cache_c_k, cache_c_v, cache_mem_k, cache_mem_v, t5_bias, ffn1_norm, ffn1_wg, ffn1_wu, ffn1_wd, mix_norm, w_in, a_qnorm, a_knorm, a_lq1, a_lk1, a_lq2, a_lk2, a_subln, c_qnorm, c_knorm, c_rel_bias, w_br_a, w_br_b, w_br_c, w_out, x_norm, mem_norm, x_wq, x_wkv, x_qnorm, x_knorm, x_wo, ffn2_norm, ffn2_wg, ffn2_wu, ffn2_wd):
    bp, sp, d = x_prompt.shape
    bs, ns, _ = x_sample.shape
    depth = w_in.shape[0]
    past = cache_a_k.shape[2]
    w_buf = cache_c_k.shape[2]
    n_mem = mem_prompt.shape[1]
    wa = cache_a_k.shape[3] * cache_a_k.shape[4]
    wb = cache_b_k.shape[3] * cache_b_k.shape[4]
    wc = cache_c_k.shape[3] * cache_c_k.shape[4]
    wx = cache_mem_k.shape[3] * cache_mem_k.shape[4]
    h_a = cache_a_k.shape[3]
    n_qkv = 3 * (wa + wb + wc)
    w_keep = min(C_PREV * CHUNK, sp)
    scale = HEAD_DIM ** -0.5
    assert sp % ATTN_TQ == 0 and ATTN_TQ == ATTN_TK and ATTN_TQ == 4 * CHUNK and sp >= w_keep
    assert past % ATTN_TK == 0
    assert sp % MIXER_A_TILE == 0 and MIXER_A_TILE % CHUNK == 0 and T5_MAX_DIST <= MIXER_A_TILE

    tile = lambda g, reps: jnp.tile(g.astype(F32), reps)
    bf = lambda a: a.astype(BF16)

    loc_a = np.arange(MIXER_A_TILE)
    loc = np.arange(ATTN_TQ)
    a_far = t5_bias[T5_BUCKETS // 2 - 1].astype(F32)
    a_bias = LOG2E * jnp.stack([
        _t5_bias_table(t5_bias, loc_a + MIXER_A_TILE, loc_a + MIXER_A_TILE, key_major=True),
        _t5_bias_table(t5_bias, loc_a + MIXER_A_TILE, loc_a, key_major=True),
        jnp.broadcast_to(a_far[:, None, None], (h_a, MIXER_A_TILE, MIXER_A_TILE))], axis=1)
    t5_log2 = LOG2E * t5_bias.astype(F32)
    a_brange = jnp.stack([LOG2E * a_far, jnp.max(t5_log2, axis=0), jnp.min(t5_log2, axis=0)], axis=1).reshape(-1)
    qpos_s = past + np.arange(ns)
    a_bias_sc = LOG2E * _t5_bias_table(t5_bias, qpos_s, np.arange(past))
    a_bias_sn = LOG2E * _t5_bias_table(t5_bias, qpos_s, qpos_s)
    kpos_sb = past - w_buf + np.arange(w_buf + ns)

    xp = x_prompt.reshape(bp * sp, d)
    xs = x_sample.reshape(bs * ns, d)
    mem = mem_prompt.reshape(bp * n_mem, d)
    outs = {k: [] for k in ("ak_p", "av_p", "bk_p", "bv_p", "ck_p", "cv_p", "mk_p", "mv_p",
                            "ak_s", "av_s", "bk_s", "bv_s", "ck_s", "cv_s")}
    qkv_segs = [(wa, True, True), (wa, True, True), (wa, False, False),
                (wb, False, True), (wb, False, False), (wb, False, False),
                (wc, True, True), (wc, True, True), (wc, False, False)]
    qkv_dtypes = [BF16, F32, F32, BF16, F32, F32, BF16, F32, F32]

    for i in range(depth):
        wg1, wu1, wd1 = bf(ffn1_wg[i]), bf(ffn1_wu[i]), bf(ffn1_wd[i])
        wg2, wu2, wd2 = bf(ffn2_wg[i]), bf(ffn2_wu[i]), bf(ffn2_wd[i])
        w_qkv, w_gate = bf(w_in[i][:, :n_qkv]), bf(w_in[i][:, n_qkv:])
        qkv_gain = jnp.concatenate([
            tile(a_qnorm[i], wa // HEAD_DIM) * (scale * LOG2E), tile(a_knorm[i], wa // HEAD_DIM), jnp.ones((wa,), F32),
            jnp.full((wb,), scale, F32), jnp.ones((2 * wb,), F32),
            tile(c_qnorm[i], wc // HEAD_DIM) * scale, tile(c_knorm[i], wc // HEAD_DIM), jnp.ones((wc,), F32),
        ]).reshape(1, n_qkv)
        sub = (tile(a_subln[i], h_a) * (1.0 - _lambda_init(i))).reshape(1, wa)
        wbr = bf(w_br_a[i]), bf(w_br_b[i]), bf(w_br_c[i])
        wout = bf(w_out[i])
        wq, wo = bf(x_wq[i]), bf(x_wo[i])
        xq_gain = (tile(x_qnorm[i], wx // HEAD_DIM) * scale).reshape(1, wx)
        dot64 = lambda a, b: jnp.exp(jnp.sum(a.astype(F32) * b.astype(F32)))
        lam = (dot64(a_lq1[i], a_lk1[i]) - dot64(a_lq2[i], a_lk2[i]) + _lambda_init(i)).reshape(1)
        c_bias_p = _band_bias_table(c_rel_bias[i], C_PREV * CHUNK + loc, np.arange(BAND_W))
        c_bias_sc = _band_bias_table(c_rel_bias[i], qpos_s, kpos_sb[:w_buf])
        c_bias_sn = _band_bias_table(c_rel_bias[i], qpos_s, kpos_sb[w_buf:])

        xp = _ffn(xp, ffn1_norm[i], wg1, wu1, wd1, tm=1024)
        qa, ka, va, qb, kb, vb, qc, kc, vc = _proj(xp, mix_norm[i], w_qkv, qkv_gain, qkv_segs, qkv_dtypes, tm=512)
        r3 = lambda a: a.reshape(bp, sp, a.shape[-1])
        oa = _attn_a(lam, a_brange, r3(qa), r3(ka), r3(va), a_bias)
        ob = _attn_b(r3(qb), r3(kb), r3(vb))
        oc = _attn_c(r3(qc), r3(kc), r3(vc), c_bias_p)
        xp = _merge(xp, oa.reshape(-1, wa), ob.reshape(-1, wb), oc.reshape(-1, wc), mix_norm[i], w_gate, sub,
                    *wbr, wout, tm=512)
        mk, mv = _proj(mem, mem_norm[i], bf(x_wkv[i]),
                       jnp.concatenate([tile(x_knorm[i], wx // HEAD_DIM), jnp.ones((wx,), F32)]).reshape(1, 2 * wx),
                       [(wx, True, True), (wx, False, False)], [F32, F32], tm=n_mem)
        mk3, mv3 = mk.reshape(bp, n_mem, wx), mv.reshape(bp, n_mem, wx)
        xp = _cross(xp.reshape(bp, sp, d), x_norm[i], wq, xq_gain, bf(mk3), bf(mv3), wo, tm=512).reshape(-1, d)
        xp = _ffn(xp, ffn2_norm[i], wg2, wu2, wd2, tm=1024)
        outs["ak_p"].append(ka.reshape(bp, sp, h_a, -1))
        outs["av_p"].append(va.reshape(bp, sp, h_a, -1))
        outs["bk_p"].append(kb.reshape(bp, sp, -1, HEAD_DIM))
        outs["bv_p"].append(vb.reshape(bp, sp, -1, HEAD_DIM))
        outs["ck_p"].append(r3(kc)[:, sp - w_keep:].reshape(bp, w_keep, -1, HEAD_DIM))
        outs["cv_p"].append(r3(vc)[:, sp - w_keep:].reshape(bp, w_keep, -1, HEAD_DIM))
        outs["mk_p"].append(mk3.reshape(bp, n_mem, -1, HEAD_DIM))
        outs["mv_p"].append(mv3.reshape(bp, n_mem, -1, HEAD_DIM))

        xs = _ffn(xs, ffn1_norm[i], wg1, wu1, wd1, tm=bs * ns)
        qa, ka, va, qb, kb, vb, qc, kc, vc = _proj(xs, mix_norm[i], w_qkv, qkv_gain, qkv_segs, qkv_dtypes, tm=bs * ns)
        s3 = lambda a: a.reshape(bs, ns, a.shape[-1])
        c3 = lambda a: a.reshape(bs, a.shape[1], -1)
        oa, ob, oc = _sample_attn(
            lam, s3(qa), s3(ka), s3(va), c3(cache_a_k[i]), c3(cache_a_v[i]), a_bias_sc, a_bias_sn,
            s3(qb), s3(kb), s3(vb), c3(cache_b_k[i]), c3(cache_b_v[i]),
            s3(qc), s3(kc), s3(vc), c3(cache_c_k[i]), c3(cache_c_v[i]), c_bias_sc, c_bias_sn)
        xs = _merge(xs, oa.reshape(-1, wa), ob.reshape(-1, wb), oc.reshape(-1, wc), mix_norm[i], w_gate, sub,
                    *wbr, wout, tm=bs * ns)
        xs = _cross(xs.reshape(bs, ns, d), x_norm[i], wq, xq_gain, bf(c3(cache_mem_k[i])), bf(c3(cache_mem_v[i])),
                    wo, tm=ns).reshape(-1, d)
        xs = _ffn(xs, ffn2_norm[i], wg2, wu2, wd2, tm=bs * ns)
        outs["ak_s"].append(ka.reshape(bs, ns, h_a, -1))
        outs["av_s"].append(va.reshape(bs, ns, h_a, -1))
        outs["bk_s"].append(kb.reshape(bs, ns, -1, HEAD_DIM))
        outs["bv_s"].append(vb.reshape(bs, ns, -1, HEAD_DIM))
        outs["ck_s"].append(jnp.concatenate([cache_c_k[i], kc.reshape(bs, ns, -1, HEAD_DIM)], axis=1)[:, ns:])
        outs["cv_s"].append(jnp.concatenate([cache_c_v[i], vc.reshape(bs, ns, -1, HEAD_DIM)], axis=1)[:, ns:])

    st = lambda k: jnp.stack(outs[k])
    return (xp.reshape(bp, sp, d), xs.reshape(bs, ns, d),
            st("ak_p"), st("av_p"), st("bk_p"), st("bv_p"), st("ck_p"), st("cv_p"), st("mk_p"), st("mv_p"),
            st("ak_s"), st("av_s"), st("bk_s"), st("bv_s"), st("ck_s"), st("cv_s"))
```

```python
import functools
import math

import numpy as np
import jax
import jax.numpy as jnp
from jax import lax
from jax.experimental import pallas as pl
from jax.experimental.pallas import tpu as pltpu

F32 = jnp.float32
BF16 = jnp.bfloat16

EPS = 1e-6
HEAD_DIM = 64
CHUNK = 64
C_PREV = 8
REL_CLIP = 128
T5_BUCKETS = 32
T5_MAX_DIST = 128
LANES = 128
BF16_SUBLANES = 16
NEG = -1e30
LOG2E = math.log2(math.e)
BOUND_SLACK = 1.001
BOUNDED_EXP2_SPAN = 100.0
STICK_SKIP = -110.0
V7X_VMEM_LIMIT_BYTES = 56 * 1024 * 1024

MIXER_A_TILE = 512
ATTN_TQ = 256
ATTN_TK = 256
BAND_W = (C_PREV + 4) * CHUNK


def _cparams(sem):
    return pltpu.CompilerParams(dimension_semantics=sem, vmem_limit_bytes=V7X_VMEM_LIMIT_BYTES)


def _rms_rows(x, g):
    return x * lax.rsqrt(jnp.mean(x * x, axis=-1, keepdims=True) + EPS) * g


def _dot(a, b):
    return jnp.dot(a, b, preferred_element_type=F32)


def _dot_nt(a, b):
    return lax.dot_general(a, b, (((1,), (1,)), ((), ())), preferred_element_type=F32)


def _keep_lanes(q, lo, hi):
    lane = lax.broadcasted_iota(jnp.int32, q.shape, 1)
    return jnp.where((lane >= lo) & (lane < hi), q.astype(F32), 0.0).astype(BF16)


def _group_rms(y, gmat):
    parts = []
    for c in range(y.shape[1] // LANES):
        yc = y[:, c * LANES:(c + 1) * LANES]
        ms = _dot((yc * yc).astype(BF16), gmat)
        parts.append(yc * lax.rsqrt(ms + EPS))
    return parts[0] if len(parts) == 1 else jnp.concatenate(parts, axis=1)


def _group_mean_matrix():
    g = np.kron(np.eye(LANES // HEAD_DIM), np.ones((HEAD_DIM, HEAD_DIM))) / HEAD_DIM
    return jnp.asarray(g, BF16)


def _ffn_kernel(x_ref, g_ref, wg_ref, wu_ref, wd_ref, o_ref, h_scr, acc_scr):
    j = pl.program_id(1)

    @pl.when(j == 0)
    def _():
        h_scr[...] = _rms_rows(x_ref[...], g_ref[...]).astype(BF16)
        acc_scr[...] = jnp.zeros_like(acc_scr)

    h = h_scr[...]
    a = _dot(h, wg_ref[...])
    u = _dot(h, wu_ref[...])
    t = a * jax.nn.sigmoid(a) * u
    acc_scr[...] += _dot(t.astype(BF16), wd_ref[...])

    @pl.when(j == pl.num_programs(1) - 1)
    def _():
        o_ref[...] = x_ref[...] + 0.5 * acc_scr[...]


def _ffn(x, g, wg, wu, wd, tm, tf=512):
    n, d = x.shape
    dff = wg.shape[1]
    return pl.pallas_call(
        _ffn_kernel,
        grid=(n // tm, dff // tf),
        in_specs=[
            pl.BlockSpec((tm, d), lambda i, j: (i, 0)),
            pl.BlockSpec((1, d), lambda i, j: (0, 0)),
            pl.BlockSpec((d, tf), lambda i, j: (0, j)),
            pl.BlockSpec((d, tf), lambda i, j: (0, j)),
            pl.BlockSpec((tf, d), lambda i, j: (j, 0)),
        ],
        out_specs=pl.BlockSpec((tm, d), lambda i, j: (i, 0)),
        out_shape=jax.ShapeDtypeStruct((n, d), F32),
        scratch_shapes=[pltpu.VMEM((tm, d), BF16), pltpu.VMEM((tm, d), F32)],
        compiler_params=_cparams(("parallel", "arbitrary")),
        name="ffn",
    )(x, g.reshape(1, d), wg, wu, wd)


def _proj_kernel(x_ref, g_ref, w_ref, gain_ref, gmat_ref, *out_refs, segs):
    h = _rms_rows(x_ref[...], g_ref[...]).astype(BF16)
    off = 0
    for (width, normed, gained), o_ref in zip(segs, out_refs):
        y = _dot(h, w_ref[:, off:off + width])
        if normed:
            y = _group_rms(y, gmat_ref[...])
        if gained:
            y = y * gain_ref[:, off:off + width]
        o_ref[...] = y.astype(o_ref.dtype)
        off += width


def _proj(x, g, w, gain, segs, dtypes, tm):
    n, d = x.shape
    wtot = w.shape[1]
    widths = [s[0] for s in segs]
    return pl.pallas_call(
        functools.partial(_proj_kernel, segs=tuple(segs)),
        grid=(n // tm,),
        in_specs=[
            pl.BlockSpec((tm, d), lambda i: (i, 0)),
            pl.BlockSpec((1, d), lambda i: (0, 0)),
            pl.BlockSpec((d, wtot), lambda i: (0, 0)),
            pl.BlockSpec((1, wtot), lambda i: (0, 0)),
            pl.BlockSpec((LANES, LANES), lambda i: (0, 0)),
        ],
        out_specs=[pl.BlockSpec((tm, wd), lambda i: (i, 0)) for wd in widths],
        out_shape=[jax.ShapeDtypeStruct((n, wd), dt) for wd, dt in zip(widths, dtypes)],
        compiler_params=_cparams(("parallel",)),
        name="proj",
    )(x, g.reshape(1, d), w, gain, _group_mean_matrix())


def _softmax_step_km(s, vt, m_ref, acc_ref):
    m_old = m_ref[...]
    m_new = jnp.maximum(m_old, jnp.max(s, axis=0, keepdims=True))
    alpha = jnp.exp2(m_old - m_new)
    p = jnp.exp2(s - m_new)
    acc_ref[...] = alpha * acc_ref[...] + _dot(vt, p.astype(BF16))
    m_ref[...] = m_new


def _cast_rows(src_ref, dst_ref, rows, dst_off=0, step=512):
    def body(i, c):
        r = pl.multiple_of(i * step, step)
        dst_ref[pl.ds(dst_off + r, step), :] = src_ref[0, pl.ds(r, step), :].astype(BF16)
        return c
    lax.fori_loop(0, rows // step, body, 0)


def _attn_a_kernel(lam_ref, brange_ref, q_ref, k_ref, v_ref, bias_ref, o_ref,
                   kbf, vtb, s_scr, knorm, m1, a1, m2, a2):
    h = pl.program_id(1)
    qi = pl.program_id(2)
    seq = kbf.shape[0]
    tq = q_ref.shape[1]
    tk = vtb.shape[2]
    dv = v_ref.shape[2]

    @pl.when(qi == 0)
    def _():
        lane = lax.broadcasted_iota(jnp.int32, (tk, LANES), 1)

        def prep(j, c):
            r = pl.multiple_of(j * tk, tk)
            kb = k_ref[0, pl.ds(r, tk), :].astype(BF16)
            kbf[pl.ds(r, tk), :] = kb
            vtb[j, 0:dv, :] = v_ref[0, pl.ds(r, tk), :].T.astype(BF16)
            vtb[j, dv:, :] = jnp.ones((vtb.shape[1] - dv, tk), BF16)
            ksq = kb.astype(F32) * kb.astype(F32)
            n1 = jnp.max(jnp.sum(jnp.where(lane < HEAD_DIM, ksq, 0.0), axis=1, keepdims=True))
            n2 = jnp.max(jnp.sum(jnp.where(lane >= HEAD_DIM, ksq, 0.0), axis=1, keepdims=True))
            return jnp.maximum(c[0], n1), jnp.maximum(c[1], n2)
        n1, n2 = lax.fori_loop(0, seq // tk, prep, (jnp.float32(0.0), jnp.float32(0.0)))
        knorm[0] = n1
        knorm[1] = n2

    qt = q_ref[0].astype(F32).T
    sub = lax.broadcasted_iota(jnp.int32, qt.shape, 0)
    q1t = jnp.where(sub < HEAD_DIM, qt, 0.0).astype(BF16)
    q2t = jnp.where(sub >= HEAD_DIM, qt, 0.0).astype(BF16)
    states = ((m1, a1), (m2, a2))
    for m_ref, a_ref in states:
        a_ref[...] = jnp.zeros_like(a_ref)

    far_bias, bias_max, bias_min = brange_ref[3 * h], brange_ref[3 * h + 1], brange_ref[3 * h + 2]
    qsq = qt * qt
    bound1 = jnp.sqrt(jnp.sum(jnp.where(sub < HEAD_DIM, qsq, 0.0), axis=0, keepdims=True) * knorm[0])
    bound2 = jnp.sqrt(jnp.sum(jnp.where(sub >= HEAD_DIM, qsq, 0.0), axis=0, keepdims=True) * knorm[1])
    bounds = (bound1 * BOUND_SLACK + bias_max, bound2 * BOUND_SLACK + bias_max)
    spread = 2.0 * BOUND_SLACK * jnp.maximum(jnp.max(bound1), jnp.max(bound2)) + (bias_max - bias_min)
    bounded = spread <= BOUNDED_EXP2_SPAN

    @pl.when(bounded)
    def _():
        qmaps = ((q1t, bounds[0], a1), (q2t, bounds[1], a2))

        def weights(j, adds):
            kb = kbf[pl.ds(pl.multiple_of(j * tk, tk), tk), :]
            return [jnp.exp2(_dot(kb, qmt) + add).astype(BF16) for (qmt, _, _), add in zip(qmaps, adds)]

        def special(j, tile):
            ps = weights(j, [tile - shift for _, shift, _ in qmaps])
            for p, (_, _, a_ref) in zip(ps, qmaps):
                a_ref[...] += _dot(vtb[j], p)

        special(qi, bias_ref[0, 0])

        @pl.when(qi >= 1)
        def _():
            special(qi - 1, bias_ref[0, 1])

        far_adds = [far_bias - shift for _, shift, _ in qmaps]
        n_far = jnp.maximum(qi - 1, 0)
        odd = n_far & 1

        @pl.when(odd == 1)
        def _():
            for p, (_, _, a_ref) in zip(weights(0, far_adds), qmaps):
                a_ref[...] += _dot(vtb[0], p)

        def pair(i, c):
            j = odd + 2 * i
            pa = weights(j, far_adds)
            pb = weights(j + 1, far_adds)
            for p0, p1, (_, _, a_ref) in zip(pa, pb, qmaps):
                a_ref[...] += _dot(vtb[j], p0) + _dot(vtb[j + 1], p1)
            return c
        lax.fori_loop(0, lax.shift_right_logical(n_far, 1), pair, 0)

    @pl.when(jnp.logical_not(bounded))
    def _():
        for m_ref, _ in states:
            m_ref[...] = jnp.full_like(m_ref, NEG)
        n_blocks = qi + 1

        def scores(t, buf):
            kb = kbf[pl.ds(pl.multiple_of((qi - t) * tk, tk), tk), :]
            s_scr[buf, 0] = _dot(kb, q1t)
            s_scr[buf, 1] = _dot(kb, q2t)

        def update(t, buf):
            bias = bias_ref[0, jnp.minimum(t, bias_ref.shape[1] - 1)]
            vt = vtb[qi - t]
            for mi, (m_ref, a_ref) in enumerate(states):
                _softmax_step_km(s_scr[buf, mi] + bias, vt, m_ref, a_ref)

        odd = n_blocks & 1

        @pl.when(odd == 1)
        def _():
            scores(0, 0)
            update(0, 0)

        n_pairs = lax.shift_right_logical(n_blocks, 1)

        @pl.when(n_pairs > 0)
        def _():
            scores(odd, 0)

        def pair(p, c):
            t0 = odd + 2 * p
            scores(t0 + 1, 1)
            update(t0, 0)
            scores(jnp.minimum(t0 + 2, qi), 0)
            update(t0 + 1, 1)
            return c
        lax.fori_loop(0, n_pairs, pair, 0)

    out_t = (a1[0:dv, :] / a1[dv:dv + 1, :]
             - lam_ref[0] * (a2[0:dv, :] / a2[dv:dv + 1, :]))
    o_ref[0] = out_t.T


def _attn_a(lam, brange, qa, ka, va, bias):
    b, s, w = qa.shape
    nh = w // LANES
    tq = tk = MIXER_A_TILE
    vrows = LANES + BF16_SUBLANES
    return pl.pallas_call(
        _attn_a_kernel,
        grid=(b, nh, s // tq),
        in_specs=[
            pl.BlockSpec(memory_space=pltpu.SMEM),
            pl.BlockSpec(memory_space=pltpu.SMEM),
            pl.BlockSpec((1, tq, LANES), lambda bi, h, qi: (bi, qi, h)),
            pl.BlockSpec((1, s, LANES), lambda bi, h, qi: (bi, 0, h)),
            pl.BlockSpec((1, s, LANES), lambda bi, h, qi: (bi, 0, h)),
            pl.BlockSpec((1,) + bias.shape[1:], lambda bi, h, qi: (h, 0, 0, 0)),
        ],
        out_specs=pl.BlockSpec((1, tq, LANES), lambda bi, h, qi: (bi, qi, h)),
        out_shape=jax.ShapeDtypeStruct((b, s, w), F32),
        scratch_shapes=[
            pltpu.VMEM((s, LANES), BF16), pltpu.VMEM((s // tk, vrows, tk), BF16),
            pltpu.VMEM((2, 2, tk, tq), F32), pltpu.SMEM((2,), F32),
            pltpu.VMEM((1, tq), F32), pltpu.VMEM((vrows, tq), F32),
            pltpu.VMEM((1, tq), F32), pltpu.VMEM((vrows, tq), F32),
        ],
        compiler_params=_cparams(("parallel", "parallel", "arbitrary")),
        name="mixer_a",
    )(lam, brange, qa, ka, va, bias)


def _stick_block(z, valid, umat, carry):
    sp = jnp.maximum(z, 0.0) + jnp.log(1.0 + jnp.exp(-jnp.abs(z)))
    log1m = -sp
    if valid is not None:
        log1m = jnp.where(valid, log1m, 0.0)
    hi = log1m.astype(BF16)
    lo = (log1m - hi.astype(F32)).astype(BF16)
    after = _dot(hi, umat) + _dot(lo, umat)
    w = jnp.exp(z - sp + after + carry)
    if valid is not None:
        w = jnp.where(valid, w, 0.0)
    return w, jnp.sum(log1m, axis=1, keepdims=True)


def _stick_block_km(z, valid, umat, carry):
    sp = jnp.maximum(z, 0.0) + jnp.log(1.0 + jnp.exp(-jnp.abs(z)))
    log1m = -sp
    if valid is not None:
        log1m = jnp.where(valid, log1m, 0.0)
    hi = log1m.astype(BF16)
    lo = (log1m - hi.astype(F32)).astype(BF16)
    after = _dot(umat, hi) + _dot(umat, lo)
    w = jnp.exp(z - sp + after + carry)
    if valid is not None:
        w = jnp.where(valid, w, 0.0)
    return w, jnp.sum(log1m, axis=0, keepdims=True)


def _attn_b_kernel(q_ref, k_ref, v_ref, u_ref, o_ref, kbf, vtb, c_scr, acc_scr):
    qi = pl.program_id(2)
    seq = kbf.shape[0]
    tq = q_ref.shape[1]
    tk = vtb.shape[2]
    n_heads = LANES // HEAD_DIM

    @pl.when(qi == 0)
    def _():
        def prep(j, c):
            r = pl.multiple_of(j * tk, tk)
            kbf[pl.ds(r, tk), :] = k_ref[0, pl.ds(r, tk), :].astype(BF16)
            vtb[j] = v_ref[0, pl.ds(r, tk), :].T.astype(BF16)
            return c
        lax.fori_loop(0, seq // tk, prep, 0)

    qt = q_ref[0].astype(F32).T
    sub = lax.broadcasted_iota(jnp.int32, qt.shape, 0)
    qts = [jnp.where((sub >= hh * HEAD_DIM) & (sub < (hh + 1) * HEAD_DIM), qt, 0.0).astype(BF16)
           for hh in range(n_heads)]
    krow = lax.broadcasted_iota(jnp.int32, (tk, tq), 0)
    qcol = lax.broadcasted_iota(jnp.int32, (tk, tq), 1)
    umat = u_ref[...]

    def blocks(js, valids, first):
        kbs = [kbf[pl.ds(pl.multiple_of(j * tk, tk), tk), :] for j in js]
        cmax = None
        for hh in range(n_heads):
            carry = 0.0 if first else c_scr[hh]
            pv = None
            for j, kb, valid in zip(js, kbs, valids):
                w, cs = _stick_block_km(_dot(kb, qts[hh]), valid, umat, carry)
                term = _dot(vtb[j], w.astype(BF16))
                pv = term if pv is None else pv + term
                carry = carry + cs
            acc_scr[hh] = pv if first else acc_scr[hh] + pv
            c_scr[hh] = carry
            cm = jnp.max(carry)
            cmax = cm if cmax is None else jnp.maximum(cmax, cm)
        return cmax

    own_valid = krow < qcol

    @pl.when(qi == 0)
    def _():
        blocks([qi], [own_valid], True)

    @pl.when(qi >= 1)
    def _():
        cmax0 = blocks([qi, qi - 1], [own_valid, None], True)

        def cond(st):
            j, cmax = st
            return (j >= 0) & (cmax > STICK_SKIP)

        def body(st):
            j, _ = st
            return j - 2, blocks([j, jnp.maximum(j - 1, 0)], [None, (krow >= 0) & (j >= 1)], False)

        lax.while_loop(cond, body, (qi - 2, cmax0))
    out_t = jnp.where(sub < HEAD_DIM, acc_scr[0], acc_scr[1])
    o_ref[0] = out_t.T.astype(o_ref.dtype)


def _strict_lower(n):
    return jnp.asarray(np.tril(np.ones((n, n)), -1), BF16)


def _strict_upper(n):
    return jnp.asarray(np.triu(np.ones((n, n)), 1), BF16)


def _attn_b(qb, kb, vb):
    b, s, w = qb.shape
    tq = ATTN_TQ
    return pl.pallas_call(
        _attn_b_kernel,
        grid=(b, w // LANES, s // tq),
        in_specs=[
            pl.BlockSpec((1, tq, LANES), lambda bi, h, qi: (bi, qi, h)),
            pl.BlockSpec((1, s, LANES), lambda bi, h, qi: (bi, 0, h)),
            pl.BlockSpec((1, s, LANES), lambda bi, h, qi: (bi, 0, h)),
            pl.BlockSpec((ATTN_TK, ATTN_TK), lambda bi, h, qi: (0, 0)),
        ],
        out_specs=pl.BlockSpec((1, tq, LANES), lambda bi, h, qi: (bi, qi, h)),
        out_shape=jax.ShapeDtypeStruct((b, s, w), BF16),
        scratch_shapes=[
            pltpu.VMEM((s, LANES), BF16), pltpu.VMEM((s // ATTN_TK, LANES, ATTN_TK), BF16),
            pltpu.VMEM((LANES // HEAD_DIM, 1, tq), F32), pltpu.VMEM((LANES // HEAD_DIM, LANES, tq), F32),
        ],
        compiler_params=_cparams(("parallel", "parallel", "arbitrary")),
        name="mixer_b",
    )(qb, kb, vb, _strict_upper(ATTN_TK))


def _attn_c_kernel(q_ref, k_ref, v_ref, bias_ref, o_ref, kbf, vtb):
    qi = pl.program_id(2)
    seq = k_ref.shape[1]
    tq = q_ref.shape[1]
    pad = C_PREV * CHUNK
    dv = v_ref.shape[2]
    n_pad_blocks = pad // tq

    @pl.when(qi == 0)
    def _():
        kbf[0:pad, :] = jnp.zeros((pad, LANES), BF16)
        vtb[0:n_pad_blocks] = jnp.zeros((n_pad_blocks,) + vtb.shape[1:], BF16)

        def prep(j, c):
            r = pl.multiple_of(j * tq, tq)
            kbf[pl.ds(pad + r, tq), :] = k_ref[0, pl.ds(r, tq), :].astype(BF16)
            vtb[n_pad_blocks + j, 0:dv, :] = v_ref[0, pl.ds(r, tq), :].T.astype(BF16)
            vtb[n_pad_blocks + j, dv:, :] = jnp.ones((vtb.shape[1] - dv, tq), BF16)
            return c
        lax.fori_loop(0, seq // tq, prep, 0)

    qt = q_ref[0].astype(F32).T
    sub = lax.broadcasted_iota(jnp.int32, qt.shape, 0)
    kw = kbf[pl.ds(pl.multiple_of(qi * tq, tq), BAND_W), :]
    krow = lax.broadcasted_iota(jnp.int32, (BAND_W, tq), 0)
    in_seq = krow >= pad - qi * tq
    outs = []
    for hh in range(LANES // HEAD_DIM):
        qmt = jnp.where((sub >= hh * HEAD_DIM) & (sub < (hh + 1) * HEAD_DIM), qt, 0.0).astype(BF16)
        s = jnp.where(in_seq, _dot(kw, qmt) + bias_ref[hh], NEG)
        p = jnp.exp(s - jnp.max(s, axis=0, keepdims=True)).astype(BF16)
        acc = _dot(vtb[qi], p[0:tq])
        for blk in range(1, BAND_W // tq):
            acc += _dot(vtb[qi + blk], p[blk * tq:(blk + 1) * tq])
        outs.append(acc[0:dv] / acc[dv:dv + 1])
    o_ref[0] = jnp.where(sub < HEAD_DIM, outs[0], outs[1]).T.astype(o_ref.dtype)


def _attn_c(qc, kc, vc, bias):
    b, s, w = qc.shape
    tq = ATTN_TQ
    pad = C_PREV * CHUNK
    hp = LANES // HEAD_DIM
    return pl.pallas_call(
        _attn_c_kernel,
        grid=(b, w // LANES, s // tq),
        in_specs=[
            pl.BlockSpec((1, tq, LANES), lambda bi, h, qi: (bi, qi, h)),
            pl.BlockSpec((1, s, LANES), lambda bi, h, qi: (bi, 0, h)),
            pl.BlockSpec((1, s, LANES), lambda bi, h, qi: (bi, 0, h)),
            pl.BlockSpec((hp, BAND_W, tq), lambda bi, h, qi: (h, 0, 0)),
        ],
        out_specs=pl.BlockSpec((1, tq, LANES), lambda bi, h, qi: (bi, qi, h)),
        out_shape=jax.ShapeDtypeStruct((b, s, w), BF16),
        scratch_shapes=[pltpu.VMEM((s + pad, LANES), BF16),
                        pltpu.VMEM(((s + pad) // tq, LANES + BF16_SUBLANES, tq), BF16)],
        compiler_params=_cparams(("parallel", "parallel", "arbitrary")),
        name="mixer_c",
    )(qc, kc, vc, bias)


def _merge_kernel(x_ref, oa_ref, ob_ref, oc_ref, g_ref, wgate_ref, sub_ref,
                  wa_ref, wb_ref, wc_ref, wout_ref, o_ref):
    x = x_ref[...]
    d = x.shape[1]
    h = _rms_rows(x, g_ref[...]).astype(BF16)
    oa = oa_ref[...]
    parts = []
    for c in range(oa.shape[1] // LANES):
        oc_ = oa[:, c * LANES:(c + 1) * LANES]
        parts.append(oc_ * lax.rsqrt(jnp.mean(oc_ * oc_, axis=-1, keepdims=True) + EPS))
    oan = (jnp.concatenate(parts, axis=1) * sub_ref[...]).astype(BF16)
    merged = jax.nn.sigmoid(_dot(h, wgate_ref[:, 0:d])) * _dot(oan, wa_ref[...])
    merged += jax.nn.sigmoid(_dot(h, wgate_ref[:, d:2 * d])) * _dot(ob_ref[...], wb_ref[...])
    merged += jax.nn.sigmoid(_dot(h, wgate_ref[:, 2 * d:3 * d])) * _dot(oc_ref[...], wc_ref[...])
    o_ref[...] = x + _dot(merged.astype(BF16), wout_ref[...])


def _merge(x, oa, ob, oc, g, wgate, sub, wa, wb, wc, wout, tm):
    n, d = x.shape
    full = lambda a: pl.BlockSpec(a.shape, lambda i: (0,) * a.ndim)
    rows = lambda a: pl.BlockSpec((tm, a.shape[1]), lambda i: (i, 0))
    g = g.reshape(1, d)
    return pl.pallas_call(
        _merge_kernel,
        grid=(n // tm,),
        in_specs=[rows(x), rows(oa), rows(ob), rows(oc), full(g), full(wgate), full(sub),
                  full(wa), full(wb), full(wc), full(wout)],
        out_specs=rows(x),
        out_shape=jax.ShapeDtypeStruct((n, d), F32),
        compiler_params=_cparams(("parallel",)),
        name="merge",
    )(x, oa, ob, oc, g, wgate, sub, wa, wb, wc, wout)


def _cross_kernel(x_ref, g_ref, wq_ref, gain_ref, gmat_ref, mk_ref, mv_ref, wo_ref, o_ref):
    x = x_ref[0]
    h = _rms_rows(x, g_ref[...]).astype(BF16)
    q = (_group_rms(_dot(h, wq_ref[...]), gmat_ref[...]) * gain_ref[...]).astype(BF16)
    mk = mk_ref[0]
    mv = mv_ref[0]
    lane = lax.broadcasted_iota(jnp.int32, q.shape, 1)
    o = jnp.zeros(q.shape, F32)
    for hh in range(q.shape[1] // HEAD_DIM):
        in_head = (lane >= hh * HEAD_DIM) & (lane < (hh + 1) * HEAD_DIM)
        s = _dot_nt(_keep_lanes(q, hh * HEAD_DIM, (hh + 1) * HEAD_DIM), mk)
        p = jnp.exp(s - jnp.max(s, axis=1, keepdims=True))
        l = jnp.sum(p, axis=1, keepdims=True)
        o = jnp.where(in_head, _dot(p.astype(BF16), mv) / l, o)
    o_ref[0] = x + _dot(o.astype(BF16), wo_ref[...])


def _cross(x, g, wq, gain, mk, mv, wo, tm):
    b, s, d = x.shape
    full = lambda a: pl.BlockSpec(a.shape, lambda bi, i: (0,) * a.ndim)
    g = g.reshape(1, d)
    gmat = _group_mean_matrix()
    return pl.pallas_call(
        _cross_kernel,
        grid=(b, s // tm),
        in_specs=[
            pl.BlockSpec((1, tm, d), lambda bi, i: (bi, i, 0)),
            full(g), full(wq), full(gain), full(gmat),
            pl.BlockSpec((1,) + mk.shape[1:], lambda bi, i: (bi, 0, 0)),
            pl.BlockSpec((1,) + mv.shape[1:], lambda bi, i: (bi, 0, 0)),
            full(wo),
        ],
        out_specs=pl.BlockSpec((1, tm, d), lambda bi, i: (bi, i, 0)),
        out_shape=jax.ShapeDtypeStruct((b, s, d), F32),
        compiler_params=_cparams(("parallel", "parallel")),
        name="cross",
    )(x, g, wq, gain, gmat, mk, mv, wo)


def _sample_attn_kernel(lam_ref,
                        qa_ref, kan_ref, van_ref, kac_ref, vac_ref, bac_ref, ban_ref,
                        qb_ref, kbn_ref, vbn_ref, kbc_ref, vbc_ref, ubig_ref, usmall_ref,
                        qc_ref, kcn_ref, vcn_ref, kcc_ref, vcc_ref, bcc_ref, bcn_ref,
                        oa_ref, ob_ref, oc_ref):
    lam = lam_ref[0]
    ns = qa_ref.shape[1]
    bf = lambda r: r[0].astype(BF16)

    def heads(q, width):
        for hh in range(LANES // width):
            yield hh, None, _keep_lanes(q, hh * width, (hh + 1) * width)

    def softmax2(s_c, s_n, exp=jnp.exp):
        m = jnp.maximum(jnp.max(s_c, axis=1, keepdims=True), jnp.max(s_n, axis=1, keepdims=True))
        p_c = exp(s_c - m)
        p_n = exp(s_n - m)
        inv = 1.0 / (jnp.sum(p_c, axis=1, keepdims=True) + jnp.sum(p_n, axis=1, keepdims=True))
        return p_c * inv, p_n * inv

    for h in range(qa_ref.shape[2] // LANES):
        sl = slice(h * LANES, (h + 1) * LANES)
        q = qa_ref[0, :, sl]
        k_c = kac_ref[0, :, sl].astype(BF16)
        k_n = kan_ref[0, :, sl].astype(BF16)
        maps = []
        for _, _, qm in heads(q, HEAD_DIM):
            maps.append(softmax2(_dot_nt(qm, k_c) + bac_ref[h], _dot_nt(qm, k_n) + ban_ref[h], exp=jnp.exp2))
        a_c = (maps[0][0] - lam * maps[1][0]).astype(BF16)
        a_n = (maps[0][1] - lam * maps[1][1]).astype(BF16)
        oa_ref[0, :, sl] = (_dot(a_c, vac_ref[0, :, sl].astype(BF16))
                            + _dot(a_n, van_ref[0, :, sl].astype(BF16)))

    past = kbc_ref.shape[1]
    tk = ubig_ref.shape[0]
    row = lax.broadcasted_iota(jnp.int32, (ns, ns), 0)
    col = lax.broadcasted_iota(jnp.int32, (ns, ns), 1)
    for pr in range(qb_ref.shape[2] // LANES):
        sl = slice(pr * LANES, (pr + 1) * LANES)
        q = qb_ref[0, :, sl]
        k_n = kbn_ref[0, :, sl].astype(BF16)
        v_n = vbn_ref[0, :, sl].astype(BF16)
        outs = []
        for _, _, qm in heads(q, HEAD_DIM):
            w, carry = _stick_block(_dot_nt(qm, k_n), col < row, usmall_ref[...], 0.0)
            acc = _dot(w.astype(BF16), v_n)
            for j in range(past // tk - 1, -1, -1):
                k_c = kbc_ref[0, j * tk:(j + 1) * tk, sl].astype(BF16)
                v_c = vbc_ref[0, j * tk:(j + 1) * tk, sl].astype(BF16)
                w, rs = _stick_block(_dot_nt(qm, k_c), None, ubig_ref[...], carry)
                acc += _dot(w.astype(BF16), v_c)
                carry = carry + rs
            outs.append(acc)
        lane = lax.broadcasted_iota(jnp.int32, q.shape, 1)
        ob_ref[0, :, sl] = jnp.where(lane < HEAD_DIM, outs[0], outs[1]).astype(ob_ref.dtype)

    for pr in range(qc_ref.shape[2] // LANES):
        sl = slice(pr * LANES, (pr + 1) * LANES)
        q = qc_ref[0, :, sl]
        k_c = kcc_ref[0, :, sl].astype(BF16)
        k_n = kcn_ref[0, :, sl].astype(BF16)
        v_c = vcc_ref[0, :, sl].astype(BF16)
        v_n = vcn_ref[0, :, sl].astype(BF16)
        outs = []
        for hh, _, qm in heads(q, HEAD_DIM):
            hd = pr * (LANES // HEAD_DIM) + hh
            p_c, p_n = softmax2(_dot_nt(qm, k_c) + bcc_ref[hd], _dot_nt(qm, k_n) + bcn_ref[hd])
            outs.append(_dot(p_c.astype(BF16), v_c) + _dot(p_n.astype(BF16), v_n))
        lane = lax.broadcasted_iota(jnp.int32, q.shape, 1)
        oc_ref[0, :, sl] = jnp.where(lane < HEAD_DIM, outs[0], outs[1]).astype(oc_ref.dtype)


def _sample_attn(lam, qa, ka, va, cak, cav, bac, ban, qb, kb, vb, cbk, cbv,
                 qc, kc, vc, cck, ccv, bcc, bcn):
    b, ns, _ = qa.shape
    per_b = lambda a: pl.BlockSpec((1,) + a.shape[1:], lambda bi: (bi,) + (0,) * (a.ndim - 1))
    full = lambda a: pl.BlockSpec(a.shape, lambda bi: (0,) * a.ndim)
    ubig = _strict_lower(ATTN_TK)
    usmall = _strict_lower(ns)
    args = [qa, ka, va, cak, cav, bac, ban, qb, kb, vb, cbk, cbv, ubig, usmall,
            qc, kc, vc, cck, ccv, bcc, bcn]
    specs = [per_b(qa), per_b(ka), per_b(va), per_b(cak), per_b(cav), full(bac), full(ban),
             per_b(qb), per_b(kb), per_b(vb), per_b(cbk), per_b(cbv), full(ubig), full(usmall),
             per_b(qc), per_b(kc), per_b(vc), per_b(cck), per_b(ccv), full(bcc), full(bcn)]
    return pl.pallas_call(
        _sample_attn_kernel,
        grid=(b,),
        in_specs=[pl.BlockSpec(memory_space=pltpu.SMEM)] + specs,
        out_specs=[per_b(qa), per_b(qb), per_b(qc)],
        out_shape=[jax.ShapeDtypeStruct(qa.shape, F32), jax.ShapeDtypeStruct(qb.shape, BF16),
                   jax.ShapeDtypeStruct(qc.shape, BF16)],
        compiler_params=_cparams(("parallel",)),
        name="sample_mixers",
    )(lam, *args)


def _t5_bucket_np(rel):
    half = T5_BUCKETS // 2
    max_exact = half // 2
    n = np.abs(rel)
    nf = np.maximum(n, 1).astype(np.float64)
    large = max_exact + (np.log(nf / max_exact) / math.log(T5_MAX_DIST / max_exact)
                         * (half - max_exact)).astype(np.int64)
    large = np.minimum(large, half - 1)
    return np.where(rel > 0, half, 0) + np.where(n < max_exact, n, large)


def _toeplitz(lookup, n_rows, n_cols):
    period = n_rows + n_cols
    slot = np.arange(period)
    diff = np.minimum((slot + n_rows - 1) % period - (n_rows - 1), n_cols - 1)
    vec = lookup(diff).astype(F32)
    flat = jnp.tile(vec, (1, n_rows))[:, :n_rows * (period - 1)]
    return flat.reshape(vec.shape[0], n_rows, period - 1)[:, :, :n_cols]


def _t5_bias_table(t5_bias, qpos, kpos, key_major=False):
    t5_rows = lambda rel: t5_bias[_t5_bucket_np(rel)].T
    mask = (kpos[None, :] // CHUNK) <= (qpos[:, None] // CHUNK)
    if key_major:
        table = _toeplitz(lambda dd: t5_rows(kpos[0] - qpos[0] - dd), len(kpos), len(qpos))
        mask = mask.T
    else:
        table = _toeplitz(lambda dd: t5_rows(kpos[0] - qpos[0] + dd), len(qpos), len(kpos))
    return jnp.where(jnp.asarray(mask)[None], table, NEG)


def _band_bias_table(rel_table, qpos, kpos):
    lookup = lambda dd: rel_table[:, np.clip(kpos[0] - qpos[0] + dd, -REL_CLIP, REL_CLIP) + REL_CLIP]
    table = _toeplitz(lookup, len(qpos), len(kpos))
    qc = qpos[:, None] // CHUNK
    kc = kpos[None, :] // CHUNK
    mask = (kpos[None, :] >= 0) & (kc <= qc) & (kc >= qc - C_PREV)
    return jnp.where(jnp.asarray(mask)[None], table, NEG)


def _lambda_init(layer):
    return 0.8 - 0.6 * math.exp(-0.3 * layer)


def kernel(x_prompt, x_sample, mem_prompt, cache_a_k, cache_a_v, cache_b_k, cache_b_v, cache_c_k, cache_c_v, cache_mem_k, cache_mem_v, t5_bias, ffn1_norm, ffn1_wg, ffn1_wu, ffn1_wd, mix_norm, w_in, a_qnorm, a_knorm, a_lq1, a_lk1, a_lq2, a_lk2, a_subln, c_qnorm, c_knorm, c_rel_bias, w_br_a, w_br_b, w_br_c, w_out, x_norm, mem_norm, x_wq, x_wkv, x_qnorm, x_knorm, x_wo, ffn2_norm, ffn2_wg, ffn2_wu, ffn2_wd):
    bp, sp, d = x_prompt.shape
    bs, ns, _ = x_sample.shape
    depth = w_in.shape[0]
    past = cache_a_k.shape[2]
    w_buf = cache_c_k.shape[2]
    n_mem = mem_prompt.shape[1]
    wa = cache_a_k.shape[3] * cache_a_k.shape[4]
    wb = cache_b_k.shape[3] * cache_b_k.shape[4]
    wc = cache_c_k.shape[3] * cache_c_k.shape[4]
    wx = cache_mem_k.shape[3] * cache_mem_k.shape[4]
    h_a = cache_a_k.shape[3]
    n_qkv = 3 * (wa + wb + wc)
    w_keep = min(C_PREV * CHUNK, sp)
    scale = HEAD_DIM ** -0.5
    assert sp % ATTN_TQ == 0 and ATTN_TQ == ATTN_TK and ATTN_TQ == 4 * CHUNK and sp >= w_keep
    assert past % ATTN_TK == 0
    assert sp % MIXER_A_TILE == 0 and MIXER_A_TILE % CHUNK == 0 and T5_MAX_DIST <= MIXER_A_TILE

    tile = lambda g, reps: jnp.tile(g.astype(F32), reps)
    bf = lambda a: a.astype(BF16)

    loc_a = np.arange(MIXER_A_TILE)
    loc = np.arange(ATTN_TQ)
    a_far = t5_bias[T5_BUCKETS // 2 - 1].astype(F32)
    a_bias = LOG2E * jnp.stack([
        _t5_bias_table(t5_bias, loc_a + MIXER_A_TILE, loc_a + MIXER_A_TILE, key_major=True),
        _t5_bias_table(t5_bias, loc_a + MIXER_A_TILE, loc_a, key_major=True),
        jnp.broadcast_to(a_far[:, None, None], (h_a, MIXER_A_TILE, MIXER_A_TILE))], axis=1)
    t5_log2 = LOG2E * t5_bias.astype(F32)
    a_brange = jnp.stack([LOG2E * a_far, jnp.max(t5_log2, axis=0), jnp.min(t5_log2, axis=0)], axis=1).reshape(-1)
    qpos_s = past + np.arange(ns)
    a_bias_sc = LOG2E * _t5_bias_table(t5_bias, qpos_s, np.arange(past))
    a_bias_sn = LOG2E * _t5_bias_table(t5_bias, qpos_s, qpos_s)
    kpos_sb = past - w_buf + np.arange(w_buf + ns)

    xp = x_prompt.reshape(bp * sp, d)
    xs = x_sample.reshape(bs * ns, d)
    mem = mem_prompt.reshape(bp * n_mem, d)
    outs = {k: [] for k in ("ak_p", "av_p", "bk_p", "bv_p", "ck_p", "cv_p", "mk_p", "mv_p",
                            "ak_s", "av_s", "bk_s", "bv_s", "ck_s", "cv_s")}
    qkv_segs = [(wa, True, True), (wa, True, True), (wa, False, False),
                (wb, False, True), (wb, False, False), (wb, False, False),
                (wc, True, True), (wc, True, True), (wc, False, False)]
    qkv_dtypes = [BF16, F32, F32, BF16, F32, F32, BF16, F32, F32]

    for i in range(depth):
        wg1, wu1, wd1 = bf(ffn1_wg[i]), bf(ffn1_wu[i]), bf(ffn1_wd[i])
        wg2, wu2, wd2 = bf(ffn2_wg[i]), bf(ffn2_wu[i]), bf(ffn2_wd[i])
        w_qkv, w_gate = bf(w_in[i][:, :n_qkv]), bf(w_in[i][:, n_qkv:])
        qkv_gain = jnp.concatenate([
            tile(a_qnorm[i], wa // HEAD_DIM) * (scale * LOG2E), tile(a_knorm[i], wa // HEAD_DIM), jnp.ones((wa,), F32),
            jnp.full((wb,), scale, F32), jnp.ones((2 * wb,), F32),
            tile(c_qnorm[i], wc // HEAD_DIM) * scale, tile(c_knorm[i], wc // HEAD_DIM), jnp.ones((wc,), F32),
        ]).reshape(1, n_qkv)
        sub = (tile(a_subln[i], h_a) * (1.0 - _lambda_init(i))).reshape(1, wa)
        wbr = bf(w_br_a[i]), bf(w_br_b[i]), bf(w_br_c[i])
        wout = bf(w_out[i])
        wq, wo = bf(x_wq[i]), bf(x_wo[i])
        xq_gain = (tile(x_qnorm[i], wx // HEAD_DIM) * scale).reshape(1, wx)
        dot64 = lambda a, b: jnp.exp(jnp.sum(a.astype(F32) * b.astype(F32)))
        lam = (dot64(a_lq1[i], a_lk1[i]) - dot64(a_lq2[i], a_lk2[i]) + _lambda_init(i)).reshape(1)
        c_bias_p = jnp.swapaxes(_band_bias_table(c_rel_bias[i], C_PREV * CHUNK + loc, np.arange(BAND_W)), 1, 2)
        c_bias_sc = _band_bias_table(c_rel_bias[i], qpos_s, kpos_sb[:w_buf])
        c_bias_sn = _band_bias_table(c_rel_bias[i], qpos_s, kpos_sb[w_buf:])

        xp = _ffn(xp, ffn1_norm[i], wg1, wu1, wd1, tm=1024, tf=1024)
        qa, ka, va, qb, kb, vb, qc, kc, vc = _proj(xp, mix_norm[i], w_qkv, qkv_gain, qkv_segs, qkv_dtypes, tm=512)
        r3 = lambda a: a.reshape(bp, sp, a.shape[-1])
        oa = _attn_a(lam, a_brange, r3(qa), r3(ka), r3(va), a_bias)
        ob = _attn_b(r3(qb), r3(kb), r3(vb))
        oc = _attn_c(r3(qc), r3(kc), r3(vc), c_bias_p)
        xp = _merge(xp, oa.reshape(-1, wa), ob.reshape(-1, wb), oc.reshape(-1, wc), mix_norm[i], w_gate, sub,
                    *wbr, wout, tm=512)
        mk, mv = _proj(mem, mem_norm[i], bf(x_wkv[i]),
                       jnp.concatenate([tile(x_knorm[i], wx // HEAD_DIM), jnp.ones((wx,), F32)]).reshape(1, 2 * wx),
                       [(wx, True, True), (wx, False, False)], [F32, F32], tm=n_mem)
        mk3, mv3 = mk.reshape(bp, n_mem, wx), mv.reshape(bp, n_mem, wx)
        xp = _cross(xp.reshape(bp, sp, d), x_norm[i], wq, xq_gain, bf(mk3), bf(mv3), wo, tm=512).reshape(-1, d)
        xp = _ffn(xp, ffn2_norm[i], wg2, wu2, wd2, tm=1024, tf=1024)
        outs["ak_p"].append(ka.reshape(bp, sp, h_a, -1))
        outs["av_p"].append(va.reshape(bp, sp, h_a, -1))
        outs["bk_p"].append(kb.reshape(bp, sp, -1, HEAD_DIM))
        outs["bv_p"].append(vb.reshape(bp, sp, -1, HEAD_DIM))
        outs["ck_p"].append(r3(kc)[:, sp - w_keep:].reshape(bp, w_keep, -1, HEAD_DIM))
        outs["cv_p"].append(r3(vc)[:, sp - w_keep:].reshape(bp, w_keep, -1, HEAD_DIM))
        outs["mk_p"].append(mk3.reshape(bp, n_mem, -1, HEAD_DIM))
        outs["mv_p"].append(mv3.reshape(bp, n_mem, -1, HEAD_DIM))

        xs = _ffn(xs, ffn1_norm[i], wg1, wu1, wd1, tm=bs * ns)
        qa, ka, va, qb, kb, vb, qc, kc, vc = _proj(xs, mix_norm[i], w_qkv, qkv_gain, qkv_segs, qkv_dtypes, tm=bs * ns)
        s3 = lambda a: a.reshape(bs, ns, a.shape[-1])
        c3 = lambda a: a.reshape(bs, a.shape[1], -1)
        oa, ob, oc = _sample_attn(
            lam, s3(qa), s3(ka), s3(va), c3(cache_a_k[i]), c3(cache_a_v[i]), a_bias_sc, a_bias_sn,
            s3(qb), s3(kb), s3(vb), c3(cache_b_k[i]), c3(cache_b_v[i]),
            s3(qc), s3(kc), s3(vc), c3(cache_c_k[i]), c3(cache_c_v[i]), c_bias_sc, c_bias_sn)
        xs = _merge(xs, oa.reshape(-1, wa), ob.reshape(-1, wb), oc.reshape(-1, wc), mix_norm[i], w_gate, sub,
                    *wbr, wout, tm=bs * ns)
        xs = _cross(xs.reshape(bs, ns, d), x_norm[i], wq, xq_gain, bf(c3(cache_mem_k[i])), bf(c3(cache_mem_v[i])),
                    wo, tm=ns).reshape(-1, d)
        xs = _ffn(xs, ffn2_norm[i], wg2, wu2, wd2, tm=bs * ns)
        outs["ak_s"].append(ka.reshape(bs, ns, h_a, -1))
        outs["av_s"].append(va.reshape(bs, ns, h_a, -1))
        outs["bk_s"].append(kb.reshape(bs, ns, -1, HEAD_DIM))
        outs["bv_s"].append(vb.reshape(bs, ns, -1, HEAD_DIM))
        outs["ck_s"].append(jnp.concatenate([cache_c_k[i], kc.reshape(bs, ns, -1, HEAD_DIM)], axis=1)[:, ns:])
        outs["cv_s"].append(jnp.concatenate([cache_c_v[i], vc.reshape(bs, ns, -1, HEAD_DIM)], axis=1)[:, ns:])

    st = lambda k: jnp.stack(outs[k])
    return (xp.reshape(bp, sp, d), xs.reshape(bs, ns, d),
            st("ak_p"), st("av_p"), st("bk_p"), st("bv_p"), st("ck_p"), st("cv_p"), st("mk_p"), st("mv_p"),
            st("ak_s"), st("av_s"), st("bk_s"), st("bv_s"), st("ck_s"), st("cv_s"))
```

```python
import functools
import math

import numpy as np
import jax
import jax.numpy as jnp
from jax import lax
from jax.experimental import pallas as pl
from jax.experimental.pallas import tpu as pltpu

F32 = jnp.float32
BF16 = jnp.bfloat16

EPS = 1e-6
HEAD_DIM = 64
CHUNK = 64
C_PREV = 8
REL_CLIP = 128
T5_BUCKETS = 32
T5_MAX_DIST = 128
LANES = 128
BF16_SUBLANES = 16
NEG = -1e30
LOG2E = math.log2(math.e)
BOUND_SLACK = 1.001
BOUNDED_EXP2_SPAN = 100.0
STICK_SKIP = -110.0
V7X_VMEM_LIMIT_BYTES = 56 * 1024 * 1024

MIXER_A_TILE = 512
ATTN_TQ = 256
ATTN_TK = 256
BAND_W = (C_PREV + 4) * CHUNK


def _cparams(sem):
    return pltpu.CompilerParams(dimension_semantics=sem, vmem_limit_bytes=V7X_VMEM_LIMIT_BYTES)


def _rms_rows(x, g):
    return x * lax.rsqrt(jnp.mean(x * x, axis=-1, keepdims=True) + EPS) * g


def _dot(a, b):
    return jnp.dot(a, b, preferred_element_type=F32)


def _dot_nt(a, b):
    return lax.dot_general(a, b, (((1,), (1,)), ((), ())), preferred_element_type=F32)


def _keep_lanes(q, lo, hi):
    lane = lax.broadcasted_iota(jnp.int32, q.shape, 1)
    return jnp.where((lane >= lo) & (lane < hi), q.astype(F32), 0.0).astype(BF16)


def _group_rms(y, gmat):
    parts = []
    for c in range(y.shape[1] // LANES):
        yc = y[:, c * LANES:(c + 1) * LANES]
        ms = _dot((yc * yc).astype(BF16), gmat)
        parts.append(yc * lax.rsqrt(ms + EPS))
    return parts[0] if len(parts) == 1 else jnp.concatenate(parts, axis=1)


def _group_mean_matrix():
    g = np.kron(np.eye(LANES // HEAD_DIM), np.ones((HEAD_DIM, HEAD_DIM))) / HEAD_DIM
    return jnp.asarray(g, BF16)


def _ffn_kernel(x_ref, g_ref, wg_ref, wu_ref, wd_ref, o_ref, h_scr, acc_scr):
    j = pl.program_id(1)

    @pl.when(j == 0)
    def _():
        h_scr[...] = _rms_rows(x_ref[...], g_ref[...]).astype(BF16)
        acc_scr[...] = jnp.zeros_like(acc_scr)

    h = h_scr[...]
    a = _dot(h, wg_ref[...])
    u = _dot(h, wu_ref[...])
    t = a * jax.nn.sigmoid(a) * u
    acc_scr[...] += _dot(t.astype(BF16), wd_ref[...])

    @pl.when(j == pl.num_programs(1) - 1)
    def _():
        o_ref[...] = x_ref[...] + 0.5 * acc_scr[...]


def _ffn(x, g, wg, wu, wd, tm, tf=512):
    n, d = x.shape
    dff = wg.shape[1]
    return pl.pallas_call(
        _ffn_kernel,
        grid=(n // tm, dff // tf),
        in_specs=[
            pl.BlockSpec((tm, d), lambda i, j: (i, 0)),
            pl.BlockSpec((1, d), lambda i, j: (0, 0)),
            pl.BlockSpec((d, tf), lambda i, j: (0, j)),
            pl.BlockSpec((d, tf), lambda i, j: (0, j)),
            pl.BlockSpec((tf, d), lambda i, j: (j, 0)),
        ],
        out_specs=pl.BlockSpec((tm, d), lambda i, j: (i, 0)),
        out_shape=jax.ShapeDtypeStruct((n, d), F32),
        scratch_shapes=[pltpu.VMEM((tm, d), BF16), pltpu.VMEM((tm, d), F32)],
        compiler_params=_cparams(("parallel", "arbitrary")),
        name="ffn",
    )(x, g.reshape(1, d), wg, wu, wd)


def _proj_kernel(x_ref, g_ref, w_ref, gain_ref, gmat_ref, *out_refs, segs):
    h = _rms_rows(x_ref[...], g_ref[...]).astype(BF16)
    off = 0
    out_refs = list(out_refs)
    for width, normed, gained, outs in segs:
        y = _dot(h, w_ref[:, off:off + width])
        if normed:
            y = _group_rms(y, gmat_ref[...])
        if gained:
            y = y * gain_ref[:, off:off + width]
        for _, head_width in outs:
            o_ref = out_refs.pop(0)
            if head_width:
                for hd in range(width // head_width):
                    o_ref[:, hd, :] = y[:, hd * head_width:(hd + 1) * head_width].astype(o_ref.dtype)
            else:
                o_ref[...] = y.astype(o_ref.dtype)
        off += width


def _proj(x, g, w, gain, segs, tm):
    n, d = x.shape
    wtot = w.shape[1]
    specs, shapes = [], []
    for width, _, _, outs in segs:
        for dt, head_width in outs:
            if head_width:
                nh = width // head_width
                specs.append(pl.BlockSpec((tm, nh, head_width), lambda i: (i, 0, 0)))
                shapes.append(jax.ShapeDtypeStruct((n, nh, head_width), dt))
            else:
                specs.append(pl.BlockSpec((tm, width), lambda i: (i, 0)))
                shapes.append(jax.ShapeDtypeStruct((n, width), dt))
    return pl.pallas_call(
        functools.partial(_proj_kernel, segs=tuple(segs)),
        grid=(n // tm,),
        in_specs=[
            pl.BlockSpec((tm, d), lambda i: (i, 0)),
            pl.BlockSpec((1, d), lambda i: (0, 0)),
            pl.BlockSpec((d, wtot), lambda i: (0, 0)),
            pl.BlockSpec((1, wtot), lambda i: (0, 0)),
            pl.BlockSpec((LANES, LANES), lambda i: (0, 0)),
        ],
        out_specs=specs,
        out_shape=shapes,
        compiler_params=_cparams(("parallel",)),
        name="proj",
    )(x, g.reshape(1, d), w, gain, _group_mean_matrix())


def _softmax_step_km(s, vt, m_ref, acc_ref):
    m_old = m_ref[...]
    m_new = jnp.maximum(m_old, jnp.max(s, axis=0, keepdims=True))
    alpha = jnp.exp2(m_old - m_new)
    p = jnp.exp2(s - m_new)
    acc_ref[...] = alpha * acc_ref[...] + _dot(vt, p.astype(BF16))
    m_ref[...] = m_new


def _cast_rows(src_ref, dst_ref, rows, dst_off=0, step=512):
    def body(i, c):
        r = pl.multiple_of(i * step, step)
        dst_ref[pl.ds(dst_off + r, step), :] = src_ref[0, pl.ds(r, step), :].astype(BF16)
        return c
    lax.fori_loop(0, rows // step, body, 0)


def _attn_a_kernel(lam_ref, brange_ref, q_ref, k_ref, v_ref, bias_ref, o_ref,
                   vtb, s_scr, knorm, m1, a1, m2, a2):
    h = pl.program_id(1)
    qi = pl.program_id(2)
    kbf = k_ref.at[0]
    seq = kbf.shape[0]
    tq = q_ref.shape[1]
    tk = vtb.shape[2]
    dv = v_ref.shape[2]

    @pl.when(qi == 0)
    def _():
        lane = lax.broadcasted_iota(jnp.int32, (tk, LANES), 1)

        def prep(j, c):
            r = pl.multiple_of(j * tk, tk)
            kb = kbf[pl.ds(r, tk), :]
            vtb[j, 0:dv, :] = v_ref[0, pl.ds(r, tk), :].astype(F32).T.astype(BF16)
            vtb[j, dv:, :] = jnp.ones((vtb.shape[1] - dv, tk), BF16)
            ksq = kb.astype(F32) * kb.astype(F32)
            n1 = jnp.max(jnp.sum(jnp.where(lane < HEAD_DIM, ksq, 0.0), axis=1, keepdims=True))
            n2 = jnp.max(jnp.sum(jnp.where(lane >= HEAD_DIM, ksq, 0.0), axis=1, keepdims=True))
            return jnp.maximum(c[0], n1), jnp.maximum(c[1], n2)
        n1, n2 = lax.fori_loop(0, seq // tk, prep, (jnp.float32(0.0), jnp.float32(0.0)))
        knorm[0] = n1
        knorm[1] = n2

    qt = q_ref[0].astype(F32).T
    sub = lax.broadcasted_iota(jnp.int32, qt.shape, 0)
    q1t = jnp.where(sub < HEAD_DIM, qt, 0.0).astype(BF16)
    q2t = jnp.where(sub >= HEAD_DIM, qt, 0.0).astype(BF16)
    states = ((m1, a1), (m2, a2))
    for m_ref, a_ref in states:
        a_ref[...] = jnp.zeros_like(a_ref)

    far_bias, bias_max, bias_min = brange_ref[3 * h], brange_ref[3 * h + 1], brange_ref[3 * h + 2]
    qsq = qt * qt
    bound1 = jnp.sqrt(jnp.sum(jnp.where(sub < HEAD_DIM, qsq, 0.0), axis=0, keepdims=True) * knorm[0])
    bound2 = jnp.sqrt(jnp.sum(jnp.where(sub >= HEAD_DIM, qsq, 0.0), axis=0, keepdims=True) * knorm[1])
    bounds = (bound1 * BOUND_SLACK + bias_max, bound2 * BOUND_SLACK + bias_max)
    spread = 2.0 * BOUND_SLACK * jnp.maximum(jnp.max(bound1), jnp.max(bound2)) + (bias_max - bias_min)
    bounded = spread <= BOUNDED_EXP2_SPAN

    @pl.when(bounded)
    def _():
        qmaps = ((q1t, bounds[0], a1), (q2t, bounds[1], a2))

        def weights(j, adds):
            kb = kbf[pl.ds(pl.multiple_of(j * tk, tk), tk), :]
            return [jnp.exp2(_dot(kb, qmt) + add).astype(BF16) for (qmt, _, _), add in zip(qmaps, adds)]

        def special(j, tile):
            ps = weights(j, [tile - shift for _, shift, _ in qmaps])
            for p, (_, _, a_ref) in zip(ps, qmaps):
                a_ref[...] += _dot(vtb[j], p)

        special(qi, bias_ref[0, 0])

        @pl.when(qi >= 1)
        def _():
            special(qi - 1, bias_ref[0, 1])

        far_adds = [far_bias - shift for _, shift, _ in qmaps]
        n_far = jnp.maximum(qi - 1, 0)
        odd = n_far & 1

        @pl.when(odd == 1)
        def _():
            for p, (_, _, a_ref) in zip(weights(0, far_adds), qmaps):
                a_ref[...] += _dot(vtb[0], p)

        def pair(i, c):
            j = odd + 2 * i
            pa = weights(j, far_adds)
            pb = weights(j + 1, far_adds)
            for p0, p1, (_, _, a_ref) in zip(pa, pb, qmaps):
                a_ref[...] += _dot(vtb[j], p0) + _dot(vtb[j + 1], p1)
            return c
        lax.fori_loop(0, lax.shift_right_logical(n_far, 1), pair, 0)

    @pl.when(jnp.logical_not(bounded))
    def _():
        for m_ref, _ in states:
            m_ref[...] = jnp.full_like(m_ref, NEG)
        n_blocks = qi + 1

        def scores(t, buf):
            kb = kbf[pl.ds(pl.multiple_of((qi - t) * tk, tk), tk), :]
            s_scr[buf, 0] = _dot(kb, q1t)
            s_scr[buf, 1] = _dot(kb, q2t)

        def update(t, buf):
            bias = bias_ref[0, jnp.minimum(t, bias_ref.shape[1] - 1)]
            vt = vtb[qi - t]
            for mi, (m_ref, a_ref) in enumerate(states):
                _softmax_step_km(s_scr[buf, mi] + bias, vt, m_ref, a_ref)

        odd = n_blocks & 1

        @pl.when(odd == 1)
        def _():
            scores(0, 0)
            update(0, 0)

        n_pairs = lax.shift_right_logical(n_blocks, 1)

        @pl.when(n_pairs > 0)
        def _():
            scores(odd, 0)

        def pair(p, c):
            t0 = odd + 2 * p
            scores(t0 + 1, 1)
            update(t0, 0)
            scores(jnp.minimum(t0 + 2, qi), 0)
            update(t0 + 1, 1)
            return c
        lax.fori_loop(0, n_pairs, pair, 0)

    out_t = (a1[0:dv, :] / a1[dv:dv + 1, :]
             - lam_ref[0] * (a2[0:dv, :] / a2[dv:dv + 1, :]))
    o_ref[0] = out_t.T


def _attn_a(lam, brange, qa, ka, va, bias):
    b, s, w = qa.shape
    nh = w // LANES
    tq = tk = MIXER_A_TILE
    vrows = LANES + BF16_SUBLANES
    return pl.pallas_call(
        _attn_a_kernel,
        grid=(b, nh, s // tq),
        in_specs=[
            pl.BlockSpec(memory_space=pltpu.SMEM),
            pl.BlockSpec(memory_space=pltpu.SMEM),
            pl.BlockSpec((1, tq, LANES), lambda bi, h, qi: (bi, qi, h)),
            pl.BlockSpec((1, s, LANES), lambda bi, h, qi: (bi, 0, h)),
            pl.BlockSpec((1, s, LANES), lambda bi, h, qi: (bi, 0, h)),
            pl.BlockSpec((1,) + bias.shape[1:], lambda bi, h, qi: (h, 0, 0, 0)),
        ],
        out_specs=pl.BlockSpec((1, tq, LANES), lambda bi, h, qi: (bi, qi, h)),
        out_shape=jax.ShapeDtypeStruct((b, s, w), F32),
        scratch_shapes=[
            pltpu.VMEM((s // tk, vrows, tk), BF16),
            pltpu.VMEM((2, 2, tk, tq), F32), pltpu.SMEM((2,), F32),
            pltpu.VMEM((1, tq), F32), pltpu.VMEM((vrows, tq), F32),
            pltpu.VMEM((1, tq), F32), pltpu.VMEM((vrows, tq), F32),
        ],
        compiler_params=_cparams(("parallel", "parallel", "arbitrary")),
        name="mixer_a",
    )(lam, brange, qa, ka, va, bias)


def _stick_block(z, valid, umat, carry):
    sp = jnp.maximum(z, 0.0) + jnp.log(1.0 + jnp.exp(-jnp.abs(z)))
    log1m = -sp
    if valid is not None:
        log1m = jnp.where(valid, log1m, 0.0)
    hi = log1m.astype(BF16)
    lo = (log1m - hi.astype(F32)).astype(BF16)
    after = _dot(hi, umat) + _dot(lo, umat)
    w = jnp.exp(z - sp + after + carry)
    if valid is not None:
        w = jnp.where(valid, w, 0.0)
    return w, jnp.sum(log1m, axis=1, keepdims=True)


def _stick_block_km(z, valid, umat, carry):
    sp = jnp.maximum(z, 0.0) + jnp.log(1.0 + jnp.exp(-jnp.abs(z)))
    log1m = -sp
    if valid is not None:
        log1m = jnp.where(valid, log1m, 0.0)
    hi = log1m.astype(BF16)
    lo = (log1m - hi.astype(F32)).astype(BF16)
    after = _dot(umat, hi) + _dot(umat, lo)
    w = jnp.exp(z - sp + after + carry)
    if valid is not None:
        w = jnp.where(valid, w, 0.0)
    return w, jnp.sum(log1m, axis=0, keepdims=True)


def _attn_b_kernel(q_ref, k_ref, v_ref, u_ref, o_ref, kbf, vtb, c_scr, acc_scr):
    qi = pl.program_id(2)
    seq = kbf.shape[0]
    tq = q_ref.shape[1]
    tk = vtb.shape[2]
    n_heads = LANES // HEAD_DIM

    @pl.when(qi == 0)
    def _():
        def prep(j, c):
            r = pl.multiple_of(j * tk, tk)
            kbf[pl.ds(r, tk), :] = k_ref[0, pl.ds(r, tk), :].astype(BF16)
            vtb[j] = v_ref[0, pl.ds(r, tk), :].T.astype(BF16)
            return c
        lax.fori_loop(0, seq // tk, prep, 0)

    qt = q_ref[0].astype(F32).T
    sub = lax.broadcasted_iota(jnp.int32, qt.shape, 0)
    qts = [jnp.where((sub >= hh * HEAD_DIM) & (sub < (hh + 1) * HEAD_DIM), qt, 0.0).astype(BF16)
           for hh in range(n_heads)]
    krow = lax.broadcasted_iota(jnp.int32, (tk, tq), 0)
    qcol = lax.broadcasted_iota(jnp.int32, (tk, tq), 1)
    umat = u_ref[...]

    def blocks(js, valids, first):
        kbs = [kbf[pl.ds(pl.multiple_of(j * tk, tk), tk), :] for j in js]
        cmax = None
        for hh in range(n_heads):
            carry = 0.0 if first else c_scr[hh]
            pv = None
            for j, kb, valid in zip(js, kbs, valids):
                w, cs = _stick_block_km(_dot(kb, qts[hh]), valid, umat, carry)
                term = _dot(vtb[j], w.astype(BF16))
                pv = term if pv is None else pv + term
                carry = carry + cs
            acc_scr[hh] = pv if first else acc_scr[hh] + pv
            c_scr[hh] = carry
            cm = jnp.max(carry)
            cmax = cm if cmax is None else jnp.maximum(cmax, cm)
        return cmax

    own_valid = krow < qcol

    @pl.when(qi == 0)
    def _():
        blocks([qi], [own_valid], True)

    @pl.when(qi >= 1)
    def _():
        cmax0 = blocks([qi, qi - 1], [own_valid, None], True)

        def cond(st):
            j, cmax = st
            return (j >= 0) & (cmax > STICK_SKIP)

        def body(st):
            j, _ = st
            return j - 2, blocks([j, jnp.maximum(j - 1, 0)], [None, (krow >= 0) & (j >= 1)], False)

        lax.while_loop(cond, body, (qi - 2, cmax0))
    out_t = jnp.where(sub < HEAD_DIM, acc_scr[0], acc_scr[1])
    o_ref[0] = out_t.T.astype(o_ref.dtype)


def _strict_lower(n):
    return jnp.asarray(np.tril(np.ones((n, n)), -1), BF16)


def _strict_upper(n):
    return jnp.asarray(np.triu(np.ones((n, n)), 1), BF16)


def _attn_b(qb, kb, vb):
    b, s, w = qb.shape
    tq = ATTN_TQ
    return pl.pallas_call(
        _attn_b_kernel,
        grid=(b, w // LANES, s // tq),
        in_specs=[
            pl.BlockSpec((1, tq, LANES), lambda bi, h, qi: (bi, qi, h)),
            pl.BlockSpec((1, s, LANES), lambda bi, h, qi: (bi, 0, h)),
            pl.BlockSpec((1, s, LANES), lambda bi, h, qi: (bi, 0, h)),
            pl.BlockSpec((ATTN_TK, ATTN_TK), lambda bi, h, qi: (0, 0)),
        ],
        out_specs=pl.BlockSpec((1, tq, LANES), lambda bi, h, qi: (bi, qi, h)),
        out_shape=jax.ShapeDtypeStruct((b, s, w), BF16),
        scratch_shapes=[
            pltpu.VMEM((s, LANES), BF16), pltpu.VMEM((s // ATTN_TK, LANES, ATTN_TK), BF16),
            pltpu.VMEM((LANES // HEAD_DIM, 1, tq), F32), pltpu.VMEM((LANES // HEAD_DIM, LANES, tq), F32),
        ],
        compiler_params=_cparams(("parallel", "parallel", "arbitrary")),
        name="mixer_b",
    )(qb, kb, vb, _strict_upper(ATTN_TK))


def _attn_c_kernel(q_ref, k_ref, v_ref, bias_ref, o_ref, kbf, vbf):
    qi = pl.program_id(2)
    seq = k_ref.shape[1]
    tq = q_ref.shape[1]
    pad = C_PREV * CHUNK

    @pl.when(qi == 0)
    def _():
        kbf[0:pad, :] = jnp.zeros((pad, LANES), BF16)
        vbf[0:pad, :] = jnp.zeros((pad, LANES), BF16)
        _cast_rows(k_ref, kbf, seq, dst_off=pad)
        _cast_rows(v_ref, vbf, seq, dst_off=pad)

    q = q_ref[0]
    lane = lax.broadcasted_iota(jnp.int32, q.shape, 1)
    wstart = pl.multiple_of(qi * tq, tq)
    kw = kbf[pl.ds(wstart, BAND_W), :]
    vw = vbf[pl.ds(wstart, BAND_W), :]
    col = lax.broadcasted_iota(jnp.int32, (tq, BAND_W), 1)
    in_seq = col >= pad - qi * tq
    outs = []
    for hh in range(LANES // HEAD_DIM):
        qm = _keep_lanes(q, hh * HEAD_DIM, (hh + 1) * HEAD_DIM)
        s = jnp.where(in_seq, _dot_nt(qm, kw) + bias_ref[hh], NEG)
        m = jnp.max(s, axis=1, keepdims=True)
        p = jnp.exp(s - m)
        l = jnp.sum(p, axis=1, keepdims=True)
        outs.append(_dot(p.astype(BF16), vw) / l)
    o_ref[0] = jnp.where(lane < HEAD_DIM, outs[0], outs[1]).astype(o_ref.dtype)


def _attn_c(qc, kc, vc, bias):
    b, s, w = qc.shape
    tq = ATTN_TQ
    pad = C_PREV * CHUNK
    hp = LANES // HEAD_DIM
    return pl.pallas_call(
        _attn_c_kernel,
        grid=(b, w // LANES, s // tq),
        in_specs=[
            pl.BlockSpec((1, tq, LANES), lambda bi, h, qi: (bi, qi, h)),
            pl.BlockSpec((1, s, LANES), lambda bi, h, qi: (bi, 0, h)),
            pl.BlockSpec((1, s, LANES), lambda bi, h, qi: (bi, 0, h)),
            pl.BlockSpec((hp, tq, BAND_W), lambda bi, h, qi: (h, 0, 0)),
        ],
        out_specs=pl.BlockSpec((1, tq, LANES), lambda bi, h, qi: (bi, qi, h)),
        out_shape=jax.ShapeDtypeStruct((b, s, w), BF16),
        scratch_shapes=[pltpu.VMEM((s + pad, LANES), BF16), pltpu.VMEM((s + pad, LANES), BF16)],
        compiler_params=_cparams(("parallel", "parallel", "arbitrary")),
        name="mixer_c",
    )(qc, kc, vc, bias)


def _merge_kernel(x_ref, oa_ref, ob_ref, oc_ref, g_ref, wgate_ref, sub_ref,
                  wa_ref, wb_ref, wc_ref, wout_ref, o_ref):
    x = x_ref[...]
    d = x.shape[1]
    h = _rms_rows(x, g_ref[...]).astype(BF16)
    oa = oa_ref[...]
    parts = []
    for c in range(oa.shape[1] // LANES):
        oc_ = oa[:, c * LANES:(c + 1) * LANES]
        parts.append(oc_ * lax.rsqrt(jnp.mean(oc_ * oc_, axis=-1, keepdims=True) + EPS))
    oan = (jnp.concatenate(parts, axis=1) * sub_ref[...]).astype(BF16)
    merged = jax.nn.sigmoid(_dot(h, wgate_ref[:, 0:d])) * _dot(oan, wa_ref[...])
    merged += jax.nn.sigmoid(_dot(h, wgate_ref[:, d:2 * d])) * _dot(ob_ref[...], wb_ref[...])
    merged += jax.nn.sigmoid(_dot(h, wgate_ref[:, 2 * d:3 * d])) * _dot(oc_ref[...], wc_ref[...])
    o_ref[...] = x + _dot(merged.astype(BF16), wout_ref[...])


def _merge(x, oa, ob, oc, g, wgate, sub, wa, wb, wc, wout, tm):
    n, d = x.shape
    full = lambda a: pl.BlockSpec(a.shape, lambda i: (0,) * a.ndim)
    rows = lambda a: pl.BlockSpec((tm, a.shape[1]), lambda i: (i, 0))
    g = g.reshape(1, d)
    return pl.pallas_call(
        _merge_kernel,
        grid=(n // tm,),
        in_specs=[rows(x), rows(oa), rows(ob), rows(oc), full(g), full(wgate), full(sub),
                  full(wa), full(wb), full(wc), full(wout)],
        out_specs=rows(x),
        out_shape=jax.ShapeDtypeStruct((n, d), F32),
        compiler_params=_cparams(("parallel",)),
        name="merge",
    )(x, oa, ob, oc, g, wgate, sub, wa, wb, wc, wout)


def _cross_kernel(x_ref, g_ref, wq_ref, gain_ref, gmat_ref, mk_ref, mv_ref, wo_ref, o_ref):
    x = x_ref[0]
    h = _rms_rows(x, g_ref[...]).astype(BF16)
    q = (_group_rms(_dot(h, wq_ref[...]), gmat_ref[...]) * gain_ref[...]).astype(BF16)
    mk = mk_ref[0]
    mv = mv_ref[0]
    lane = lax.broadcasted_iota(jnp.int32, q.shape, 1)
    o = jnp.zeros(q.shape, F32)
    for hh in range(q.shape[1] // HEAD_DIM):
        in_head = (lane >= hh * HEAD_DIM) & (lane < (hh + 1) * HEAD_DIM)
        s = _dot_nt(_keep_lanes(q, hh * HEAD_DIM, (hh + 1) * HEAD_DIM), mk)
        p = jnp.exp(s - jnp.max(s, axis=1, keepdims=True))
        l = jnp.sum(p, axis=1, keepdims=True)
        o = jnp.where(in_head, _dot(p.astype(BF16), mv) / l, o)
    o_ref[0] = x + _dot(o.astype(BF16), wo_ref[...])


def _cross(x, g, wq, gain, mk, mv, wo, tm):
    b, s, d = x.shape
    full = lambda a: pl.BlockSpec(a.shape, lambda bi, i: (0,) * a.ndim)
    g = g.reshape(1, d)
    gmat = _group_mean_matrix()
    return pl.pallas_call(
        _cross_kernel,
        grid=(b, s // tm),
        in_specs=[
            pl.BlockSpec((1, tm, d), lambda bi, i: (bi, i, 0)),
            full(g), full(wq), full(gain), full(gmat),
            pl.BlockSpec((1,) + mk.shape[1:], lambda bi, i: (bi, 0, 0)),
            pl.BlockSpec((1,) + mv.shape[1:], lambda bi, i: (bi, 0, 0)),
            full(wo),
        ],
        out_specs=pl.BlockSpec((1, tm, d), lambda bi, i: (bi, i, 0)),
        out_shape=jax.ShapeDtypeStruct((b, s, d), F32),
        compiler_params=_cparams(("parallel", "parallel")),
        name="cross",
    )(x, g, wq, gain, gmat, mk, mv, wo)


def _sample_attn_kernel(lam_ref,
                        qa_ref, kan_ref, van_ref, kac_ref, vac_ref, bac_ref, ban_ref,
                        qb_ref, kbn_ref, vbn_ref, kbc_ref, vbc_ref, ubig_ref, usmall_ref,
                        qc_ref, kcn_ref, vcn_ref, kcc_ref, vcc_ref, bcc_ref, bcn_ref,
                        oa_ref, ob_ref, oc_ref):
    lam = lam_ref[0]
    ns = qa_ref.shape[1]
    bf = lambda r: r[0].astype(BF16)

    def heads(q, width):
        for hh in range(LANES // width):
            yield hh, None, _keep_lanes(q, hh * width, (hh + 1) * width)

    def softmax2(s_c, s_n, exp=jnp.exp):
        m = jnp.maximum(jnp.max(s_c, axis=1, keepdims=True), jnp.max(s_n, axis=1, keepdims=True))
        p_c = exp(s_c - m)
        p_n = exp(s_n - m)
        inv = 1.0 / (jnp.sum(p_c, axis=1, keepdims=True) + jnp.sum(p_n, axis=1, keepdims=True))
        return p_c * inv, p_n * inv

    for h in range(qa_ref.shape[2] // LANES):
        sl = slice(h * LANES, (h + 1) * LANES)
        q = qa_ref[0, :, sl]
        k_c = kac_ref[0, :, sl].astype(BF16)
        k_n = kan_ref[0, :, sl].astype(BF16)
        maps = []
        for _, _, qm in heads(q, HEAD_DIM):
            maps.append(softmax2(_dot_nt(qm, k_c) + bac_ref[h], _dot_nt(qm, k_n) + ban_ref[h], exp=jnp.exp2))
        a_c = (maps[0][0] - lam * maps[1][0]).astype(BF16)
        a_n = (maps[0][1] - lam * maps[1][1]).astype(BF16)
        oa_ref[0, :, sl] = (_dot(a_c, vac_ref[0, :, sl].astype(BF16))
                            + _dot(a_n, van_ref[0, :, sl].astype(BF16)))

    past = kbc_ref.shape[1]
    tk = ubig_ref.shape[0]
    row = lax.broadcasted_iota(jnp.int32, (ns, ns), 0)
    col = lax.broadcasted_iota(jnp.int32, (ns, ns), 1)
    for pr in range(qb_ref.shape[2] // LANES):
        sl = slice(pr * LANES, (pr + 1) * LANES)
        q = qb_ref[0, :, sl]
        k_n = kbn_ref[0, :, sl].astype(BF16)
        v_n = vbn_ref[0, :, sl].astype(BF16)
        outs = []
        for _, _, qm in heads(q, HEAD_DIM):
            w, carry = _stick_block(_dot_nt(qm, k_n), col < row, usmall_ref[...], 0.0)
            acc = _dot(w.astype(BF16), v_n)
            for j in range(past // tk - 1, -1, -1):
                k_c = kbc_ref[0, j * tk:(j + 1) * tk, sl].astype(BF16)
                v_c = vbc_ref[0, j * tk:(j + 1) * tk, sl].astype(BF16)
                w, rs = _stick_block(_dot_nt(qm, k_c), None, ubig_ref[...], carry)
                acc += _dot(w.astype(BF16), v_c)
                carry = carry + rs
            outs.append(acc)
        lane = lax.broadcasted_iota(jnp.int32, q.shape, 1)
        ob_ref[0, :, sl] = jnp.where(lane < HEAD_DIM, outs[0], outs[1]).astype(ob_ref.dtype)

    for pr in range(qc_ref.shape[2] // LANES):
        sl = slice(pr * LANES, (pr + 1) * LANES)
        q = qc_ref[0, :, sl]
        k_c = kcc_ref[0, :, sl].astype(BF16)
        k_n = kcn_ref[0, :, sl].astype(BF16)
        v_c = vcc_ref[0, :, sl].astype(BF16)
        v_n = vcn_ref[0, :, sl].astype(BF16)
        outs = []
        for hh, _, qm in heads(q, HEAD_DIM):
            hd = pr * (LANES // HEAD_DIM) + hh
            p_c, p_n = softmax2(_dot_nt(qm, k_c) + bcc_ref[hd], _dot_nt(qm, k_n) + bcn_ref[hd])
            outs.append(_dot(p_c.astype(BF16), v_c) + _dot(p_n.astype(BF16), v_n))
        lane = lax.broadcasted_iota(jnp.int32, q.shape, 1)
        oc_ref[0, :, sl] = jnp.where(lane < HEAD_DIM, outs[0], outs[1]).astype(oc_ref.dtype)


def _sample_attn(lam, qa, ka, va, cak, cav, bac, ban, qb, kb, vb, cbk, cbv,
                 qc, kc, vc, cck, ccv, bcc, bcn):
    b, ns, _ = qa.shape
    per_b = lambda a: pl.BlockSpec((1,) + a.shape[1:], lambda bi: (bi,) + (0,) * (a.ndim - 1))
    full = lambda a: pl.BlockSpec(a.shape, lambda bi: (0,) * a.ndim)
    ubig = _strict_lower(ATTN_TK)
    usmall = _strict_lower(ns)
    args = [qa, ka, va, cak, cav, bac, ban, qb, kb, vb, cbk, cbv, ubig, usmall,
            qc, kc, vc, cck, ccv, bcc, bcn]
    specs = [per_b(qa), per_b(ka), per_b(va), per_b(cak), per_b(cav), full(bac), full(ban),
             per_b(qb), per_b(kb), per_b(vb), per_b(cbk), per_b(cbv), full(ubig), full(usmall),
             per_b(qc), per_b(kc), per_b(vc), per_b(cck), per_b(ccv), full(bcc), full(bcn)]
    return pl.pallas_call(
        _sample_attn_kernel,
        grid=(b,),
        in_specs=[pl.BlockSpec(memory_space=pltpu.SMEM)] + specs,
        out_specs=[per_b(qa), per_b(qb), per_b(qc)],
        out_shape=[jax.ShapeDtypeStruct(qa.shape, F32), jax.ShapeDtypeStruct(qb.shape, BF16),
                   jax.ShapeDtypeStruct(qc.shape, BF16)],
        compiler_params=_cparams(("parallel",)),
        name="sample_mixers",
    )(lam, *args)


def _t5_bucket_np(rel):
    half = T5_BUCKETS // 2
    max_exact = half // 2
    n = np.abs(rel)
    nf = np.maximum(n, 1).astype(np.float64)
    large = max_exact + (np.log(nf / max_exact) / math.log(T5_MAX_DIST / max_exact)
                         * (half - max_exact)).astype(np.int64)
    large = np.minimum(large, half - 1)
    return np.where(rel > 0, half, 0) + np.where(n < max_exact, n, large)


def _toeplitz(lookup, n_rows, n_cols):
    period = n_rows + n_cols
    slot = np.arange(period)
    diff = np.minimum((slot + n_rows - 1) % period - (n_rows - 1), n_cols - 1)
    vec = lookup(diff).astype(F32)
    flat = jnp.tile(vec, (1, n_rows))[:, :n_rows * (period - 1)]
    return flat.reshape(vec.shape[0], n_rows, period - 1)[:, :, :n_cols]


def _t5_bias_table(t5_bias, qpos, kpos, key_major=False):
    t5_rows = lambda rel: t5_bias[_t5_bucket_np(rel)].T
    mask = (kpos[None, :] // CHUNK) <= (qpos[:, None] // CHUNK)
    if key_major:
        table = _toeplitz(lambda dd: t5_rows(kpos[0] - qpos[0] - dd), len(kpos), len(qpos))
        mask = mask.T
    else:
        table = _toeplitz(lambda dd: t5_rows(kpos[0] - qpos[0] + dd), len(qpos), len(kpos))
    return jnp.where(jnp.asarray(mask)[None], table, NEG)


def _band_bias_table(rel_table, qpos, kpos):
    lookup = lambda dd: rel_table[:, np.clip(kpos[0] - qpos[0] + dd, -REL_CLIP, REL_CLIP) + REL_CLIP]
    table = _toeplitz(lookup, len(qpos), len(kpos))
    qc = qpos[:, None] // CHUNK
    kc = kpos[None, :] // CHUNK
    mask = (kpos[None, :] >= 0) & (kc <= qc) & (kc >= qc - C_PREV)
    return jnp.where(jnp.asarray(mask)[None], table, NEG)


def _lambda_init(layer):
    return 0.8 - 0.6 * math.exp(-0.3 * layer)


def kernel(x_prompt, x_sample, mem_prompt, cache_a_k, cache_a_v, cache_b_k, cache_b_v, cache_c_k, cache_c_v, cache_mem_k, cache_mem_v, t5_bias, ffn1_norm, ffn1_wg, ffn1_wu, ffn1_wd, mix_norm, w_in, a_qnorm, a_knorm, a_lq1, a_lk1, a_lq2, a_lk2, a_subln, c_qnorm, c_knorm, c_rel_bias, w_br_a, w_br_b, w_br_c, w_out, x_norm, mem_norm, x_wq, x_wkv, x_qnorm, x_knorm, x_wo, ffn2_norm, ffn2_wg, ffn2_wu, ffn2_wd):
    bp, sp, d = x_prompt.shape
    bs, ns, _ = x_sample.shape
    depth = w_in.shape[0]
    past = cache_a_k.shape[2]
    w_buf = cache_c_k.shape[2]
    n_mem = mem_prompt.shape[1]
    wa = cache_a_k.shape[3] * cache_a_k.shape[4]
    wb = cache_b_k.shape[3] * cache_b_k.shape[4]
    wc = cache_c_k.shape[3] * cache_c_k.shape[4]
    wx = cache_mem_k.shape[3] * cache_mem_k.shape[4]
    h_a = cache_a_k.shape[3]
    n_qkv = 3 * (wa + wb + wc)
    w_keep = min(C_PREV * CHUNK, sp)
    scale = HEAD_DIM ** -0.5
    assert sp % ATTN_TQ == 0 and ATTN_TQ == ATTN_TK and ATTN_TQ == 4 * CHUNK and sp >= w_keep
    assert past % ATTN_TK == 0
    assert sp % MIXER_A_TILE == 0 and MIXER_A_TILE % CHUNK == 0 and T5_MAX_DIST <= MIXER_A_TILE

    tile = lambda g, reps: jnp.tile(g.astype(F32), reps)
    bf = lambda a: a.astype(BF16)

    loc_a = np.arange(MIXER_A_TILE)
    loc = np.arange(ATTN_TQ)
    a_far = t5_bias[T5_BUCKETS // 2 - 1].astype(F32)
    a_bias = LOG2E * jnp.stack([
        _t5_bias_table(t5_bias, loc_a + MIXER_A_TILE, loc_a + MIXER_A_TILE, key_major=True),
        _t5_bias_table(t5_bias, loc_a + MIXER_A_TILE, loc_a, key_major=True),
        jnp.broadcast_to(a_far[:, None, None], (h_a, MIXER_A_TILE, MIXER_A_TILE))], axis=1)
    t5_log2 = LOG2E * t5_bias.astype(F32)
    a_brange = jnp.stack([LOG2E * a_far, jnp.max(t5_log2, axis=0), jnp.min(t5_log2, axis=0)], axis=1).reshape(-1)
    qpos_s = past + np.arange(ns)
    a_bias_sc = LOG2E * _t5_bias_table(t5_bias, qpos_s, np.arange(past))
    a_bias_sn = LOG2E * _t5_bias_table(t5_bias, qpos_s, qpos_s)
    kpos_sb = past - w_buf + np.arange(w_buf + ns)

    xp = x_prompt.reshape(bp * sp, d)
    xs = x_sample.reshape(bs * ns, d)
    mem = mem_prompt.reshape(bp * n_mem, d)
    outs = {k: [] for k in ("ak_p", "av_p", "bk_p", "bv_p", "ck_p", "cv_p", "mk_p", "mv_p",
                            "ak_s", "av_s", "bk_s", "bv_s", "ck_s", "cv_s")}
    flat = lambda dt: ((dt, 0),)
    dv_a = wa // h_a
    a_kv_prompt = ((F32, dv_a), (BF16, 0))
    qkv_tail = [(wb, False, True, flat(BF16)), (wb, False, False, flat(F32)), (wb, False, False, flat(F32)),
                (wc, True, True, flat(BF16)), (wc, True, True, flat(F32)), (wc, False, False, flat(F32))]
    qkv_segs_p = [(wa, True, True, flat(BF16)), (wa, True, True, a_kv_prompt),
                  (wa, False, False, a_kv_prompt)] + qkv_tail
    qkv_segs_s = [(wa, True, True, flat(BF16)), (wa, True, True, flat(F32)), (wa, False, False, flat(F32))] + qkv_tail

    for i in range(depth):
        wg1, wu1, wd1 = bf(ffn1_wg[i]), bf(ffn1_wu[i]), bf(ffn1_wd[i])
        wg2, wu2, wd2 = bf(ffn2_wg[i]), bf(ffn2_wu[i]), bf(ffn2_wd[i])
        w_qkv, w_gate = bf(w_in[i][:, :n_qkv]), bf(w_in[i][:, n_qkv:])
        qkv_gain = jnp.concatenate([
            tile(a_qnorm[i], wa // HEAD_DIM) * (scale * LOG2E), tile(a_knorm[i], wa // HEAD_DIM), jnp.ones((wa,), F32),
            jnp.full((wb,), scale, F32), jnp.ones((2 * wb,), F32),
            tile(c_qnorm[i], wc // HEAD_DIM) * scale, tile(c_knorm[i], wc // HEAD_DIM), jnp.ones((wc,), F32),
        ]).reshape(1, n_qkv)
        sub = (tile(a_subln[i], h_a) * (1.0 - _lambda_init(i))).reshape(1, wa)
        wbr = bf(w_br_a[i]), bf(w_br_b[i]), bf(w_br_c[i])
        wout = bf(w_out[i])
        wq, wo = bf(x_wq[i]), bf(x_wo[i])
        xq_gain = (tile(x_qnorm[i], wx // HEAD_DIM) * scale).reshape(1, wx)
        dot64 = lambda a, b: jnp.exp(jnp.sum(a.astype(F32) * b.astype(F32)))
        lam = (dot64(a_lq1[i], a_lk1[i]) - dot64(a_lq2[i], a_lk2[i]) + _lambda_init(i)).reshape(1)
        c_bias_p = _band_bias_table(c_rel_bias[i], C_PREV * CHUNK + loc, np.arange(BAND_W))
        c_bias_sc = _band_bias_table(c_rel_bias[i], qpos_s, kpos_sb[:w_buf])
        c_bias_sn = _band_bias_table(c_rel_bias[i], qpos_s, kpos_sb[w_buf:])

        xp = _ffn(xp, ffn1_norm[i], wg1, wu1, wd1, tm=1024, tf=1024)
        qa, ka, ka_bf, va, va_bf, qb, kb, vb, qc, kc, vc = _proj(xp, mix_norm[i], w_qkv, qkv_gain, qkv_segs_p, tm=512)
        r3 = lambda a: a.reshape(bp, sp, a.shape[-1])
        oa = _attn_a(lam, a_brange, r3(qa), r3(ka_bf), r3(va_bf), a_bias)
        ob = _attn_b(r3(qb), r3(kb), r3(vb))
        oc = _attn_c(r3(qc), r3(kc), r3(vc), c_bias_p)
        xp = _merge(xp, oa.reshape(-1, wa), ob.reshape(-1, wb), oc.reshape(-1, wc), mix_norm[i], w_gate, sub,
                    *wbr, wout, tm=512)
        mk, mv = _proj(mem, mem_norm[i], bf(x_wkv[i]),
                       jnp.concatenate([tile(x_knorm[i], wx // HEAD_DIM), jnp.ones((wx,), F32)]).reshape(1, 2 * wx),
                       [(wx, True, True, flat(F32)), (wx, False, False, flat(F32))], tm=n_mem)
        mk3, mv3 = mk.reshape(bp, n_mem, wx), mv.reshape(bp, n_mem, wx)
        xp = _cross(xp.reshape(bp, sp, d), x_norm[i], wq, xq_gain, bf(mk3), bf(mv3), wo, tm=512).reshape(-1, d)
        xp = _ffn(xp, ffn2_norm[i], wg2, wu2, wd2, tm=1024, tf=1024)
        outs["ak_p"].append(ka.reshape(bp, sp, h_a, -1))
        outs["av_p"].append(va.reshape(bp, sp, h_a, -1))
        outs["bk_p"].append(kb.reshape(bp, sp, -1, HEAD_DIM))
        outs["bv_p"].append(vb.reshape(bp, sp, -1, HEAD_DIM))
        outs["ck_p"].append(r3(kc)[:, sp - w_keep:].reshape(bp, w_keep, -1, HEAD_DIM))
        outs["cv_p"].append(r3(vc)[:, sp - w_keep:].reshape(bp, w_keep, -1, HEAD_DIM))
        outs["mk_p"].append(mk3.reshape(bp, n_mem, -1, HEAD_DIM))
        outs["mv_p"].append(mv3.reshape(bp, n_mem, -1, HEAD_DIM))

        xs = _ffn(xs, ffn1_norm[i], wg1, wu1, wd1, tm=bs * ns)
        qa, ka, va, qb, kb, vb, qc, kc, vc = _proj(xs, mix_norm[i], w_qkv, qkv_gain, qkv_segs_s, tm=bs * ns)
        s3 = lambda a: a.reshape(bs, ns, a.shape[-1])
        c3 = lambda a: a.reshape(bs, a.shape[1], -1)
        oa, ob, oc = _sample_attn(
            lam, s3(qa), s3(ka), s3(va), c3(cache_a_k[i]), c3(cache_a_v[i]), a_bias_sc, a_bias_sn,
            s3(qb), s3(kb), s3(vb), c3(cache_b_k[i]), c3(cache_b_v[i]),
            s3(qc), s3(kc), s3(vc), c3(cache_c_k[i]), c3(cache_c_v[i]), c_bias_sc, c_bias_sn)
        xs = _merge(xs, oa.reshape(-1, wa), ob.reshape(-1, wb), oc.reshape(-1, wc), mix_norm[i], w_gate, sub,
                    *wbr, wout, tm=bs * ns)
        xs = _cross(xs.reshape(bs, ns, d), x_norm[i], wq, xq_gain, bf(c3(cache_mem_k[i])), bf(c3(cache_mem_v[i])),
                    wo, tm=ns).reshape(-1, d)
        xs = _ffn(xs, ffn2_norm[i], wg2, wu2, wd2, tm=bs * ns)
        outs["ak_s"].append(ka.reshape(bs, ns, h_a, -1))
        outs["av_s"].append(va.reshape(bs, ns, h_a, -1))
        outs["bk_s"].append(kb.reshape(bs, ns, -1, HEAD_DIM))
        outs["bv_s"].append(vb.reshape(bs, ns, -1, HEAD_DIM))
        outs["ck_s"].append(jnp.concatenate([cache_c_k[i], kc.reshape(bs, ns, -1, HEAD_DIM)], axis=1)[:, ns:])
        outs["cv_s"].append(jnp.concatenate([cache_c_v[i], vc.reshape(bs, ns, -1, HEAD_DIM)], axis=1)[:, ns:])

    st = lambda k: jnp.stack(outs[k])
    return (xp.reshape(bp, sp, d), xs.reshape(bs, ns, d),
            st("ak_p"), st("av_p"), st("bk_p"), st("bv_p"), st("ck_p"), st("cv_p"), st("mk_p"), st("mv_p"),
            st("ak_s"), st("av_s"), st("bk_s"), st("bv_s"), st("ck_s"), st("cv_s"))
```

```python
import functools
import math

import numpy as np
import jax
import jax.numpy as jnp
from jax import lax
from jax.experimental import pallas as pl
from jax.experimental.pallas import tpu as pltpu

F32 = jnp.float32
BF16 = jnp.bfloat16

EPS = 1e-6
HEAD_DIM = 64
CHUNK = 64
C_PREV = 8
REL_CLIP = 128
T5_BUCKETS = 32
T5_MAX_DIST = 128
LANES = 128
BF16_SUBLANES = 16
V7X_MXU_DIM = 256
NEG = -1e30
LOG2E = math.log2(math.e)
BOUND_SLACK = 1.001
BOUNDED_EXP2_SPAN = 100.0
STICK_SKIP = -110.0
V7X_VMEM_LIMIT_BYTES = 56 * 1024 * 1024

MIXER_A_TILE = 512
ATTN_TQ = 256
ATTN_TK = 256
BAND_W = (C_PREV + 4) * CHUNK


def _cparams(sem):
    return pltpu.CompilerParams(dimension_semantics=sem, vmem_limit_bytes=V7X_VMEM_LIMIT_BYTES)


def _rms_rows(x, g):
    return x * lax.rsqrt(jnp.mean(x * x, axis=-1, keepdims=True) + EPS) * g


def _dot(a, b):
    return jnp.dot(a, b, preferred_element_type=F32)


def _dot_nt(a, b):
    return lax.dot_general(a, b, (((1,), (1,)), ((), ())), preferred_element_type=F32)


def _keep_lanes(q, lo, hi):
    lane = lax.broadcasted_iota(jnp.int32, q.shape, 1)
    return jnp.where((lane >= lo) & (lane < hi), q.astype(F32), 0.0).astype(BF16)


def _group_rms(y, gmat):
    wide = gmat.shape[0]
    parts = []
    for c in range(y.shape[1] // wide):
        yc = y[:, c * wide:(c + 1) * wide]
        ms = _dot((yc * yc).astype(BF16), gmat)
        parts.append(yc * lax.rsqrt(ms + EPS))
    return parts[0] if len(parts) == 1 else jnp.concatenate(parts, axis=1)


def _group_mean_matrix():
    g = np.kron(np.eye(V7X_MXU_DIM // HEAD_DIM), np.ones((HEAD_DIM, HEAD_DIM))) / HEAD_DIM
    return jnp.asarray(g, BF16)


def _ffn_kernel(x_ref, g_ref, wg_ref, wu_ref, wd_ref, o_ref, h_scr, acc_scr):
    j = pl.program_id(1)

    @pl.when(j == 0)
    def _():
        h_scr[...] = _rms_rows(x_ref[...], g_ref[...]).astype(BF16)
        acc_scr[...] = jnp.zeros_like(acc_scr)

    h = h_scr[...]
    a = _dot(h, wg_ref[...])
    u = _dot(h, wu_ref[...])
    t = a * jax.nn.sigmoid(a) * u
    acc_scr[...] += _dot(t.astype(BF16), wd_ref[...])

    @pl.when(j == pl.num_programs(1) - 1)
    def _():
        o_ref[...] = x_ref[...] + 0.5 * acc_scr[...]


def _ffn(x, g, wg, wu, wd, tm, tf=512):
    n, d = x.shape
    dff = wg.shape[1]
    return pl.pallas_call(
        _ffn_kernel,
        grid=(n // tm, dff // tf),
        in_specs=[
            pl.BlockSpec((tm, d), lambda i, j: (i, 0)),
            pl.BlockSpec((1, d), lambda i, j: (0, 0)),
            pl.BlockSpec((d, tf), lambda i, j: (0, j)),
            pl.BlockSpec((d, tf), lambda i, j: (0, j)),
            pl.BlockSpec((tf, d), lambda i, j: (j, 0)),
        ],
        out_specs=pl.BlockSpec((tm, d), lambda i, j: (i, 0)),
        out_shape=jax.ShapeDtypeStruct((n, d), F32),
        scratch_shapes=[pltpu.VMEM((tm, d), BF16), pltpu.VMEM((tm, d), F32)],
        compiler_params=_cparams(("parallel", "arbitrary")),
        name="ffn",
    )(x, g.reshape(1, d), wg, wu, wd)


def _proj_kernel(x_ref, g_ref, w_ref, gain_ref, gmat_ref, *out_refs, segs):
    h = _rms_rows(x_ref[...], g_ref[...]).astype(BF16)
    off = 0
    out_refs = list(out_refs)
    for width, normed, gained, outs in segs:
        y = _dot(h, w_ref[:, off:off + width])
        if normed:
            y = _group_rms(y, gmat_ref[...])
        if gained:
            y = y * gain_ref[:, off:off + width]
        for _, head_width in outs:
            o_ref = out_refs.pop(0)
            if head_width:
                for hd in range(width // head_width):
                    o_ref[:, hd, :] = y[:, hd * head_width:(hd + 1) * head_width].astype(o_ref.dtype)
            else:
                o_ref[...] = y.astype(o_ref.dtype)
        off += width


def _proj(x, g, w, gain, segs, tm):
    n, d = x.shape
    wtot = w.shape[1]
    specs, shapes = [], []
    for width, _, _, outs in segs:
        for dt, head_width in outs:
            if head_width:
                nh = width // head_width
                specs.append(pl.BlockSpec((tm, nh, head_width), lambda i: (i, 0, 0)))
                shapes.append(jax.ShapeDtypeStruct((n, nh, head_width), dt))
            else:
                specs.append(pl.BlockSpec((tm, width), lambda i: (i, 0)))
                shapes.append(jax.ShapeDtypeStruct((n, width), dt))
    return pl.pallas_call(
        functools.partial(_proj_kernel, segs=tuple(segs)),
        grid=(n // tm,),
        in_specs=[
            pl.BlockSpec((tm, d), lambda i: (i, 0)),
            pl.BlockSpec((1, d), lambda i: (0, 0)),
            pl.BlockSpec((d, wtot), lambda i: (0, 0)),
            pl.BlockSpec((1, wtot), lambda i: (0, 0)),
            pl.BlockSpec((V7X_MXU_DIM, V7X_MXU_DIM), lambda i: (0, 0)),
        ],
        out_specs=specs,
        out_shape=shapes,
        compiler_params=_cparams(("parallel",)),
        name="proj",
    )(x, g.reshape(1, d), w, gain, _group_mean_matrix())


def _softmax_step_km(s, vt, m_ref, acc_ref):
    m_old = m_ref[...]
    m_new = jnp.maximum(m_old, jnp.max(s, axis=0, keepdims=True))
    alpha = jnp.exp2(m_old - m_new)
    p = jnp.exp2(s - m_new)
    acc_ref[...] = alpha * acc_ref[...] + _dot(vt, p.astype(BF16))
    m_ref[...] = m_new


def _cast_rows(src_ref, dst_ref, rows, dst_off=0, step=512):
    def body(i, c):
        r = pl.multiple_of(i * step, step)
        dst_ref[pl.ds(dst_off + r, step), :] = src_ref[0, pl.ds(r, step), :].astype(BF16)
        return c
    lax.fori_loop(0, rows // step, body, 0)


def _attn_a_kernel(lam_ref, brange_ref, q_ref, k_ref, v_ref, bias_ref, o_ref,
                   vtb, s_scr, knorm, m1, a1, m2, a2):
    h = pl.program_id(1)
    qi = pl.program_id(2)
    kbf = k_ref.at[0]
    seq = kbf.shape[0]
    tq = q_ref.shape[1]
    tk = vtb.shape[2]
    dv = v_ref.shape[2]

    @pl.when(qi == 0)
    def _():
        lane = lax.broadcasted_iota(jnp.int32, (tk, LANES), 1)

        def prep(j, c):
            r = pl.multiple_of(j * tk, tk)
            kb = kbf[pl.ds(r, tk), :]
            vtb[j, 0:dv, :] = v_ref[0, pl.ds(r, tk), :].astype(F32).T.astype(BF16)
            vtb[j, dv:, :] = jnp.ones((vtb.shape[1] - dv, tk), BF16)
            ksq = kb.astype(F32) * kb.astype(F32)
            n1 = jnp.max(jnp.sum(jnp.where(lane < HEAD_DIM, ksq, 0.0), axis=1, keepdims=True))
            n2 = jnp.max(jnp.sum(jnp.where(lane >= HEAD_DIM, ksq, 0.0), axis=1, keepdims=True))
            return jnp.maximum(c[0], n1), jnp.maximum(c[1], n2)
        n1, n2 = lax.fori_loop(0, seq // tk, prep, (jnp.float32(0.0), jnp.float32(0.0)))
        knorm[0] = n1
        knorm[1] = n2

    qt = q_ref[0].astype(F32).T
    sub = lax.broadcasted_iota(jnp.int32, qt.shape, 0)
    q1t = jnp.where(sub < HEAD_DIM, qt, 0.0).astype(BF16)
    q2t = jnp.where(sub >= HEAD_DIM, qt, 0.0).astype(BF16)
    states = ((m1, a1), (m2, a2))
    for m_ref, a_ref in states:
        a_ref[...] = jnp.zeros_like(a_ref)

    far_bias, bias_max, bias_min = brange_ref[3 * h], brange_ref[3 * h + 1], brange_ref[3 * h + 2]
    qsq = qt * qt
    bound1 = jnp.sqrt(jnp.sum(jnp.where(sub < HEAD_DIM, qsq, 0.0), axis=0, keepdims=True) * knorm[0])
    bound2 = jnp.sqrt(jnp.sum(jnp.where(sub >= HEAD_DIM, qsq, 0.0), axis=0, keepdims=True) * knorm[1])
    bounds = (bound1 * BOUND_SLACK + bias_max, bound2 * BOUND_SLACK + bias_max)
    spread = 2.0 * BOUND_SLACK * jnp.maximum(jnp.max(bound1), jnp.max(bound2)) + (bias_max - bias_min)
    bounded = spread <= BOUNDED_EXP2_SPAN

    @pl.when(bounded)
    def _():
        qmaps = ((q1t, bounds[0], a1), (q2t, bounds[1], a2))

        def weights(j, adds):
            kb = kbf[pl.ds(pl.multiple_of(j * tk, tk), tk), :]
            return [jnp.exp2(_dot(kb, qmt) + add).astype(BF16) for (qmt, _, _), add in zip(qmaps, adds)]

        def special(j, tile):
            ps = weights(j, [tile - shift for _, shift, _ in qmaps])
            for p, (_, _, a_ref) in zip(ps, qmaps):
                a_ref[...] += _dot(vtb[j], p)

        special(qi, bias_ref[0, 0])

        @pl.when(qi >= 1)
        def _():
            special(qi - 1, bias_ref[0, 1])

        far_adds = [far_bias - shift for _, shift, _ in qmaps]
        n_far = jnp.maximum(qi - 1, 0)
        odd = n_far & 1

        @pl.when(odd == 1)
        def _():
            for p, (_, _, a_ref) in zip(weights(0, far_adds), qmaps):
                a_ref[...] += _dot(vtb[0], p)

        def pair(i, c):
            j = odd + 2 * i
            pa = weights(j, far_adds)
            pb = weights(j + 1, far_adds)
            for p0, p1, (_, _, a_ref) in zip(pa, pb, qmaps):
                a_ref[...] += _dot(vtb[j], p0) + _dot(vtb[j + 1], p1)
            return c
        lax.fori_loop(0, lax.shift_right_logical(n_far, 1), pair, 0)

    @pl.when(jnp.logical_not(bounded))
    def _():
        for m_ref, _ in states:
            m_ref[...] = jnp.full_like(m_ref, NEG)
        n_blocks = qi + 1

        def scores(t, buf):
            kb = kbf[pl.ds(pl.multiple_of((qi - t) * tk, tk), tk), :]
            s_scr[buf, 0] = _dot(kb, q1t)
            s_scr[buf, 1] = _dot(kb, q2t)

        def update(t, buf):
            bias = bias_ref[0, jnp.minimum(t, bias_ref.shape[1] - 1)]
            vt = vtb[qi - t]
            for mi, (m_ref, a_ref) in enumerate(states):
                _softmax_step_km(s_scr[buf, mi] + bias, vt, m_ref, a_ref)

        odd = n_blocks & 1

        @pl.when(odd == 1)
        def _():
            scores(0, 0)
            update(0, 0)

        n_pairs = lax.shift_right_logical(n_blocks, 1)

        @pl.when(n_pairs > 0)
        def _():
            scores(odd, 0)

        def pair(p, c):
            t0 = odd + 2 * p
            scores(t0 + 1, 1)
            update(t0, 0)
            scores(jnp.minimum(t0 + 2, qi), 0)
            update(t0 + 1, 1)
            return c
        lax.fori_loop(0, n_pairs, pair, 0)

    out_t = (a1[0:dv, :] / a1[dv:dv + 1, :]
             - lam_ref[0] * (a2[0:dv, :] / a2[dv:dv + 1, :]))
    o_ref[0] = out_t.T


def _attn_a(lam, brange, qa, ka, va, bias):
    b, s, w = qa.shape
    nh = w // LANES
    tq = tk = MIXER_A_TILE
    vrows = LANES + BF16_SUBLANES
    return pl.pallas_call(
        _attn_a_kernel,
        grid=(b, nh, s // tq),
        in_specs=[
            pl.BlockSpec(memory_space=pltpu.SMEM),
            pl.BlockSpec(memory_space=pltpu.SMEM),
            pl.BlockSpec((1, tq, LANES), lambda bi, h, qi: (bi, qi, h)),
            pl.BlockSpec((1, s, LANES), lambda bi, h, qi: (bi, 0, h)),
            pl.BlockSpec((1, s, LANES), lambda bi, h, qi: (bi, 0, h)),
            pl.BlockSpec((1,) + bias.shape[1:], lambda bi, h, qi: (h, 0, 0, 0)),
        ],
        out_specs=pl.BlockSpec((1, tq, LANES), lambda bi, h, qi: (bi, qi, h)),
        out_shape=jax.ShapeDtypeStruct((b, s, w), F32),
        scratch_shapes=[
            pltpu.VMEM((s // tk, vrows, tk), BF16),
            pltpu.VMEM((2, 2, tk, tq), F32), pltpu.SMEM((2,), F32),
            pltpu.VMEM((1, tq), F32), pltpu.VMEM((vrows, tq), F32),
            pltpu.VMEM((1, tq), F32), pltpu.VMEM((vrows, tq), F32),
        ],
        compiler_params=_cparams(("parallel", "parallel", "arbitrary")),
        name="mixer_a",
    )(lam, brange, qa, ka, va, bias)


def _stick_block(z, valid, umat, carry):
    sp = jnp.maximum(z, 0.0) + jnp.log(1.0 + jnp.exp(-jnp.abs(z)))
    log1m = -sp
    if valid is not None:
        log1m = jnp.where(valid, log1m, 0.0)
    hi = log1m.astype(BF16)
    lo = (log1m - hi.astype(F32)).astype(BF16)
    after = _dot(hi, umat) + _dot(lo, umat)
    w = jnp.exp(z - sp + after + carry)
    if valid is not None:
        w = jnp.where(valid, w, 0.0)
    return w, jnp.sum(log1m, axis=1, keepdims=True)


def _stick_block_km(z, valid, umat, carry):
    sp = jnp.maximum(z, 0.0) + jnp.log(1.0 + jnp.exp(-jnp.abs(z)))
    log1m = -sp
    if valid is not None:
        log1m = jnp.where(valid, log1m, 0.0)
    hi = log1m.astype(BF16)
    lo = (log1m - hi.astype(F32)).astype(BF16)
    after = _dot(umat, hi) + _dot(umat, lo)
    w = jnp.exp(z - sp + after + carry)
    if valid is not None:
        w = jnp.where(valid, w, 0.0)
    return w, jnp.sum(log1m, axis=0, keepdims=True)


def _attn_b_kernel(q_ref, k_ref, v_ref, u_ref, o_ref, kbf, vtb, c_scr, acc_scr):
    step = pl.program_id(2)
    seq = kbf.shape[0]
    tk = vtb.shape[2]
    n_halves = q_ref.shape[1] // tk
    n_heads = LANES // HEAD_DIM

    @pl.when(step == 0)
    def _():
        def prep(j, c):
            r = pl.multiple_of(j * tk, tk)
            kbf[pl.ds(r, tk), :] = k_ref[0, pl.ds(r, tk), :].astype(BF16)
            vtb[j] = v_ref[0, pl.ds(r, tk), :].T.astype(BF16)
            return c
        lax.fori_loop(0, seq // tk, prep, 0)

    sub = lax.broadcasted_iota(jnp.int32, (LANES, tk), 0)
    qts = []
    for half in range(n_halves):
        qt = q_ref[0, half * tk:(half + 1) * tk, :].astype(F32).T
        qts.append([jnp.where((sub >= hh * HEAD_DIM) & (sub < (hh + 1) * HEAD_DIM), qt, 0.0).astype(BF16)
                    for hh in range(n_heads)])
    krow = lax.broadcasted_iota(jnp.int32, (tk, tk), 0)
    qcol = lax.broadcasted_iota(jnp.int32, (tk, tk), 1)
    umat = u_ref[...]

    def blocks(half, js, valids, first):
        kbs = [kbf[pl.ds(pl.multiple_of(j * tk, tk), tk), :] for j in js]
        cmax = None
        for hh in range(n_heads):
            carry = 0.0 if first else c_scr[half, hh]
            pv = None
            for j, kb, valid in zip(js, kbs, valids):
                w, cs = _stick_block_km(_dot(kb, qts[half][hh]), valid, umat, carry)
                term = _dot(vtb[j], w.astype(BF16))
                pv = term if pv is None else pv + term
                carry = carry + cs
            acc_scr[half, hh] = pv if first else acc_scr[half, hh] + pv
            c_scr[half, hh] = carry
            cm = jnp.max(carry)
            cmax = cm if cmax is None else jnp.maximum(cmax, cm)
        return cmax

    own_valid = krow < qcol
    everywhere = krow >= 0
    cmaxes = []
    for half in range(n_halves):
        qb = step * n_halves + half
        prev_valid = None if half >= 1 else everywhere & (qb >= 1)
        cmaxes.append(blocks(half, [qb, jnp.maximum(qb - 1, 0)], [own_valid, prev_valid], True))

    for half in range(n_halves):
        qb = step * n_halves + half

        def cond(st):
            j, cmax = st
            return (j >= 0) & (cmax > STICK_SKIP)

        def body(st, half=half):
            j, _ = st
            return j - 2, blocks(half, [j, jnp.maximum(j - 1, 0)], [None, everywhere & (j >= 1)], False)

        lax.while_loop(cond, body, (qb - 2, cmaxes[half]))
        out_t = jnp.where(sub < HEAD_DIM, acc_scr[half, 0], acc_scr[half, 1])
        o_ref[0, half * tk:(half + 1) * tk, :] = out_t.T.astype(o_ref.dtype)


def _strict_lower(n):
    return jnp.asarray(np.tril(np.ones((n, n)), -1), BF16)


def _strict_upper(n):
    return jnp.asarray(np.triu(np.ones((n, n)), 1), BF16)


def _attn_b(qb, kb, vb):
    b, s, w = qb.shape
    n_halves = 2
    tq = n_halves * ATTN_TK
    return pl.pallas_call(
        _attn_b_kernel,
        grid=(b, w // LANES, s // tq),
        in_specs=[
            pl.BlockSpec((1, tq, LANES), lambda bi, h, qi: (bi, qi, h)),
            pl.BlockSpec((1, s, LANES), lambda bi, h, qi: (bi, 0, h)),
            pl.BlockSpec((1, s, LANES), lambda bi, h, qi: (bi, 0, h)),
            pl.BlockSpec((ATTN_TK, ATTN_TK), lambda bi, h, qi: (0, 0)),
        ],
        out_specs=pl.BlockSpec((1, tq, LANES), lambda bi, h, qi: (bi, qi, h)),
        out_shape=jax.ShapeDtypeStruct((b, s, w), BF16),
        scratch_shapes=[
            pltpu.VMEM((s, LANES), BF16), pltpu.VMEM((s // ATTN_TK, LANES, ATTN_TK), BF16),
            pltpu.VMEM((n_halves, LANES // HEAD_DIM, 1, ATTN_TK), F32),
            pltpu.VMEM((n_halves, LANES // HEAD_DIM, LANES, ATTN_TK), F32),
        ],
        compiler_params=_cparams(("parallel", "parallel", "arbitrary")),
        name="mixer_b",
    )(qb, kb, vb, _strict_upper(ATTN_TK))


def _attn_c_kernel(q_ref, k_ref, v_ref, bias_ref, o_ref, kbf, vbf):
    qi = pl.program_id(2)
    seq = k_ref.shape[1]
    tq = q_ref.shape[1]
    pad = C_PREV * CHUNK

    @pl.when(qi == 0)
    def _():
        kbf[0:pad, :] = jnp.zeros((pad, LANES), BF16)
        vbf[0:pad, :] = jnp.zeros((pad, LANES), BF16)
        _cast_rows(k_ref, kbf, seq, dst_off=pad)
        _cast_rows(v_ref, vbf, seq, dst_off=pad)

    q = q_ref[0]
    lane = lax.broadcasted_iota(jnp.int32, q.shape, 1)
    wstart = pl.multiple_of(qi * tq, tq)
    kw = kbf[pl.ds(wstart, BAND_W), :]
    vw = vbf[pl.ds(wstart, BAND_W), :]
    col = lax.broadcasted_iota(jnp.int32, (tq, BAND_W), 1)
    in_seq = col >= pad - qi * tq
    outs = []
    for hh in range(LANES // HEAD_DIM):
        qm = _keep_lanes(q, hh * HEAD_DIM, (hh + 1) * HEAD_DIM)
        s = jnp.where(in_seq, _dot_nt(qm, kw) + bias_ref[hh], NEG)
        m = jnp.max(s, axis=1, keepdims=True)
        p = jnp.exp(s - m)
        l = jnp.sum(p, axis=1, keepdims=True)
        outs.append(_dot(p.astype(BF16), vw) / l)
    o_ref[0] = jnp.where(lane < HEAD_DIM, outs[0], outs[1]).astype(o_ref.dtype)


def _attn_c(qc, kc, vc, bias):
    b, s, w = qc.shape
    tq = ATTN_TQ
    pad = C_PREV * CHUNK
    hp = LANES // HEAD_DIM
    return pl.pallas_call(
        _attn_c_kernel,
        grid=(b, w // LANES, s // tq),
        in_specs=[
            pl.BlockSpec((1, tq, LANES), lambda bi, h, qi: (bi, qi, h)),
            pl.BlockSpec((1, s, LANES), lambda bi, h, qi: (bi, 0, h)),
            pl.BlockSpec((1, s, LANES), lambda bi, h, qi: (bi, 0, h)),
            pl.BlockSpec((hp, tq, BAND_W), lambda bi, h, qi: (h, 0, 0)),
        ],
        out_specs=pl.BlockSpec((1, tq, LANES), lambda bi, h, qi: (bi, qi, h)),
        out_shape=jax.ShapeDtypeStruct((b, s, w), BF16),
        scratch_shapes=[pltpu.VMEM((s + pad, LANES), BF16), pltpu.VMEM((s + pad, LANES), BF16)],
        compiler_params=_cparams(("parallel", "parallel", "arbitrary")),
        name="mixer_c",
    )(qc, kc, vc, bias)


def _merge_kernel(x_ref, oa_ref, ob_ref, oc_ref, g_ref, wgate_ref, sub_ref,
                  wa_ref, wb_ref, wc_ref, wout_ref, o_ref):
    x = x_ref[...]
    d = x.shape[1]
    h = _rms_rows(x, g_ref[...]).astype(BF16)
    oa = oa_ref[...]
    parts = []
    for c in range(oa.shape[1] // LANES):
        oc_ = oa[:, c * LANES:(c + 1) * LANES]
        parts.append(oc_ * lax.rsqrt(jnp.mean(oc_ * oc_, axis=-1, keepdims=True) + EPS))
    oan = (jnp.concatenate(parts, axis=1) * sub_ref[...]).astype(BF16)
    merged = jax.nn.sigmoid(_dot(h, wgate_ref[:, 0:d])) * _dot(oan, wa_ref[...])
    merged += jax.nn.sigmoid(_dot(h, wgate_ref[:, d:2 * d])) * _dot(ob_ref[...], wb_ref[...])
    merged += jax.nn.sigmoid(_dot(h, wgate_ref[:, 2 * d:3 * d])) * _dot(oc_ref[...], wc_ref[...])
    o_ref[...] = x + _dot(merged.astype(BF16), wout_ref[...])


def _merge(x, oa, ob, oc, g, wgate, sub, wa, wb, wc, wout, tm):
    n, d = x.shape
    full = lambda a: pl.BlockSpec(a.shape, lambda i: (0,) * a.ndim)
    rows = lambda a: pl.BlockSpec((tm, a.shape[1]), lambda i: (i, 0))
    g = g.reshape(1, d)
    return pl.pallas_call(
        _merge_kernel,
        grid=(n // tm,),
        in_specs=[rows(x), rows(oa), rows(ob), rows(oc), full(g), full(wgate), full(sub),
                  full(wa), full(wb), full(wc), full(wout)],
        out_specs=rows(x),
        out_shape=jax.ShapeDtypeStruct((n, d), F32),
        compiler_params=_cparams(("parallel",)),
        name="merge",
    )(x, oa, ob, oc, g, wgate, sub, wa, wb, wc, wout)


def _cross_kernel(x_ref, g_ref, wq_ref, gain_ref, gmat_ref, mk_ref, mv_ref, wo_ref, o_ref):
    x = x_ref[0]
    h = _rms_rows(x, g_ref[...]).astype(BF16)
    q = (_group_rms(_dot(h, wq_ref[...]), gmat_ref[...]) * gain_ref[...]).astype(BF16)
    mk = mk_ref[0]
    mv = mv_ref[0]
    lane = lax.broadcasted_iota(jnp.int32, q.shape, 1)
    o = jnp.zeros(q.shape, F32)
    for hh in range(q.shape[1] // HEAD_DIM):
        in_head = (lane >= hh * HEAD_DIM) & (lane < (hh + 1) * HEAD_DIM)
        s = _dot_nt(_keep_lanes(q, hh * HEAD_DIM, (hh + 1) * HEAD_DIM), mk)
        p = jnp.exp(s - jnp.max(s, axis=1, keepdims=True))
        l = jnp.sum(p, axis=1, keepdims=True)
        o = jnp.where(in_head, _dot(p.astype(BF16), mv) / l, o)
    o_ref[0] = x + _dot(o.astype(BF16), wo_ref[...])


def _cross(x, g, wq, gain, mk, mv, wo, tm):
    b, s, d = x.shape
    full = lambda a: pl.BlockSpec(a.shape, lambda bi, i: (0,) * a.ndim)
    g = g.reshape(1, d)
    gmat = _group_mean_matrix()
    return pl.pallas_call(
        _cross_kernel,
        grid=(b, s // tm),
        in_specs=[
            pl.BlockSpec((1, tm, d), lambda bi, i: (bi, i, 0)),
            full(g), full(wq), full(gain), full(gmat),
            pl.BlockSpec((1,) + mk.shape[1:], lambda bi, i: (bi, 0, 0)),
            pl.BlockSpec((1,) + mv.shape[1:], lambda bi, i: (bi, 0, 0)),
            full(wo),
        ],
        out_specs=pl.BlockSpec((1, tm, d), lambda bi, i: (bi, i, 0)),
        out_shape=jax.ShapeDtypeStruct((b, s, d), F32),
        compiler_params=_cparams(("parallel", "parallel")),
        name="cross",
    )(x, g, wq, gain, gmat, mk, mv, wo)


def _sample_attn_kernel(lam_ref,
                        qa_ref, kan_ref, van_ref, kac_ref, vac_ref, bac_ref, ban_ref,
                        qb_ref, kbn_ref, vbn_ref, kbc_ref, vbc_ref, ubig_ref, usmall_ref,
                        qc_ref, kcn_ref, vcn_ref, kcc_ref, vcc_ref, bcc_ref, bcn_ref,
                        oa_ref, ob_ref, oc_ref):
    lam = lam_ref[0]
    ns = qa_ref.shape[1]
    bf = lambda r: r[0].astype(BF16)

    def heads(q, width):
        for hh in range(LANES // width):
            yield hh, None, _keep_lanes(q, hh * width, (hh + 1) * width)

    def softmax2(s_c, s_n, exp=jnp.exp):
        m = jnp.maximum(jnp.max(s_c, axis=1, keepdims=True), jnp.max(s_n, axis=1, keepdims=True))
        p_c = exp(s_c - m)
        p_n = exp(s_n - m)
        inv = 1.0 / (jnp.sum(p_c, axis=1, keepdims=True) + jnp.sum(p_n, axis=1, keepdims=True))
        return p_c * inv, p_n * inv

    for h in range(qa_ref.shape[2] // LANES):
        sl = slice(h * LANES, (h + 1) * LANES)
        q = qa_ref[0, :, sl]
        k_c = kac_ref[0, :, h, :].astype(BF16)
        k_n = kan_ref[0, :, sl].astype(BF16)
        maps = []
        for _, _, qm in heads(q, HEAD_DIM):
            maps.append(softmax2(_dot_nt(qm, k_c) + bac_ref[h], _dot_nt(qm, k_n) + ban_ref[h], exp=jnp.exp2))
        a_c = (maps[0][0] - lam * maps[1][0]).astype(BF16)
        a_n = (maps[0][1] - lam * maps[1][1]).astype(BF16)
        oa_ref[0, :, sl] = (_dot(a_c, vac_ref[0, :, h, :].astype(BF16))
                            + _dot(a_n, van_ref[0, :, sl].astype(BF16)))

    past = kbc_ref.shape[1]
    tk = ubig_ref.shape[0]
    row = lax.broadcasted_iota(jnp.int32, (ns, ns), 0)
    col = lax.broadcasted_iota(jnp.int32, (ns, ns), 1)
    for pr in range(qb_ref.shape[2] // LANES):
        sl = slice(pr * LANES, (pr + 1) * LANES)
        q = qb_ref[0, :, sl]
        k_n = kbn_ref[0, :, sl].astype(BF16)
        v_n = vbn_ref[0, :, sl].astype(BF16)
        outs = []
        for _, _, qm in heads(q, HEAD_DIM):
            w, carry = _stick_block(_dot_nt(qm, k_n), col < row, usmall_ref[...], 0.0)
            acc = _dot(w.astype(BF16), v_n)
            for j in range(past // tk - 1, -1, -1):
                k_c = kbc_ref[0, j * tk:(j + 1) * tk, sl].astype(BF16)
                v_c = vbc_ref[0, j * tk:(j + 1) * tk, sl].astype(BF16)
                w, rs = _stick_block(_dot_nt(qm, k_c), None, ubig_ref[...], carry)
                acc += _dot(w.astype(BF16), v_c)
                carry = carry + rs
            outs.append(acc)
        lane = lax.broadcasted_iota(jnp.int32, q.shape, 1)
        ob_ref[0, :, sl] = jnp.where(lane < HEAD_DIM, outs[0], outs[1]).astype(ob_ref.dtype)

    for pr in range(qc_ref.shape[2] // LANES):
        sl = slice(pr * LANES, (pr + 1) * LANES)
        q = qc_ref[0, :, sl]
        k_c = kcc_ref[0, :, sl].astype(BF16)
        k_n = kcn_ref[0, :, sl].astype(BF16)
        v_c = vcc_ref[0, :, sl].astype(BF16)
        v_n = vcn_ref[0, :, sl].astype(BF16)
        outs = []
        for hh, _, qm in heads(q, HEAD_DIM):
            hd = pr * (LANES // HEAD_DIM) + hh
            p_c, p_n = softmax2(_dot_nt(qm, k_c) + bcc_ref[hd], _dot_nt(qm, k_n) + bcn_ref[hd])
            outs.append(_dot(p_c.astype(BF16), v_c) + _dot(p_n.astype(BF16), v_n))
        lane = lax.broadcasted_iota(jnp.int32, q.shape, 1)
        oc_ref[0, :, sl] = jnp.where(lane < HEAD_DIM, outs[0], outs[1]).astype(oc_ref.dtype)


def _sample_attn(layer, lam, qa, ka, va, cak, cav, bac, ban, qb, kb, vb, cbk, cbv,
                 qc, kc, vc, cck, ccv, bcc, bcn):
    b, ns, _ = qa.shape
    per_b = lambda a: pl.BlockSpec((1,) + a.shape[1:], lambda bi: (bi,) + (0,) * (a.ndim - 1))
    full = lambda a: pl.BlockSpec(a.shape, lambda bi: (0,) * a.ndim)
    layer_b = lambda a: pl.BlockSpec((None, 1) + a.shape[2:], lambda bi: (layer, bi) + (0,) * (a.ndim - 2))
    ubig = _strict_lower(ATTN_TK)
    usmall = _strict_lower(ns)
    args = [qa, ka, va, cak, cav, bac, ban, qb, kb, vb, cbk, cbv, ubig, usmall,
            qc, kc, vc, cck, ccv, bcc, bcn]
    specs = [per_b(qa), per_b(ka), per_b(va), layer_b(cak), layer_b(cav), full(bac), full(ban),
             per_b(qb), per_b(kb), per_b(vb), per_b(cbk), per_b(cbv), full(ubig), full(usmall),
             per_b(qc), per_b(kc), per_b(vc), per_b(cck), per_b(ccv), full(bcc), full(bcn)]
    return pl.pallas_call(
        _sample_attn_kernel,
        grid=(b,),
        in_specs=[pl.BlockSpec(memory_space=pltpu.SMEM)] + specs,
        out_specs=[per_b(qa), per_b(qb), per_b(qc)],
        out_shape=[jax.ShapeDtypeStruct(qa.shape, F32), jax.ShapeDtypeStruct(qb.shape, BF16),
                   jax.ShapeDtypeStruct(qc.shape, BF16)],
        compiler_params=_cparams(("parallel",)),
        name="sample_mixers",
    )(lam, *args)


def _t5_bucket_np(rel):
    half = T5_BUCKETS // 2
    max_exact = half // 2
    n = np.abs(rel)
    nf = np.maximum(n, 1).astype(np.float64)
    large = max_exact + (np.log(nf / max_exact) / math.log(T5_MAX_DIST / max_exact)
                         * (half - max_exact)).astype(np.int64)
    large = np.minimum(large, half - 1)
    return np.where(rel > 0, half, 0) + np.where(n < max_exact, n, large)


def _toeplitz(lookup, n_rows, n_cols):
    period = n_rows + n_cols
    slot = np.arange(period)
    diff = np.minimum((slot + n_rows - 1) % period - (n_rows - 1), n_cols - 1)
    vec = lookup(diff).astype(F32)
    flat = jnp.tile(vec, (1, n_rows))[:, :n_rows * (period - 1)]
    return flat.reshape(vec.shape[0], n_rows, period - 1)[:, :, :n_cols]


def _t5_bias_table(t5_bias, qpos, kpos, key_major=False):
    t5_rows = lambda rel: t5_bias[_t5_bucket_np(rel)].T
    mask = (kpos[None, :] // CHUNK) <= (qpos[:, None] // CHUNK)
    if key_major:
        table = _toeplitz(lambda dd: t5_rows(kpos[0] - qpos[0] - dd), len(kpos), len(qpos))
        mask = mask.T
    else:
        table = _toeplitz(lambda dd: t5_rows(kpos[0] - qpos[0] + dd), len(qpos), len(kpos))
    return jnp.where(jnp.asarray(mask)[None], table, NEG)


def _band_bias_table(rel_table, qpos, kpos):
    lookup = lambda dd: rel_table[:, np.clip(kpos[0] - qpos[0] + dd, -REL_CLIP, REL_CLIP) + REL_CLIP]
    table = _toeplitz(lookup, len(qpos), len(kpos))
    qc = qpos[:, None] // CHUNK
    kc = kpos[None, :] // CHUNK
    mask = (kpos[None, :] >= 0) & (kc <= qc) & (kc >= qc - C_PREV)
    return jnp.where(jnp.asarray(mask)[None], table, NEG)


def _lambda_init(layer):
    return 0.8 - 0.6 * math.exp(-0.3 * layer)


def kernel(x_prompt, x_sample, mem_prompt, cache_a_k, cache_a_v, cache_b_k, cache_b_v, cache_c_k, cache_c_v, cache_mem_k, cache_mem_v, t5_bias, ffn1_norm, ffn1_wg, ffn1_wu, ffn1_wd, mix_norm, w_in, a_qnorm, a_knorm, a_lq1, a_lk1, a_lq2, a_lk2, a_subln, c_qnorm, c_knorm, c_rel_bias, w_br_a, w_br_b, w_br_c, w_out, x_norm, mem_norm, x_wq, x_wkv, x_qnorm, x_knorm, x_wo, ffn2_norm, ffn2_wg, ffn2_wu, ffn2_wd):
    bp, sp, d = x_prompt.shape
    bs, ns, _ = x_sample.shape
    depth = w_in.shape[0]
    past = cache_a_k.shape[2]
    w_buf = cache_c_k.shape[2]
    n_mem = mem_prompt.shape[1]
    wa = cache_a_k.shape[3] * cache_a_k.shape[4]
    wb = cache_b_k.shape[3] * cache_b_k.shape[4]
    wc = cache_c_k.shape[3] * cache_c_k.shape[4]
    wx = cache_mem_k.shape[3] * cache_mem_k.shape[4]
    h_a = cache_a_k.shape[3]
    n_qkv = 3 * (wa + wb + wc)
    w_keep = min(C_PREV * CHUNK, sp)
    scale = HEAD_DIM ** -0.5
    assert sp % ATTN_TQ == 0 and ATTN_TQ == ATTN_TK and ATTN_TQ == 4 * CHUNK and sp >= w_keep
    assert past % ATTN_TK == 0
    assert sp % MIXER_A_TILE == 0 and MIXER_A_TILE % CHUNK == 0 and T5_MAX_DIST <= MIXER_A_TILE

    tile = lambda g, reps: jnp.tile(g.astype(F32), reps)
    bf = lambda a: a.astype(BF16)

    loc_a = np.arange(MIXER_A_TILE)
    loc = np.arange(ATTN_TQ)
    a_far = t5_bias[T5_BUCKETS // 2 - 1].astype(F32)
    a_bias = LOG2E * jnp.stack([
        _t5_bias_table(t5_bias, loc_a + MIXER_A_TILE, loc_a + MIXER_A_TILE, key_major=True),
        _t5_bias_table(t5_bias, loc_a + MIXER_A_TILE, loc_a, key_major=True),
        jnp.broadcast_to(a_far[:, None, None], (h_a, MIXER_A_TILE, MIXER_A_TILE))], axis=1)
    t5_log2 = LOG2E * t5_bias.astype(F32)
    a_brange = jnp.stack([LOG2E * a_far, jnp.max(t5_log2, axis=0), jnp.min(t5_log2, axis=0)], axis=1).reshape(-1)
    qpos_s = past + np.arange(ns)
    a_bias_sc = LOG2E * _t5_bias_table(t5_bias, qpos_s, np.arange(past))
    a_bias_sn = LOG2E * _t5_bias_table(t5_bias, qpos_s, qpos_s)
    kpos_sb = past - w_buf + np.arange(w_buf + ns)

    xp = x_prompt.reshape(bp * sp, d)
    xs = x_sample.reshape(bs * ns, d)
    mem = mem_prompt.reshape(bp * n_mem, d)
    outs = {k: [] for k in ("ak_p", "av_p", "bk_p", "bv_p", "ck_p", "cv_p", "mk_p", "mv_p",
                            "ak_s", "av_s", "bk_s", "bv_s", "ck_s", "cv_s")}
    flat = lambda dt: ((dt, 0),)
    dv_a = wa // h_a
    a_kv_prompt = ((F32, dv_a), (BF16, 0))
    qkv_tail = [(wb, False, True, flat(BF16)), (wb, False, False, flat(F32)), (wb, False, False, flat(F32)),
                (wc, True, True, flat(BF16)), (wc, True, True, flat(F32)), (wc, False, False, flat(F32))]
    qkv_segs_p = [(wa, True, True, flat(BF16)), (wa, True, True, a_kv_prompt),
                  (wa, False, False, a_kv_prompt)] + qkv_tail
    qkv_segs_s = [(wa, True, True, flat(BF16)), (wa, True, True, flat(F32)), (wa, False, False, flat(F32))] + qkv_tail

    for i in range(depth):
        wg1, wu1, wd1 = bf(ffn1_wg[i]), bf(ffn1_wu[i]), bf(ffn1_wd[i])
        wg2, wu2, wd2 = bf(ffn2_wg[i]), bf(ffn2_wu[i]), bf(ffn2_wd[i])
        w_qkv, w_gate = bf(w_in[i][:, :n_qkv]), bf(w_in[i][:, n_qkv:])
        qkv_gain = jnp.concatenate([
            tile(a_qnorm[i], wa // HEAD_DIM) * (scale * LOG2E), tile(a_knorm[i], wa // HEAD_DIM), jnp.ones((wa,), F32),
            jnp.full((wb,), scale, F32), jnp.ones((2 * wb,), F32),
            tile(c_qnorm[i], wc // HEAD_DIM) * scale, tile(c_knorm[i], wc // HEAD_DIM), jnp.ones((wc,), F32),
        ]).reshape(1, n_qkv)
        sub = (tile(a_subln[i], h_a) * (1.0 - _lambda_init(i))).reshape(1, wa)
        wbr = bf(w_br_a[i]), bf(w_br_b[i]), bf(w_br_c[i])
        wout = bf(w_out[i])
        wq, wo = bf(x_wq[i]), bf(x_wo[i])
        xq_gain = (tile(x_qnorm[i], wx // HEAD_DIM) * scale).reshape(1, wx)
        dot64 = lambda a, b: jnp.exp(jnp.sum(a.astype(F32) * b.astype(F32)))
        lam = (dot64(a_lq1[i], a_lk1[i]) - dot64(a_lq2[i], a_lk2[i]) + _lambda_init(i)).reshape(1)
        c_bias_p = _band_bias_table(c_rel_bias[i], C_PREV * CHUNK + loc, np.arange(BAND_W))
        c_bias_sc = _band_bias_table(c_rel_bias[i], qpos_s, kpos_sb[:w_buf])
        c_bias_sn = _band_bias_table(c_rel_bias[i], qpos_s, kpos_sb[w_buf:])

        xp = _ffn(xp, ffn1_norm[i], wg1, wu1, wd1, tm=1024, tf=1024)
        qa, ka, ka_bf, va, va_bf, qb, kb, vb, qc, kc, vc = _proj(xp, mix_norm[i], w_qkv, qkv_gain, qkv_segs_p, tm=512)
        r3 = lambda a: a.reshape(bp, sp, a.shape[-1])
        oa = _attn_a(lam, a_brange, r3(qa), r3(ka_bf), r3(va_bf), a_bias)
        ob = _attn_b(r3(qb), r3(kb), r3(vb))
        oc = _attn_c(r3(qc), r3(kc), r3(vc), c_bias_p)
        xp = _merge(xp, oa.reshape(-1, wa), ob.reshape(-1, wb), oc.reshape(-1, wc), mix_norm[i], w_gate, sub,
                    *wbr, wout, tm=512)
        mk, mv = _proj(mem, mem_norm[i], bf(x_wkv[i]),
                       jnp.concatenate([tile(x_knorm[i], wx // HEAD_DIM), jnp.ones((wx,), F32)]).reshape(1, 2 * wx),
                       [(wx, True, True, flat(F32)), (wx, False, False, flat(F32))], tm=n_mem)
        mk3, mv3 = mk.reshape(bp, n_mem, wx), mv.reshape(bp, n_mem, wx)
        xp = _cross(xp.reshape(bp, sp, d), x_norm[i], wq, xq_gain, bf(mk3), bf(mv3), wo, tm=512).reshape(-1, d)
        xp = _ffn(xp, ffn2_norm[i], wg2, wu2, wd2, tm=1024, tf=1024)
        outs["ak_p"].append(ka.reshape(bp, sp, h_a, -1))
        outs["av_p"].append(va.reshape(bp, sp, h_a, -1))
        outs["bk_p"].append(kb.reshape(bp, sp, -1, HEAD_DIM))
        outs["bv_p"].append(vb.reshape(bp, sp, -1, HEAD_DIM))
        outs["ck_p"].append(r3(kc)[:, sp - w_keep:].reshape(bp, w_keep, -1, HEAD_DIM))
        outs["cv_p"].append(r3(vc)[:, sp - w_keep:].reshape(bp, w_keep, -1, HEAD_DIM))
        outs["mk_p"].append(mk3.reshape(bp, n_mem, -1, HEAD_DIM))
        outs["mv_p"].append(mv3.reshape(bp, n_mem, -1, HEAD_DIM))

        xs = _ffn(xs, ffn1_norm[i], wg1, wu1, wd1, tm=bs * ns)
        qa, ka, va, qb, kb, vb, qc, kc, vc = _proj(xs, mix_norm[i], w_qkv, qkv_gain, qkv_segs_s, tm=bs * ns)
        s3 = lambda a: a.reshape(bs, ns, a.shape[-1])
        c3 = lambda a: a.reshape(bs, a.shape[1], -1)
        oa, ob, oc = _sample_attn(
            i, lam, s3(qa), s3(ka), s3(va), cache_a_k, cache_a_v, a_bias_sc, a_bias_sn,
            s3(qb), s3(kb), s3(vb), c3(cache_b_k[i]), c3(cache_b_v[i]),
            s3(qc), s3(kc), s3(vc), c3(cache_c_k[i]), c3(cache_c_v[i]), c_bias_sc, c_bias_sn)
        xs = _merge(xs, oa.reshape(-1, wa), ob.reshape(-1, wb), oc.reshape(-1, wc), mix_norm[i], w_gate, sub,
                    *wbr, wout, tm=bs * ns)
        xs = _cross(xs.reshape(bs, ns, d), x_norm[i], wq, xq_gain, bf(c3(cache_mem_k[i])), bf(c3(cache_mem_v[i])),
                    wo, tm=ns).reshape(-1, d)
        xs = _ffn(xs, ffn2_norm[i], wg2, wu2, wd2, tm=bs * ns)
        outs["ak_s"].append(ka.reshape(bs, ns, h_a, -1))
        outs["av_s"].append(va.reshape(bs, ns, h_a, -1))
        outs["bk_s"].append(kb.reshape(bs, ns, -1, HEAD_DIM))
        outs["bv_s"].append(vb.reshape(bs, ns, -1, HEAD_DIM))
        outs["ck_s"].append(jnp.concatenate([cache_c_k[i], kc.reshape(bs, ns, -1, HEAD_DIM)], axis=1)[:, ns:])
        outs["cv_s"].append(jnp.concatenate([cache_c_v[i], vc.reshape(bs, ns, -1, HEAD_DIM)], axis=1)[:, ns:])

    st = lambda k: jnp.stack(outs[k])
    return (xp.reshape(bp, sp, d), xs.reshape(bs, ns, d),
            st("ak_p"), st("av_p"), st("bk_p"), st("bv_p"), st("ck_p"), st("cv_p"), st("mk_p"), st("mv_p"),
            st("ak_s"), st("av_s"), st("bk_s"), st("bv_s"), st("ck_s"), st("cv_s"))
```

```python
import functools
import math

import numpy as np
import jax
import jax.numpy as jnp
from jax import lax
from jax.experimental import pallas as pl
from jax.experimental.pallas import tpu as pltpu

F32 = jnp.float32
BF16 = jnp.bfloat16

EPS = 1e-6
HEAD_DIM = 64
CHUNK = 64
C_PREV = 8
REL_CLIP = 128
T5_BUCKETS = 32
T5_MAX_DIST = 128
LANES = 128
BF16_SUBLANES = 16
V7X_MXU_DIM = 256
NEG = -1e30
LOG2E = math.log2(math.e)
BOUND_SLACK = 1.001
BOUNDED_EXP2_SPAN = 100.0
STICK_SKIP = -110.0
V7X_VMEM_LIMIT_BYTES = 56 * 1024 * 1024

MIXER_A_TILE = 512
ATTN_TQ = 256
ATTN_TK = 256
BAND_W = (C_PREV + 4) * CHUNK


def _cparams(sem):
    return pltpu.CompilerParams(dimension_semantics=sem, vmem_limit_bytes=V7X_VMEM_LIMIT_BYTES)


def _rms_rows(x, g):
    return x * lax.rsqrt(jnp.mean(x * x, axis=-1, keepdims=True) + EPS) * g


def _dot(a, b):
    return jnp.dot(a, b, preferred_element_type=F32)


def _dot_nt(a, b):
    return lax.dot_general(a, b, (((1,), (1,)), ((), ())), preferred_element_type=F32)


def _keep_lanes(q, lo, hi):
    lane = lax.broadcasted_iota(jnp.int32, q.shape, 1)
    return jnp.where((lane >= lo) & (lane < hi), q.astype(F32), 0.0).astype(BF16)


def _group_rms(y, gmat):
    wide = gmat.shape[0]
    parts = []
    for c in range(y.shape[1] // wide):
        yc = y[:, c * wide:(c + 1) * wide]
        ms = _dot((yc * yc).astype(BF16), gmat)
        parts.append(yc * lax.rsqrt(ms + EPS))
    return parts[0] if len(parts) == 1 else jnp.concatenate(parts, axis=1)


def _group_mean_matrix():
    g = np.kron(np.eye(V7X_MXU_DIM // HEAD_DIM), np.ones((HEAD_DIM, HEAD_DIM))) / HEAD_DIM
    return jnp.asarray(g, BF16)


def _ffn_kernel(x_ref, g_ref, wg_ref, wu_ref, wd_ref, o_ref, h_scr, acc_scr):
    j = pl.program_id(1)

    @pl.when(j == 0)
    def _():
        h_scr[...] = _rms_rows(x_ref[...], g_ref[...]).astype(BF16)
        acc_scr[...] = jnp.zeros_like(acc_scr)

    h = h_scr[...]
    a = _dot(h, wg_ref[...])
    u = _dot(h, wu_ref[...])
    t = a * jax.nn.sigmoid(a) * u
    acc_scr[...] += _dot(t.astype(BF16), wd_ref[...])

    @pl.when(j == pl.num_programs(1) - 1)
    def _():
        o_ref[...] = x_ref[...] + 0.5 * acc_scr[...]


def _ffn(x, g, wg, wu, wd, tm, tf=512):
    n, d = x.shape
    dff = wg.shape[1]
    return pl.pallas_call(
        _ffn_kernel,
        grid=(n // tm, dff // tf),
        in_specs=[
            pl.BlockSpec((tm, d), lambda i, j: (i, 0)),
            pl.BlockSpec((1, d), lambda i, j: (0, 0)),
            pl.BlockSpec((d, tf), lambda i, j: (0, j)),
            pl.BlockSpec((d, tf), lambda i, j: (0, j)),
            pl.BlockSpec((tf, d), lambda i, j: (j, 0)),
        ],
        out_specs=pl.BlockSpec((tm, d), lambda i, j: (i, 0)),
        out_shape=jax.ShapeDtypeStruct((n, d), F32),
        scratch_shapes=[pltpu.VMEM((tm, d), BF16), pltpu.VMEM((tm, d), F32)],
        compiler_params=_cparams(("parallel", "arbitrary")),
        name="ffn",
    )(x, g.reshape(1, d), wg, wu, wd)


def _proj_kernel(x_ref, g_ref, w_ref, gain_ref, gmat_ref, *out_refs, segs):
    h = _rms_rows(x_ref[...], g_ref[...]).astype(BF16)
    off = 0
    out_refs = list(out_refs)
    for width, normed, gained, outs in segs:
        y = _dot(h, w_ref[:, off:off + width])
        if normed:
            y = _group_rms(y, gmat_ref[...])
        if gained:
            y = y * gain_ref[:, off:off + width]
        for _, head_width in outs:
            o_ref = out_refs.pop(0)
            if head_width:
                for hd in range(width // head_width):
                    o_ref[:, hd, :] = y[:, hd * head_width:(hd + 1) * head_width].astype(o_ref.dtype)
            else:
                o_ref[...] = y.astype(o_ref.dtype)
        off += width


def _proj(x, g, w, gain, segs, tm):
    n, d = x.shape
    wtot = w.shape[1]
    specs, shapes = [], []
    for width, _, _, outs in segs:
        for dt, head_width in outs:
            if head_width:
                nh = width // head_width
                specs.append(pl.BlockSpec((tm, nh, head_width), lambda i: (i, 0, 0)))
                shapes.append(jax.ShapeDtypeStruct((n, nh, head_width), dt))
            else:
                specs.append(pl.BlockSpec((tm, width), lambda i: (i, 0)))
                shapes.append(jax.ShapeDtypeStruct((n, width), dt))
    return pl.pallas_call(
        functools.partial(_proj_kernel, segs=tuple(segs)),
        grid=(n // tm,),
        in_specs=[
            pl.BlockSpec((tm, d), lambda i: (i, 0)),
            pl.BlockSpec((1, d), lambda i: (0, 0)),
            pl.BlockSpec((d, wtot), lambda i: (0, 0), pipeline_mode=pl.Buffered(1)),
            pl.BlockSpec((1, wtot), lambda i: (0, 0)),
            pl.BlockSpec((V7X_MXU_DIM, V7X_MXU_DIM), lambda i: (0, 0)),
        ],
        out_specs=specs,
        out_shape=shapes,
        compiler_params=_cparams(("parallel",)),
        name="proj",
    )(x, g.reshape(1, d), w, gain, _group_mean_matrix())


def _softmax_step_km(s, vt, m_ref, acc_ref):
    m_old = m_ref[...]
    m_new = jnp.maximum(m_old, jnp.max(s, axis=0, keepdims=True))
    alpha = jnp.exp2(m_old - m_new)
    p = jnp.exp2(s - m_new)
    acc_ref[...] = alpha * acc_ref[...] + _dot(vt, p.astype(BF16))
    m_ref[...] = m_new


def _cast_rows(src_ref, dst_ref, rows, dst_off=0, step=512):
    def body(i, c):
        r = pl.multiple_of(i * step, step)
        dst_ref[pl.ds(dst_off + r, step), :] = src_ref[0, pl.ds(r, step), :].astype(BF16)
        return c
    lax.fori_loop(0, rows // step, body, 0)


def _attn_a_kernel(lam_ref, brange_ref, q_ref, k_ref, v_ref, bias_ref, o_ref,
                   vtb, s_scr, knorm, m1, a1, m2, a2):
    h = pl.program_id(1)
    qi = pl.program_id(2)
    kbf = k_ref.at[0]
    seq = kbf.shape[0]
    tq = q_ref.shape[1]
    tk = vtb.shape[2]
    dv = v_ref.shape[2]

    @pl.when(qi == 0)
    def _():
        lane = lax.broadcasted_iota(jnp.int32, (tk, LANES), 1)

        def prep(j, c):
            r = pl.multiple_of(j * tk, tk)
            kb = kbf[pl.ds(r, tk), :]
            vtb[j, 0:dv, :] = v_ref[0, pl.ds(r, tk), :].astype(F32).T.astype(BF16)
            vtb[j, dv:, :] = jnp.ones((vtb.shape[1] - dv, tk), BF16)
            ksq = kb.astype(F32) * kb.astype(F32)
            n1 = jnp.max(jnp.sum(jnp.where(lane < HEAD_DIM, ksq, 0.0), axis=1, keepdims=True))
            n2 = jnp.max(jnp.sum(jnp.where(lane >= HEAD_DIM, ksq, 0.0), axis=1, keepdims=True))
            return jnp.maximum(c[0], n1), jnp.maximum(c[1], n2)
        n1, n2 = lax.fori_loop(0, seq // tk, prep, (jnp.float32(0.0), jnp.float32(0.0)))
        knorm[0] = n1
        knorm[1] = n2

    qt = q_ref[0].astype(F32).T
    sub = lax.broadcasted_iota(jnp.int32, qt.shape, 0)
    q1t = jnp.where(sub < HEAD_DIM, qt, 0.0).astype(BF16)
    q2t = jnp.where(sub >= HEAD_DIM, qt, 0.0).astype(BF16)
    states = ((m1, a1), (m2, a2))
    for m_ref, a_ref in states:
        a_ref[...] = jnp.zeros_like(a_ref)

    far_bias, bias_max, bias_min = brange_ref[3 * h], brange_ref[3 * h + 1], brange_ref[3 * h + 2]
    qsq = qt * qt
    bound1 = jnp.sqrt(jnp.sum(jnp.where(sub < HEAD_DIM, qsq, 0.0), axis=0, keepdims=True) * knorm[0])
    bound2 = jnp.sqrt(jnp.sum(jnp.where(sub >= HEAD_DIM, qsq, 0.0), axis=0, keepdims=True) * knorm[1])
    bounds = (bound1 * BOUND_SLACK + bias_max, bound2 * BOUND_SLACK + bias_max)
    spread = 2.0 * BOUND_SLACK * jnp.maximum(jnp.max(bound1), jnp.max(bound2)) + (bias_max - bias_min)
    bounded = spread <= BOUNDED_EXP2_SPAN

    @pl.when(bounded)
    def _():
        qmaps = ((q1t, bounds[0], a1), (q2t, bounds[1], a2))

        def weights(j, adds):
            kb = kbf[pl.ds(pl.multiple_of(j * tk, tk), tk), :]
            return [jnp.exp2(_dot(kb, qmt) + add).astype(BF16) for (qmt, _, _), add in zip(qmaps, adds)]

        def special(j, tile):
            ps = weights(j, [tile - shift for _, shift, _ in qmaps])
            for p, (_, _, a_ref) in zip(ps, qmaps):
                a_ref[...] += _dot(vtb[j], p)

        special(qi, bias_ref[0, 0])

        @pl.when(qi >= 1)
        def _():
            special(qi - 1, bias_ref[0, 1])

        far_adds = [far_bias - shift for _, shift, _ in qmaps]
        n_far = jnp.maximum(qi - 1, 0)
        odd = n_far & 1

        @pl.when(odd == 1)
        def _():
            for p, (_, _, a_ref) in zip(weights(0, far_adds), qmaps):
                a_ref[...] += _dot(vtb[0], p)

        def pair(i, c):
            j = odd + 2 * i
            pa = weights(j, far_adds)
            pb = weights(j + 1, far_adds)
            for p0, p1, (_, _, a_ref) in zip(pa, pb, qmaps):
                a_ref[...] += _dot(vtb[j], p0) + _dot(vtb[j + 1], p1)
            return c
        lax.fori_loop(0, lax.shift_right_logical(n_far, 1), pair, 0)

    @pl.when(jnp.logical_not(bounded))
    def _():
        for m_ref, _ in states:
            m_ref[...] = jnp.full_like(m_ref, NEG)
        n_blocks = qi + 1

        def scores(t, buf):
            kb = kbf[pl.ds(pl.multiple_of((qi - t) * tk, tk), tk), :]
            s_scr[buf, 0] = _dot(kb, q1t)
            s_scr[buf, 1] = _dot(kb, q2t)

        def update(t, buf):
            bias = bias_ref[0, jnp.minimum(t, bias_ref.shape[1] - 1)]
            vt = vtb[qi - t]
            for mi, (m_ref, a_ref) in enumerate(states):
                _softmax_step_km(s_scr[buf, mi] + bias, vt, m_ref, a_ref)

        odd = n_blocks & 1

        @pl.when(odd == 1)
        def _():
            scores(0, 0)
            update(0, 0)

        n_pairs = lax.shift_right_logical(n_blocks, 1)

        @pl.when(n_pairs > 0)
        def _():
            scores(odd, 0)

        def pair(p, c):
            t0 = odd + 2 * p
            scores(t0 + 1, 1)
            update(t0, 0)
            scores(jnp.minimum(t0 + 2, qi), 0)
            update(t0 + 1, 1)
            return c
        lax.fori_loop(0, n_pairs, pair, 0)

    out_t = (a1[0:dv, :] / a1[dv:dv + 1, :]
             - lam_ref[0] * (a2[0:dv, :] / a2[dv:dv + 1, :]))
    o_ref[0] = out_t.T


def _attn_a(lam, brange, qa, ka, va, bias):
    b, s, w = qa.shape
    nh = w // LANES
    tq = tk = MIXER_A_TILE
    vrows = LANES + BF16_SUBLANES
    return pl.pallas_call(
        _attn_a_kernel,
        grid=(b, nh, s // tq),
        in_specs=[
            pl.BlockSpec(memory_space=pltpu.SMEM),
            pl.BlockSpec(memory_space=pltpu.SMEM),
            pl.BlockSpec((1, tq, LANES), lambda bi, h, qi: (bi, qi, h)),
            pl.BlockSpec((1, s, LANES), lambda bi, h, qi: (bi, 0, h)),
            pl.BlockSpec((1, s, LANES), lambda bi, h, qi: (bi, 0, h)),
            pl.BlockSpec((1,) + bias.shape[1:], lambda bi, h, qi: (h, 0, 0, 0)),
        ],
        out_specs=pl.BlockSpec((1, tq, LANES), lambda bi, h, qi: (bi, qi, h)),
        out_shape=jax.ShapeDtypeStruct((b, s, w), F32),
        scratch_shapes=[
            pltpu.VMEM((s // tk, vrows, tk), BF16),
            pltpu.VMEM((2, 2, tk, tq), F32), pltpu.SMEM((2,), F32),
            pltpu.VMEM((1, tq), F32), pltpu.VMEM((vrows, tq), F32),
            pltpu.VMEM((1, tq), F32), pltpu.VMEM((vrows, tq), F32),
        ],
        compiler_params=_cparams(("parallel", "parallel", "arbitrary")),
        name="mixer_a",
    )(lam, brange, qa, ka, va, bias)


def _stick_block(z, valid, umat, carry):
    sp = jnp.maximum(z, 0.0) + jnp.log(1.0 + jnp.exp(-jnp.abs(z)))
    log1m = -sp
    if valid is not None:
        log1m = jnp.where(valid, log1m, 0.0)
    hi = log1m.astype(BF16)
    lo = (log1m - hi.astype(F32)).astype(BF16)
    after = _dot(hi, umat) + _dot(lo, umat)
    w = jnp.exp(z - sp + after + carry)
    if valid is not None:
        w = jnp.where(valid, w, 0.0)
    return w, jnp.sum(log1m, axis=1, keepdims=True)


def _stick_block_km(z, valid, umat, carry):
    sp = jnp.maximum(z, 0.0) + jnp.log(1.0 + jnp.exp(-jnp.abs(z)))
    log1m = -sp
    if valid is not None:
        log1m = jnp.where(valid, log1m, 0.0)
    hi = log1m.astype(BF16)
    lo = (log1m - hi.astype(F32)).astype(BF16)
    after = _dot(umat, hi) + _dot(umat, lo)
    w = jnp.exp(z - sp + after + carry)
    if valid is not None:
        w = jnp.where(valid, w, 0.0)
    return w, jnp.sum(log1m, axis=0, keepdims=True)


def _attn_b_kernel(q_ref, k_ref, v_ref, u_ref, o_ref, kbf, vtb, c_scr, acc_scr):
    step = pl.program_id(2)
    seq = kbf.shape[0]
    tk = vtb.shape[2]
    n_halves = q_ref.shape[1] // tk
    n_heads = LANES // HEAD_DIM

    @pl.when(step == 0)
    def _():
        def prep(j, c):
            r = pl.multiple_of(j * tk, tk)
            kbf[pl.ds(r, tk), :] = k_ref[0, pl.ds(r, tk), :].astype(BF16)
            vtb[j] = v_ref[0, pl.ds(r, tk), :].T.astype(BF16)
            return c
        lax.fori_loop(0, seq // tk, prep, 0)

    sub = lax.broadcasted_iota(jnp.int32, (LANES, tk), 0)
    qts = []
    for half in range(n_halves):
        qt = q_ref[0, half * tk:(half + 1) * tk, :].astype(F32).T
        qts.append([jnp.where((sub >= hh * HEAD_DIM) & (sub < (hh + 1) * HEAD_DIM), qt, 0.0).astype(BF16)
                    for hh in range(n_heads)])
    krow = lax.broadcasted_iota(jnp.int32, (tk, tk), 0)
    qcol = lax.broadcasted_iota(jnp.int32, (tk, tk), 1)
    umat = u_ref[...]

    def blocks(half, js, valids, first):
        kbs = [kbf[pl.ds(pl.multiple_of(j * tk, tk), tk), :] for j in js]
        cmax = None
        for hh in range(n_heads):
            carry = 0.0 if first else c_scr[half, hh]
            pv = None
            for j, kb, valid in zip(js, kbs, valids):
                w, cs = _stick_block_km(_dot(kb, qts[half][hh]), valid, umat, carry)
                term = _dot(vtb[j], w.astype(BF16))
                pv = term if pv is None else pv + term
                carry = carry + cs
            acc_scr[half, hh] = pv if first else acc_scr[half, hh] + pv
            c_scr[half, hh] = carry
            cm = jnp.max(carry)
            cmax = cm if cmax is None else jnp.maximum(cmax, cm)
        return cmax

    own_valid = krow < qcol
    everywhere = krow >= 0
    cmaxes = []
    for half in range(n_halves):
        qb = step * n_halves + half
        prev_valid = None if half >= 1 else everywhere & (qb >= 1)
        cmaxes.append(blocks(half, [qb, jnp.maximum(qb - 1, 0)], [own_valid, prev_valid], True))

    for half in range(n_halves):
        qb = step * n_halves + half

        def cond(st):
            j, cmax = st
            return (j >= 0) & (cmax > STICK_SKIP)

        def body(st, half=half):
            j, _ = st
            return j - 2, blocks(half, [j, jnp.maximum(j - 1, 0)], [None, everywhere & (j >= 1)], False)

        lax.while_loop(cond, body, (qb - 2, cmaxes[half]))
        out_t = jnp.where(sub < HEAD_DIM, acc_scr[half, 0], acc_scr[half, 1])
        o_ref[0, half * tk:(half + 1) * tk, :] = out_t.T.astype(o_ref.dtype)


def _strict_lower(n):
    return jnp.asarray(np.tril(np.ones((n, n)), -1), BF16)


def _strict_upper(n):
    return jnp.asarray(np.triu(np.ones((n, n)), 1), BF16)


def _attn_b(qb, kb, vb):
    b, s, w = qb.shape
    n_halves = 2
    tq = n_halves * ATTN_TK
    return pl.pallas_call(
        _attn_b_kernel,
        grid=(b, w // LANES, s // tq),
        in_specs=[
            pl.BlockSpec((1, tq, LANES), lambda bi, h, qi: (bi, qi, h)),
            pl.BlockSpec((1, s, LANES), lambda bi, h, qi: (bi, 0, h)),
            pl.BlockSpec((1, s, LANES), lambda bi, h, qi: (bi, 0, h)),
            pl.BlockSpec((ATTN_TK, ATTN_TK), lambda bi, h, qi: (0, 0)),
        ],
        out_specs=pl.BlockSpec((1, tq, LANES), lambda bi, h, qi: (bi, qi, h)),
        out_shape=jax.ShapeDtypeStruct((b, s, w), BF16),
        scratch_shapes=[
            pltpu.VMEM((s, LANES), BF16), pltpu.VMEM((s // ATTN_TK, LANES, ATTN_TK), BF16),
            pltpu.VMEM((n_halves, LANES // HEAD_DIM, 1, ATTN_TK), F32),
            pltpu.VMEM((n_halves, LANES // HEAD_DIM, LANES, ATTN_TK), F32),
        ],
        compiler_params=_cparams(("parallel", "parallel", "arbitrary")),
        name="mixer_b",
    )(qb, kb, vb, _strict_upper(ATTN_TK))


def _attn_c_kernel(q_ref, k_ref, v_ref, bias_ref, o_ref, kbf, vbf, s_scr):
    qi = pl.program_id(2)
    seq = k_ref.shape[1]
    tq = q_ref.shape[1]
    pad = C_PREV * CHUNK
    n_heads = LANES // HEAD_DIM

    @pl.when(qi == 0)
    def _():
        kbf[0:pad, :] = jnp.zeros((pad, LANES), BF16)
        vbf[0:pad, 0:LANES] = jnp.zeros((pad, LANES), BF16)
        vbf[:, LANES:] = jnp.ones((vbf.shape[0], vbf.shape[1] - LANES), BF16)
        _cast_rows(k_ref, kbf, seq, dst_off=pad)

        def vcast(i, c):
            r = pl.multiple_of(i * tq, tq)
            vbf[pl.ds(pad + r, tq), 0:LANES] = v_ref[0, pl.ds(r, tq), :].astype(BF16)
            return c
        lax.fori_loop(0, seq // tq, vcast, 0)

    q = q_ref[0]
    lane = lax.broadcasted_iota(jnp.int32, q.shape, 1)
    wstart = pl.multiple_of(qi * tq, tq)
    kw = kbf[pl.ds(wstart, BAND_W), :]
    vw = vbf[pl.ds(wstart, BAND_W), :]
    col = lax.broadcasted_iota(jnp.int32, (tq, BAND_W), 1)
    in_seq = col >= pad - qi * tq
    for hh in range(n_heads):
        qm = _keep_lanes(q, hh * HEAD_DIM, (hh + 1) * HEAD_DIM)
        s_scr[hh] = jnp.where(in_seq, _dot_nt(qm, kw) + bias_ref[hh], NEG)
    outs = []
    for hh in range(n_heads):
        s = s_scr[hh]
        p = jnp.exp(s - jnp.max(s, axis=1, keepdims=True))
        acc = _dot(p.astype(BF16), vw)
        outs.append(acc[:, 0:LANES] / acc[:, LANES:])
    o_ref[0] = jnp.where(lane < HEAD_DIM, outs[0], outs[1]).astype(o_ref.dtype)


def _attn_c(qc, kc, vc, bias):
    b, s, w = qc.shape
    tq = ATTN_TQ
    pad = C_PREV * CHUNK
    hp = LANES // HEAD_DIM
    return pl.pallas_call(
        _attn_c_kernel,
        grid=(b, w // LANES, s // tq),
        in_specs=[
            pl.BlockSpec((1, tq, LANES), lambda bi, h, qi: (bi, qi, h)),
            pl.BlockSpec((1, s, LANES), lambda bi, h, qi: (bi, 0, h)),
            pl.BlockSpec((1, s, LANES), lambda bi, h, qi: (bi, 0, h)),
            pl.BlockSpec((hp, tq, BAND_W), lambda bi, h, qi: (h, 0, 0)),
        ],
        out_specs=pl.BlockSpec((1, tq, LANES), lambda bi, h, qi: (bi, qi, h)),
        out_shape=jax.ShapeDtypeStruct((b, s, w), BF16),
        scratch_shapes=[pltpu.VMEM((s + pad, LANES), BF16), pltpu.VMEM((s + pad, 2 * LANES), BF16),
                        pltpu.VMEM((hp, tq, BAND_W), F32)],
        compiler_params=_cparams(("parallel", "parallel", "arbitrary")),
        name="mixer_c",
    )(qc, kc, vc, bias)


def _merge_kernel(x_ref, oa_ref, ob_ref, oc_ref, g_ref, wgate_ref, sub_ref,
                  wa_ref, wb_ref, wc_ref, wout_ref, o_ref):
    x = x_ref[...]
    d = x.shape[1]
    h = _rms_rows(x, g_ref[...]).astype(BF16)
    oa = oa_ref[...]
    parts = []
    for c in range(oa.shape[1] // LANES):
        oc_ = oa[:, c * LANES:(c + 1) * LANES]
        parts.append(oc_ * lax.rsqrt(jnp.mean(oc_ * oc_, axis=-1, keepdims=True) + EPS))
    oan = (jnp.concatenate(parts, axis=1) * sub_ref[...]).astype(BF16)
    merged = jax.nn.sigmoid(_dot(h, wgate_ref[:, 0:d])) * _dot(oan, wa_ref[...])
    merged += jax.nn.sigmoid(_dot(h, wgate_ref[:, d:2 * d])) * _dot(ob_ref[...], wb_ref[...])
    merged += jax.nn.sigmoid(_dot(h, wgate_ref[:, 2 * d:3 * d])) * _dot(oc_ref[...], wc_ref[...])
    o_ref[...] = x + _dot(merged.astype(BF16), wout_ref[...])


def _merge(x, oa, ob, oc, g, wgate, sub, wa, wb, wc, wout, tm):
    n, d = x.shape
    full = lambda a: pl.BlockSpec(a.shape, lambda i: (0,) * a.ndim, pipeline_mode=pl.Buffered(1))
    rows = lambda a: pl.BlockSpec((tm, a.shape[1]), lambda i: (i, 0))
    g = g.reshape(1, d)
    return pl.pallas_call(
        _merge_kernel,
        grid=(n // tm,),
        in_specs=[rows(x), rows(oa), rows(ob), rows(oc), full(g), full(wgate), full(sub),
                  full(wa), full(wb), full(wc), full(wout)],
        out_specs=rows(x),
        out_shape=jax.ShapeDtypeStruct((n, d), F32),
        compiler_params=_cparams(("parallel",)),
        name="merge",
    )(x, oa, ob, oc, g, wgate, sub, wa, wb, wc, wout)


def _cross_kernel(x_ref, g_ref, wq_ref, gain_ref, gmat_ref, mk_ref, mv_ref, wo_ref, o_ref):
    x = x_ref[0]
    h = _rms_rows(x, g_ref[...]).astype(BF16)
    q = (_group_rms(_dot(h, wq_ref[...]), gmat_ref[...]) * gain_ref[...]).astype(BF16)
    mk = mk_ref[0]
    mv = mv_ref[0]
    lane = lax.broadcasted_iota(jnp.int32, q.shape, 1)
    o = jnp.zeros(q.shape, F32)
    for hh in range(q.shape[1] // HEAD_DIM):
        in_head = (lane >= hh * HEAD_DIM) & (lane < (hh + 1) * HEAD_DIM)
        s = _dot_nt(_keep_lanes(q, hh * HEAD_DIM, (hh + 1) * HEAD_DIM), mk)
        p = jnp.exp(s - jnp.max(s, axis=1, keepdims=True))
        l = jnp.sum(p, axis=1, keepdims=True)
        o = jnp.where(in_head, _dot(p.astype(BF16), mv) / l, o)
    o_ref[0] = x + _dot(o.astype(BF16), wo_ref[...])


def _cross(x, g, wq, gain, mk, mv, wo, tm):
    b, s, d = x.shape
    full = lambda a: pl.BlockSpec(a.shape, lambda bi, i: (0,) * a.ndim)
    g = g.reshape(1, d)
    gmat = _group_mean_matrix()
    return pl.pallas_call(
        _cross_kernel,
        grid=(b, s // tm),
        in_specs=[
            pl.BlockSpec((1, tm, d), lambda bi, i: (bi, i, 0)),
            full(g), full(wq), full(gain), full(gmat),
            pl.BlockSpec((1,) + mk.shape[1:], lambda bi, i: (bi, 0, 0)),
            pl.BlockSpec((1,) + mv.shape[1:], lambda bi, i: (bi, 0, 0)),
            full(wo),
        ],
        out_specs=pl.BlockSpec((1, tm, d), lambda bi, i: (bi, i, 0)),
        out_shape=jax.ShapeDtypeStruct((b, s, d), F32),
        compiler_params=_cparams(("parallel", "parallel")),
        name="cross",
    )(x, g, wq, gain, gmat, mk, mv, wo)


def _sample_attn_kernel(lam_ref,
                        qa_ref, kan_ref, van_ref, kac_ref, vac_ref, bac_ref, ban_ref,
                        qb_ref, kbn_ref, vbn_ref, kbc_ref, vbc_ref, ubig_ref, usmall_ref,
                        qc_ref, kcn_ref, vcn_ref, kcc_ref, vcc_ref, bcc_ref, bcn_ref,
                        oa_ref, ob_ref, oc_ref):
    lam = lam_ref[0]
    ns = qa_ref.shape[1]
    bf = lambda r: r[0].astype(BF16)

    def heads(q, width):
        for hh in range(LANES // width):
            yield hh, None, _keep_lanes(q, hh * width, (hh + 1) * width)

    def softmax2(s_c, s_n, exp=jnp.exp):
        m = jnp.maximum(jnp.max(s_c, axis=1, keepdims=True), jnp.max(s_n, axis=1, keepdims=True))
        p_c = exp(s_c - m)
        p_n = exp(s_n - m)
        inv = 1.0 / (jnp.sum(p_c, axis=1, keepdims=True) + jnp.sum(p_n, axis=1, keepdims=True))
        return p_c * inv, p_n * inv

    for h in range(qa_ref.shape[2] // LANES):
        sl = slice(h * LANES, (h + 1) * LANES)
        q = qa_ref[0, :, sl]
        k_c = kac_ref[0, :, h, :].astype(BF16)
        k_n = kan_ref[0, :, sl].astype(BF16)
        maps = []
        for _, _, qm in heads(q, HEAD_DIM):
            maps.append(softmax2(_dot_nt(qm, k_c) + bac_ref[h], _dot_nt(qm, k_n) + ban_ref[h], exp=jnp.exp2))
        a_c = (maps[0][0] - lam * maps[1][0]).astype(BF16)
        a_n = (maps[0][1] - lam * maps[1][1]).astype(BF16)
        oa_ref[0, :, sl] = (_dot(a_c, vac_ref[0, :, h, :].astype(BF16))
                            + _dot(a_n, van_ref[0, :, sl].astype(BF16)))

    past = kbc_ref.shape[1]
    tk = ubig_ref.shape[0]
    row = lax.broadcasted_iota(jnp.int32, (ns, ns), 0)
    col = lax.broadcasted_iota(jnp.int32, (ns, ns), 1)
    for pr in range(qb_ref.shape[2] // LANES):
        sl = slice(pr * LANES, (pr + 1) * LANES)
        q = qb_ref[0, :, sl]
        k_n = kbn_ref[0, :, sl].astype(BF16)
        v_n = vbn_ref[0, :, sl].astype(BF16)
        outs = []
        for _, _, qm in heads(q, HEAD_DIM):
            w, carry = _stick_block(_dot_nt(qm, k_n), col < row, usmall_ref[...], 0.0)
            acc = _dot(w.astype(BF16), v_n)
            for j in range(past // tk - 1, -1, -1):
                k_c = kbc_ref[0, j * tk:(j + 1) * tk, sl].astype(BF16)
                v_c = vbc_ref[0, j * tk:(j + 1) * tk, sl].astype(BF16)
                w, rs = _stick_block(_dot_nt(qm, k_c), None, ubig_ref[...], carry)
                acc += _dot(w.astype(BF16), v_c)
                carry = carry + rs
            outs.append(acc)
        lane = lax.broadcasted_iota(jnp.int32, q.shape, 1)
        ob_ref[0, :, sl] = jnp.where(lane < HEAD_DIM, outs[0], outs[1]).astype(ob_ref.dtype)

    for pr in range(qc_ref.shape[2] // LANES):
        sl = slice(pr * LANES, (pr + 1) * LANES)
        q = qc_ref[0, :, sl]
        k_c = kcc_ref[0, :, sl].astype(BF16)
        k_n = kcn_ref[0, :, sl].astype(BF16)
        v_c = vcc_ref[0, :, sl].astype(BF16)
        v_n = vcn_ref[0, :, sl].astype(BF16)
        outs = []
        for hh, _, qm in heads(q, HEAD_DIM):
            hd = pr * (LANES // HEAD_DIM) + hh
            p_c, p_n = softmax2(_dot_nt(qm, k_c) + bcc_ref[hd], _dot_nt(qm, k_n) + bcn_ref[hd])
            outs.append(_dot(p_c.astype(BF16), v_c) + _dot(p_n.astype(BF16), v_n))
        lane = lax.broadcasted_iota(jnp.int32, q.shape, 1)
        oc_ref[0, :, sl] = jnp.where(lane < HEAD_DIM, outs[0], outs[1]).astype(oc_ref.dtype)


def _sample_attn(layer, lam, qa, ka, va, cak, cav, bac, ban, qb, kb, vb, cbk, cbv,
                 qc, kc, vc, cck, ccv, bcc, bcn):
    b, ns, _ = qa.shape
    per_b = lambda a: pl.BlockSpec((1,) + a.shape[1:], lambda bi: (bi,) + (0,) * (a.ndim - 1))
    full = lambda a: pl.BlockSpec(a.shape, lambda bi: (0,) * a.ndim)
    layer_b = lambda a: pl.BlockSpec((None, 1) + a.shape[2:], lambda bi: (layer, bi) + (0,) * (a.ndim - 2))
    ubig = _strict_lower(ATTN_TK)
    usmall = _strict_lower(ns)
    args = [qa, ka, va, cak, cav, bac, ban, qb, kb, vb, cbk, cbv, ubig, usmall,
            qc, kc, vc, cck, ccv, bcc, bcn]
    specs = [per_b(qa), per_b(ka), per_b(va), layer_b(cak), layer_b(cav), full(bac), full(ban),
             per_b(qb), per_b(kb), per_b(vb), per_b(cbk), per_b(cbv), full(ubig), full(usmall),
             per_b(qc), per_b(kc), per_b(vc), per_b(cck), per_b(ccv), full(bcc), full(bcn)]
    return pl.pallas_call(
        _sample_attn_kernel,
        grid=(b,),
        in_specs=[pl.BlockSpec(memory_space=pltpu.SMEM)] + specs,
        out_specs=[per_b(qa), per_b(qb), per_b(qc)],
        out_shape=[jax.ShapeDtypeStruct(qa.shape, F32), jax.ShapeDtypeStruct(qb.shape, BF16),
                   jax.ShapeDtypeStruct(qc.shape, BF16)],
        compiler_params=_cparams(("parallel",)),
        name="sample_mixers",
    )(lam, *args)


def _t5_bucket_np(rel):
    half = T5_BUCKETS // 2
    max_exact = half // 2
    n = np.abs(rel)
    nf = np.maximum(n, 1).astype(np.float64)
    large = max_exact + (np.log(nf / max_exact) / math.log(T5_MAX_DIST / max_exact)
                         * (half - max_exact)).astype(np.int64)
    large = np.minimum(large, half - 1)
    return np.where(rel > 0, half, 0) + np.where(n < max_exact, n, large)


def _toeplitz(lookup, n_rows, n_cols):
    period = n_rows + n_cols
    slot = np.arange(period)
    diff = np.minimum((slot + n_rows - 1) % period - (n_rows - 1), n_cols - 1)
    vec = lookup(diff).astype(F32)
    flat = jnp.tile(vec, (1, n_rows))[:, :n_rows * (period - 1)]
    return flat.reshape(vec.shape[0], n_rows, period - 1)[:, :, :n_cols]


def _t5_bias_table(t5_bias, qpos, kpos, key_major=False):
    t5_rows = lambda rel: t5_bias[_t5_bucket_np(rel)].T
    mask = (kpos[None, :] // CHUNK) <= (qpos[:, None] // CHUNK)
    if key_major:
        table = _toeplitz(lambda dd: t5_rows(kpos[0] - qpos[0] - dd), len(kpos), len(qpos))
        mask = mask.T
    else:
        table = _toeplitz(lambda dd: t5_rows(kpos[0] - qpos[0] + dd), len(qpos), len(kpos))
    return jnp.where(jnp.asarray(mask)[None], table, NEG)


def _band_bias_table(rel_table, qpos, kpos):
    lookup = lambda dd: rel_table[:, np.clip(kpos[0] - qpos[0] + dd, -REL_CLIP, REL_CLIP) + REL_CLIP]
    table = _toeplitz(lookup, len(qpos), len(kpos))
    qc = qpos[:, None] // CHUNK
    kc = kpos[None, :] // CHUNK
    mask = (kpos[None, :] >= 0) & (kc <= qc) & (kc >= qc - C_PREV)
    return jnp.where(jnp.asarray(mask)[None], table, NEG)


def _lambda_init(layer):
    return 0.8 - 0.6 * math.exp(-0.3 * layer)


def kernel(x_prompt, x_sample, mem_prompt, cache_a_k, cache_a_v, cache_b_k, cache_b_v, cache_c_k, cache_c_v, cache_mem_k, cache_mem_v, t5_bias, ffn1_norm, ffn1_wg, ffn1_wu, ffn1_wd, mix_norm, w_in, a_qnorm, a_knorm, a_lq1, a_lk1, a_lq2, a_lk2, a_subln, c_qnorm, c_knorm, c_rel_bias, w_br_a, w_br_b, w_br_c, w_out, x_norm, mem_norm, x_wq, x_wkv, x_qnorm, x_knorm, x_wo, ffn2_norm, ffn2_wg, ffn2_wu, ffn2_wd):
    bp, sp, d = x_prompt.shape
    bs, ns, _ = x_sample.shape
    depth = w_in.shape[0]
    past = cache_a_k.shape[2]
    w_buf = cache_c_k.shape[2]
    n_mem = mem_prompt.shape[1]
    wa = cache_a_k.shape[3] * cache_a_k.shape[4]
    wb = cache_b_k.shape[3] * cache_b_k.shape[4]
    wc = cache_c_k.shape[3] * cache_c_k.shape[4]
    wx = cache_mem_k.shape[3] * cache_mem_k.shape[4]
    h_a = cache_a_k.shape[3]
    n_qkv = 3 * (wa + wb + wc)
    w_keep = min(C_PREV * CHUNK, sp)
    scale = HEAD_DIM ** -0.5
    assert sp % ATTN_TQ == 0 and ATTN_TQ == ATTN_TK and ATTN_TQ == 4 * CHUNK and sp >= w_keep
    assert past % ATTN_TK == 0
    assert sp % MIXER_A_TILE == 0 and MIXER_A_TILE % CHUNK == 0 and T5_MAX_DIST <= MIXER_A_TILE

    tile = lambda g, reps: jnp.tile(g.astype(F32), reps)
    bf = lambda a: a.astype(BF16)

    loc_a = np.arange(MIXER_A_TILE)
    loc = np.arange(ATTN_TQ)
    a_far = t5_bias[T5_BUCKETS // 2 - 1].astype(F32)
    a_bias = LOG2E * jnp.stack([
        _t5_bias_table(t5_bias, loc_a + MIXER_A_TILE, loc_a + MIXER_A_TILE, key_major=True),
        _t5_bias_table(t5_bias, loc_a + MIXER_A_TILE, loc_a, key_major=True),
        jnp.broadcast_to(a_far[:, None, None], (h_a, MIXER_A_TILE, MIXER_A_TILE))], axis=1)
    t5_log2 = LOG2E * t5_bias.astype(F32)
    a_brange = jnp.stack([LOG2E * a_far, jnp.max(t5_log2, axis=0), jnp.min(t5_log2, axis=0)], axis=1).reshape(-1)
    qpos_s = past + np.arange(ns)
    a_bias_sc = LOG2E * _t5_bias_table(t5_bias, qpos_s, np.arange(past))
    a_bias_sn = LOG2E * _t5_bias_table(t5_bias, qpos_s, qpos_s)
    kpos_sb = past - w_buf + np.arange(w_buf + ns)

    xp = x_prompt.reshape(bp * sp, d)
    xs = x_sample.reshape(bs * ns, d)
    mem = mem_prompt.reshape(bp * n_mem, d)
    outs = {k: [] for k in ("ak_p", "av_p", "bk_p", "bv_p", "ck_p", "cv_p", "mk_p", "mv_p",
                            "ak_s", "av_s", "bk_s", "bv_s", "ck_s", "cv_s")}
    flat = lambda dt: ((dt, 0),)
    dv_a = wa // h_a
    a_kv_prompt = ((F32, dv_a), (BF16, 0))
    qkv_tail = [(wb, False, True, flat(BF16)), (wb, False, False, flat(F32)), (wb, False, False, flat(F32)),
                (wc, True, True, flat(BF16)), (wc, True, True, flat(F32)), (wc, False, False, flat(F32))]
    qkv_segs_p = [(wa, True, True, flat(BF16)), (wa, True, True, a_kv_prompt),
                  (wa, False, False, a_kv_prompt)] + qkv_tail
    qkv_segs_s = [(wa, True, True, flat(BF16)), (wa, True, True, flat(F32)), (wa, False, False, flat(F32))] + qkv_tail

    for i in range(depth):
        wg1, wu1, wd1 = bf(ffn1_wg[i]), bf(ffn1_wu[i]), bf(ffn1_wd[i])
        wg2, wu2, wd2 = bf(ffn2_wg[i]), bf(ffn2_wu[i]), bf(ffn2_wd[i])
        w_qkv, w_gate = bf(w_in[i][:, :n_qkv]), bf(w_in[i][:, n_qkv:])
        qkv_gain = jnp.concatenate([
            tile(a_qnorm[i], wa // HEAD_DIM) * (scale * LOG2E), tile(a_knorm[i], wa // HEAD_DIM), jnp.ones((wa,), F32),
            jnp.full((wb,), scale, F32), jnp.ones((2 * wb,), F32),
            tile(c_qnorm[i], wc // HEAD_DIM) * scale, tile(c_knorm[i], wc // HEAD_DIM), jnp.ones((wc,), F32),
        ]).reshape(1, n_qkv)
        sub = (tile(a_subln[i], h_a) * (1.0 - _lambda_init(i))).reshape(1, wa)
        wbr = bf(w_br_a[i]), bf(w_br_b[i]), bf(w_br_c[i])
        wout = bf(w_out[i])
        wq, wo = bf(x_wq[i]), bf(x_wo[i])
        xq_gain = (tile(x_qnorm[i], wx // HEAD_DIM) * scale).reshape(1, wx)
        dot64 = lambda a, b: jnp.exp(jnp.sum(a.astype(F32) * b.astype(F32)))
        lam = (dot64(a_lq1[i], a_lk1[i]) - dot64(a_lq2[i], a_lk2[i]) + _lambda_init(i)).reshape(1)
        c_bias_p = _band_bias_table(c_rel_bias[i], C_PREV * CHUNK + loc, np.arange(BAND_W))
        c_bias_sc = _band_bias_table(c_rel_bias[i], qpos_s, kpos_sb[:w_buf])
        c_bias_sn = _band_bias_table(c_rel_bias[i], qpos_s, kpos_sb[w_buf:])

        xp = _ffn(xp, ffn1_norm[i], wg1, wu1, wd1, tm=1024, tf=1024)
        qa, ka, ka_bf, va, va_bf, qb, kb, vb, qc, kc, vc = _proj(xp, mix_norm[i], w_qkv, qkv_gain, qkv_segs_p, tm=1024)
        r3 = lambda a: a.reshape(bp, sp, a.shape[-1])
        oa = _attn_a(lam, a_brange, r3(qa), r3(ka_bf), r3(va_bf), a_bias)
        ob = _attn_b(r3(qb), r3(kb), r3(vb))
        oc = _attn_c(r3(qc), r3(kc), r3(vc), c_bias_p)
        xp = _merge(xp, oa.reshape(-1, wa), ob.reshape(-1, wb), oc.reshape(-1, wc), mix_norm[i], w_gate, sub,
                    *wbr, wout, tm=1024)
        mk, mv = _proj(mem, mem_norm[i], bf(x_wkv[i]),
                       jnp.concatenate([tile(x_knorm[i], wx // HEAD_DIM), jnp.ones((wx,), F32)]).reshape(1, 2 * wx),
                       [(wx, True, True, flat(F32)), (wx, False, False, flat(F32))], tm=n_mem)
        mk3, mv3 = mk.reshape(bp, n_mem, wx), mv.reshape(bp, n_mem, wx)
        xp = _cross(xp.reshape(bp, sp, d), x_norm[i], wq, xq_gain, bf(mk3), bf(mv3), wo, tm=1024).reshape(-1, d)
        xp = _ffn(xp, ffn2_norm[i], wg2, wu2, wd2, tm=1024, tf=1024)
        outs["ak_p"].append(ka.reshape(bp, sp, h_a, -1))
        outs["av_p"].append(va.reshape(bp, sp, h_a, -1))
        outs["bk_p"].append(kb.reshape(bp, sp, -1, HEAD_DIM))
        outs["bv_p"].append(vb.reshape(bp, sp, -1, HEAD_DIM))
        outs["ck_p"].append(r3(kc)[:, sp - w_keep:].reshape(bp, w_keep, -1, HEAD_DIM))
        outs["cv_p"].append(r3(vc)[:, sp - w_keep:].reshape(bp, w_keep, -1, HEAD_DIM))
        outs["mk_p"].append(mk3.reshape(bp, n_mem, -1, HEAD_DIM))
        outs["mv_p"].append(mv3.reshape(bp, n_mem, -1, HEAD_DIM))

        xs = _ffn(xs, ffn1_norm[i], wg1, wu1, wd1, tm=bs * ns)
        qa, ka, va, qb, kb, vb, qc, kc, vc = _proj(xs, mix_norm[i], w_qkv, qkv_gain, qkv_segs_s, tm=bs * ns)
        s3 = lambda a: a.reshape(bs, ns, a.shape[-1])
        c3 = lambda a: a.reshape(bs, a.shape[1], -1)
        oa, ob, oc = _sample_attn(
            i, lam, s3(qa), s3(ka), s3(va), cache_a_k, cache_a_v, a_bias_sc, a_bias_sn,
            s3(qb), s3(kb), s3(vb), c3(cache_b_k[i]), c3(cache_b_v[i]),
            s3(qc), s3(kc), s3(vc), c3(cache_c_k[i]), c3(cache_c_v[i]), c_bias_sc, c_bias_sn)
        xs = _merge(xs, oa.reshape(-1, wa), ob.reshape(-1, wb), oc.reshape(-1, wc), mix_norm[i], w_gate, sub,
                    *wbr, wout, tm=bs * ns)
        xs = _cross(xs.reshape(bs, ns, d), x_norm[i], wq, xq_gain, bf(c3(cache_mem_k[i])), bf(c3(cache_mem_v[i])),
                    wo, tm=ns).reshape(-1, d)
        xs = _ffn(xs, ffn2_norm[i], wg2, wu2, wd2, tm=bs * ns)
        outs["ak_s"].append(ka.reshape(bs, ns, h_a, -1))
        outs["av_s"].append(va.reshape(bs, ns, h_a, -1))
        outs["bk_s"].append(kb.reshape(bs, ns, -1, HEAD_DIM))
        outs["bv_s"].append(vb.reshape(bs, ns, -1, HEAD_DIM))
        outs["ck_s"].append(jnp.concatenate([cache_c_k[i], kc.reshape(bs, ns, -1, HEAD_DIM)], axis=1)[:, ns:])
        outs["cv_s"].append(jnp.concatenate([cache_c_v[i], vc.reshape(bs, ns, -1, HEAD_DIM)], axis=1)[:, ns:])

    st = lambda k: jnp.stack(outs[k])
    return (xp.reshape(bp, sp, d), xs.reshape(bs, ns, d),
            st("ak_p"), st("av_p"), st("bk_p"), st("bv_p"), st("ck_p"), st("cv_p"), st("mk_p"), st("mv_p"),
            st("ak_s"), st("av_s"), st("bk_s"), st("bv_s"), st("ck_s"), st("cv_s"))
```

```python
import functools
import math

import numpy as np
import jax
import jax.numpy as jnp
from jax import lax
from jax.experimental import pallas as pl
from jax.experimental.pallas import tpu as pltpu

F32 = jnp.float32
BF16 = jnp.bfloat16

EPS = 1e-6
HEAD_DIM = 64
CHUNK = 64
C_PREV = 8
REL_CLIP = 128
T5_BUCKETS = 32
T5_MAX_DIST = 128
LANES = 128
BF16_SUBLANES = 16
V7X_MXU_DIM = 256
NEG = -1e30
LOG2E = math.log2(math.e)
BOUND_SLACK = 1.001
BOUNDED_EXP2_SPAN = 100.0
STICK_SKIP = -110.0
V7X_VMEM_LIMIT_BYTES = 56 * 1024 * 1024

MIXER_A_TILE = 512
ATTN_TQ = 256
ATTN_TK = 256
BAND_W = (C_PREV + 4) * CHUNK


def _cparams(sem):
    return pltpu.CompilerParams(dimension_semantics=sem, vmem_limit_bytes=V7X_VMEM_LIMIT_BYTES)


def _rms_rows(x, g):
    return x * lax.rsqrt(jnp.mean(x * x, axis=-1, keepdims=True) + EPS) * g


def _dot(a, b):
    return jnp.dot(a, b, preferred_element_type=F32)


def _dot_nt(a, b):
    return lax.dot_general(a, b, (((1,), (1,)), ((), ())), preferred_element_type=F32)


def _keep_lanes(q, lo, hi):
    lane = lax.broadcasted_iota(jnp.int32, q.shape, 1)
    return jnp.where((lane >= lo) & (lane < hi), q.astype(F32), 0.0).astype(BF16)


def _group_rms(y, gmat):
    wide = gmat.shape[0]
    parts = []
    for c in range(y.shape[1] // wide):
        yc = y[:, c * wide:(c + 1) * wide]
        ms = _dot((yc * yc).astype(BF16), gmat)
        parts.append(yc * lax.rsqrt(ms + EPS))
    return parts[0] if len(parts) == 1 else jnp.concatenate(parts, axis=1)


def _group_mean_matrix():
    g = np.kron(np.eye(V7X_MXU_DIM // HEAD_DIM), np.ones((HEAD_DIM, HEAD_DIM))) / HEAD_DIM
    return jnp.asarray(g, BF16)


def _ffn_kernel(x_ref, g_ref, wg_ref, wu_ref, wd_ref, o_ref, h_scr, acc_scr):
    j = pl.program_id(1)

    @pl.when(j == 0)
    def _():
        h_scr[...] = _rms_rows(x_ref[...], g_ref[...]).astype(BF16)
        acc_scr[...] = jnp.zeros_like(acc_scr)

    h = h_scr[...]
    a = _dot(h, wg_ref[...])
    u = _dot(h, wu_ref[...])
    t = a * jax.nn.sigmoid(a) * u
    acc_scr[...] += _dot(t.astype(BF16), wd_ref[...])

    @pl.when(j == pl.num_programs(1) - 1)
    def _():
        o_ref[...] = x_ref[...] + 0.5 * acc_scr[...]


def _ffn(x, g, wg, wu, wd, tm, tf=512):
    n, d = x.shape
    dff = wg.shape[1]
    return pl.pallas_call(
        _ffn_kernel,
        grid=(n // tm, dff // tf),
        in_specs=[
            pl.BlockSpec((tm, d), lambda i, j: (i, 0)),
            pl.BlockSpec((1, d), lambda i, j: (0, 0)),
            pl.BlockSpec((d, tf), lambda i, j: (0, j)),
            pl.BlockSpec((d, tf), lambda i, j: (0, j)),
            pl.BlockSpec((tf, d), lambda i, j: (j, 0)),
        ],
        out_specs=pl.BlockSpec((tm, d), lambda i, j: (i, 0)),
        out_shape=jax.ShapeDtypeStruct((n, d), F32),
        scratch_shapes=[pltpu.VMEM((tm, d), BF16), pltpu.VMEM((tm, d), F32)],
        compiler_params=_cparams(("parallel", "arbitrary")),
        name="ffn",
    )(x, g.reshape(1, d), wg, wu, wd)


def _proj_kernel(x_ref, g_ref, w_ref, gain_ref, gmat_ref, *out_refs, segs):
    h = _rms_rows(x_ref[...], g_ref[...]).astype(BF16)
    off = 0
    out_refs = list(out_refs)
    for width, normed, gained, outs in segs:
        y = _dot(h, w_ref[:, off:off + width])
        if normed:
            y = _group_rms(y, gmat_ref[...])
        if gained:
            y = y * gain_ref[:, off:off + width]
        for _, head_width in outs:
            o_ref = out_refs.pop(0)
            if head_width:
                for hd in range(width // head_width):
                    o_ref[:, hd, :] = y[:, hd * head_width:(hd + 1) * head_width].astype(o_ref.dtype)
            else:
                o_ref[...] = y.astype(o_ref.dtype)
        off += width


def _proj(x, g, w, gain, segs, tm):
    n, d = x.shape
    wtot = w.shape[1]
    specs, shapes = [], []
    for width, _, _, outs in segs:
        for dt, head_width in outs:
            if head_width:
                nh = width // head_width
                specs.append(pl.BlockSpec((tm, nh, head_width), lambda i: (i, 0, 0)))
                shapes.append(jax.ShapeDtypeStruct((n, nh, head_width), dt))
            else:
                specs.append(pl.BlockSpec((tm, width), lambda i: (i, 0)))
                shapes.append(jax.ShapeDtypeStruct((n, width), dt))
    return pl.pallas_call(
        functools.partial(_proj_kernel, segs=tuple(segs)),
        grid=(n // tm,),
        in_specs=[
            pl.BlockSpec((tm, d), lambda i: (i, 0)),
            pl.BlockSpec((1, d), lambda i: (0, 0)),
            pl.BlockSpec((d, wtot), lambda i: (0, 0), pipeline_mode=pl.Buffered(1)),
            pl.BlockSpec((1, wtot), lambda i: (0, 0)),
            pl.BlockSpec((V7X_MXU_DIM, V7X_MXU_DIM), lambda i: (0, 0)),
        ],
        out_specs=specs,
        out_shape=shapes,
        compiler_params=_cparams(("parallel",)),
        name="proj",
    )(x, g.reshape(1, d), w, gain, _group_mean_matrix())


def _softmax_step_km(s, vt, m_ref, acc_ref):
    m_old = m_ref[...]
    m_new = jnp.maximum(m_old, jnp.max(s, axis=0, keepdims=True))
    alpha = jnp.exp2(m_old - m_new)
    p = jnp.exp2(s - m_new)
    acc_ref[...] = alpha * acc_ref[...] + _dot(vt, p.astype(BF16))
    m_ref[...] = m_new


def _cast_rows(src_ref, dst_ref, rows, dst_off=0, step=512):
    def body(i, c):
        r = pl.multiple_of(i * step, step)
        dst_ref[pl.ds(dst_off + r, step), :] = src_ref[0, pl.ds(r, step), :].astype(BF16)
        return c
    lax.fori_loop(0, rows // step, body, 0)


def _attn_a_kernel(lam_ref, brange_ref, q_ref, k_ref, v_ref, bias_ref, o_ref,
                   vtb, s_scr, knorm, m1, a1, m2, a2):
    h = pl.program_id(1)
    qi = pl.program_id(2)
    kbf = k_ref.at[0]
    seq = kbf.shape[0]
    tq = q_ref.shape[1]
    tk = vtb.shape[2]
    dv = v_ref.shape[2]

    @pl.when(qi == 0)
    def _():
        lane = lax.broadcasted_iota(jnp.int32, (tk, LANES), 1)

        def prep(j, c):
            r = pl.multiple_of(j * tk, tk)
            kb = kbf[pl.ds(r, tk), :]
            vtb[j, 0:dv, :] = v_ref[0, pl.ds(r, tk), :].astype(F32).T.astype(BF16)
            vtb[j, dv:, :] = jnp.ones((vtb.shape[1] - dv, tk), BF16)
            ksq = kb.astype(F32) * kb.astype(F32)
            n1 = jnp.max(jnp.sum(jnp.where(lane < HEAD_DIM, ksq, 0.0), axis=1, keepdims=True))
            n2 = jnp.max(jnp.sum(jnp.where(lane >= HEAD_DIM, ksq, 0.0), axis=1, keepdims=True))
            return jnp.maximum(c[0], n1), jnp.maximum(c[1], n2)
        n1, n2 = lax.fori_loop(0, seq // tk, prep, (jnp.float32(0.0), jnp.float32(0.0)))
        knorm[0] = n1
        knorm[1] = n2

    qt = q_ref[0].astype(F32).T
    sub = lax.broadcasted_iota(jnp.int32, qt.shape, 0)
    q1t = jnp.where(sub < HEAD_DIM, qt, 0.0).astype(BF16)
    q2t = jnp.where(sub >= HEAD_DIM, qt, 0.0).astype(BF16)
    states = ((m1, a1), (m2, a2))
    for m_ref, a_ref in states:
        a_ref[...] = jnp.zeros_like(a_ref)

    far_bias, bias_max, bias_min = brange_ref[3 * h], brange_ref[3 * h + 1], brange_ref[3 * h + 2]
    qsq = qt * qt
    bound1 = jnp.sqrt(jnp.sum(jnp.where(sub < HEAD_DIM, qsq, 0.0), axis=0, keepdims=True) * knorm[0])
    bound2 = jnp.sqrt(jnp.sum(jnp.where(sub >= HEAD_DIM, qsq, 0.0), axis=0, keepdims=True) * knorm[1])
    bounds = (bound1 * BOUND_SLACK + bias_max, bound2 * BOUND_SLACK + bias_max)
    spread = 2.0 * BOUND_SLACK * jnp.maximum(jnp.max(bound1), jnp.max(bound2)) + (bias_max - bias_min)
    bounded = spread <= BOUNDED_EXP2_SPAN

    @pl.when(bounded)
    def _():
        qmaps = ((q1t, bounds[0], a1), (q2t, bounds[1], a2))

        def accumulate(blocks):
            kbs = [kbf[pl.ds(pl.multiple_of(j * tk, tk), tk), :] for j, _ in blocks]
            ps = []
            for (j, tile), kb in zip(blocks, kbs):
                for qmt, shift, _ in qmaps:
                    add = (far_bias - shift) if tile is None else (tile - shift)
                    ps.append(jnp.exp2(_dot(kb, qmt) + add).astype(BF16))
            for mi, (_, _, a_ref) in enumerate(qmaps):
                total = None
                for bi, (j, _) in enumerate(blocks):
                    term = _dot(vtb[j], ps[bi * len(qmaps) + mi])
                    total = term if total is None else total + term
                a_ref[...] += total

        @pl.when(qi == 0)
        def _():
            accumulate([(qi, bias_ref[0, 0])])

        @pl.when(qi >= 1)
        def _():
            accumulate([(qi, bias_ref[0, 0]), (qi - 1, bias_ref[0, 1])])

        n_far = jnp.maximum(qi - 1, 0)
        one = n_far & 1
        two = n_far & 2

        @pl.when(one == 1)
        def _():
            accumulate([(0, None)])

        @pl.when(two == 2)
        def _():
            accumulate([(one, None), (one + 1, None)])

        def quad(i, c):
            j = one + two + 4 * i
            accumulate([(j, None), (j + 1, None), (j + 2, None), (j + 3, None)])
            return c
        lax.fori_loop(0, lax.shift_right_logical(n_far, 2), quad, 0)

    @pl.when(jnp.logical_not(bounded))
    def _():
        for m_ref, _ in states:
            m_ref[...] = jnp.full_like(m_ref, NEG)
        n_blocks = qi + 1

        def scores(t, buf):
            kb = kbf[pl.ds(pl.multiple_of((qi - t) * tk, tk), tk), :]
            s_scr[buf, 0] = _dot(kb, q1t)
            s_scr[buf, 1] = _dot(kb, q2t)

        def update(t, buf):
            bias = bias_ref[0, jnp.minimum(t, bias_ref.shape[1] - 1)]
            vt = vtb[qi - t]
            for mi, (m_ref, a_ref) in enumerate(states):
                _softmax_step_km(s_scr[buf, mi] + bias, vt, m_ref, a_ref)

        odd = n_blocks & 1

        @pl.when(odd == 1)
        def _():
            scores(0, 0)
            update(0, 0)

        n_pairs = lax.shift_right_logical(n_blocks, 1)

        @pl.when(n_pairs > 0)
        def _():
            scores(odd, 0)

        def pair(p, c):
            t0 = odd + 2 * p
            scores(t0 + 1, 1)
            update(t0, 0)
            scores(jnp.minimum(t0 + 2, qi), 0)
            update(t0 + 1, 1)
            return c
        lax.fori_loop(0, n_pairs, pair, 0)

    out_t = (a1[0:dv, :] / a1[dv:dv + 1, :]
             - lam_ref[0] * (a2[0:dv, :] / a2[dv:dv + 1, :]))
    o_ref[0] = out_t.T


def _attn_a(lam, brange, qa, ka, va, bias):
    b, s, w = qa.shape
    nh = w // LANES
    tq = tk = MIXER_A_TILE
    vrows = LANES + BF16_SUBLANES
    return pl.pallas_call(
        _attn_a_kernel,
        grid=(b, nh, s // tq),
        in_specs=[
            pl.BlockSpec(memory_space=pltpu.SMEM),
            pl.BlockSpec(memory_space=pltpu.SMEM),
            pl.BlockSpec((1, tq, LANES), lambda bi, h, qi: (bi, qi, h)),
            pl.BlockSpec((1, s, LANES), lambda bi, h, qi: (bi, 0, h)),
            pl.BlockSpec((1, s, LANES), lambda bi, h, qi: (bi, 0, h)),
            pl.BlockSpec((1,) + bias.shape[1:], lambda bi, h, qi: (h, 0, 0, 0)),
        ],
        out_specs=pl.BlockSpec((1, tq, LANES), lambda bi, h, qi: (bi, qi, h)),
        out_shape=jax.ShapeDtypeStruct((b, s, w), F32),
        scratch_shapes=[
            pltpu.VMEM((s // tk, vrows, tk), BF16),
            pltpu.VMEM((2, 2, tk, tq), F32), pltpu.SMEM((2,), F32),
            pltpu.VMEM((1, tq), F32), pltpu.VMEM((vrows, tq), F32),
            pltpu.VMEM((1, tq), F32), pltpu.VMEM((vrows, tq), F32),
        ],
        compiler_params=_cparams(("parallel", "parallel", "arbitrary")),
        name="mixer_a",
    )(lam, brange, qa, ka, va, bias)


def _stick_block(z, valid, umat, carry):
    sp = jnp.maximum(z, 0.0) + jnp.log(1.0 + jnp.exp(-jnp.abs(z)))
    log1m = -sp
    if valid is not None:
        log1m = jnp.where(valid, log1m, 0.0)
    hi = log1m.astype(BF16)
    lo = (log1m - hi.astype(F32)).astype(BF16)
    after = _dot(hi, umat) + _dot(lo, umat)
    w = jnp.exp(z - sp + after + carry)
    if valid is not None:
        w = jnp.where(valid, w, 0.0)
    return w, jnp.sum(log1m, axis=1, keepdims=True)


def _stick_block_km(z, valid, umat, carry):
    sp = jnp.maximum(z, 0.0) + jnp.log(1.0 + jnp.exp(-jnp.abs(z)))
    log1m = -sp
    if valid is not None:
        log1m = jnp.where(valid, log1m, 0.0)
    hi = log1m.astype(BF16)
    lo = (log1m - hi.astype(F32)).astype(BF16)
    after = _dot(umat, hi) + _dot(umat, lo)
    w = jnp.exp(z - sp + after + carry)
    if valid is not None:
        w = jnp.where(valid, w, 0.0)
    return w, jnp.sum(log1m, axis=0, keepdims=True)


def _attn_b_kernel(q_ref, k_ref, v_ref, u_ref, o_ref, kbf, vtb, c_scr, acc_scr, z_scr):
    step = pl.program_id(2)
    seq = kbf.shape[0]
    tk = vtb.shape[2]
    n_halves = q_ref.shape[1] // tk
    n_heads = LANES // HEAD_DIM

    @pl.when(step == 0)
    def _():
        def prep(j, c):
            r = pl.multiple_of(j * tk, tk)
            kbf[pl.ds(r, tk), :] = k_ref[0, pl.ds(r, tk), :].astype(BF16)
            vtb[j] = v_ref[0, pl.ds(r, tk), :].T.astype(BF16)
            return c
        lax.fori_loop(0, seq // tk, prep, 0)

    sub = lax.broadcasted_iota(jnp.int32, (LANES, tk), 0)
    qts = []
    for half in range(n_halves):
        qt = q_ref[0, half * tk:(half + 1) * tk, :].astype(F32).T
        qts.append([jnp.where((sub >= hh * HEAD_DIM) & (sub < (hh + 1) * HEAD_DIM), qt, 0.0).astype(BF16)
                    for hh in range(n_heads)])
    krow = lax.broadcasted_iota(jnp.int32, (tk, tk), 0)
    qcol = lax.broadcasted_iota(jnp.int32, (tk, tk), 1)
    umat = u_ref[...]

    def blocks(half, js, valids, first):
        kbs = [kbf[pl.ds(pl.multiple_of(j * tk, tk), tk), :] for j in js]
        cmax = None
        for hh in range(n_heads):
            carry = 0.0 if first else c_scr[half, hh]
            pv = None
            for j, kb, valid in zip(js, kbs, valids):
                w, cs = _stick_block_km(_dot(kb, qts[half][hh]), valid, umat, carry)
                term = _dot(vtb[j], w.astype(BF16))
                pv = term if pv is None else pv + term
                carry = carry + cs
            acc_scr[half, hh] = pv if first else acc_scr[half, hh] + pv
            c_scr[half, hh] = carry
            cm = jnp.max(carry)
            cmax = cm if cmax is None else jnp.maximum(cmax, cm)
        return cmax

    own_valid = krow < qcol
    everywhere = krow >= 0

    chains = []
    for half in range(n_halves):
        qb = step * n_halves + half
        prev_valid = None if half >= 1 else everywhere & (qb >= 1)
        for hh in range(n_heads):
            for j, valid in ((qb, own_valid), (jnp.maximum(qb - 1, 0), prev_valid)):
                chains.append((half, hh, j, valid))
    for c, (half, hh, j, valid) in enumerate(chains):
        z_scr[c] = _dot(kbf[pl.ds(pl.multiple_of(j * tk, tk), tk), :], qts[half][hh])
    col_sums = []
    for c, (half, hh, j, valid) in enumerate(chains):
        z = z_scr[c]
        sp = jnp.maximum(z, 0.0) + jnp.log(1.0 + jnp.exp(-jnp.abs(z)))
        log1m = -sp if valid is None else jnp.where(valid, -sp, 0.0)
        hi = log1m.astype(BF16)
        lo = (log1m - hi.astype(F32)).astype(BF16)
        z_scr[c] = z - sp + _dot(umat, hi) + _dot(umat, lo)
        col_sums.append(jnp.sum(log1m, axis=0, keepdims=True))
    cmaxes = []
    for half in range(n_halves):
        cmax = None
        for hh in range(n_heads):
            carry, pv = 0.0, None
            for c, (ch, chh, j, valid) in enumerate(chains):
                if (ch, chh) != (half, hh):
                    continue
                w = jnp.exp(z_scr[c] + carry)
                if valid is not None:
                    w = jnp.where(valid, w, 0.0)
                term = _dot(vtb[j], w.astype(BF16))
                pv = term if pv is None else pv + term
                carry = carry + col_sums[c]
            acc_scr[half, hh] = pv
            c_scr[half, hh] = carry
            cm = jnp.max(carry)
            cmax = cm if cmax is None else jnp.maximum(cmax, cm)
        cmaxes.append(cmax)

    for half in range(n_halves):
        qb = step * n_halves + half

        def cond(st):
            j, cmax = st
            return (j >= 0) & (cmax > STICK_SKIP)

        def body(st, half=half):
            j, _ = st
            return j - 2, blocks(half, [j, jnp.maximum(j - 1, 0)], [None, everywhere & (j >= 1)], False)

        lax.while_loop(cond, body, (qb - 2, cmaxes[half]))
        out_t = jnp.where(sub < HEAD_DIM, acc_scr[half, 0], acc_scr[half, 1])
        o_ref[0, half * tk:(half + 1) * tk, :] = out_t.T.astype(o_ref.dtype)


def _strict_lower(n):
    return jnp.asarray(np.tril(np.ones((n, n)), -1), BF16)


def _strict_upper(n):
    return jnp.asarray(np.triu(np.ones((n, n)), 1), BF16)


def _attn_b(qb, kb, vb):
    b, s, w = qb.shape
    n_halves = 2
    tq = n_halves * ATTN_TK
    return pl.pallas_call(
        _attn_b_kernel,
        grid=(b, w // LANES, s // tq),
        in_specs=[
            pl.BlockSpec((1, tq, LANES), lambda bi, h, qi: (bi, qi, h)),
            pl.BlockSpec((1, s, LANES), lambda bi, h, qi: (bi, 0, h)),
            pl.BlockSpec((1, s, LANES), lambda bi, h, qi: (bi, 0, h)),
            pl.BlockSpec((ATTN_TK, ATTN_TK), lambda bi, h, qi: (0, 0)),
        ],
        out_specs=pl.BlockSpec((1, tq, LANES), lambda bi, h, qi: (bi, qi, h)),
        out_shape=jax.ShapeDtypeStruct((b, s, w), BF16),
        scratch_shapes=[
            pltpu.VMEM((s, LANES), BF16), pltpu.VMEM((s // ATTN_TK, LANES, ATTN_TK), BF16),
            pltpu.VMEM((n_halves, LANES // HEAD_DIM, 1, ATTN_TK), F32),
            pltpu.VMEM((n_halves, LANES // HEAD_DIM, LANES, ATTN_TK), F32),
            pltpu.VMEM((2 * n_halves * (LANES // HEAD_DIM), ATTN_TK, ATTN_TK), F32),
        ],
        compiler_params=_cparams(("parallel", "parallel", "arbitrary")),
        name="mixer_b",
    )(qb, kb, vb, _strict_upper(ATTN_TK))


def _attn_c_kernel(q_ref, k_ref, v_ref, bias_ref, o_ref, kbf, vbf, s_scr):
    qi = pl.program_id(2)
    seq = k_ref.shape[1]
    tq = q_ref.shape[1]
    pad = C_PREV * CHUNK
    n_heads = LANES // HEAD_DIM

    @pl.when(qi == 0)
    def _():
        kbf[0:pad, :] = jnp.zeros((pad, LANES), BF16)
        vbf[0:pad, 0:LANES] = jnp.zeros((pad, LANES), BF16)
        vbf[:, LANES:] = jnp.ones((vbf.shape[0], vbf.shape[1] - LANES), BF16)
        _cast_rows(k_ref, kbf, seq, dst_off=pad)

        def vcast(i, c):
            r = pl.multiple_of(i * tq, tq)
            vbf[pl.ds(pad + r, tq), 0:LANES] = v_ref[0, pl.ds(r, tq), :].astype(BF16)
            return c
        lax.fori_loop(0, seq // tq, vcast, 0)

    q = q_ref[0]
    lane = lax.broadcasted_iota(jnp.int32, q.shape, 1)
    wstart = pl.multiple_of(qi * tq, tq)
    kw = kbf[pl.ds(wstart, BAND_W), :]
    vw = vbf[pl.ds(wstart, BAND_W), :]
    col = lax.broadcasted_iota(jnp.int32, (tq, BAND_W), 1)
    in_seq = col >= pad - qi * tq
    for hh in range(n_heads):
        qm = _keep_lanes(q, hh * HEAD_DIM, (hh + 1) * HEAD_DIM)
        s_scr[hh] = jnp.where(in_seq, _dot_nt(qm, kw) + bias_ref[hh], NEG)
    outs = []
    for hh in range(n_heads):
        s = s_scr[hh]
        p = jnp.exp(s - jnp.max(s, axis=1, keepdims=True))
        acc = _dot(p.astype(BF16), vw)
        outs.append(acc[:, 0:LANES] / acc[:, LANES:])
    o_ref[0] = jnp.where(lane < HEAD_DIM, outs[0], outs[1]).astype(o_ref.dtype)


def _attn_c(qc, kc, vc, bias):
    b, s, w = qc.shape
    tq = ATTN_TQ
    pad = C_PREV * CHUNK
    hp = LANES // HEAD_DIM
    return pl.pallas_call(
        _attn_c_kernel,
        grid=(b, w // LANES, s // tq),
        in_specs=[
            pl.BlockSpec((1, tq, LANES), lambda bi, h, qi: (bi, qi, h)),
            pl.BlockSpec((1, s, LANES), lambda bi, h, qi: (bi, 0, h)),
            pl.BlockSpec((1, s, LANES), lambda bi, h, qi: (bi, 0, h)),
            pl.BlockSpec((hp, tq, BAND_W), lambda bi, h, qi: (h, 0, 0)),
        ],
        out_specs=pl.BlockSpec((1, tq, LANES), lambda bi, h, qi: (bi, qi, h)),
        out_shape=jax.ShapeDtypeStruct((b, s, w), BF16),
        scratch_shapes=[pltpu.VMEM((s + pad, LANES), BF16), pltpu.VMEM((s + pad, 2 * LANES), BF16),
                        pltpu.VMEM((hp, tq, BAND_W), F32)],
        compiler_params=_cparams(("parallel", "parallel", "arbitrary")),
        name="mixer_c",
    )(qc, kc, vc, bias)


def _merge_kernel(x_ref, oa_ref, ob_ref, oc_ref, g_ref, wgate_ref, sub_ref,
                  wa_ref, wb_ref, wc_ref, wout_ref, o_ref):
    x = x_ref[...]
    d = x.shape[1]
    h = _rms_rows(x, g_ref[...]).astype(BF16)
    oa = oa_ref[...]
    parts = []
    for c in range(oa.shape[1] // LANES):
        oc_ = oa[:, c * LANES:(c + 1) * LANES]
        parts.append(oc_ * lax.rsqrt(jnp.mean(oc_ * oc_, axis=-1, keepdims=True) + EPS))
    oan = (jnp.concatenate(parts, axis=1) * sub_ref[...]).astype(BF16)
    merged = jax.nn.sigmoid(_dot(h, wgate_ref[:, 0:d])) * _dot(oan, wa_ref[...])
    merged += jax.nn.sigmoid(_dot(h, wgate_ref[:, d:2 * d])) * _dot(ob_ref[...], wb_ref[...])
    merged += jax.nn.sigmoid(_dot(h, wgate_ref[:, 2 * d:3 * d])) * _dot(oc_ref[...], wc_ref[...])
    o_ref[...] = x + _dot(merged.astype(BF16), wout_ref[...])


def _merge(x, oa, ob, oc, g, wgate, sub, wa, wb, wc, wout, tm):
    n, d = x.shape
    full = lambda a: pl.BlockSpec(a.shape, lambda i: (0,) * a.ndim, pipeline_mode=pl.Buffered(1))
    rows = lambda a: pl.BlockSpec((tm, a.shape[1]), lambda i: (i, 0))
    g = g.reshape(1, d)
    return pl.pallas_call(
        _merge_kernel,
        grid=(n // tm,),
        in_specs=[rows(x), rows(oa), rows(ob), rows(oc), full(g), full(wgate), full(sub),
                  full(wa), full(wb), full(wc), full(wout)],
        out_specs=rows(x),
        out_shape=jax.ShapeDtypeStruct((n, d), F32),
        compiler_params=_cparams(("parallel",)),
        name="merge",
    )(x, oa, ob, oc, g, wgate, sub, wa, wb, wc, wout)


def _cross_kernel(x_ref, g_ref, wq_ref, gain_ref, gmat_ref, mk_ref, mv_ref, wo_ref, o_ref):
    x = x_ref[0]
    h = _rms_rows(x, g_ref[...]).astype(BF16)
    q = (_group_rms(_dot(h, wq_ref[...]), gmat_ref[...]) * gain_ref[...]).astype(BF16)
    mk = mk_ref[0]
    mv = mv_ref[0]
    lane = lax.broadcasted_iota(jnp.int32, q.shape, 1)
    o = jnp.zeros(q.shape, F32)
    for hh in range(q.shape[1] // HEAD_DIM):
        in_head = (lane >= hh * HEAD_DIM) & (lane < (hh + 1) * HEAD_DIM)
        s = _dot_nt(_keep_lanes(q, hh * HEAD_DIM, (hh + 1) * HEAD_DIM), mk)
        p = jnp.exp(s - jnp.max(s, axis=1, keepdims=True))
        l = jnp.sum(p, axis=1, keepdims=True)
        o = jnp.where(in_head, _dot(p.astype(BF16), mv) / l, o)
    o_ref[0] = x + _dot(o.astype(BF16), wo_ref[...])


def _cross(x, g, wq, gain, mk, mv, wo, tm):
    b, s, d = x.shape
    full = lambda a: pl.BlockSpec(a.shape, lambda bi, i: (0,) * a.ndim)
    g = g.reshape(1, d)
    gmat = _group_mean_matrix()
    return pl.pallas_call(
        _cross_kernel,
        grid=(b, s // tm),
        in_specs=[
            pl.BlockSpec((1, tm, d), lambda bi, i: (bi, i, 0)),
            full(g), full(wq), full(gain), full(gmat),
            pl.BlockSpec((1,) + mk.shape[1:], lambda bi, i: (bi, 0, 0)),
            pl.BlockSpec((1,) + mv.shape[1:], lambda bi, i: (bi, 0, 0)),
            full(wo),
        ],
        out_specs=pl.BlockSpec((1, tm, d), lambda bi, i: (bi, i, 0)),
        out_shape=jax.ShapeDtypeStruct((b, s, d), F32),
        compiler_params=_cparams(("parallel", "parallel")),
        name="cross",
    )(x, g, wq, gain, gmat, mk, mv, wo)


def _sample_attn_kernel(lam_ref,
                        qa_ref, kan_ref, van_ref, kac_ref, vac_ref, bac_ref, ban_ref,
                        qb_ref, kbn_ref, vbn_ref, kbc_ref, vbc_ref, ubig_ref, usmall_ref,
                        qc_ref, kcn_ref, vcn_ref, kcc_ref, vcc_ref, bcc_ref, bcn_ref,
                        oa_ref, ob_ref, oc_ref):
    lam = lam_ref[0]
    ns = qa_ref.shape[1]
    bf = lambda r: r[0].astype(BF16)

    def heads(q, width):
        for hh in range(LANES // width):
            yield hh, None, _keep_lanes(q, hh * width, (hh + 1) * width)

    def softmax2(s_c, s_n, exp=jnp.exp):
        m = jnp.maximum(jnp.max(s_c, axis=1, keepdims=True), jnp.max(s_n, axis=1, keepdims=True))
        p_c = exp(s_c - m)
        p_n = exp(s_n - m)
        inv = 1.0 / (jnp.sum(p_c, axis=1, keepdims=True) + jnp.sum(p_n, axis=1, keepdims=True))
        return p_c * inv, p_n * inv

    for h in range(qa_ref.shape[2] // LANES):
        sl = slice(h * LANES, (h + 1) * LANES)
        q = qa_ref[0, :, sl]
        k_c = kac_ref[0, :, h, :].astype(BF16)
        k_n = kan_ref[0, :, sl].astype(BF16)
        maps = []
        for _, _, qm in heads(q, HEAD_DIM):
            maps.append(softmax2(_dot_nt(qm, k_c) + bac_ref[h], _dot_nt(qm, k_n) + ban_ref[h], exp=jnp.exp2))
        a_c = (maps[0][0] - lam * maps[1][0]).astype(BF16)
        a_n = (maps[0][1] - lam * maps[1][1]).astype(BF16)
        oa_ref[0, :, sl] = (_dot(a_c, vac_ref[0, :, h, :].astype(BF16))
                            + _dot(a_n, van_ref[0, :, sl].astype(BF16)))

    past = kbc_ref.shape[1]
    tk = ubig_ref.shape[0]
    row = lax.broadcasted_iota(jnp.int32, (ns, ns), 0)
    col = lax.broadcasted_iota(jnp.int32, (ns, ns), 1)
    for pr in range(qb_ref.shape[2] // LANES):
        sl = slice(pr * LANES, (pr + 1) * LANES)
        q = qb_ref[0, :, sl]
        k_n = kbn_ref[0, :, sl].astype(BF16)
        v_n = vbn_ref[0, :, sl].astype(BF16)
        outs = []
        for _, _, qm in heads(q, HEAD_DIM):
            w, carry = _stick_block(_dot_nt(qm, k_n), col < row, usmall_ref[...], 0.0)
            acc = _dot(w.astype(BF16), v_n)
            for j in range(past // tk - 1, -1, -1):
                k_c = kbc_ref[0, j * tk:(j + 1) * tk, sl].astype(BF16)
                v_c = vbc_ref[0, j * tk:(j + 1) * tk, sl].astype(BF16)
                w, rs = _stick_block(_dot_nt(qm, k_c), None, ubig_ref[...], carry)
                acc += _dot(w.astype(BF16), v_c)
                carry = carry + rs
            outs.append(acc)
        lane = lax.broadcasted_iota(jnp.int32, q.shape, 1)
        ob_ref[0, :, sl] = jnp.where(lane < HEAD_DIM, outs[0], outs[1]).astype(ob_ref.dtype)

    for pr in range(qc_ref.shape[2] // LANES):
        sl = slice(pr * LANES, (pr + 1) * LANES)
        q = qc_ref[0, :, sl]
        k_c = kcc_ref[0, :, sl].astype(BF16)
        k_n = kcn_ref[0, :, sl].astype(BF16)
        v_c = vcc_ref[0, :, sl].astype(BF16)
        v_n = vcn_ref[0, :, sl].astype(BF16)
        outs = []
        for hh, _, qm in heads(q, HEAD_DIM):
            hd = pr * (LANES // HEAD_DIM) + hh
            p_c, p_n = softmax2(_dot_nt(qm, k_c) + bcc_ref[hd], _dot_nt(qm, k_n) + bcn_ref[hd])
            outs.append(_dot(p_c.astype(BF16), v_c) + _dot(p_n.astype(BF16), v_n))
        lane = lax.broadcasted_iota(jnp.int32, q.shape, 1)
        oc_ref[0, :, sl] = jnp.where(lane < HEAD_DIM, outs[0], outs[1]).astype(oc_ref.dtype)


def _sample_attn(layer, lam, qa, ka, va, cak, cav, bac, ban, qb, kb, vb, cbk, cbv,
                 qc, kc, vc, cck, ccv, bcc, bcn):
    b, ns, _ = qa.shape
    per_b = lambda a: pl.BlockSpec((1,) + a.shape[1:], lambda bi: (bi,) + (0,) * (a.ndim - 1))
    full = lambda a: pl.BlockSpec(a.shape, lambda bi: (0,) * a.ndim)
    layer_b = lambda a: pl.BlockSpec((None, 1) + a.shape[2:], lambda bi: (layer, bi) + (0,) * (a.ndim - 2))
    ubig = _strict_lower(ATTN_TK)
    usmall = _strict_lower(ns)
    args = [qa, ka, va, cak, cav, bac, ban, qb, kb, vb, cbk, cbv, ubig, usmall,
            qc, kc, vc, cck, ccv, bcc, bcn]
    specs = [per_b(qa), per_b(ka), per_b(va), layer_b(cak), layer_b(cav), full(bac), full(ban),
             per_b(qb), per_b(kb), per_b(vb), per_b(cbk), per_b(cbv), full(ubig), full(usmall),
             per_b(qc), per_b(kc), per_b(vc), per_b(cck), per_b(ccv), full(bcc), full(bcn)]
    return pl.pallas_call(
        _sample_attn_kernel,
        grid=(b,),
        in_specs=[pl.BlockSpec(memory_space=pltpu.SMEM)] + specs,
        out_specs=[per_b(qa), per_b(qb), per_b(qc)],
        out_shape=[jax.ShapeDtypeStruct(qa.shape, F32), jax.ShapeDtypeStruct(qb.shape, BF16),
                   jax.ShapeDtypeStruct(qc.shape, BF16)],
        compiler_params=_cparams(("parallel",)),
        name="sample_mixers",
    )(lam, *args)


def _t5_bucket_np(rel):
    half = T5_BUCKETS // 2
    max_exact = half // 2
    n = np.abs(rel)
    nf = np.maximum(n, 1).astype(np.float64)
    large = max_exact + (np.log(nf / max_exact) / math.log(T5_MAX_DIST / max_exact)
                         * (half - max_exact)).astype(np.int64)
    large = np.minimum(large, half - 1)
    return np.where(rel > 0, half, 0) + np.where(n < max_exact, n, large)


def _toeplitz(lookup, n_rows, n_cols):
    period = n_rows + n_cols
    slot = np.arange(period)
    diff = np.minimum((slot + n_rows - 1) % period - (n_rows - 1), n_cols - 1)
    vec = lookup(diff).astype(F32)
    flat = jnp.tile(vec, (1, n_rows))[:, :n_rows * (period - 1)]
    return flat.reshape(vec.shape[0], n_rows, period - 1)[:, :, :n_cols]


def _t5_bias_table(t5_bias, qpos, kpos, key_major=False):
    t5_rows = lambda rel: t5_bias[_t5_bucket_np(rel)].T
    mask = (kpos[None, :] // CHUNK) <= (qpos[:, None] // CHUNK)
    if key_major:
        table = _toeplitz(lambda dd: t5_rows(kpos[0] - qpos[0] - dd), len(kpos), len(qpos))
        mask = mask.T
    else:
        table = _toeplitz(lambda dd: t5_rows(kpos[0] - qpos[0] + dd), len(qpos), len(kpos))
    return jnp.where(jnp.asarray(mask)[None], table, NEG)


def _band_bias_table(rel_table, qpos, kpos):
    lookup = lambda dd: rel_table[:, np.clip(kpos[0] - qpos[0] + dd, -REL_CLIP, REL_CLIP) + REL_CLIP]
    table = _toeplitz(lookup, len(qpos), len(kpos))
    qc = qpos[:, None] // CHUNK
    kc = kpos[None, :] // CHUNK
    mask = (kpos[None, :] >= 0) & (kc <= qc) & (kc >= qc - C_PREV)
    return jnp.where(jnp.asarray(mask)[None], table, NEG)


def _lambda_init(layer):
    return 0.8 - 0.6 * math.exp(-0.3 * layer)


def kernel(x_prompt, x_sample, mem_prompt, cache_a_k, cache_a_v, cache_b_k, cache_b_v, cache_c_k, cache_c_v, cache_mem_k, cache_mem_v, t5_bias, ffn1_norm, ffn1_wg, ffn1_wu, ffn1_wd, mix_norm, w_in, a_qnorm, a_knorm, a_lq1, a_lk1, a_lq2, a_lk2, a_subln, c_qnorm, c_knorm, c_rel_bias, w_br_a, w_br_b, w_br_c, w_out, x_norm, mem_norm, x_wq, x_wkv, x_qnorm, x_knorm, x_wo, ffn2_norm, ffn2_wg, ffn2_wu, ffn2_wd):
    bp, sp, d = x_prompt.shape
    bs, ns, _ = x_sample.shape
    depth = w_in.shape[0]
    past = cache_a_k.shape[2]
    w_buf = cache_c_k.shape[2]
    n_mem = mem_prompt.shape[1]
    wa = cache_a_k.shape[3] * cache_a_k.shape[4]
    wb = cache_b_k.shape[3] * cache_b_k.shape[4]
    wc = cache_c_k.shape[3] * cache_c_k.shape[4]
    wx = cache_mem_k.shape[3] * cache_mem_k.shape[4]
    h_a = cache_a_k.shape[3]
    n_qkv = 3 * (wa + wb + wc)
    w_keep = min(C_PREV * CHUNK, sp)
    scale = HEAD_DIM ** -0.5
    assert sp % ATTN_TQ == 0 and ATTN_TQ == ATTN_TK and ATTN_TQ == 4 * CHUNK and sp >= w_keep
    assert past % ATTN_TK == 0
    assert sp % MIXER_A_TILE == 0 and MIXER_A_TILE % CHUNK == 0 and T5_MAX_DIST <= MIXER_A_TILE

    tile = lambda g, reps: jnp.tile(g.astype(F32), reps)
    bf = lambda a: a.astype(BF16)

    loc_a = np.arange(MIXER_A_TILE)
    loc = np.arange(ATTN_TQ)
    a_far = t5_bias[T5_BUCKETS // 2 - 1].astype(F32)
    a_bias = LOG2E * jnp.stack([
        _t5_bias_table(t5_bias, loc_a + MIXER_A_TILE, loc_a + MIXER_A_TILE, key_major=True),
        _t5_bias_table(t5_bias, loc_a + MIXER_A_TILE, loc_a, key_major=True),
        jnp.broadcast_to(a_far[:, None, None], (h_a, MIXER_A_TILE, MIXER_A_TILE))], axis=1)
    t5_log2 = LOG2E * t5_bias.astype(F32)
    a_brange = jnp.stack([LOG2E * a_far, jnp.max(t5_log2, axis=0), jnp.min(t5_log2, axis=0)], axis=1).reshape(-1)
    qpos_s = past + np.arange(ns)
    a_bias_sc = LOG2E * _t5_bias_table(t5_bias, qpos_s, np.arange(past))
    a_bias_sn = LOG2E * _t5_bias_table(t5_bias, qpos_s, qpos_s)
    kpos_sb = past - w_buf + np.arange(w_buf + ns)

    xp = x_prompt.reshape(bp * sp, d)
    xs = x_sample.reshape(bs * ns, d)
    mem = mem_prompt.reshape(bp * n_mem, d)
    outs = {k: [] for k in ("ak_p", "av_p", "bk_p", "bv_p", "ck_p", "cv_p", "mk_p", "mv_p",
                            "ak_s", "av_s", "bk_s", "bv_s", "ck_s", "cv_s")}
    flat = lambda dt: ((dt, 0),)
    dv_a = wa // h_a
    a_kv_prompt = ((F32, dv_a), (BF16, 0))
    qkv_tail = [(wb, False, True, flat(BF16)), (wb, False, False, flat(F32)), (wb, False, False, flat(F32)),
                (wc, True, True, flat(BF16)), (wc, True, True, flat(F32)), (wc, False, False, flat(F32))]
    qkv_segs_p = [(wa, True, True, flat(BF16)), (wa, True, True, a_kv_prompt),
                  (wa, False, False, a_kv_prompt)] + qkv_tail
    qkv_segs_s = [(wa, True, True, flat(BF16)), (wa, True, True, flat(F32)), (wa, False, False, flat(F32))] + qkv_tail

    for i in range(depth):
        wg1, wu1, wd1 = bf(ffn1_wg[i]), bf(ffn1_wu[i]), bf(ffn1_wd[i])
        wg2, wu2, wd2 = bf(ffn2_wg[i]), bf(ffn2_wu[i]), bf(ffn2_wd[i])
        w_qkv, w_gate = bf(w_in[i][:, :n_qkv]), bf(w_in[i][:, n_qkv:])
        qkv_gain = jnp.concatenate([
            tile(a_qnorm[i], wa // HEAD_DIM) * (scale * LOG2E), tile(a_knorm[i], wa // HEAD_DIM), jnp.ones((wa,), F32),
            jnp.full((wb,), scale, F32), jnp.ones((2 * wb,), F32),
            tile(c_qnorm[i], wc // HEAD_DIM) * scale, tile(c_knorm[i], wc // HEAD_DIM), jnp.ones((wc,), F32),
        ]).reshape(1, n_qkv)
        sub = (tile(a_subln[i], h_a) * (1.0 - _lambda_init(i))).reshape(1, wa)
        wbr = bf(w_br_a[i]), bf(w_br_b[i]), bf(w_br_c[i])
        wout = bf(w_out[i])
        wq, wo = bf(x_wq[i]), bf(x_wo[i])
        xq_gain = (tile(x_qnorm[i], wx // HEAD_DIM) * scale).reshape(1, wx)
        dot64 = lambda a, b: jnp.exp(jnp.sum(a.astype(F32) * b.astype(F32)))
        lam = (dot64(a_lq1[i], a_lk1[i]) - dot64(a_lq2[i], a_lk2[i]) + _lambda_init(i)).reshape(1)
        c_bias_p = _band_bias_table(c_rel_bias[i], C_PREV * CHUNK + loc, np.arange(BAND_W))
        c_bias_sc = _band_bias_table(c_rel_bias[i], qpos_s, kpos_sb[:w_buf])
        c_bias_sn = _band_bias_table(c_rel_bias[i], qpos_s, kpos_sb[w_buf:])

        xp = _ffn(xp, ffn1_norm[i], wg1, wu1, wd1, tm=1024, tf=1024)
        qa, ka, ka_bf, va, va_bf, qb, kb, vb, qc, kc, vc = _proj(xp, mix_norm[i], w_qkv, qkv_gain, qkv_segs_p, tm=1024)
        r3 = lambda a: a.reshape(bp, sp, a.shape[-1])
        oa = _attn_a(lam, a_brange, r3(qa), r3(ka_bf), r3(va_bf), a_bias)
        ob = _attn_b(r3(qb), r3(kb), r3(vb))
        oc = _attn_c(r3(qc), r3(kc), r3(vc), c_bias_p)
        xp = _merge(xp, oa.reshape(-1, wa), ob.reshape(-1, wb), oc.reshape(-1, wc), mix_norm[i], w_gate, sub,
                    *wbr, wout, tm=1024)
        mk, mv = _proj(mem, mem_norm[i], bf(x_wkv[i]),
                       jnp.concatenate([tile(x_knorm[i], wx // HEAD_DIM), jnp.ones((wx,), F32)]).reshape(1, 2 * wx),
                       [(wx, True, True, flat(F32)), (wx, False, False, flat(F32))], tm=n_mem)
        mk3, mv3 = mk.reshape(bp, n_mem, wx), mv.reshape(bp, n_mem, wx)
        xp = _cross(xp.reshape(bp, sp, d), x_norm[i], wq, xq_gain, bf(mk3), bf(mv3), wo, tm=1024).reshape(-1, d)
        xp = _ffn(xp, ffn2_norm[i], wg2, wu2, wd2, tm=1024, tf=1024)
        outs["ak_p"].append(ka.reshape(bp, sp, h_a, -1))
        outs["av_p"].append(va.reshape(bp, sp, h_a, -1))
        outs["bk_p"].append(kb.reshape(bp, sp, -1, HEAD_DIM))
        outs["bv_p"].append(vb.reshape(bp, sp, -1, HEAD_DIM))
        outs["ck_p"].append(r3(kc)[:, sp - w_keep:].reshape(bp, w_keep, -1, HEAD_DIM))
        outs["cv_p"].append(r3(vc)[:, sp - w_keep:].reshape(bp, w_keep, -1, HEAD_DIM))
        outs["mk_p"].append(mk3.reshape(bp, n_mem, -1, HEAD_DIM))
        outs["mv_p"].append(mv3.reshape(bp, n_mem, -1, HEAD_DIM))

        xs = _ffn(xs, ffn1_norm[i], wg1, wu1, wd1, tm=bs * ns)
        qa, ka, va, qb, kb, vb, qc, kc, vc = _proj(xs, mix_norm[i], w_qkv, qkv_gain, qkv_segs_s, tm=bs * ns)
        s3 = lambda a: a.reshape(bs, ns, a.shape[-1])
        c3 = lambda a: a.reshape(bs, a.shape[1], -1)
        oa, ob, oc = _sample_attn(
            i, lam, s3(qa), s3(ka), s3(va), cache_a_k, cache_a_v, a_bias_sc, a_bias_sn,
            s3(qb), s3(kb), s3(vb), c3(cache_b_k[i]), c3(cache_b_v[i]),
            s3(qc), s3(kc), s3(vc), c3(cache_c_k[i]), c3(cache_c_v[i]), c_bias_sc, c_bias_sn)
        xs = _merge(xs, oa.reshape(-1, wa), ob.reshape(-1, wb), oc.reshape(-1, wc), mix_norm[i], w_gate, sub,
                    *wbr, wout, tm=bs * ns)
        xs = _cross(xs.reshape(bs, ns, d), x_norm[i], wq, xq_gain, bf(c3(cache_mem_k[i])), bf(c3(cache_mem_v[i])),
                    wo, tm=ns).reshape(-1, d)
        xs = _ffn(xs, ffn2_norm[i], wg2, wu2, wd2, tm=bs * ns)
        outs["ak_s"].append(ka.reshape(bs, ns, h_a, -1))
        outs["av_s"].append(va.reshape(bs, ns, h_a, -1))
        outs["bk_s"].append(kb.reshape(bs, ns, -1, HEAD_DIM))
        outs["bv_s"].append(vb.reshape(bs, ns, -1, HEAD_DIM))
        outs["ck_s"].append(jnp.concatenate([cache_c_k[i], kc.reshape(bs, ns, -1, HEAD_DIM)], axis=1)[:, ns:])
        outs["cv_s"].append(jnp.concatenate([cache_c_v[i], vc.reshape(bs, ns, -1, HEAD_DIM)], axis=1)[:, ns:])

    st = lambda k: jnp.stack(outs[k])
    return (xp.reshape(bp, sp, d), xs.reshape(bs, ns, d),
            st("ak_p"), st("av_p"), st("bk_p"), st("bv_p"), st("ck_p"), st("cv_p"), st("mk_p"), st("mv_p"),
            st("ak_s"), st("av_s"), st("bk_s"), st("bv_s"), st("ck_s"), st("cv_s"))
```

```python
import functools
import math

import numpy as np
import jax
import jax.numpy as jnp
from jax import lax
from jax.experimental import pallas as pl
from jax.experimental.pallas import tpu as pltpu

F32 = jnp.float32
BF16 = jnp.bfloat16

EPS = 1e-6
HEAD_DIM = 64
CHUNK = 64
C_PREV = 8
REL_CLIP = 128
T5_BUCKETS = 32
T5_MAX_DIST = 128
LANES = 128
BF16_SUBLANES = 16
V7X_MXU_DIM = 256
NEG = -1e30
LOG2E = math.log2(math.e)
BOUND_SLACK = 1.001
BOUNDED_EXP2_SPAN = 100.0
STICK_SKIP = -110.0
V7X_VMEM_LIMIT_BYTES = 56 * 1024 * 1024

MIXER_A_TILE = 512
ATTN_TQ = 256
ATTN_TK = 256
BAND_W = (C_PREV + 4) * CHUNK


def _cparams(sem):
    return pltpu.CompilerParams(dimension_semantics=sem, vmem_limit_bytes=V7X_VMEM_LIMIT_BYTES)


def _rms_rows(x, g):
    return x * lax.rsqrt(jnp.mean(x * x, axis=-1, keepdims=True) + EPS) * g


def _dot(a, b):
    return jnp.dot(a, b, preferred_element_type=F32)


def _dot_nt(a, b):
    return lax.dot_general(a, b, (((1,), (1,)), ((), ())), preferred_element_type=F32)


def _keep_lanes(q, lo, hi):
    lane = lax.broadcasted_iota(jnp.int32, q.shape, 1)
    return jnp.where((lane >= lo) & (lane < hi), q.astype(F32), 0.0).astype(BF16)


def _group_rms(y, gmat):
    wide = gmat.shape[0]
    parts = []
    for c in range(y.shape[1] // wide):
        yc = y[:, c * wide:(c + 1) * wide]
        ms = _dot((yc * yc).astype(BF16), gmat)
        parts.append(yc * lax.rsqrt(ms + EPS))
    return parts[0] if len(parts) == 1 else jnp.concatenate(parts, axis=1)


def _group_mean_matrix():
    g = np.kron(np.eye(V7X_MXU_DIM // HEAD_DIM), np.ones((HEAD_DIM, HEAD_DIM))) / HEAD_DIM
    return jnp.asarray(g, BF16)


def _ffn_kernel(x_ref, g_ref, wg_ref, wu_ref, wd_ref, o_ref, h_scr, acc_scr):
    j = pl.program_id(1)

    @pl.when(j == 0)
    def _():
        h_scr[...] = _rms_rows(x_ref[...], g_ref[...]).astype(BF16)
        acc_scr[...] = jnp.zeros_like(acc_scr)

    h = h_scr[...]
    a = _dot(h, wg_ref[...])
    u = _dot(h, wu_ref[...])
    t = a * jax.nn.sigmoid(a) * u
    acc_scr[...] += _dot(t.astype(BF16), wd_ref[...])

    @pl.when(j == pl.num_programs(1) - 1)
    def _():
        o_ref[...] = x_ref[...] + 0.5 * acc_scr[...]


def _ffn(x, g, wg, wu, wd, tm, tf=512):
    n, d = x.shape
    dff = wg.shape[1]
    return pl.pallas_call(
        _ffn_kernel,
        grid=(n // tm, dff // tf),
        in_specs=[
            pl.BlockSpec((tm, d), lambda i, j: (i, 0)),
            pl.BlockSpec((1, d), lambda i, j: (0, 0)),
            pl.BlockSpec((d, tf), lambda i, j: (0, j)),
            pl.BlockSpec((d, tf), lambda i, j: (0, j)),
            pl.BlockSpec((tf, d), lambda i, j: (j, 0)),
        ],
        out_specs=pl.BlockSpec((tm, d), lambda i, j: (i, 0)),
        out_shape=jax.ShapeDtypeStruct((n, d), F32),
        scratch_shapes=[pltpu.VMEM((tm, d), BF16), pltpu.VMEM((tm, d), F32)],
        compiler_params=_cparams(("parallel", "arbitrary")),
        name="ffn",
    )(x, g.reshape(1, d), wg, wu, wd)


def _proj_kernel(x_ref, g_ref, w_ref, gain_ref, gmat_ref, *out_refs, segs):
    h = _rms_rows(x_ref[...], g_ref[...]).astype(BF16)
    off = 0
    out_refs = list(out_refs)
    for width, normed, gained, outs in segs:
        y = _dot(h, w_ref[:, off:off + width])
        if normed:
            y = _group_rms(y, gmat_ref[...])
        if gained:
            y = y * gain_ref[:, off:off + width]
        for _, head_width in outs:
            o_ref = out_refs.pop(0)
            if head_width:
                for hd in range(width // head_width):
                    o_ref[:, hd, :] = y[:, hd * head_width:(hd + 1) * head_width].astype(o_ref.dtype)
            else:
                o_ref[...] = y.astype(o_ref.dtype)
        off += width


def _proj(x, g, w, gain, segs, tm):
    n, d = x.shape
    wtot = w.shape[1]
    specs, shapes = [], []
    for width, _, _, outs in segs:
        for dt, head_width in outs:
            if head_width:
                nh = width // head_width
                specs.append(pl.BlockSpec((tm, nh, head_width), lambda i: (i, 0, 0)))
                shapes.append(jax.ShapeDtypeStruct((n, nh, head_width), dt))
            else:
                specs.append(pl.BlockSpec((tm, width), lambda i: (i, 0)))
                shapes.append(jax.ShapeDtypeStruct((n, width), dt))
    return pl.pallas_call(
        functools.partial(_proj_kernel, segs=tuple(segs)),
        grid=(n // tm,),
        in_specs=[
            pl.BlockSpec((tm, d), lambda i: (i, 0)),
            pl.BlockSpec((1, d), lambda i: (0, 0)),
            pl.BlockSpec((d, wtot), lambda i: (0, 0), pipeline_mode=pl.Buffered(1)),
            pl.BlockSpec((1, wtot), lambda i: (0, 0)),
            pl.BlockSpec((V7X_MXU_DIM, V7X_MXU_DIM), lambda i: (0, 0)),
        ],
        out_specs=specs,
        out_shape=shapes,
        compiler_params=_cparams(("parallel",)),
        name="proj",
    )(x, g.reshape(1, d), w, gain, _group_mean_matrix())


def _softmax_step_km(s, vt, m_ref, acc_ref):
    m_old = m_ref[...]
    m_new = jnp.maximum(m_old, jnp.max(s, axis=0, keepdims=True))
    alpha = jnp.exp2(m_old - m_new)
    p = jnp.exp2(s - m_new)
    acc_ref[...] = alpha * acc_ref[...] + _dot(vt, p.astype(BF16))
    m_ref[...] = m_new


def _cast_rows(src_ref, dst_ref, rows, dst_off=0, step=512):
    def body(i, c):
        r = pl.multiple_of(i * step, step)
        dst_ref[pl.ds(dst_off + r, step), :] = src_ref[0, pl.ds(r, step), :].astype(BF16)
        return c
    lax.fori_loop(0, rows // step, body, 0)


def _attn_a_kernel(lam_ref, brange_ref, q_ref, k_ref, v_ref, bias_ref, o_ref,
                   vtb, s_scr, knorm, m1, a1, m2, a2):
    h = pl.program_id(1)
    qi = pl.program_id(2)
    kbf = k_ref.at[0]
    seq = kbf.shape[0]
    tq = q_ref.shape[1]
    tk = vtb.shape[2]
    dv = v_ref.shape[2]

    @pl.when(qi == 0)
    def _():
        lane = lax.broadcasted_iota(jnp.int32, (tk, LANES), 1)

        def prep(j, c):
            r = pl.multiple_of(j * tk, tk)
            kb = kbf[pl.ds(r, tk), :]
            vtb[j, 0:dv, :] = v_ref[0, pl.ds(r, tk), :].astype(F32).T.astype(BF16)
            vtb[j, dv:, :] = jnp.ones((vtb.shape[1] - dv, tk), BF16)
            ksq = kb.astype(F32) * kb.astype(F32)
            n1 = jnp.max(jnp.sum(jnp.where(lane < HEAD_DIM, ksq, 0.0), axis=1, keepdims=True))
            n2 = jnp.max(jnp.sum(jnp.where(lane >= HEAD_DIM, ksq, 0.0), axis=1, keepdims=True))
            return jnp.maximum(c[0], n1), jnp.maximum(c[1], n2)
        n1, n2 = lax.fori_loop(0, seq // tk, prep, (jnp.float32(0.0), jnp.float32(0.0)))
        knorm[0] = n1
        knorm[1] = n2

    qt = q_ref[0].astype(F32).T
    sub = lax.broadcasted_iota(jnp.int32, qt.shape, 0)
    q1t = jnp.where(sub < HEAD_DIM, qt, 0.0).astype(BF16)
    q2t = jnp.where(sub >= HEAD_DIM, qt, 0.0).astype(BF16)
    states = ((m1, a1), (m2, a2))
    for m_ref, a_ref in states:
        a_ref[...] = jnp.zeros_like(a_ref)

    far_bias, bias_max, bias_min = brange_ref[3 * h], brange_ref[3 * h + 1], brange_ref[3 * h + 2]
    qsq = qt * qt
    bound1 = jnp.sqrt(jnp.sum(jnp.where(sub < HEAD_DIM, qsq, 0.0), axis=0, keepdims=True) * knorm[0])
    bound2 = jnp.sqrt(jnp.sum(jnp.where(sub >= HEAD_DIM, qsq, 0.0), axis=0, keepdims=True) * knorm[1])
    bounds = (bound1 * BOUND_SLACK + bias_max, bound2 * BOUND_SLACK + bias_max)
    spread = 2.0 * BOUND_SLACK * jnp.maximum(jnp.max(bound1), jnp.max(bound2)) + (bias_max - bias_min)
    bounded = spread <= BOUNDED_EXP2_SPAN

    @pl.when(bounded)
    def _():
        qmaps = ((q1t, bounds[0], a1), (q2t, bounds[1], a2))

        def accumulate(blocks):
            kbs = [kbf[pl.ds(pl.multiple_of(j * tk, tk), tk), :] for j, _ in blocks]
            ps = []
            for (j, tile), kb in zip(blocks, kbs):
                for qmt, shift, _ in qmaps:
                    add = (far_bias - shift) if tile is None else (tile - shift)
                    ps.append(jnp.exp2(_dot(kb, qmt) + add).astype(BF16))
            for mi, (_, _, a_ref) in enumerate(qmaps):
                total = None
                for bi, (j, _) in enumerate(blocks):
                    term = _dot(vtb[j], ps[bi * len(qmaps) + mi])
                    total = term if total is None else total + term
                a_ref[...] += total

        @pl.when(qi == 0)
        def _():
            accumulate([(qi, bias_ref[0, 0])])

        @pl.when(qi >= 1)
        def _():
            accumulate([(qi, bias_ref[0, 0]), (qi - 1, bias_ref[0, 1])])

        n_far = jnp.maximum(qi - 1, 0)
        one = n_far & 1
        two = n_far & 2

        @pl.when(one == 1)
        def _():
            accumulate([(0, None)])

        @pl.when(two == 2)
        def _():
            accumulate([(one, None), (one + 1, None)])

        def quad(i, c):
            j = one + two + 4 * i
            accumulate([(j, None), (j + 1, None), (j + 2, None), (j + 3, None)])
            return c
        lax.fori_loop(0, lax.shift_right_logical(n_far, 2), quad, 0)

    @pl.when(jnp.logical_not(bounded))
    def _():
        for m_ref, _ in states:
            m_ref[...] = jnp.full_like(m_ref, NEG)
        n_blocks = qi + 1

        def scores(t, buf):
            kb = kbf[pl.ds(pl.multiple_of((qi - t) * tk, tk), tk), :]
            s_scr[buf, 0] = _dot(kb, q1t)
            s_scr[buf, 1] = _dot(kb, q2t)

        def update(t, buf):
            bias = bias_ref[0, jnp.minimum(t, bias_ref.shape[1] - 1)]
            vt = vtb[qi - t]
            for mi, (m_ref, a_ref) in enumerate(states):
                _softmax_step_km(s_scr[buf, mi] + bias, vt, m_ref, a_ref)

        odd = n_blocks & 1

        @pl.when(odd == 1)
        def _():
            scores(0, 0)
            update(0, 0)

        n_pairs = lax.shift_right_logical(n_blocks, 1)

        @pl.when(n_pairs > 0)
        def _():
            scores(odd, 0)

        def pair(p, c):
            t0 = odd + 2 * p
            scores(t0 + 1, 1)
            update(t0, 0)
            scores(jnp.minimum(t0 + 2, qi), 0)
            update(t0 + 1, 1)
            return c
        lax.fori_loop(0, n_pairs, pair, 0)

    out_t = (a1[0:dv, :] / a1[dv:dv + 1, :]
             - lam_ref[0] * (a2[0:dv, :] / a2[dv:dv + 1, :]))
    o_ref[0] = out_t.T


def _attn_a(lam, brange, qa, ka, va, bias):
    b, s, w = qa.shape
    nh = w // LANES
    tq = tk = MIXER_A_TILE
    vrows = LANES + BF16_SUBLANES
    return pl.pallas_call(
        _attn_a_kernel,
        grid=(b, nh, s // tq),
        in_specs=[
            pl.BlockSpec(memory_space=pltpu.SMEM),
            pl.BlockSpec(memory_space=pltpu.SMEM),
            pl.BlockSpec((1, tq, LANES), lambda bi, h, qi: (bi, qi, h)),
            pl.BlockSpec((1, s, LANES), lambda bi, h, qi: (bi, 0, h)),
            pl.BlockSpec((1, s, LANES), lambda bi, h, qi: (bi, 0, h)),
            pl.BlockSpec((1,) + bias.shape[1:], lambda bi, h, qi: (h, 0, 0, 0)),
        ],
        out_specs=pl.BlockSpec((1, tq, LANES), lambda bi, h, qi: (bi, qi, h)),
        out_shape=jax.ShapeDtypeStruct((b, s, w), F32),
        scratch_shapes=[
            pltpu.VMEM((s // tk, vrows, tk), BF16),
            pltpu.VMEM((2, 2, tk, tq), F32), pltpu.SMEM((2,), F32),
            pltpu.VMEM((1, tq), F32), pltpu.VMEM((vrows, tq), F32),
            pltpu.VMEM((1, tq), F32), pltpu.VMEM((vrows, tq), F32),
        ],
        compiler_params=_cparams(("parallel", "parallel", "arbitrary")),
        name="mixer_a",
    )(lam, brange, qa, ka, va, bias)


def _stick_block(z, valid, umat, carry):
    sp = jnp.maximum(z, 0.0) + jnp.log(1.0 + jnp.exp(-jnp.abs(z)))
    log1m = -sp
    if valid is not None:
        log1m = jnp.where(valid, log1m, 0.0)
    hi = log1m.astype(BF16)
    lo = (log1m - hi.astype(F32)).astype(BF16)
    after = _dot(hi, umat) + _dot(lo, umat)
    w = jnp.exp(z - sp + after + carry)
    if valid is not None:
        w = jnp.where(valid, w, 0.0)
    return w, jnp.sum(log1m, axis=1, keepdims=True)


def _stick_block_km(z, valid, umat, carry):
    sp = jnp.maximum(z, 0.0) + jnp.log(1.0 + jnp.exp(-jnp.abs(z)))
    log1m = -sp
    if valid is not None:
        log1m = jnp.where(valid, log1m, 0.0)
    hi = log1m.astype(BF16)
    lo = (log1m - hi.astype(F32)).astype(BF16)
    after = _dot(umat, hi) + _dot(umat, lo)
    w = jnp.exp(z - sp + after + carry)
    if valid is not None:
        w = jnp.where(valid, w, 0.0)
    return w, jnp.sum(log1m, axis=0, keepdims=True)


def _attn_b_kernel(q_ref, k_ref, v_ref, u_ref, o_ref, kbf, vtb, c_scr, acc_scr, z_scr):
    step = pl.program_id(2)
    seq = kbf.shape[0]
    tk = vtb.shape[2]
    n_halves = q_ref.shape[1] // tk
    n_heads = LANES // HEAD_DIM

    @pl.when(step == 0)
    def _():
        def prep(j, c):
            r = pl.multiple_of(j * tk, tk)
            kbf[pl.ds(r, tk), :] = k_ref[0, pl.ds(r, tk), :].astype(BF16)
            vtb[j] = v_ref[0, pl.ds(r, tk), :].T.astype(BF16)
            return c
        lax.fori_loop(0, seq // tk, prep, 0)

    sub = lax.broadcasted_iota(jnp.int32, (LANES, tk), 0)
    qts = []
    for half in range(n_halves):
        qt = q_ref[0, half * tk:(half + 1) * tk, :].astype(F32).T
        qts.append([jnp.where((sub >= hh * HEAD_DIM) & (sub < (hh + 1) * HEAD_DIM), qt, 0.0).astype(BF16)
                    for hh in range(n_heads)])
    krow = lax.broadcasted_iota(jnp.int32, (tk, tk), 0)
    qcol = lax.broadcasted_iota(jnp.int32, (tk, tk), 1)
    umat = u_ref[...]

    def blocks(half, js, valids, first):
        kbs = [kbf[pl.ds(pl.multiple_of(j * tk, tk), tk), :] for j in js]
        cmax = None
        for hh in range(n_heads):
            carry = 0.0 if first else c_scr[half, hh]
            pv = None
            for j, kb, valid in zip(js, kbs, valids):
                w, cs = _stick_block_km(_dot(kb, qts[half][hh]), valid, umat, carry)
                term = _dot(vtb[j], w.astype(BF16))
                pv = term if pv is None else pv + term
                carry = carry + cs
            acc_scr[half, hh] = pv if first else acc_scr[half, hh] + pv
            c_scr[half, hh] = carry
            cm = jnp.max(carry)
            cmax = cm if cmax is None else jnp.maximum(cmax, cm)
        return cmax

    own_valid = krow < qcol
    everywhere = krow >= 0

    chains = []
    for half in range(n_halves):
        qb = step * n_halves + half
        prev_valid = None if half >= 1 else everywhere & (qb >= 1)
        for hh in range(n_heads):
            for j, valid in ((qb, own_valid), (jnp.maximum(qb - 1, 0), prev_valid)):
                chains.append((half, hh, j, valid))
    for c, (half, hh, j, valid) in enumerate(chains):
        z_scr[c] = _dot(kbf[pl.ds(pl.multiple_of(j * tk, tk), tk), :], qts[half][hh])
    col_sums = []
    for c, (half, hh, j, valid) in enumerate(chains):
        z = z_scr[c]
        sp = jnp.maximum(z, 0.0) + jnp.log(1.0 + jnp.exp(-jnp.abs(z)))
        log1m = -sp if valid is None else jnp.where(valid, -sp, 0.0)
        hi = log1m.astype(BF16)
        lo = (log1m - hi.astype(F32)).astype(BF16)
        z_scr[c] = z - sp + _dot(umat, hi) + _dot(umat, lo)
        col_sums.append(jnp.sum(log1m, axis=0, keepdims=True))
    cmaxes = []
    for half in range(n_halves):
        cmax = None
        for hh in range(n_heads):
            carry, pv = 0.0, None
            for c, (ch, chh, j, valid) in enumerate(chains):
                if (ch, chh) != (half, hh):
                    continue
                w = jnp.exp(z_scr[c] + carry)
                if valid is not None:
                    w = jnp.where(valid, w, 0.0)
                term = _dot(vtb[j], w.astype(BF16))
                pv = term if pv is None else pv + term
                carry = carry + col_sums[c]
            acc_scr[half, hh] = pv
            c_scr[half, hh] = carry
            cm = jnp.max(carry)
            cmax = cm if cmax is None else jnp.maximum(cmax, cm)
        cmaxes.append(cmax)

    for half in range(n_halves):
        qb = step * n_halves + half

        def cond(st):
            j, cmax = st
            return (j >= 0) & (cmax > STICK_SKIP)

        def body(st, half=half):
            j, _ = st
            return j - 2, blocks(half, [j, jnp.maximum(j - 1, 0)], [None, everywhere & (j >= 1)], False)

        lax.while_loop(cond, body, (qb - 2, cmaxes[half]))
        out_t = jnp.where(sub < HEAD_DIM, acc_scr[half, 0], acc_scr[half, 1])
        o_ref[0, half * tk:(half + 1) * tk, :] = out_t.T.astype(o_ref.dtype)


def _strict_lower(n):
    return jnp.asarray(np.tril(np.ones((n, n)), -1), BF16)


def _strict_upper(n):
    return jnp.asarray(np.triu(np.ones((n, n)), 1), BF16)


def _attn_b(qb, kb, vb):
    b, s, w = qb.shape
    n_halves = 4
    tq = n_halves * ATTN_TK
    return pl.pallas_call(
        _attn_b_kernel,
        grid=(b, w // LANES, s // tq),
        in_specs=[
            pl.BlockSpec((1, tq, LANES), lambda bi, h, qi: (bi, qi, h)),
            pl.BlockSpec((1, s, LANES), lambda bi, h, qi: (bi, 0, h)),
            pl.BlockSpec((1, s, LANES), lambda bi, h, qi: (bi, 0, h)),
            pl.BlockSpec((ATTN_TK, ATTN_TK), lambda bi, h, qi: (0, 0)),
        ],
        out_specs=pl.BlockSpec((1, tq, LANES), lambda bi, h, qi: (bi, qi, h)),
        out_shape=jax.ShapeDtypeStruct((b, s, w), BF16),
        scratch_shapes=[
            pltpu.VMEM((s, LANES), BF16), pltpu.VMEM((s // ATTN_TK, LANES, ATTN_TK), BF16),
            pltpu.VMEM((n_halves, LANES // HEAD_DIM, 1, ATTN_TK), F32),
            pltpu.VMEM((n_halves, LANES // HEAD_DIM, LANES, ATTN_TK), F32),
            pltpu.VMEM((2 * n_halves * (LANES // HEAD_DIM), ATTN_TK, ATTN_TK), F32),
        ],
        compiler_params=_cparams(("parallel", "parallel", "arbitrary")),
        name="mixer_b",
    )(qb, kb, vb, _strict_upper(ATTN_TK))


def _attn_c_kernel(q_ref, k_ref, v_ref, bias_ref, o_ref, kbf, vbf, s_scr):
    step = pl.program_id(2)
    seq = k_ref.shape[1]
    tq = ATTN_TQ
    n_blocks = q_ref.shape[1] // tq
    pad = C_PREV * CHUNK
    n_heads = LANES // HEAD_DIM

    @pl.when(step == 0)
    def _():
        kbf[0:pad, :] = jnp.zeros((pad, LANES), BF16)
        vbf[0:pad, 0:LANES] = jnp.zeros((pad, LANES), BF16)
        vbf[:, LANES:] = jnp.ones((vbf.shape[0], vbf.shape[1] - LANES), BF16)
        _cast_rows(k_ref, kbf, seq, dst_off=pad)

        def vcast(i, c):
            r = pl.multiple_of(i * tq, tq)
            vbf[pl.ds(pad + r, tq), 0:LANES] = v_ref[0, pl.ds(r, tq), :].astype(BF16)
            return c
        lax.fori_loop(0, seq // tq, vcast, 0)

    lane = lax.broadcasted_iota(jnp.int32, (tq, LANES), 1)
    col = lax.broadcasted_iota(jnp.int32, (tq, BAND_W), 1)
    for blk in range(n_blocks):
        qb = step * n_blocks + blk
        q = q_ref[0, blk * tq:(blk + 1) * tq, :]
        kw = kbf[pl.ds(pl.multiple_of(qb * tq, tq), BAND_W), :]
        in_seq = col >= pad - qb * tq
        for hh in range(n_heads):
            qm = _keep_lanes(q, hh * HEAD_DIM, (hh + 1) * HEAD_DIM)
            s_scr[blk * n_heads + hh] = jnp.where(in_seq, _dot_nt(qm, kw) + bias_ref[hh], NEG)
    for blk in range(n_blocks):
        qb = step * n_blocks + blk
        vw = vbf[pl.ds(pl.multiple_of(qb * tq, tq), BAND_W), :]
        outs = []
        for hh in range(n_heads):
            s = s_scr[blk * n_heads + hh]
            p = jnp.exp(s - jnp.max(s, axis=1, keepdims=True))
            acc = _dot(p.astype(BF16), vw)
            outs.append(acc[:, 0:LANES] / acc[:, LANES:])
        o_ref[0, blk * tq:(blk + 1) * tq, :] = jnp.where(lane < HEAD_DIM, outs[0], outs[1]).astype(o_ref.dtype)


def _attn_c(qc, kc, vc, bias):
    b, s, w = qc.shape
    n_blocks = 4
    tq = n_blocks * ATTN_TQ
    pad = C_PREV * CHUNK
    hp = LANES // HEAD_DIM
    return pl.pallas_call(
        _attn_c_kernel,
        grid=(b, w // LANES, s // tq),
        in_specs=[
            pl.BlockSpec((1, tq, LANES), lambda bi, h, qi: (bi, qi, h)),
            pl.BlockSpec((1, s, LANES), lambda bi, h, qi: (bi, 0, h)),
            pl.BlockSpec((1, s, LANES), lambda bi, h, qi: (bi, 0, h)),
            pl.BlockSpec((hp, ATTN_TQ, BAND_W), lambda bi, h, qi: (h, 0, 0)),
        ],
        out_specs=pl.BlockSpec((1, tq, LANES), lambda bi, h, qi: (bi, qi, h)),
        out_shape=jax.ShapeDtypeStruct((b, s, w), BF16),
        scratch_shapes=[pltpu.VMEM((s + pad, LANES), BF16), pltpu.VMEM((s + pad, 2 * LANES), BF16),
                        pltpu.VMEM((n_blocks * hp, ATTN_TQ, BAND_W), F32)],
        compiler_params=_cparams(("parallel", "parallel", "arbitrary")),
        name="mixer_c",
    )(qc, kc, vc, bias)


def _merge_kernel(x_ref, oa_ref, ob_ref, oc_ref, g_ref, wgate_ref, sub_ref,
                  wa_ref, wb_ref, wc_ref, wout_ref, o_ref):
    x = x_ref[...]
    d = x.shape[1]
    h = _rms_rows(x, g_ref[...]).astype(BF16)
    oa = oa_ref[...]
    parts = []
    for c in range(oa.shape[1] // LANES):
        oc_ = oa[:, c * LANES:(c + 1) * LANES]
        parts.append(oc_ * lax.rsqrt(jnp.mean(oc_ * oc_, axis=-1, keepdims=True) + EPS))
    oan = (jnp.concatenate(parts, axis=1) * sub_ref[...]).astype(BF16)
    merged = jax.nn.sigmoid(_dot(h, wgate_ref[:, 0:d])) * _dot(oan, wa_ref[...])
    merged += jax.nn.sigmoid(_dot(h, wgate_ref[:, d:2 * d])) * _dot(ob_ref[...], wb_ref[...])
    merged += jax.nn.sigmoid(_dot(h, wgate_ref[:, 2 * d:3 * d])) * _dot(oc_ref[...], wc_ref[...])
    o_ref[...] = x + _dot(merged.astype(BF16), wout_ref[...])


def _merge(x, oa, ob, oc, g, wgate, sub, wa, wb, wc, wout, tm):
    n, d = x.shape
    full = lambda a: pl.BlockSpec(a.shape, lambda i: (0,) * a.ndim, pipeline_mode=pl.Buffered(1))
    rows = lambda a: pl.BlockSpec((tm, a.shape[1]), lambda i: (i, 0))
    g = g.reshape(1, d)
    return pl.pallas_call(
        _merge_kernel,
        grid=(n // tm,),
        in_specs=[rows(x), rows(oa), rows(ob), rows(oc), full(g), full(wgate), full(sub),
                  full(wa), full(wb), full(wc), full(wout)],
        out_specs=rows(x),
        out_shape=jax.ShapeDtypeStruct((n, d), F32),
        compiler_params=_cparams(("parallel",)),
        name="merge",
    )(x, oa, ob, oc, g, wgate, sub, wa, wb, wc, wout)


def _cross_kernel(x_ref, g_ref, wq_ref, gain_ref, gmat_ref, mk_ref, mv_ref, wo_ref, o_ref):
    x = x_ref[0]
    h = _rms_rows(x, g_ref[...]).astype(BF16)
    q = (_group_rms(_dot(h, wq_ref[...]), gmat_ref[...]) * gain_ref[...]).astype(BF16)
    mk = mk_ref[0]
    mv = mv_ref[0]
    lane = lax.broadcasted_iota(jnp.int32, q.shape, 1)
    o = jnp.zeros(q.shape, F32)
    for hh in range(q.shape[1] // HEAD_DIM):
        in_head = (lane >= hh * HEAD_DIM) & (lane < (hh + 1) * HEAD_DIM)
        s = _dot_nt(_keep_lanes(q, hh * HEAD_DIM, (hh + 1) * HEAD_DIM), mk)
        p = jnp.exp(s - jnp.max(s, axis=1, keepdims=True))
        l = jnp.sum(p, axis=1, keepdims=True)
        o = jnp.where(in_head, _dot(p.astype(BF16), mv) / l, o)
    o_ref[0] = x + _dot(o.astype(BF16), wo_ref[...])


def _cross(x, g, wq, gain, mk, mv, wo, tm):
    b, s, d = x.shape
    full = lambda a: pl.BlockSpec(a.shape, lambda bi, i: (0,) * a.ndim)
    g = g.reshape(1, d)
    gmat = _group_mean_matrix()
    return pl.pallas_call(
        _cross_kernel,
        grid=(b, s // tm),
        in_specs=[
            pl.BlockSpec((1, tm, d), lambda bi, i: (bi, i, 0)),
            full(g), full(wq), full(gain), full(gmat),
            pl.BlockSpec((1,) + mk.shape[1:], lambda bi, i: (bi, 0, 0)),
            pl.BlockSpec((1,) + mv.shape[1:], lambda bi, i: (bi, 0, 0)),
            full(wo),
        ],
        out_specs=pl.BlockSpec((1, tm, d), lambda bi, i: (bi, i, 0)),
        out_shape=jax.ShapeDtypeStruct((b, s, d), F32),
        compiler_params=_cparams(("parallel", "parallel")),
        name="cross",
    )(x, g, wq, gain, gmat, mk, mv, wo)


def _sample_attn_kernel(lam_ref,
                        qa_ref, kan_ref, van_ref, kac_ref, vac_ref, bac_ref, ban_ref,
                        qb_ref, kbn_ref, vbn_ref, kbc_ref, vbc_ref, ubig_ref, usmall_ref,
                        qc_ref, kcn_ref, vcn_ref, kcc_ref, vcc_ref, bcc_ref, bcn_ref,
                        oa_ref, ob_ref, oc_ref):
    lam = lam_ref[0]
    ns = qa_ref.shape[1]
    bf = lambda r: r[0].astype(BF16)

    def heads(q, width):
        for hh in range(LANES // width):
            yield hh, None, _keep_lanes(q, hh * width, (hh + 1) * width)

    def softmax2(s_c, s_n, exp=jnp.exp):
        m = jnp.maximum(jnp.max(s_c, axis=1, keepdims=True), jnp.max(s_n, axis=1, keepdims=True))
        p_c = exp(s_c - m)
        p_n = exp(s_n - m)
        inv = 1.0 / (jnp.sum(p_c, axis=1, keepdims=True) + jnp.sum(p_n, axis=1, keepdims=True))
        return p_c * inv, p_n * inv

    for h in range(qa_ref.shape[2] // LANES):
        sl = slice(h * LANES, (h + 1) * LANES)
        q = qa_ref[0, :, sl]
        k_c = kac_ref[0, :, h, :].astype(BF16)
        k_n = kan_ref[0, :, sl].astype(BF16)
        maps = []
        for _, _, qm in heads(q, HEAD_DIM):
            maps.append(softmax2(_dot_nt(qm, k_c) + bac_ref[h], _dot_nt(qm, k_n) + ban_ref[h], exp=jnp.exp2))
        a_c = (maps[0][0] - lam * maps[1][0]).astype(BF16)
        a_n = (maps[0][1] - lam * maps[1][1]).astype(BF16)
        oa_ref[0, :, sl] = (_dot(a_c, vac_ref[0, :, h, :].astype(BF16))
                            + _dot(a_n, van_ref[0, :, sl].astype(BF16)))

    past = kbc_ref.shape[1]
    tk = ubig_ref.shape[0]
    row = lax.broadcasted_iota(jnp.int32, (ns, ns), 0)
    col = lax.broadcasted_iota(jnp.int32, (ns, ns), 1)
    for pr in range(qb_ref.shape[2] // LANES):
        sl = slice(pr * LANES, (pr + 1) * LANES)
        q = qb_ref[0, :, sl]
        k_n = kbn_ref[0, :, sl].astype(BF16)
        v_n = vbn_ref[0, :, sl].astype(BF16)
        outs = []
        for _, _, qm in heads(q, HEAD_DIM):
            w, carry = _stick_block(_dot_nt(qm, k_n), col < row, usmall_ref[...], 0.0)
            acc = _dot(w.astype(BF16), v_n)
            for j in range(past // tk - 1, -1, -1):
                k_c = kbc_ref[0, j * tk:(j + 1) * tk, sl].astype(BF16)
                v_c = vbc_ref[0, j * tk:(j + 1) * tk, sl].astype(BF16)
                w, rs = _stick_block(_dot_nt(qm, k_c), None, ubig_ref[...], carry)
                acc += _dot(w.astype(BF16), v_c)
                carry = carry + rs
            outs.append(acc)
        lane = lax.broadcasted_iota(jnp.int32, q.shape, 1)
        ob_ref[0, :, sl] = jnp.where(lane < HEAD_DIM, outs[0], outs[1]).astype(ob_ref.dtype)

    for pr in range(qc_ref.shape[2] // LANES):
        sl = slice(pr * LANES, (pr + 1) * LANES)
        q = qc_ref[0, :, sl]
        k_c = kcc_ref[0, :, sl].astype(BF16)
        k_n = kcn_ref[0, :, sl].astype(BF16)
        v_c = vcc_ref[0, :, sl].astype(BF16)
        v_n = vcn_ref[0, :, sl].astype(BF16)
        outs = []
        for hh, _, qm in heads(q, HEAD_DIM):
            hd = pr * (LANES // HEAD_DIM) + hh
            p_c, p_n = softmax2(_dot_nt(qm, k_c) + bcc_ref[hd], _dot_nt(qm, k_n) + bcn_ref[hd])
            outs.append(_dot(p_c.astype(BF16), v_c) + _dot(p_n.astype(BF16), v_n))
        lane = lax.broadcasted_iota(jnp.int32, q.shape, 1)
        oc_ref[0, :, sl] = jnp.where(lane < HEAD_DIM, outs[0], outs[1]).astype(oc_ref.dtype)


def _sample_attn(layer, lam, qa, ka, va, cak, cav, bac, ban, qb, kb, vb, cbk, cbv,
                 qc, kc, vc, cck, ccv, bcc, bcn):
    b, ns, _ = qa.shape
    per_b = lambda a: pl.BlockSpec((1,) + a.shape[1:], lambda bi: (bi,) + (0,) * (a.ndim - 1))
    full = lambda a: pl.BlockSpec(a.shape, lambda bi: (0,) * a.ndim)
    layer_b = lambda a: pl.BlockSpec((None, 1) + a.shape[2:], lambda bi: (layer, bi) + (0,) * (a.ndim - 2))
    ubig = _strict_lower(ATTN_TK)
    usmall = _strict_lower(ns)
    args = [qa, ka, va, cak, cav, bac, ban, qb, kb, vb, cbk, cbv, ubig, usmall,
            qc, kc, vc, cck, ccv, bcc, bcn]
    specs = [per_b(qa), per_b(ka), per_b(va), layer_b(cak), layer_b(cav), full(bac), full(ban),
             per_b(qb), per_b(kb), per_b(vb), per_b(cbk), per_b(cbv), full(ubig), full(usmall),
             per_b(qc), per_b(kc), per_b(vc), per_b(cck), per_b(ccv), full(bcc), full(bcn)]
    return pl.pallas_call(
        _sample_attn_kernel,
        grid=(b,),
        in_specs=[pl.BlockSpec(memory_space=pltpu.SMEM)] + specs,
        out_specs=[per_b(qa), per_b(qb), per_b(qc)],
        out_shape=[jax.ShapeDtypeStruct(qa.shape, F32), jax.ShapeDtypeStruct(qb.shape, BF16),
                   jax.ShapeDtypeStruct(qc.shape, BF16)],
        compiler_params=_cparams(("parallel",)),
        name="sample_mixers",
    )(lam, *args)


def _t5_bucket_np(rel):
    half = T5_BUCKETS // 2
    max_exact = half // 2
    n = np.abs(rel)
    nf = np.maximum(n, 1).astype(np.float64)
    large = max_exact + (np.log(nf / max_exact) / math.log(T5_MAX_DIST / max_exact)
                         * (half - max_exact)).astype(np.int64)
    large = np.minimum(large, half - 1)
    return np.where(rel > 0, half, 0) + np.where(n < max_exact, n, large)


def _toeplitz(lookup, n_rows, n_cols):
    period = n_rows + n_cols
    slot = np.arange(period)
    diff = np.minimum((slot + n_rows - 1) % period - (n_rows - 1), n_cols - 1)
    vec = lookup(diff).astype(F32)
    flat = jnp.tile(vec, (1, n_rows))[:, :n_rows * (period - 1)]
    return flat.reshape(vec.shape[0], n_rows, period - 1)[:, :, :n_cols]


def _t5_bias_table(t5_bias, qpos, kpos, key_major=False):
    t5_rows = lambda rel: t5_bias[_t5_bucket_np(rel)].T
    mask = (kpos[None, :] // CHUNK) <= (qpos[:, None] // CHUNK)
    if key_major:
        table = _toeplitz(lambda dd: t5_rows(kpos[0] - qpos[0] - dd), len(kpos), len(qpos))
        mask = mask.T
    else:
        table = _toeplitz(lambda dd: t5_rows(kpos[0] - qpos[0] + dd), len(qpos), len(kpos))
    return jnp.where(jnp.asarray(mask)[None], table, NEG)


def _band_bias_table(rel_table, qpos, kpos):
    lookup = lambda dd: rel_table[:, np.clip(kpos[0] - qpos[0] + dd, -REL_CLIP, REL_CLIP) + REL_CLIP]
    table = _toeplitz(lookup, len(qpos), len(kpos))
    qc = qpos[:, None] // CHUNK
    kc = kpos[None, :] // CHUNK
    mask = (kpos[None, :] >= 0) & (kc <= qc) & (kc >= qc - C_PREV)
    return jnp.where(jnp.asarray(mask)[None], table, NEG)


def _lambda_init(layer):
    return 0.8 - 0.6 * math.exp(-0.3 * layer)


def kernel(x_prompt, x_sample, mem_prompt, cache_a_k, cache_a_v, cache_b_k, cache_b_v, cache_c_k, cache_c_v, cache_mem_k, cache_mem_v, t5_bias, ffn1_norm, ffn1_wg, ffn1_wu, ffn1_wd, mix_norm, w_in, a_qnorm, a_knorm, a_lq1, a_lk1, a_lq2, a_lk2, a_subln, c_qnorm, c_knorm, c_rel_bias, w_br_a, w_br_b, w_br_c, w_out, x_norm, mem_norm, x_wq, x_wkv, x_qnorm, x_knorm, x_wo, ffn2_norm, ffn2_wg, ffn2_wu, ffn2_wd):
    bp, sp, d = x_prompt.shape
    bs, ns, _ = x_sample.shape
    depth = w_in.shape[0]
    past = cache_a_k.shape[2]
    w_buf = cache_c_k.shape[2]
    n_mem = mem_prompt.shape[1]
    wa = cache_a_k.shape[3] * cache_a_k.shape[4]
    wb = cache_b_k.shape[3] * cache_b_k.shape[4]
    wc = cache_c_k.shape[3] * cache_c_k.shape[4]
    wx = cache_mem_k.shape[3] * cache_mem_k.shape[4]
    h_a = cache_a_k.shape[3]
    n_qkv = 3 * (wa + wb + wc)
    w_keep = min(C_PREV * CHUNK, sp)
    scale = HEAD_DIM ** -0.5
    assert sp % ATTN_TQ == 0 and ATTN_TQ == ATTN_TK and ATTN_TQ == 4 * CHUNK and sp >= w_keep
    assert past % ATTN_TK == 0
    assert sp % MIXER_A_TILE == 0 and MIXER_A_TILE % CHUNK == 0 and T5_MAX_DIST <= MIXER_A_TILE

    tile = lambda g, reps: jnp.tile(g.astype(F32), reps)
    bf = lambda a: a.astype(BF16)

    loc_a = np.arange(MIXER_A_TILE)
    loc = np.arange(ATTN_TQ)
    a_far = t5_bias[T5_BUCKETS // 2 - 1].astype(F32)
    a_bias = LOG2E * jnp.stack([
        _t5_bias_table(t5_bias, loc_a + MIXER_A_TILE, loc_a + MIXER_A_TILE, key_major=True),
        _t5_bias_table(t5_bias, loc_a + MIXER_A_TILE, loc_a, key_major=True),
        jnp.broadcast_to(a_far[:, None, None], (h_a, MIXER_A_TILE, MIXER_A_TILE))], axis=1)
    t5_log2 = LOG2E * t5_bias.astype(F32)
    a_brange = jnp.stack([LOG2E * a_far, jnp.max(t5_log2, axis=0), jnp.min(t5_log2, axis=0)], axis=1).reshape(-1)
    qpos_s = past + np.arange(ns)
    a_bias_sc = LOG2E * _t5_bias_table(t5_bias, qpos_s, np.arange(past))
    a_bias_sn = LOG2E * _t5_bias_table(t5_bias, qpos_s, qpos_s)
    kpos_sb = past - w_buf + np.arange(w_buf + ns)

    xp = x_prompt.reshape(bp * sp, d)
    xs = x_sample.reshape(bs * ns, d)
    mem = mem_prompt.reshape(bp * n_mem, d)
    outs = {k: [] for k in ("ak_p", "av_p", "bk_p", "bv_p", "ck_p", "cv_p", "mk_p", "mv_p",
                            "ak_s", "av_s", "bk_s", "bv_s", "ck_s", "cv_s")}
    flat = lambda dt: ((dt, 0),)
    dv_a = wa // h_a
    a_kv_prompt = ((F32, dv_a), (BF16, 0))
    qkv_tail = [(wb, False, True, flat(BF16)), (wb, False, False, flat(F32)), (wb, False, False, flat(F32)),
                (wc, True, True, flat(BF16)), (wc, True, True, flat(F32)), (wc, False, False, flat(F32))]
    qkv_segs_p = [(wa, True, True, flat(BF16)), (wa, True, True, a_kv_prompt),
                  (wa, False, False, a_kv_prompt)] + qkv_tail
    qkv_segs_s = [(wa, True, True, flat(BF16)), (wa, True, True, flat(F32)), (wa, False, False, flat(F32))] + qkv_tail

    for i in range(depth):
        wg1, wu1, wd1 = bf(ffn1_wg[i]), bf(ffn1_wu[i]), bf(ffn1_wd[i])
        wg2, wu2, wd2 = bf(ffn2_wg[i]), bf(ffn2_wu[i]), bf(ffn2_wd[i])
        w_qkv, w_gate = bf(w_in[i][:, :n_qkv]), bf(w_in[i][:, n_qkv:])
        qkv_gain = jnp.concatenate([
            tile(a_qnorm[i], wa // HEAD_DIM) * (scale * LOG2E), tile(a_knorm[i], wa // HEAD_DIM), jnp.ones((wa,), F32),
            jnp.full((wb,), scale, F32), jnp.ones((2 * wb,), F32),
            tile(c_qnorm[i], wc // HEAD_DIM) * scale, tile(c_knorm[i], wc // HEAD_DIM), jnp.ones((wc,), F32),
        ]).reshape(1, n_qkv)
        sub = (tile(a_subln[i], h_a) * (1.0 - _lambda_init(i))).reshape(1, wa)
        wbr = bf(w_br_a[i]), bf(w_br_b[i]), bf(w_br_c[i])
        wout = bf(w_out[i])
        wq, wo = bf(x_wq[i]), bf(x_wo[i])
        xq_gain = (tile(x_qnorm[i], wx // HEAD_DIM) * scale).reshape(1, wx)
        dot64 = lambda a, b: jnp.exp(jnp.sum(a.astype(F32) * b.astype(F32)))
        lam = (dot64(a_lq1[i], a_lk1[i]) - dot64(a_lq2[i], a_lk2[i]) + _lambda_init(i)).reshape(1)
        c_bias_p = _band_bias_table(c_rel_bias[i], C_PREV * CHUNK + loc, np.arange(BAND_W))
        c_bias_sc = _band_bias_table(c_rel_bias[i], qpos_s, kpos_sb[:w_buf])
        c_bias_sn = _band_bias_table(c_rel_bias[i], qpos_s, kpos_sb[w_buf:])

        xp = _ffn(xp, ffn1_norm[i], wg1, wu1, wd1, tm=1024, tf=1024)
        qa, ka, ka_bf, va, va_bf, qb, kb, vb, qc, kc, vc = _proj(xp, mix_norm[i], w_qkv, qkv_gain, qkv_segs_p, tm=1024)
        r3 = lambda a: a.reshape(bp, sp, a.shape[-1])
        oa = _attn_a(lam, a_brange, r3(qa), r3(ka_bf), r3(va_bf), a_bias)
        ob = _attn_b(r3(qb), r3(kb), r3(vb))
        oc = _attn_c(r3(qc), r3(kc), r3(vc), c_bias_p)
        xp = _merge(xp, oa.reshape(-1, wa), ob.reshape(-1, wb), oc.reshape(-1, wc), mix_norm[i], w_gate, sub,
                    *wbr, wout, tm=1024)
        mk, mv = _proj(mem, mem_norm[i], bf(x_wkv[i]),
                       jnp.concatenate([tile(x_knorm[i], wx // HEAD_DIM), jnp.ones((wx,), F32)]).reshape(1, 2 * wx),
                       [(wx, True, True, flat(F32)), (wx, False, False, flat(F32))], tm=n_mem)
        mk3, mv3 = mk.reshape(bp, n_mem, wx), mv.reshape(bp, n_mem, wx)
        xp = _cross(xp.reshape(bp, sp, d), x_norm[i], wq, xq_gain, bf(mk3), bf(mv3), wo, tm=1024).reshape(-1, d)
        xp = _ffn(xp, ffn2_norm[i], wg2, wu2, wd2, tm=1024, tf=1024)
        outs["ak_p"].append(ka.reshape(bp, sp, h_a, -1))
        outs["av_p"].append(va.reshape(bp, sp, h_a, -1))
        outs["bk_p"].append(kb.reshape(bp, sp, -1, HEAD_DIM))
        outs["bv_p"].append(vb.reshape(bp, sp, -1, HEAD_DIM))
        outs["ck_p"].append(r3(kc)[:, sp - w_keep:].reshape(bp, w_keep, -1, HEAD_DIM))
        outs["cv_p"].append(r3(vc)[:, sp - w_keep:].reshape(bp, w_keep, -1, HEAD_DIM))
        outs["mk_p"].append(mk3.reshape(bp, n_mem, -1, HEAD_DIM))
        outs["mv_p"].append(mv3.reshape(bp, n_mem, -1, HEAD_DIM))

        xs = _ffn(xs, ffn1_norm[i], wg1, wu1, wd1, tm=bs * ns)
        qa, ka, va, qb, kb, vb, qc, kc, vc = _proj(xs, mix_norm[i], w_qkv, qkv_gain, qkv_segs_s, tm=bs * ns)
        s3 = lambda a: a.reshape(bs, ns, a.shape[-1])
        c3 = lambda a: a.reshape(bs, a.shape[1], -1)
        oa, ob, oc = _sample_attn(
            i, lam, s3(qa), s3(ka), s3(va), cache_a_k, cache_a_v, a_bias_sc, a_bias_sn,
            s3(qb), s3(kb), s3(vb), c3(cache_b_k[i]), c3(cache_b_v[i]),
            s3(qc), s3(kc), s3(vc), c3(cache_c_k[i]), c3(cache_c_v[i]), c_bias_sc, c_bias_sn)
        xs = _merge(xs, oa.reshape(-1, wa), ob.reshape(-1, wb), oc.reshape(-1, wc), mix_norm[i], w_gate, sub,
                    *wbr, wout, tm=bs * ns)
        xs = _cross(xs.reshape(bs, ns, d), x_norm[i], wq, xq_gain, bf(c3(cache_mem_k[i])), bf(c3(cache_mem_v[i])),
                    wo, tm=ns).reshape(-1, d)
        xs = _ffn(xs, ffn2_norm[i], wg2, wu2, wd2, tm=bs * ns)
        outs["ak_s"].append(ka.reshape(bs, ns, h_a, -1))
        outs["av_s"].append(va.reshape(bs, ns, h_a, -1))
        outs["bk_s"].append(kb.reshape(bs, ns, -1, HEAD_DIM))
        outs["bv_s"].append(vb.reshape(bs, ns, -1, HEAD_DIM))
        outs["ck_s"].append(jnp.concatenate([cache_c_k[i], kc.reshape(bs, ns, -1, HEAD_DIM)], axis=1)[:, ns:])
        outs["cv_s"].append(jnp.concatenate([cache_c_v[i], vc.reshape(bs, ns, -1, HEAD_DIM)], axis=1)[:, ns:])

    st = lambda k: jnp.stack(outs[k])
    return (xp.reshape(bp, sp, d), xs.reshape(bs, ns, d),
            st("ak_p"), st("av_p"), st("bk_p"), st("bv_p"), st("ck_p"), st("cv_p"), st("mk_p"), st("mv_p"),
            st("ak_s"), st("av_s"), st("bk_s"), st("bv_s"), st("ck_s"), st("cv_s"))
```

```python
import functools
import math

import numpy as np
import jax
import jax.numpy as jnp
from jax import lax
from jax.experimental import pallas as pl
from jax.experimental.pallas import tpu as pltpu

F32 = jnp.float32
BF16 = jnp.bfloat16

EPS = 1e-6
HEAD_DIM = 64
CHUNK = 64
C_PREV = 8
REL_CLIP = 128
T5_BUCKETS = 32
T5_MAX_DIST = 128
LANES = 128
BF16_SUBLANES = 16
V7X_MXU_DIM = 256
NEG = -1e30
LOG2E = math.log2(math.e)
BOUND_SLACK = 1.001
BOUNDED_EXP2_SPAN = 100.0
STICK_SKIP = -110.0
V7X_VMEM_LIMIT_BYTES = 56 * 1024 * 1024

MIXER_A_TILE = 512
ATTN_TQ = 256
ATTN_TK = 256
BAND_W = (C_PREV + 4) * CHUNK


def _cparams(sem):
    return pltpu.CompilerParams(dimension_semantics=sem, vmem_limit_bytes=V7X_VMEM_LIMIT_BYTES)


def _rms_rows(x, g):
    return x * lax.rsqrt(jnp.mean(x * x, axis=-1, keepdims=True) + EPS) * g


def _dot(a, b):
    return jnp.dot(a, b, preferred_element_type=F32)


def _dot_nt(a, b):
    return lax.dot_general(a, b, (((1,), (1,)), ((), ())), preferred_element_type=F32)


def _keep_lanes(q, lo, hi):
    lane = lax.broadcasted_iota(jnp.int32, q.shape, 1)
    return jnp.where((lane >= lo) & (lane < hi), q.astype(F32), 0.0).astype(BF16)


def _group_rms(y, gmat):
    wide = gmat.shape[0]
    parts = []
    for c in range(y.shape[1] // wide):
        yc = y[:, c * wide:(c + 1) * wide]
        ms = _dot((yc * yc).astype(BF16), gmat)
        parts.append(yc * lax.rsqrt(ms + EPS))
    return parts[0] if len(parts) == 1 else jnp.concatenate(parts, axis=1)


def _group_mean_matrix():
    g = np.kron(np.eye(V7X_MXU_DIM // HEAD_DIM), np.ones((HEAD_DIM, HEAD_DIM))) / HEAD_DIM
    return jnp.asarray(g, BF16)


def _ffn_kernel(x_ref, g_ref, wg_ref, wu_ref, wd_ref, o_ref, h_scr):
    j = pl.program_id(1)

    @pl.when(j == 0)
    def _():
        x = x_ref[...]
        h_scr[...] = _rms_rows(x, g_ref[...]).astype(BF16)
        o_ref[...] = x

    h = h_scr[...]
    a = _dot(h, wg_ref[...])
    u = _dot(h, wu_ref[...])
    t = a * jax.nn.sigmoid(a) * u
    o_ref[...] += 0.5 * _dot(t.astype(BF16), wd_ref[...])


def _ffn(x, g, wg, wu, wd, layer, tm, tf=512):
    n, d = x.shape
    dff = wg.shape[2]
    return pl.pallas_call(
        _ffn_kernel,
        grid=(n // tm, dff // tf),
        in_specs=[
            pl.BlockSpec((tm, d), lambda i, j: (i, 0)),
            pl.BlockSpec((1, d), lambda i, j: (0, 0)),
            pl.BlockSpec((None, d, tf), lambda i, j: (layer, 0, j)),
            pl.BlockSpec((None, d, tf), lambda i, j: (layer, 0, j)),
            pl.BlockSpec((None, tf, d), lambda i, j: (layer, j, 0)),
        ],
        out_specs=pl.BlockSpec((tm, d), lambda i, j: (i, 0)),
        out_shape=jax.ShapeDtypeStruct((n, d), F32),
        scratch_shapes=[pltpu.VMEM((tm, d), BF16)],
        compiler_params=_cparams(("parallel", "arbitrary")),
        name="ffn",
    )(x, g.reshape(1, d), wg, wu, wd)


def _proj_kernel(x_ref, g_ref, w_ref, gain_ref, gmat_ref, *out_refs, segs):
    h = _rms_rows(x_ref[...], g_ref[...]).astype(BF16)
    off = 0
    out_refs = list(out_refs)
    for width, normed, gained, outs in segs:
        y = _dot(h, w_ref[:, off:off + width])
        if normed:
            y = _group_rms(y, gmat_ref[...])
        if gained:
            y = y * gain_ref[:, off:off + width]
        for _, head_width in outs:
            o_ref = out_refs.pop(0)
            if head_width:
                for hd in range(width // head_width):
                    o_ref[:, hd, :] = y[:, hd * head_width:(hd + 1) * head_width].astype(o_ref.dtype)
            else:
                o_ref[...] = y.astype(o_ref.dtype)
        off += width


def _proj(x, g, w, layer, gain, segs, tm):
    n, d = x.shape
    wtot = w.shape[2]
    specs, shapes = [], []
    for width, _, _, outs in segs:
        for dt, head_width in outs:
            if head_width:
                nh = width // head_width
                specs.append(pl.BlockSpec((tm, nh, head_width), lambda i: (i, 0, 0)))
                shapes.append(jax.ShapeDtypeStruct((n, nh, head_width), dt))
            else:
                specs.append(pl.BlockSpec((tm, width), lambda i: (i, 0)))
                shapes.append(jax.ShapeDtypeStruct((n, width), dt))
    return pl.pallas_call(
        functools.partial(_proj_kernel, segs=tuple(segs)),
        grid=(n // tm,),
        in_specs=[
            pl.BlockSpec((tm, d), lambda i: (i, 0)),
            pl.BlockSpec((1, d), lambda i: (0, 0)),
            pl.BlockSpec((None, d, wtot), lambda i: (layer, 0, 0), pipeline_mode=pl.Buffered(1)),
            pl.BlockSpec((1, wtot), lambda i: (0, 0)),
            pl.BlockSpec((V7X_MXU_DIM, V7X_MXU_DIM), lambda i: (0, 0)),
        ],
        out_specs=specs,
        out_shape=shapes,
        compiler_params=_cparams(("parallel",)),
        name="proj",
    )(x, g.reshape(1, d), w, gain, _group_mean_matrix())


def _softmax_step_km(s, vt, m_ref, acc_ref):
    m_old = m_ref[...]
    m_new = jnp.maximum(m_old, jnp.max(s, axis=0, keepdims=True))
    alpha = jnp.exp2(m_old - m_new)
    p = jnp.exp2(s - m_new)
    acc_ref[...] = alpha * acc_ref[...] + _dot(vt, p.astype(BF16))
    m_ref[...] = m_new


def _cast_rows(src_ref, dst_ref, rows, dst_off=0, step=512):
    def body(i, c):
        r = pl.multiple_of(i * step, step)
        dst_ref[pl.ds(dst_off + r, step), :] = src_ref[0, pl.ds(r, step), :].astype(BF16)
        return c
    lax.fori_loop(0, rows // step, body, 0)


def _attn_a_kernel(lam_ref, brange_ref, q_ref, k_ref, v_ref, bias_ref, o_ref,
                   vtb, s_scr, knorm, m1, a1, m2, a2):
    h = pl.program_id(1)
    qi = pl.program_id(2)
    kbf = k_ref.at[0]
    seq = kbf.shape[0]
    tq = q_ref.shape[1]
    tk = vtb.shape[2]
    dv = v_ref.shape[2]

    @pl.when(qi == 0)
    def _():
        lane = lax.broadcasted_iota(jnp.int32, (tk, LANES), 1)

        def prep(j, c):
            r = pl.multiple_of(j * tk, tk)
            kb = kbf[pl.ds(r, tk), :]
            vtb[j, 0:dv, :] = v_ref[0, pl.ds(r, tk), :].astype(F32).T.astype(BF16)
            vtb[j, dv:, :] = jnp.ones((vtb.shape[1] - dv, tk), BF16)
            ksq = kb.astype(F32) * kb.astype(F32)
            n1 = jnp.max(jnp.sum(jnp.where(lane < HEAD_DIM, ksq, 0.0), axis=1, keepdims=True))
            n2 = jnp.max(jnp.sum(jnp.where(lane >= HEAD_DIM, ksq, 0.0), axis=1, keepdims=True))
            return jnp.maximum(c[0], n1), jnp.maximum(c[1], n2)
        n1, n2 = lax.fori_loop(0, seq // tk, prep, (jnp.float32(0.0), jnp.float32(0.0)))
        knorm[0] = n1
        knorm[1] = n2

    qt = q_ref[0].astype(F32).T
    sub = lax.broadcasted_iota(jnp.int32, qt.shape, 0)
    q1t = jnp.where(sub < HEAD_DIM, qt, 0.0).astype(BF16)
    q2t = jnp.where(sub >= HEAD_DIM, qt, 0.0).astype(BF16)
    states = ((m1, a1), (m2, a2))
    for m_ref, a_ref in states:
        a_ref[...] = jnp.zeros_like(a_ref)

    far_bias, bias_max, bias_min = brange_ref[3 * h], brange_ref[3 * h + 1], brange_ref[3 * h + 2]
    qsq = qt * qt
    bound1 = jnp.sqrt(jnp.sum(jnp.where(sub < HEAD_DIM, qsq, 0.0), axis=0, keepdims=True) * knorm[0])
    bound2 = jnp.sqrt(jnp.sum(jnp.where(sub >= HEAD_DIM, qsq, 0.0), axis=0, keepdims=True) * knorm[1])
    bounds = (bound1 * BOUND_SLACK + bias_max, bound2 * BOUND_SLACK + bias_max)
    spread = 2.0 * BOUND_SLACK * jnp.maximum(jnp.max(bound1), jnp.max(bound2)) + (bias_max - bias_min)
    bounded = spread <= BOUNDED_EXP2_SPAN

    @pl.when(bounded)
    def _():
        qmaps = ((q1t, bounds[0], a1), (q2t, bounds[1], a2))

        def accumulate(blocks):
            kbs = [kbf[pl.ds(pl.multiple_of(j * tk, tk), tk), :] for j, _ in blocks]
            ps = []
            for (j, tile), kb in zip(blocks, kbs):
                for qmt, shift, _ in qmaps:
                    add = (far_bias - shift) if tile is None else (tile - shift)
                    ps.append(jnp.exp2(_dot(kb, qmt) + add).astype(BF16))
            for mi, (_, _, a_ref) in enumerate(qmaps):
                total = None
                for bi, (j, _) in enumerate(blocks):
                    term = _dot(vtb[j], ps[bi * len(qmaps) + mi])
                    total = term if total is None else total + term
                a_ref[...] += total

        @pl.when(qi == 0)
        def _():
            accumulate([(qi, bias_ref[0, 0])])

        @pl.when(qi >= 1)
        def _():
            accumulate([(qi, bias_ref[0, 0]), (qi - 1, bias_ref[0, 1])])

        n_far = jnp.maximum(qi - 1, 0)
        one = n_far & 1
        two = n_far & 2

        @pl.when(one == 1)
        def _():
            accumulate([(0, None)])

        @pl.when(two == 2)
        def _():
            accumulate([(one, None), (one + 1, None)])

        def quad(i, c):
            j = one + two + 4 * i
            accumulate([(j, None), (j + 1, None), (j + 2, None), (j + 3, None)])
            return c
        lax.fori_loop(0, lax.shift_right_logical(n_far, 2), quad, 0)

    @pl.when(jnp.logical_not(bounded))
    def _():
        for m_ref, _ in states:
            m_ref[...] = jnp.full_like(m_ref, NEG)
        n_blocks = qi + 1

        def scores(t, buf):
            kb = kbf[pl.ds(pl.multiple_of((qi - t) * tk, tk), tk), :]
            s_scr[buf, 0] = _dot(kb, q1t)
            s_scr[buf, 1] = _dot(kb, q2t)

        def update(t, buf):
            bias = bias_ref[0, jnp.minimum(t, bias_ref.shape[1] - 1)]
            vt = vtb[qi - t]
            for mi, (m_ref, a_ref) in enumerate(states):
                _softmax_step_km(s_scr[buf, mi] + bias, vt, m_ref, a_ref)

        odd = n_blocks & 1

        @pl.when(odd == 1)
        def _():
            scores(0, 0)
            update(0, 0)

        n_pairs = lax.shift_right_logical(n_blocks, 1)

        @pl.when(n_pairs > 0)
        def _():
            scores(odd, 0)

        def pair(p, c):
            t0 = odd + 2 * p
            scores(t0 + 1, 1)
            update(t0, 0)
            scores(jnp.minimum(t0 + 2, qi), 0)
            update(t0 + 1, 1)
            return c
        lax.fori_loop(0, n_pairs, pair, 0)

    out_t = (a1[0:dv, :] / a1[dv:dv + 1, :]
             - lam_ref[0] * (a2[0:dv, :] / a2[dv:dv + 1, :]))
    o_ref[0] = out_t.T


def _attn_a(lam, brange, qa, ka, va, bias):
    b, s, w = qa.shape
    nh = w // LANES
    tq = tk = MIXER_A_TILE
    vrows = LANES + BF16_SUBLANES
    return pl.pallas_call(
        _attn_a_kernel,
        grid=(b, nh, s // tq),
        in_specs=[
            pl.BlockSpec(memory_space=pltpu.SMEM),
            pl.BlockSpec(memory_space=pltpu.SMEM),
            pl.BlockSpec((1, tq, LANES), lambda bi, h, qi: (bi, qi, h)),
            pl.BlockSpec((1, s, LANES), lambda bi, h, qi: (bi, 0, h)),
            pl.BlockSpec((1, s, LANES), lambda bi, h, qi: (bi, 0, h)),
            pl.BlockSpec((1,) + bias.shape[1:], lambda bi, h, qi: (h, 0, 0, 0)),
        ],
        out_specs=pl.BlockSpec((1, tq, LANES), lambda bi, h, qi: (bi, qi, h)),
        out_shape=jax.ShapeDtypeStruct((b, s, w), F32),
        scratch_shapes=[
            pltpu.VMEM((s // tk, vrows, tk), BF16),
            pltpu.VMEM((2, 2, tk, tq), F32), pltpu.SMEM((2,), F32),
            pltpu.VMEM((1, tq), F32), pltpu.VMEM((vrows, tq), F32),
            pltpu.VMEM((1, tq), F32), pltpu.VMEM((vrows, tq), F32),
        ],
        compiler_params=_cparams(("parallel", "parallel", "arbitrary")),
        name="mixer_a",
    )(lam, brange, qa, ka, va, bias)


def _stick_block(z, valid, umat, carry):
    sp = jnp.maximum(z, 0.0) + jnp.log(1.0 + jnp.exp(-jnp.abs(z)))
    log1m = -sp
    if valid is not None:
        log1m = jnp.where(valid, log1m, 0.0)
    hi = log1m.astype(BF16)
    lo = (log1m - hi.astype(F32)).astype(BF16)
    after = _dot(hi, umat) + _dot(lo, umat)
    w = jnp.exp(z - sp + after + carry)
    if valid is not None:
        w = jnp.where(valid, w, 0.0)
    return w, jnp.sum(log1m, axis=1, keepdims=True)


def _stick_block_km(z, valid, umat, carry):
    sp = jnp.maximum(z, 0.0) + jnp.log(1.0 + jnp.exp(-jnp.abs(z)))
    log1m = -sp
    if valid is not None:
        log1m = jnp.where(valid, log1m, 0.0)
    hi = log1m.astype(BF16)
    lo = (log1m - hi.astype(F32)).astype(BF16)
    after = _dot(umat, hi) + _dot(umat, lo)
    w = jnp.exp(z - sp + after + carry)
    if valid is not None:
        w = jnp.where(valid, w, 0.0)
    return w, jnp.sum(log1m, axis=0, keepdims=True)


def _attn_b_kernel(q_ref, k_ref, v_ref, u_ref, o_ref, kbf, vtb, c_scr, acc_scr, z_scr):
    step = pl.program_id(2)
    seq = kbf.shape[0]
    tk = vtb.shape[2]
    n_halves = q_ref.shape[1] // tk
    n_heads = LANES // HEAD_DIM

    @pl.when(step == 0)
    def _():
        def prep(j, c):
            r = pl.multiple_of(j * tk, tk)
            kbf[pl.ds(r, tk), :] = k_ref[0, pl.ds(r, tk), :].astype(BF16)
            vtb[j] = v_ref[0, pl.ds(r, tk), :].T.astype(BF16)
            return c
        lax.fori_loop(0, seq // tk, prep, 0)

    sub = lax.broadcasted_iota(jnp.int32, (LANES, tk), 0)
    qts = []
    for half in range(n_halves):
        qt = q_ref[0, half * tk:(half + 1) * tk, :].astype(F32).T
        qts.append([jnp.where((sub >= hh * HEAD_DIM) & (sub < (hh + 1) * HEAD_DIM), qt, 0.0).astype(BF16)
                    for hh in range(n_heads)])
    krow = lax.broadcasted_iota(jnp.int32, (tk, tk), 0)
    qcol = lax.broadcasted_iota(jnp.int32, (tk, tk), 1)
    umat = u_ref[...]

    def blocks(half, js, valids, first):
        kbs = [kbf[pl.ds(pl.multiple_of(j * tk, tk), tk), :] for j in js]
        cmax = None
        for hh in range(n_heads):
            carry = 0.0 if first else c_scr[half, hh]
            pv = None
            for j, kb, valid in zip(js, kbs, valids):
                w, cs = _stick_block_km(_dot(kb, qts[half][hh]), valid, umat, carry)
                term = _dot(vtb[j], w.astype(BF16))
                pv = term if pv is None else pv + term
                carry = carry + cs
            acc_scr[half, hh] = pv if first else acc_scr[half, hh] + pv
            c_scr[half, hh] = carry
            cm = jnp.max(carry)
            cmax = cm if cmax is None else jnp.maximum(cmax, cm)
        return cmax

    own_valid = krow < qcol
    everywhere = krow >= 0

    chains = []
    for half in range(n_halves):
        qb = step * n_halves + half
        prev_valid = None if half >= 1 else everywhere & (qb >= 1)
        for hh in range(n_heads):
            for j, valid in ((qb, own_valid), (jnp.maximum(qb - 1, 0), prev_valid)):
                chains.append((half, hh, j, valid))
    for c, (half, hh, j, valid) in enumerate(chains):
        z_scr[c] = _dot(kbf[pl.ds(pl.multiple_of(j * tk, tk), tk), :], qts[half][hh])
    col_sums = []
    for c, (half, hh, j, valid) in enumerate(chains):
        z = z_scr[c]
        sp = jnp.maximum(z, 0.0) + jnp.log(1.0 + jnp.exp(-jnp.abs(z)))
        log1m = -sp if valid is None else jnp.where(valid, -sp, 0.0)
        hi = log1m.astype(BF16)
        lo = (log1m - hi.astype(F32)).astype(BF16)
        z_scr[c] = z - sp + _dot(umat, hi) + _dot(umat, lo)
        col_sums.append(jnp.sum(log1m, axis=0, keepdims=True))
    cmaxes = []
    for half in range(n_halves):
        cmax = None
        for hh in range(n_heads):
            carry, pv = 0.0, None
            for c, (ch, chh, j, valid) in enumerate(chains):
                if (ch, chh) != (half, hh):
                    continue
                w = jnp.exp(z_scr[c] + carry)
                if valid is not None:
                    w = jnp.where(valid, w, 0.0)
                term = _dot(vtb[j], w.astype(BF16))
                pv = term if pv is None else pv + term
                carry = carry + col_sums[c]
            acc_scr[half, hh] = pv
            c_scr[half, hh] = carry
            cm = jnp.max(carry)
            cmax = cm if cmax is None else jnp.maximum(cmax, cm)
        cmaxes.append(cmax)

    for half in range(n_halves):
        qb = step * n_halves + half

        def cond(st):
            j, cmax = st
            return (j >= 0) & (cmax > STICK_SKIP)

        def body(st, half=half):
            j, _ = st
            return j - 2, blocks(half, [j, jnp.maximum(j - 1, 0)], [None, everywhere & (j >= 1)], False)

        lax.while_loop(cond, body, (qb - 2, cmaxes[half]))
        out_t = jnp.where(sub < HEAD_DIM, acc_scr[half, 0], acc_scr[half, 1])
        o_ref[0, half * tk:(half + 1) * tk, :] = out_t.T.astype(o_ref.dtype)


def _strict_lower(n):
    return jnp.asarray(np.tril(np.ones((n, n)), -1), BF16)


def _strict_upper(n):
    return jnp.asarray(np.triu(np.ones((n, n)), 1), BF16)


def _attn_b(qb, kb, vb):
    b, s, w = qb.shape
    n_halves = 4
    tq = n_halves * ATTN_TK
    return pl.pallas_call(
        _attn_b_kernel,
        grid=(b, w // LANES, s // tq),
        in_specs=[
            pl.BlockSpec((1, tq, LANES), lambda bi, h, qi: (bi, qi, h)),
            pl.BlockSpec((1, s, LANES), lambda bi, h, qi: (bi, 0, h)),
            pl.BlockSpec((1, s, LANES), lambda bi, h, qi: (bi, 0, h)),
            pl.BlockSpec((ATTN_TK, ATTN_TK), lambda bi, h, qi: (0, 0)),
        ],
        out_specs=pl.BlockSpec((1, tq, LANES), lambda bi, h, qi: (bi, qi, h)),
        out_shape=jax.ShapeDtypeStruct((b, s, w), BF16),
        scratch_shapes=[
            pltpu.VMEM((s, LANES), BF16), pltpu.VMEM((s // ATTN_TK, LANES, ATTN_TK), BF16),
            pltpu.VMEM((n_halves, LANES // HEAD_DIM, 1, ATTN_TK), F32),
            pltpu.VMEM((n_halves, LANES // HEAD_DIM, LANES, ATTN_TK), F32),
            pltpu.VMEM((2 * n_halves * (LANES // HEAD_DIM), ATTN_TK, ATTN_TK), F32),
        ],
        compiler_params=_cparams(("parallel", "parallel", "arbitrary")),
        name="mixer_b",
    )(qb, kb, vb, _strict_upper(ATTN_TK))


def _attn_c_kernel(q_ref, k_ref, v_ref, bias_ref, o_ref, kbf, vbf, s_scr):
    step = pl.program_id(2)
    seq = k_ref.shape[1]
    tq = ATTN_TQ
    n_blocks = q_ref.shape[1] // tq
    pad = C_PREV * CHUNK
    n_heads = LANES // HEAD_DIM

    @pl.when(step == 0)
    def _():
        kbf[0:pad, :] = jnp.zeros((pad, LANES), BF16)
        vbf[0:pad, 0:LANES] = jnp.zeros((pad, LANES), BF16)
        vbf[:, LANES:] = jnp.ones((vbf.shape[0], vbf.shape[1] - LANES), BF16)
        _cast_rows(k_ref, kbf, seq, dst_off=pad)

        def vcast(i, c):
            r = pl.multiple_of(i * tq, tq)
            vbf[pl.ds(pad + r, tq), 0:LANES] = v_ref[0, pl.ds(r, tq), :].astype(BF16)
            return c
        lax.fori_loop(0, seq // tq, vcast, 0)

    lane = lax.broadcasted_iota(jnp.int32, (tq, LANES), 1)
    col = lax.broadcasted_iota(jnp.int32, (tq, BAND_W), 1)
    for blk in range(n_blocks):
        qb = step * n_blocks + blk
        q = q_ref[0, blk * tq:(blk + 1) * tq, :]
        kw = kbf[pl.ds(pl.multiple_of(qb * tq, tq), BAND_W), :]
        in_seq = col >= pad - qb * tq
        for hh in range(n_heads):
            qm = _keep_lanes(q, hh * HEAD_DIM, (hh + 1) * HEAD_DIM)
            s_scr[blk * n_heads + hh] = jnp.where(in_seq, _dot_nt(qm, kw) + bias_ref[hh], NEG)
    for blk in range(n_blocks):
        qb = step * n_blocks + blk
        vw = vbf[pl.ds(pl.multiple_of(qb * tq, tq), BAND_W), :]
        outs = []
        for hh in range(n_heads):
            s = s_scr[blk * n_heads + hh]
            p = jnp.exp(s - jnp.max(s, axis=1, keepdims=True))
            acc = _dot(p.astype(BF16), vw)
            outs.append(acc[:, 0:LANES] / acc[:, LANES:])
        o_ref[0, blk * tq:(blk + 1) * tq, :] = jnp.where(lane < HEAD_DIM, outs[0], outs[1]).astype(o_ref.dtype)


def _attn_c(qc, kc, vc, bias):
    b, s, w = qc.shape
    n_blocks = 4
    tq = n_blocks * ATTN_TQ
    pad = C_PREV * CHUNK
    hp = LANES // HEAD_DIM
    return pl.pallas_call(
        _attn_c_kernel,
        grid=(b, w // LANES, s // tq),
        in_specs=[
            pl.BlockSpec((1, tq, LANES), lambda bi, h, qi: (bi, qi, h)),
            pl.BlockSpec((1, s, LANES), lambda bi, h, qi: (bi, 0, h)),
            pl.BlockSpec((1, s, LANES), lambda bi, h, qi: (bi, 0, h)),
            pl.BlockSpec((hp, ATTN_TQ, BAND_W), lambda bi, h, qi: (h, 0, 0)),
        ],
        out_specs=pl.BlockSpec((1, tq, LANES), lambda bi, h, qi: (bi, qi, h)),
        out_shape=jax.ShapeDtypeStruct((b, s, w), BF16),
        scratch_shapes=[pltpu.VMEM((s + pad, LANES), BF16), pltpu.VMEM((s + pad, 2 * LANES), BF16),
                        pltpu.VMEM((n_blocks * hp, ATTN_TQ, BAND_W), F32)],
        compiler_params=_cparams(("parallel", "parallel", "arbitrary")),
        name="mixer_c",
    )(qc, kc, vc, bias)


def _merge_kernel(x_ref, oa_ref, ob_ref, oc_ref, g_ref, wgate_ref, sub_ref,
                  wa_ref, wb_ref, wc_ref, wout_ref, o_ref):
    x = x_ref[...]
    d = x.shape[1]
    h = _rms_rows(x, g_ref[...]).astype(BF16)
    oa = oa_ref[...]
    parts = []
    for c in range(oa.shape[1] // LANES):
        oc_ = oa[:, c * LANES:(c + 1) * LANES]
        parts.append(oc_ * lax.rsqrt(jnp.mean(oc_ * oc_, axis=-1, keepdims=True) + EPS))
    oan = (jnp.concatenate(parts, axis=1) * sub_ref[...]).astype(BF16)
    merged = jax.nn.sigmoid(_dot(h, wgate_ref[:, 0:d])) * _dot(oan, wa_ref[...])
    merged += jax.nn.sigmoid(_dot(h, wgate_ref[:, d:2 * d])) * _dot(ob_ref[...], wb_ref[...])
    merged += jax.nn.sigmoid(_dot(h, wgate_ref[:, 2 * d:3 * d])) * _dot(oc_ref[...], wc_ref[...])
    o_ref[...] = x + _dot(merged.astype(BF16), wout_ref[...])


def _merge(x, oa, ob, oc, g, wgate, sub, wa, wb, wc, wout, layer, tm):
    n, d = x.shape
    full = lambda a: pl.BlockSpec(a.shape, lambda i: (0,) * a.ndim, pipeline_mode=pl.Buffered(1))
    of_layer = lambda a: pl.BlockSpec((None,) + a.shape[1:], lambda i: (layer,) + (0,) * (a.ndim - 1),
                                      pipeline_mode=pl.Buffered(1))
    rows = lambda a: pl.BlockSpec((tm, a.shape[1]), lambda i: (i, 0))
    g = g.reshape(1, d)
    return pl.pallas_call(
        _merge_kernel,
        grid=(n // tm,),
        in_specs=[rows(x), rows(oa), rows(ob), rows(oc), full(g), of_layer(wgate), full(sub),
                  of_layer(wa), of_layer(wb), of_layer(wc), of_layer(wout)],
        out_specs=rows(x),
        out_shape=jax.ShapeDtypeStruct((n, d), F32),
        compiler_params=_cparams(("parallel",)),
        name="merge",
    )(x, oa, ob, oc, g, wgate, sub, wa, wb, wc, wout)


def _cross_kernel(x_ref, g_ref, wq_ref, gain_ref, gmat_ref, mk_ref, mv_ref, wo_ref, o_ref):
    x = x_ref[0]
    h = _rms_rows(x, g_ref[...]).astype(BF16)
    q = (_group_rms(_dot(h, wq_ref[...]), gmat_ref[...]) * gain_ref[...]).astype(BF16)
    mk = mk_ref[0]
    mv = mv_ref[0]
    lane = lax.broadcasted_iota(jnp.int32, q.shape, 1)
    o = jnp.zeros(q.shape, F32)
    for hh in range(q.shape[1] // HEAD_DIM):
        in_head = (lane >= hh * HEAD_DIM) & (lane < (hh + 1) * HEAD_DIM)
        s = _dot_nt(_keep_lanes(q, hh * HEAD_DIM, (hh + 1) * HEAD_DIM), mk)
        p = jnp.exp(s - jnp.max(s, axis=1, keepdims=True))
        l = jnp.sum(p, axis=1, keepdims=True)
        o = jnp.where(in_head, _dot(p.astype(BF16), mv) / l, o)
    o_ref[0] = x + _dot(o.astype(BF16), wo_ref[...])


def _cross(x, g, wq, gain, mk, mv, wo, layer, tm):
    b, s, d = x.shape
    full = lambda a: pl.BlockSpec(a.shape, lambda bi, i: (0,) * a.ndim)
    of_layer = lambda a: pl.BlockSpec((None,) + a.shape[1:], lambda bi, i: (layer,) + (0,) * (a.ndim - 1))
    g = g.reshape(1, d)
    gmat = _group_mean_matrix()
    return pl.pallas_call(
        _cross_kernel,
        grid=(b, s // tm),
        in_specs=[
            pl.BlockSpec((1, tm, d), lambda bi, i: (bi, i, 0)),
            full(g), of_layer(wq), full(gain), full(gmat),
            pl.BlockSpec((1,) + mk.shape[1:], lambda bi, i: (bi, 0, 0)),
            pl.BlockSpec((1,) + mv.shape[1:], lambda bi, i: (bi, 0, 0)),
            of_layer(wo),
        ],
        out_specs=pl.BlockSpec((1, tm, d), lambda bi, i: (bi, i, 0)),
        out_shape=jax.ShapeDtypeStruct((b, s, d), F32),
        compiler_params=_cparams(("parallel", "parallel")),
        name="cross",
    )(x, g, wq, gain, gmat, mk, mv, wo)


def _sample_attn_kernel(lam_ref,
                        qa_ref, kan_ref, van_ref, kac_ref, vac_ref, bac_ref, ban_ref,
                        qb_ref, kbn_ref, vbn_ref, kbc_ref, vbc_ref, ubig_ref, usmall_ref,
                        qc_ref, kcn_ref, vcn_ref, kcc_ref, vcc_ref, bcc_ref, bcn_ref,
                        oa_ref, ob_ref, oc_ref):
    lam = lam_ref[0]
    ns = qa_ref.shape[1]
    bf = lambda r: r[0].astype(BF16)

    def heads(q, width):
        for hh in range(LANES // width):
            yield hh, None, _keep_lanes(q, hh * width, (hh + 1) * width)

    def softmax2(s_c, s_n, exp=jnp.exp):
        m = jnp.maximum(jnp.max(s_c, axis=1, keepdims=True), jnp.max(s_n, axis=1, keepdims=True))
        p_c = exp(s_c - m)
        p_n = exp(s_n - m)
        inv = 1.0 / (jnp.sum(p_c, axis=1, keepdims=True) + jnp.sum(p_n, axis=1, keepdims=True))
        return p_c * inv, p_n * inv

    for h in range(qa_ref.shape[2] // LANES):
        sl = slice(h * LANES, (h + 1) * LANES)
        q = qa_ref[0, :, sl]
        k_c = kac_ref[0, :, h, :].astype(BF16)
        k_n = kan_ref[0, :, sl].astype(BF16)
        maps = []
        for _, _, qm in heads(q, HEAD_DIM):
            maps.append(softmax2(_dot_nt(qm, k_c) + bac_ref[h], _dot_nt(qm, k_n) + ban_ref[h], exp=jnp.exp2))
        a_c = (maps[0][0] - lam * maps[1][0]).astype(BF16)
        a_n = (maps[0][1] - lam * maps[1][1]).astype(BF16)
        oa_ref[0, :, sl] = (_dot(a_c, vac_ref[0, :, h, :].astype(BF16))
                            + _dot(a_n, van_ref[0, :, sl].astype(BF16)))

    past = kbc_ref.shape[1]
    tk = ubig_ref.shape[0]
    row = lax.broadcasted_iota(jnp.int32, (ns, ns), 0)
    col = lax.broadcasted_iota(jnp.int32, (ns, ns), 1)
    for pr in range(qb_ref.shape[2] // LANES):
        sl = slice(pr * LANES, (pr + 1) * LANES)
        q = qb_ref[0, :, sl]
        k_n = kbn_ref[0, :, sl].astype(BF16)
        v_n = vbn_ref[0, :, sl].astype(BF16)
        outs = []
        for _, _, qm in heads(q, HEAD_DIM):
            w, carry = _stick_block(_dot_nt(qm, k_n), col < row, usmall_ref[...], 0.0)
            acc = _dot(w.astype(BF16), v_n)
            for j in range(past // tk - 1, -1, -1):
                k_c = kbc_ref[0, j * tk:(j + 1) * tk, sl].astype(BF16)
                v_c = vbc_ref[0, j * tk:(j + 1) * tk, sl].astype(BF16)
                w, rs = _stick_block(_dot_nt(qm, k_c), None, ubig_ref[...], carry)
                acc += _dot(w.astype(BF16), v_c)
                carry = carry + rs
            outs.append(acc)
        lane = lax.broadcasted_iota(jnp.int32, q.shape, 1)
        ob_ref[0, :, sl] = jnp.where(lane < HEAD_DIM, outs[0], outs[1]).astype(ob_ref.dtype)

    for pr in range(qc_ref.shape[2] // LANES):
        sl = slice(pr * LANES, (pr + 1) * LANES)
        q = qc_ref[0, :, sl]
        k_c = kcc_ref[0, :, sl].astype(BF16)
        k_n = kcn_ref[0, :, sl].astype(BF16)
        v_c = vcc_ref[0, :, sl].astype(BF16)
        v_n = vcn_ref[0, :, sl].astype(BF16)
        outs = []
        for hh, _, qm in heads(q, HEAD_DIM):
            hd = pr * (LANES // HEAD_DIM) + hh
            p_c, p_n = softmax2(_dot_nt(qm, k_c) + bcc_ref[hd], _dot_nt(qm, k_n) + bcn_ref[hd])
            outs.append(_dot(p_c.astype(BF16), v_c) + _dot(p_n.astype(BF16), v_n))
        lane = lax.broadcasted_iota(jnp.int32, q.shape, 1)
        oc_ref[0, :, sl] = jnp.where(lane < HEAD_DIM, outs[0], outs[1]).astype(oc_ref.dtype)


def _sample_attn(layer, lam, qa, ka, va, cak, cav, bac, ban, qb, kb, vb, cbk, cbv,
                 qc, kc, vc, cck, ccv, bcc, bcn):
    b, ns, _ = qa.shape
    per_b = lambda a: pl.BlockSpec((1,) + a.shape[1:], lambda bi: (bi,) + (0,) * (a.ndim - 1))
    full = lambda a: pl.BlockSpec(a.shape, lambda bi: (0,) * a.ndim)
    layer_b = lambda a: pl.BlockSpec((None, 1) + a.shape[2:], lambda bi: (layer, bi) + (0,) * (a.ndim - 2))
    ubig = _strict_lower(ATTN_TK)
    usmall = _strict_lower(ns)
    args = [qa, ka, va, cak, cav, bac, ban, qb, kb, vb, cbk, cbv, ubig, usmall,
            qc, kc, vc, cck, ccv, bcc, bcn]
    specs = [per_b(qa), per_b(ka), per_b(va), layer_b(cak), layer_b(cav), full(bac), full(ban),
             per_b(qb), per_b(kb), per_b(vb), per_b(cbk), per_b(cbv), full(ubig), full(usmall),
             per_b(qc), per_b(kc), per_b(vc), per_b(cck), per_b(ccv), full(bcc), full(bcn)]
    return pl.pallas_call(
        _sample_attn_kernel,
        grid=(b,),
        in_specs=[pl.BlockSpec(memory_space=pltpu.SMEM)] + specs,
        out_specs=[per_b(qa), per_b(qb), per_b(qc)],
        out_shape=[jax.ShapeDtypeStruct(qa.shape, F32), jax.ShapeDtypeStruct(qb.shape, BF16),
                   jax.ShapeDtypeStruct(qc.shape, BF16)],
        compiler_params=_cparams(("parallel",)),
        name="sample_mixers",
    )(lam, *args)


def _t5_bucket_np(rel):
    half = T5_BUCKETS // 2
    max_exact = half // 2
    n = np.abs(rel)
    nf = np.maximum(n, 1).astype(np.float64)
    large = max_exact + (np.log(nf / max_exact) / math.log(T5_MAX_DIST / max_exact)
                         * (half - max_exact)).astype(np.int64)
    large = np.minimum(large, half - 1)
    return np.where(rel > 0, half, 0) + np.where(n < max_exact, n, large)


def _toeplitz(lookup, n_rows, n_cols):
    period = n_rows + n_cols
    slot = np.arange(period)
    diff = np.minimum((slot + n_rows - 1) % period - (n_rows - 1), n_cols - 1)
    vec = lookup(diff).astype(F32)
    flat = jnp.tile(vec, (1, n_rows))[:, :n_rows * (period - 1)]
    return flat.reshape(vec.shape[0], n_rows, period - 1)[:, :, :n_cols]


def _t5_bias_table(t5_bias, qpos, kpos, key_major=False):
    t5_rows = lambda rel: t5_bias[_t5_bucket_np(rel)].T
    mask = (kpos[None, :] // CHUNK) <= (qpos[:, None] // CHUNK)
    if key_major:
        table = _toeplitz(lambda dd: t5_rows(kpos[0] - qpos[0] - dd), len(kpos), len(qpos))
        mask = mask.T
    else:
        table = _toeplitz(lambda dd: t5_rows(kpos[0] - qpos[0] + dd), len(qpos), len(kpos))
    return jnp.where(jnp.asarray(mask)[None], table, NEG)


def _band_bias_table(rel_table, qpos, kpos):
    lookup = lambda dd: rel_table[:, np.clip(kpos[0] - qpos[0] + dd, -REL_CLIP, REL_CLIP) + REL_CLIP]
    table = _toeplitz(lookup, len(qpos), len(kpos))
    qc = qpos[:, None] // CHUNK
    kc = kpos[None, :] // CHUNK
    mask = (kpos[None, :] >= 0) & (kc <= qc) & (kc >= qc - C_PREV)
    return jnp.where(jnp.asarray(mask)[None], table, NEG)


def _lambda_init(layer):
    return 0.8 - 0.6 * math.exp(-0.3 * layer)


def kernel(x_prompt, x_sample, mem_prompt, cache_a_k, cache_a_v, cache_b_k, cache_b_v, cache_c_k, cache_c_v, cache_mem_k, cache_mem_v, t5_bias, ffn1_norm, ffn1_wg, ffn1_wu, ffn1_wd, mix_norm, w_in, a_qnorm, a_knorm, a_lq1, a_lk1, a_lq2, a_lk2, a_subln, c_qnorm, c_knorm, c_rel_bias, w_br_a, w_br_b, w_br_c, w_out, x_norm, mem_norm, x_wq, x_wkv, x_qnorm, x_knorm, x_wo, ffn2_norm, ffn2_wg, ffn2_wu, ffn2_wd):
    bp, sp, d = x_prompt.shape
    bs, ns, _ = x_sample.shape
    depth = w_in.shape[0]
    past = cache_a_k.shape[2]
    w_buf = cache_c_k.shape[2]
    n_mem = mem_prompt.shape[1]
    wa = cache_a_k.shape[3] * cache_a_k.shape[4]
    wb = cache_b_k.shape[3] * cache_b_k.shape[4]
    wc = cache_c_k.shape[3] * cache_c_k.shape[4]
    wx = cache_mem_k.shape[3] * cache_mem_k.shape[4]
    h_a = cache_a_k.shape[3]
    n_qkv = 3 * (wa + wb + wc)
    w_keep = min(C_PREV * CHUNK, sp)
    scale = HEAD_DIM ** -0.5
    assert sp % ATTN_TQ == 0 and ATTN_TQ == ATTN_TK and ATTN_TQ == 4 * CHUNK and sp >= w_keep
    assert past % ATTN_TK == 0
    assert sp % MIXER_A_TILE == 0 and MIXER_A_TILE % CHUNK == 0 and T5_MAX_DIST <= MIXER_A_TILE

    tile = lambda g, reps: jnp.tile(g.astype(F32), reps)
    bf = lambda a: a.astype(BF16)

    loc_a = np.arange(MIXER_A_TILE)
    loc = np.arange(ATTN_TQ)
    a_far = t5_bias[T5_BUCKETS // 2 - 1].astype(F32)
    a_bias = LOG2E * jnp.stack([
        _t5_bias_table(t5_bias, loc_a + MIXER_A_TILE, loc_a + MIXER_A_TILE, key_major=True),
        _t5_bias_table(t5_bias, loc_a + MIXER_A_TILE, loc_a, key_major=True),
        jnp.broadcast_to(a_far[:, None, None], (h_a, MIXER_A_TILE, MIXER_A_TILE))], axis=1)
    t5_log2 = LOG2E * t5_bias.astype(F32)
    a_brange = jnp.stack([LOG2E * a_far, jnp.max(t5_log2, axis=0), jnp.min(t5_log2, axis=0)], axis=1).reshape(-1)
    qpos_s = past + np.arange(ns)
    a_bias_sc = LOG2E * _t5_bias_table(t5_bias, qpos_s, np.arange(past))
    a_bias_sn = LOG2E * _t5_bias_table(t5_bias, qpos_s, qpos_s)
    kpos_sb = past - w_buf + np.arange(w_buf + ns)

    xp = x_prompt.reshape(bp * sp, d)
    xs = x_sample.reshape(bs * ns, d)
    mem = mem_prompt.reshape(bp * n_mem, d)
    outs = {k: [] for k in ("ak_p", "av_p", "bk_p", "bv_p", "ck_p", "cv_p", "mk_p", "mv_p",
                            "ak_s", "av_s", "bk_s", "bv_s", "ck_s", "cv_s")}
    flat = lambda dt: ((dt, 0),)
    dv_a = wa // h_a
    a_kv_prompt = ((F32, dv_a), (BF16, 0))
    qkv_tail = [(wb, False, True, flat(BF16)), (wb, False, False, flat(F32)), (wb, False, False, flat(F32)),
                (wc, True, True, flat(BF16)), (wc, True, True, flat(F32)), (wc, False, False, flat(F32))]
    qkv_segs_p = [(wa, True, True, flat(BF16)), (wa, True, True, a_kv_prompt),
                  (wa, False, False, a_kv_prompt)] + qkv_tail
    qkv_segs_s = [(wa, True, True, flat(BF16)), (wa, True, True, flat(F32)), (wa, False, False, flat(F32))] + qkv_tail

    wg1, wu1, wd1 = bf(ffn1_wg), bf(ffn1_wu), bf(ffn1_wd)
    wg2, wu2, wd2 = bf(ffn2_wg), bf(ffn2_wu), bf(ffn2_wd)
    w_qkv, w_gate = bf(w_in[:, :, :n_qkv]), bf(w_in[:, :, n_qkv:])
    wbr = bf(w_br_a), bf(w_br_b), bf(w_br_c)
    wout = bf(w_out)
    wq, wo, wkv = bf(x_wq), bf(x_wo), bf(x_wkv)

    for i in range(depth):
        qkv_gain = jnp.concatenate([
            tile(a_qnorm[i], wa // HEAD_DIM) * (scale * LOG2E), tile(a_knorm[i], wa // HEAD_DIM), jnp.ones((wa,), F32),
            jnp.full((wb,), scale, F32), jnp.ones((2 * wb,), F32),
            tile(c_qnorm[i], wc // HEAD_DIM) * scale, tile(c_knorm[i], wc // HEAD_DIM), jnp.ones((wc,), F32),
        ]).reshape(1, n_qkv)
        sub = (tile(a_subln[i], h_a) * (1.0 - _lambda_init(i))).reshape(1, wa)
        xq_gain = (tile(x_qnorm[i], wx // HEAD_DIM) * scale).reshape(1, wx)
        dot64 = lambda a, b: jnp.exp(jnp.sum(a.astype(F32) * b.astype(F32)))
        lam = (dot64(a_lq1[i], a_lk1[i]) - dot64(a_lq2[i], a_lk2[i]) + _lambda_init(i)).reshape(1)
        c_bias_p = _band_bias_table(c_rel_bias[i], C_PREV * CHUNK + loc, np.arange(BAND_W))
        c_bias_sc = _band_bias_table(c_rel_bias[i], qpos_s, kpos_sb[:w_buf])
        c_bias_sn = _band_bias_table(c_rel_bias[i], qpos_s, kpos_sb[w_buf:])

        xp = _ffn(xp, ffn1_norm[i], wg1, wu1, wd1, i, tm=1024, tf=1024)
        qa, ka, ka_bf, va, va_bf, qb, kb, vb, qc, kc, vc = _proj(xp, mix_norm[i], w_qkv, i, qkv_gain, qkv_segs_p, tm=1024)
        r3 = lambda a: a.reshape(bp, sp, a.shape[-1])
        oa = _attn_a(lam, a_brange, r3(qa), r3(ka_bf), r3(va_bf), a_bias)
        ob = _attn_b(r3(qb), r3(kb), r3(vb))
        oc = _attn_c(r3(qc), r3(kc), r3(vc), c_bias_p)
        xp = _merge(xp, oa.reshape(-1, wa), ob.reshape(-1, wb), oc.reshape(-1, wc), mix_norm[i], w_gate, sub,
                    *wbr, wout, i, tm=1024)
        mk, mv = _proj(mem, mem_norm[i], wkv, i,
                       jnp.concatenate([tile(x_knorm[i], wx // HEAD_DIM), jnp.ones((wx,), F32)]).reshape(1, 2 * wx),
                       [(wx, True, True, flat(F32)), (wx, False, False, flat(F32))], tm=n_mem)
        mk3, mv3 = mk.reshape(bp, n_mem, wx), mv.reshape(bp, n_mem, wx)
        xp = _cross(xp.reshape(bp, sp, d), x_norm[i], wq, xq_gain, bf(mk3), bf(mv3), wo, i, tm=1024).reshape(-1, d)
        xp = _ffn(xp, ffn2_norm[i], wg2, wu2, wd2, i, tm=1024, tf=1024)
        outs["ak_p"].append(ka.reshape(bp, sp, h_a, -1))
        outs["av_p"].append(va.reshape(bp, sp, h_a, -1))
        outs["bk_p"].append(kb.reshape(bp, sp, -1, HEAD_DIM))
        outs["bv_p"].append(vb.reshape(bp, sp, -1, HEAD_DIM))
        outs["ck_p"].append(r3(kc)[:, sp - w_keep:].reshape(bp, w_keep, -1, HEAD_DIM))
        outs["cv_p"].append(r3(vc)[:, sp - w_keep:].reshape(bp, w_keep, -1, HEAD_DIM))
        outs["mk_p"].append(mk3.reshape(bp, n_mem, -1, HEAD_DIM))
        outs["mv_p"].append(mv3.reshape(bp, n_mem, -1, HEAD_DIM))

        xs = _ffn(xs, ffn1_norm[i], wg1, wu1, wd1, i, tm=bs * ns, tf=1024)
        qa, ka, va, qb, kb, vb, qc, kc, vc = _proj(xs, mix_norm[i], w_qkv, i, qkv_gain, qkv_segs_s, tm=bs * ns)
        s3 = lambda a: a.reshape(bs, ns, a.shape[-1])
        c3 = lambda a: a.reshape(bs, a.shape[1], -1)
        oa, ob, oc = _sample_attn(
            i, lam, s3(qa), s3(ka), s3(va), cache_a_k, cache_a_v, a_bias_sc, a_bias_sn,
            s3(qb), s3(kb), s3(vb), c3(cache_b_k[i]), c3(cache_b_v[i]),
            s3(qc), s3(kc), s3(vc), c3(cache_c_k[i]), c3(cache_c_v[i]), c_bias_sc, c_bias_sn)
        xs = _merge(xs, oa.reshape(-1, wa), ob.reshape(-1, wb), oc.reshape(-1, wc), mix_norm[i], w_gate, sub,
                    *wbr, wout, i, tm=bs * ns)
        xs = _cross(xs.reshape(bs, ns, d), x_norm[i], wq, xq_gain, bf(c3(cache_mem_k[i])), bf(c3(cache_mem_v[i])),
                    wo, i, tm=ns).reshape(-1, d)
        xs = _ffn(xs, ffn2_norm[i], wg2, wu2, wd2, i, tm=bs * ns, tf=1024)
        outs["ak_s"].append(ka.reshape(bs, ns, h_a, -1))
        outs["av_s"].append(va.reshape(bs, ns, h_a, -1))
        outs["bk_s"].append(kb.reshape(bs, ns, -1, HEAD_DIM))
        outs["bv_s"].append(vb.reshape(bs, ns, -1, HEAD_DIM))
        outs["ck_s"].append(jnp.concatenate([cache_c_k[i], kc.reshape(bs, ns, -1, HEAD_DIM)], axis=1)[:, ns:])
        outs["cv_s"].append(jnp.concatenate([cache_c_v[i], vc.reshape(bs, ns, -1, HEAD_DIM)], axis=1)[:, ns:])

    st = lambda k: jnp.stack(outs[k])
    return (xp.reshape(bp, sp, d), xs.reshape(bs, ns, d),
            st("ak_p"), st("av_p"), st("bk_p"), st("bv_p"), st("ck_p"), st("cv_p"), st("mk_p"), st("mv_p"),
            st("ak_s"), st("av_s"), st("bk_s"), st("bv_s"), st("ck_s"), st("cv_s"))
```

```python
import functools
import math

import numpy as np
import jax
import jax.numpy as jnp
from jax import lax
from jax.experimental import pallas as pl
from jax.experimental.pallas import tpu as pltpu

F32 = jnp.float32
BF16 = jnp.bfloat16

EPS = 1e-6
HEAD_DIM = 64
CHUNK = 64
C_PREV = 8
REL_CLIP = 128
T5_BUCKETS = 32
T5_MAX_DIST = 128
LANES = 128
BF16_SUBLANES = 16
V7X_MXU_DIM = 256
NEG = -1e30
LOG2E = math.log2(math.e)
BOUND_SLACK = 1.001
BOUNDED_EXP2_SPAN = 100.0
STICK_SKIP = -110.0
V7X_VMEM_LIMIT_BYTES = 56 * 1024 * 1024

MIXER_A_TILE = 512
ATTN_TQ = 256
ATTN_TK = 256
BAND_W = (C_PREV + 4) * CHUNK


def _cparams(sem):
    return pltpu.CompilerParams(dimension_semantics=sem, vmem_limit_bytes=V7X_VMEM_LIMIT_BYTES)


def _rms_rows(x, g):
    return x * lax.rsqrt(jnp.mean(x * x, axis=-1, keepdims=True) + EPS) * g


def _dot(a, b):
    return jnp.dot(a, b, preferred_element_type=F32)


def _dot_nt(a, b):
    return lax.dot_general(a, b, (((1,), (1,)), ((), ())), preferred_element_type=F32)


def _keep_lanes(q, lo, hi):
    lane = lax.broadcasted_iota(jnp.int32, q.shape, 1)
    return jnp.where((lane >= lo) & (lane < hi), q.astype(F32), 0.0).astype(BF16)


def _group_rms(y, gmat):
    wide = gmat.shape[0]
    parts = []
    for c in range(y.shape[1] // wide):
        yc = y[:, c * wide:(c + 1) * wide]
        ms = _dot((yc * yc).astype(BF16), gmat)
        parts.append(yc * lax.rsqrt(ms + EPS))
    return parts[0] if len(parts) == 1 else jnp.concatenate(parts, axis=1)


def _group_mean_matrix():
    g = np.kron(np.eye(V7X_MXU_DIM // HEAD_DIM), np.ones((HEAD_DIM, HEAD_DIM))) / HEAD_DIM
    return jnp.asarray(g, BF16)


def _ffn_kernel(x_ref, g_ref, wg_ref, wu_ref, wd_ref, o_ref, h_scr):
    j = pl.program_id(1)

    @pl.when(j == 0)
    def _():
        x = x_ref[...]
        h_scr[...] = _rms_rows(x, g_ref[...]).astype(BF16)
        o_ref[...] = x

    h = h_scr[...]
    a = _dot(h, wg_ref[...])
    u = _dot(h, wu_ref[...])
    t = a * jax.nn.sigmoid(a) * u
    o_ref[...] += 0.5 * _dot(t.astype(BF16), wd_ref[...])


def _ffn(x, g, wg, wu, wd, layer, tm, tf=512):
    n, d = x.shape
    dff = wg.shape[2]
    return pl.pallas_call(
        _ffn_kernel,
        grid=(n // tm, dff // tf),
        in_specs=[
            pl.BlockSpec((tm, d), lambda i, j: (i, 0)),
            pl.BlockSpec((1, d), lambda i, j: (0, 0)),
            pl.BlockSpec((None, d, tf), lambda i, j: (layer, 0, j)),
            pl.BlockSpec((None, d, tf), lambda i, j: (layer, 0, j)),
            pl.BlockSpec((None, tf, d), lambda i, j: (layer, j, 0)),
        ],
        out_specs=pl.BlockSpec((tm, d), lambda i, j: (i, 0)),
        out_shape=jax.ShapeDtypeStruct((n, d), F32),
        scratch_shapes=[pltpu.VMEM((tm, d), BF16)],
        compiler_params=_cparams(("parallel", "arbitrary")),
        name="ffn",
    )(x, g.reshape(1, d), wg, wu, wd)


def _proj_kernel(x_ref, g_ref, w_ref, gain_ref, gmat_ref, *out_refs, segs):
    h = _rms_rows(x_ref[...], g_ref[...]).astype(BF16)
    off = 0
    out_refs = list(out_refs)
    for width, normed, gained, outs in segs:
        y = _dot(h, w_ref[:, off:off + width])
        if normed:
            y = _group_rms(y, gmat_ref[...])
        if gained:
            y = y * gain_ref[:, off:off + width]
        for _, head_width in outs:
            o_ref = out_refs.pop(0)
            if head_width:
                o_ref[...] = y.reshape(o_ref.shape).astype(o_ref.dtype)
            else:
                o_ref[...] = y.astype(o_ref.dtype)
        off += width


def _proj(x, g, w, layer, gain, segs, tm):
    n, d = x.shape
    wtot = w.shape[2]
    specs, shapes = [], []
    for width, _, _, outs in segs:
        for dt, head_width in outs:
            if head_width:
                nh = width // head_width
                specs.append(pl.BlockSpec((tm, nh, head_width), lambda i: (i, 0, 0)))
                shapes.append(jax.ShapeDtypeStruct((n, nh, head_width), dt))
            else:
                specs.append(pl.BlockSpec((tm, width), lambda i: (i, 0)))
                shapes.append(jax.ShapeDtypeStruct((n, width), dt))
    return pl.pallas_call(
        functools.partial(_proj_kernel, segs=tuple(segs)),
        grid=(n // tm,),
        in_specs=[
            pl.BlockSpec((tm, d), lambda i: (i, 0)),
            pl.BlockSpec((1, d), lambda i: (0, 0)),
            pl.BlockSpec((None, d, wtot), lambda i: (layer, 0, 0), pipeline_mode=pl.Buffered(1)),
            pl.BlockSpec((1, wtot), lambda i: (0, 0)),
            pl.BlockSpec((V7X_MXU_DIM, V7X_MXU_DIM), lambda i: (0, 0)),
        ],
        out_specs=specs,
        out_shape=shapes,
        compiler_params=_cparams(("parallel",)),
        name="proj",
    )(x, g.reshape(1, d), w, gain, _group_mean_matrix())


def _softmax_step_km(s, vt, m_ref, acc_ref):
    m_old = m_ref[...]
    m_new = jnp.maximum(m_old, jnp.max(s, axis=0, keepdims=True))
    alpha = jnp.exp2(m_old - m_new)
    p = jnp.exp2(s - m_new)
    acc_ref[...] = alpha * acc_ref[...] + _dot(vt, p.astype(BF16))
    m_ref[...] = m_new


def _cast_rows(src_ref, dst_ref, rows, dst_off=0, step=512):
    def body(i, c):
        r = pl.multiple_of(i * step, step)
        dst_ref[pl.ds(dst_off + r, step), :] = src_ref[0, pl.ds(r, step), :].astype(BF16)
        return c
    lax.fori_loop(0, rows // step, body, 0)


def _attn_a_kernel(lam_ref, brange_ref, q_ref, k_ref, v_ref, bias_ref, o_ref,
                   vtb, s_scr, knorm, m1, a1, m2, a2):
    h = pl.program_id(1)
    qi = pl.program_id(2)
    kbf = k_ref.at[0]
    seq = kbf.shape[0]
    tq = q_ref.shape[1]
    tk = vtb.shape[2]
    dv = v_ref.shape[2]

    @pl.when(qi == 0)
    def _():
        lane = lax.broadcasted_iota(jnp.int32, (tk, LANES), 1)

        def prep(j, c):
            r = pl.multiple_of(j * tk, tk)
            kb = kbf[pl.ds(r, tk), :]
            vtb[j, 0:dv, :] = v_ref[0, pl.ds(r, tk), :].astype(F32).T.astype(BF16)
            vtb[j, dv:, :] = jnp.ones((vtb.shape[1] - dv, tk), BF16)
            ksq = kb.astype(F32) * kb.astype(F32)
            n1 = jnp.max(jnp.sum(jnp.where(lane < HEAD_DIM, ksq, 0.0), axis=1, keepdims=True))
            n2 = jnp.max(jnp.sum(jnp.where(lane >= HEAD_DIM, ksq, 0.0), axis=1, keepdims=True))
            return jnp.maximum(c[0], n1), jnp.maximum(c[1], n2)
        n1, n2 = lax.fori_loop(0, seq // tk, prep, (jnp.float32(0.0), jnp.float32(0.0)))
        knorm[0] = n1
        knorm[1] = n2

    qt = q_ref[0].astype(F32).T
    sub = lax.broadcasted_iota(jnp.int32, qt.shape, 0)
    q1t = jnp.where(sub < HEAD_DIM, qt, 0.0).astype(BF16)
    q2t = jnp.where(sub >= HEAD_DIM, qt, 0.0).astype(BF16)
    states = ((m1, a1), (m2, a2))
    for m_ref, a_ref in states:
        a_ref[...] = jnp.zeros_like(a_ref)

    far_bias, bias_max, bias_min = brange_ref[3 * h], brange_ref[3 * h + 1], brange_ref[3 * h + 2]
    qsq = qt * qt
    bound1 = jnp.sqrt(jnp.sum(jnp.where(sub < HEAD_DIM, qsq, 0.0), axis=0, keepdims=True) * knorm[0])
    bound2 = jnp.sqrt(jnp.sum(jnp.where(sub >= HEAD_DIM, qsq, 0.0), axis=0, keepdims=True) * knorm[1])
    bounds = (bound1 * BOUND_SLACK + bias_max, bound2 * BOUND_SLACK + bias_max)
    spread = 2.0 * BOUND_SLACK * jnp.maximum(jnp.max(bound1), jnp.max(bound2)) + (bias_max - bias_min)
    bounded = spread <= BOUNDED_EXP2_SPAN

    @pl.when(bounded)
    def _():
        qmaps = ((q1t, bounds[0], a1), (q2t, bounds[1], a2))

        def accumulate(blocks):
            kbs = [kbf[pl.ds(pl.multiple_of(j * tk, tk), tk), :] for j, _ in blocks]
            ps = []
            for (j, tile), kb in zip(blocks, kbs):
                for qmt, shift, _ in qmaps:
                    add = (far_bias - shift) if tile is None else (tile - shift)
                    ps.append(jnp.exp2(_dot(kb, qmt) + add).astype(BF16))
            for mi, (_, _, a_ref) in enumerate(qmaps):
                total = None
                for bi, (j, _) in enumerate(blocks):
                    term = _dot(vtb[j], ps[bi * len(qmaps) + mi])
                    total = term if total is None else total + term
                a_ref[...] += total

        @pl.when(qi == 0)
        def _():
            accumulate([(qi, bias_ref[0, 0])])

        @pl.when(qi >= 1)
        def _():
            accumulate([(qi, bias_ref[0, 0]), (qi - 1, bias_ref[0, 1])])

        n_far = jnp.maximum(qi - 1, 0)
        one = n_far & 1
        two = n_far & 2

        @pl.when(one == 1)
        def _():
            accumulate([(0, None)])

        @pl.when(two == 2)
        def _():
            accumulate([(one, None), (one + 1, None)])

        def quad(i, c):
            j = one + two + 4 * i
            accumulate([(j, None), (j + 1, None), (j + 2, None), (j + 3, None)])
            return c
        lax.fori_loop(0, lax.shift_right_logical(n_far, 2), quad, 0)

    @pl.when(jnp.logical_not(bounded))
    def _():
        for m_ref, _ in states:
            m_ref[...] = jnp.full_like(m_ref, NEG)
        n_blocks = qi + 1

        def scores(t, buf):
            kb = kbf[pl.ds(pl.multiple_of((qi - t) * tk, tk), tk), :]
            s_scr[buf, 0] = _dot(kb, q1t)
            s_scr[buf, 1] = _dot(kb, q2t)

        def update(t, buf):
            bias = bias_ref[0, jnp.minimum(t, bias_ref.shape[1] - 1)]
            vt = vtb[qi - t]
            for mi, (m_ref, a_ref) in enumerate(states):
                _softmax_step_km(s_scr[buf, mi] + bias, vt, m_ref, a_ref)

        odd = n_blocks & 1

        @pl.when(odd == 1)
        def _():
            scores(0, 0)
            update(0, 0)

        n_pairs = lax.shift_right_logical(n_blocks, 1)

        @pl.when(n_pairs > 0)
        def _():
            scores(odd, 0)

        def pair(p, c):
            t0 = odd + 2 * p
            scores(t0 + 1, 1)
            update(t0, 0)
            scores(jnp.minimum(t0 + 2, qi), 0)
            update(t0 + 1, 1)
            return c
        lax.fori_loop(0, n_pairs, pair, 0)

    out_t = (a1[0:dv, :] / a1[dv:dv + 1, :]
             - lam_ref[0] * (a2[0:dv, :] / a2[dv:dv + 1, :]))
    o_ref[0] = out_t.T


def _attn_a(lam, brange, qa, ka, va, bias):
    b, s, w = qa.shape
    nh = w // LANES
    tq = tk = MIXER_A_TILE
    vrows = LANES + BF16_SUBLANES
    return pl.pallas_call(
        _attn_a_kernel,
        grid=(b, nh, s // tq),
        in_specs=[
            pl.BlockSpec(memory_space=pltpu.SMEM),
            pl.BlockSpec(memory_space=pltpu.SMEM),
            pl.BlockSpec((1, tq, LANES), lambda bi, h, qi: (bi, qi, h)),
            pl.BlockSpec((1, s, LANES), lambda bi, h, qi: (bi, 0, h)),
            pl.BlockSpec((1, s, LANES), lambda bi, h, qi: (bi, 0, h)),
            pl.BlockSpec((1,) + bias.shape[1:], lambda bi, h, qi: (h, 0, 0, 0)),
        ],
        out_specs=pl.BlockSpec((1, tq, LANES), lambda bi, h, qi: (bi, qi, h)),
        out_shape=jax.ShapeDtypeStruct((b, s, w), F32),
        scratch_shapes=[
            pltpu.VMEM((s // tk, vrows, tk), BF16),
            pltpu.VMEM((2, 2, tk, tq), F32), pltpu.SMEM((2,), F32),
            pltpu.VMEM((1, tq), F32), pltpu.VMEM((vrows, tq), F32),
            pltpu.VMEM((1, tq), F32), pltpu.VMEM((vrows, tq), F32),
        ],
        compiler_params=_cparams(("parallel", "parallel", "arbitrary")),
        name="mixer_a",
    )(lam, brange, qa, ka, va, bias)


def _stick_block(z, valid, umat, carry):
    sp = jnp.maximum(z, 0.0) + jnp.log(1.0 + jnp.exp(-jnp.abs(z)))
    log1m = -sp
    if valid is not None:
        log1m = jnp.where(valid, log1m, 0.0)
    hi = log1m.astype(BF16)
    lo = (log1m - hi.astype(F32)).astype(BF16)
    after = _dot(hi, umat) + _dot(lo, umat)
    w = jnp.exp(z - sp + after + carry)
    if valid is not None:
        w = jnp.where(valid, w, 0.0)
    return w, jnp.sum(log1m, axis=1, keepdims=True)


def _stick_block_km(z, valid, umat, carry):
    sp = jnp.maximum(z, 0.0) + jnp.log(1.0 + jnp.exp(-jnp.abs(z)))
    log1m = -sp
    if valid is not None:
        log1m = jnp.where(valid, log1m, 0.0)
    hi = log1m.astype(BF16)
    lo = (log1m - hi.astype(F32)).astype(BF16)
    after = _dot(umat, hi) + _dot(umat, lo)
    w = jnp.exp(z - sp + after + carry)
    if valid is not None:
        w = jnp.where(valid, w, 0.0)
    return w, jnp.sum(log1m, axis=0, keepdims=True)


def _attn_b_kernel(q_ref, k_ref, v_ref, u_ref, o_ref, kbf, vtb, c_scr, acc_scr, z_scr):
    step = pl.program_id(2)
    seq = kbf.shape[0]
    tk = vtb.shape[2]
    n_halves = q_ref.shape[1] // tk
    n_heads = LANES // HEAD_DIM

    @pl.when(step == 0)
    def _():
        def prep(j, c):
            r = pl.multiple_of(j * tk, tk)
            kbf[pl.ds(r, tk), :] = k_ref[0, pl.ds(r, tk), :].astype(BF16)
            vtb[j] = v_ref[0, pl.ds(r, tk), :].astype(F32).T.astype(BF16)
            return c
        lax.fori_loop(0, seq // tk, prep, 0)

    sub = lax.broadcasted_iota(jnp.int32, (LANES, tk), 0)
    qts = []
    for half in range(n_halves):
        qt = q_ref[0, half * tk:(half + 1) * tk, :].astype(F32).T
        qts.append([jnp.where((sub >= hh * HEAD_DIM) & (sub < (hh + 1) * HEAD_DIM), qt, 0.0).astype(BF16)
                    for hh in range(n_heads)])
    krow = lax.broadcasted_iota(jnp.int32, (tk, tk), 0)
    qcol = lax.broadcasted_iota(jnp.int32, (tk, tk), 1)
    umat = u_ref[...]

    def blocks(half, js, valids, first):
        kbs = [kbf[pl.ds(pl.multiple_of(j * tk, tk), tk), :] for j in js]
        cmax = None
        for hh in range(n_heads):
            carry = 0.0 if first else c_scr[half, hh]
            pv = None
            for j, kb, valid in zip(js, kbs, valids):
                w, cs = _stick_block_km(_dot(kb, qts[half][hh]), valid, umat, carry)
                term = _dot(vtb[j], w.astype(BF16))
                pv = term if pv is None else pv + term
                carry = carry + cs
            acc_scr[half, hh] = pv if first else acc_scr[half, hh] + pv
            c_scr[half, hh] = carry
            cm = jnp.max(carry)
            cmax = cm if cmax is None else jnp.maximum(cmax, cm)
        return cmax

    own_valid = krow < qcol
    everywhere = krow >= 0

    chains = []
    for half in range(n_halves):
        qb = step * n_halves + half
        prev_valid = None if half >= 1 else everywhere & (qb >= 1)
        for hh in range(n_heads):
            for j, valid in ((qb, own_valid), (jnp.maximum(qb - 1, 0), prev_valid)):
                chains.append((half, hh, j, valid))
    for c, (half, hh, j, valid) in enumerate(chains):
        z_scr[c] = _dot(kbf[pl.ds(pl.multiple_of(j * tk, tk), tk), :], qts[half][hh])
    col_sums = []
    for c, (half, hh, j, valid) in enumerate(chains):
        z = z_scr[c]
        sp = jnp.maximum(z, 0.0) + jnp.log(1.0 + jnp.exp(-jnp.abs(z)))
        log1m = -sp if valid is None else jnp.where(valid, -sp, 0.0)
        hi = log1m.astype(BF16)
        lo = (log1m - hi.astype(F32)).astype(BF16)
        z_scr[c] = z - sp + _dot(umat, hi) + _dot(umat, lo)
        col_sums.append(jnp.sum(log1m, axis=0, keepdims=True))
    cmaxes = []
    for half in range(n_halves):
        cmax = None
        for hh in range(n_heads):
            carry, pv = 0.0, None
            for c, (ch, chh, j, valid) in enumerate(chains):
                if (ch, chh) != (half, hh):
                    continue
                w = jnp.exp(z_scr[c] + carry)
                if valid is not None:
                    w = jnp.where(valid, w, 0.0)
                term = _dot(vtb[j], w.astype(BF16))
                pv = term if pv is None else pv + term
                carry = carry + col_sums[c]
            acc_scr[half, hh] = pv
            c_scr[half, hh] = carry
            cm = jnp.max(carry)
            cmax = cm if cmax is None else jnp.maximum(cmax, cm)
        cmaxes.append(cmax)

    for half in range(n_halves):
        qb = step * n_halves + half

        def cond(st):
            j, cmax = st
            return (j >= 0) & (cmax > STICK_SKIP)

        def body(st, half=half):
            j, _ = st
            return j - 2, blocks(half, [j, jnp.maximum(j - 1, 0)], [None, everywhere & (j >= 1)], False)

        lax.while_loop(cond, body, (qb - 2, cmaxes[half]))
        out_t = jnp.where(sub < HEAD_DIM, acc_scr[half, 0], acc_scr[half, 1])
        o_ref[0, half * tk:(half + 1) * tk, :] = out_t.T.astype(o_ref.dtype)


def _strict_lower(n):
    return jnp.asarray(np.tril(np.ones((n, n)), -1), BF16)


def _strict_upper(n):
    return jnp.asarray(np.triu(np.ones((n, n)), 1), BF16)


def _attn_b(qb, kb, vb):
    b, s, w = qb.shape
    n_halves = 4
    tq = n_halves * ATTN_TK
    return pl.pallas_call(
        _attn_b_kernel,
        grid=(b, w // LANES, s // tq),
        in_specs=[
            pl.BlockSpec((1, tq, LANES), lambda bi, h, qi: (bi, qi, h)),
            pl.BlockSpec((1, s, LANES), lambda bi, h, qi: (bi, 0, h)),
            pl.BlockSpec((1, s, LANES), lambda bi, h, qi: (bi, 0, h)),
            pl.BlockSpec((ATTN_TK, ATTN_TK), lambda bi, h, qi: (0, 0)),
        ],
        out_specs=pl.BlockSpec((1, tq, LANES), lambda bi, h, qi: (bi, qi, h)),
        out_shape=jax.ShapeDtypeStruct((b, s, w), BF16),
        scratch_shapes=[
            pltpu.VMEM((s, LANES), BF16), pltpu.VMEM((s // ATTN_TK, LANES, ATTN_TK), BF16),
            pltpu.VMEM((n_halves, LANES // HEAD_DIM, 1, ATTN_TK), F32),
            pltpu.VMEM((n_halves, LANES // HEAD_DIM, LANES, ATTN_TK), F32),
            pltpu.VMEM((2 * n_halves * (LANES // HEAD_DIM), ATTN_TK, ATTN_TK), F32),
        ],
        compiler_params=_cparams(("parallel", "parallel", "arbitrary")),
        name="mixer_b",
    )(qb, kb, vb, _strict_upper(ATTN_TK))


def _attn_c_kernel(q_ref, k_ref, v_ref, bias_ref, o_ref, kbf, vbf, s_scr):
    step = pl.program_id(2)
    seq = k_ref.shape[1]
    tq = ATTN_TQ
    n_blocks = q_ref.shape[1] // tq
    pad = C_PREV * CHUNK
    n_heads = LANES // HEAD_DIM

    @pl.when(step == 0)
    def _():
        kbf[0:pad, :] = jnp.zeros((pad, LANES), BF16)
        vbf[0:pad, 0:LANES] = jnp.zeros((pad, LANES), BF16)
        vbf[:, LANES:] = jnp.ones((vbf.shape[0], vbf.shape[1] - LANES), BF16)
        _cast_rows(k_ref, kbf, seq, dst_off=pad)

        def vcast(i, c):
            r = pl.multiple_of(i * tq, tq)
            vbf[pl.ds(pad + r, tq), 0:LANES] = v_ref[0, pl.ds(r, tq), :].astype(BF16)
            return c
        lax.fori_loop(0, seq // tq, vcast, 0)

    lane = lax.broadcasted_iota(jnp.int32, (tq, LANES), 1)
    col = lax.broadcasted_iota(jnp.int32, (tq, BAND_W), 1)
    for blk in range(n_blocks):
        qb = step * n_blocks + blk
        q = q_ref[0, blk * tq:(blk + 1) * tq, :]
        kw = kbf[pl.ds(pl.multiple_of(qb * tq, tq), BAND_W), :]
        in_seq = col >= pad - qb * tq
        for hh in range(n_heads):
            qm = _keep_lanes(q, hh * HEAD_DIM, (hh + 1) * HEAD_DIM)
            s_scr[blk * n_heads + hh] = jnp.where(in_seq, _dot_nt(qm, kw) + bias_ref[hh], NEG)
    for blk in range(n_blocks):
        qb = step * n_blocks + blk
        vw = vbf[pl.ds(pl.multiple_of(qb * tq, tq), BAND_W), :]
        outs = []
        for hh in range(n_heads):
            s = s_scr[blk * n_heads + hh]
            p = jnp.exp(s - jnp.max(s, axis=1, keepdims=True))
            acc = _dot(p.astype(BF16), vw)
            outs.append(acc[:, 0:LANES] / acc[:, LANES:])
        o_ref[0, blk * tq:(blk + 1) * tq, :] = jnp.where(lane < HEAD_DIM, outs[0], outs[1]).astype(o_ref.dtype)


def _attn_c(qc, kc, vc, bias):
    b, s, w = qc.shape
    n_blocks = 4
    tq = n_blocks * ATTN_TQ
    pad = C_PREV * CHUNK
    hp = LANES // HEAD_DIM
    return pl.pallas_call(
        _attn_c_kernel,
        grid=(b, w // LANES, s // tq),
        in_specs=[
            pl.BlockSpec((1, tq, LANES), lambda bi, h, qi: (bi, qi, h)),
            pl.BlockSpec((1, s, LANES), lambda bi, h, qi: (bi, 0, h)),
            pl.BlockSpec((1, s, LANES), lambda bi, h, qi: (bi, 0, h)),
            pl.BlockSpec((hp, ATTN_TQ, BAND_W), lambda bi, h, qi: (h, 0, 0)),
        ],
        out_specs=pl.BlockSpec((1, tq, LANES), lambda bi, h, qi: (bi, qi, h)),
        out_shape=jax.ShapeDtypeStruct((b, s, w), BF16),
        scratch_shapes=[pltpu.VMEM((s + pad, LANES), BF16), pltpu.VMEM((s + pad, 2 * LANES), BF16),
                        pltpu.VMEM((n_blocks * hp, ATTN_TQ, BAND_W), F32)],
        compiler_params=_cparams(("parallel", "parallel", "arbitrary")),
        name="mixer_c",
    )(qc, kc, vc, bias)


def _merge_kernel(x_ref, oa_ref, ob_ref, oc_ref, g_ref, wgate_ref, sub_ref,
                  wa_ref, wb_ref, wc_ref, wout_ref, o_ref):
    x = x_ref[...]
    d = x.shape[1]
    h = _rms_rows(x, g_ref[...]).astype(BF16)
    oa = oa_ref[...]
    parts = []
    for c in range(oa.shape[1] // LANES):
        oc_ = oa[:, c * LANES:(c + 1) * LANES]
        parts.append(oc_ * lax.rsqrt(jnp.mean(oc_ * oc_, axis=-1, keepdims=True) + EPS))
    oan = (jnp.concatenate(parts, axis=1) * sub_ref[...]).astype(BF16)
    merged = jax.nn.sigmoid(_dot(h, wgate_ref[:, 0:d])) * _dot(oan, wa_ref[...])
    merged += jax.nn.sigmoid(_dot(h, wgate_ref[:, d:2 * d])) * _dot(ob_ref[...], wb_ref[...])
    merged += jax.nn.sigmoid(_dot(h, wgate_ref[:, 2 * d:3 * d])) * _dot(oc_ref[...], wc_ref[...])
    o_ref[...] = x + _dot(merged.astype(BF16), wout_ref[...])


def _merge(x, oa, ob, oc, g, wgate, sub, wa, wb, wc, wout, layer, tm):
    n, d = x.shape
    full = lambda a: pl.BlockSpec(a.shape, lambda i: (0,) * a.ndim, pipeline_mode=pl.Buffered(1))
    of_layer = lambda a: pl.BlockSpec((None,) + a.shape[1:], lambda i: (layer,) + (0,) * (a.ndim - 1),
                                      pipeline_mode=pl.Buffered(1))
    rows = lambda a: pl.BlockSpec((tm, a.shape[1]), lambda i: (i, 0))
    g = g.reshape(1, d)
    return pl.pallas_call(
        _merge_kernel,
        grid=(n // tm,),
        in_specs=[rows(x), rows(oa), rows(ob), rows(oc), full(g), of_layer(wgate), full(sub),
                  of_layer(wa), of_layer(wb), of_layer(wc), of_layer(wout)],
        out_specs=rows(x),
        out_shape=jax.ShapeDtypeStruct((n, d), F32),
        compiler_params=_cparams(("parallel",)),
        name="merge",
    )(x, oa, ob, oc, g, wgate, sub, wa, wb, wc, wout)


def _cross_kernel(x_ref, g_ref, wq_ref, gain_ref, gmat_ref, mk_ref, mv_ref, wo_ref, o_ref):
    x = x_ref[0]
    h = _rms_rows(x, g_ref[...]).astype(BF16)
    q = (_group_rms(_dot(h, wq_ref[...]), gmat_ref[...]) * gain_ref[...]).astype(BF16)
    mk = mk_ref[0]
    mv = mv_ref[0]
    lane = lax.broadcasted_iota(jnp.int32, q.shape, 1)
    o = jnp.zeros(q.shape, F32)
    for hh in range(q.shape[1] // HEAD_DIM):
        in_head = (lane >= hh * HEAD_DIM) & (lane < (hh + 1) * HEAD_DIM)
        s = _dot_nt(_keep_lanes(q, hh * HEAD_DIM, (hh + 1) * HEAD_DIM), mk)
        p = jnp.exp(s - jnp.max(s, axis=1, keepdims=True))
        l = jnp.sum(p, axis=1, keepdims=True)
        o = jnp.where(in_head, _dot(p.astype(BF16), mv) / l, o)
    o_ref[0] = x + _dot(o.astype(BF16), wo_ref[...])


def _cross(x, g, wq, gain, mk, mv, wo, layer, tm):
    b, s, d = x.shape
    full = lambda a: pl.BlockSpec(a.shape, lambda bi, i: (0,) * a.ndim)
    of_layer = lambda a: pl.BlockSpec((None,) + a.shape[1:], lambda bi, i: (layer,) + (0,) * (a.ndim - 1))
    g = g.reshape(1, d)
    gmat = _group_mean_matrix()
    return pl.pallas_call(
        _cross_kernel,
        grid=(b, s // tm),
        in_specs=[
            pl.BlockSpec((1, tm, d), lambda bi, i: (bi, i, 0)),
            full(g), of_layer(wq), full(gain), full(gmat),
            pl.BlockSpec((1,) + mk.shape[1:], lambda bi, i: (bi, 0, 0)),
            pl.BlockSpec((1,) + mv.shape[1:], lambda bi, i: (bi, 0, 0)),
            of_layer(wo),
        ],
        out_specs=pl.BlockSpec((1, tm, d), lambda bi, i: (bi, i, 0)),
        out_shape=jax.ShapeDtypeStruct((b, s, d), F32),
        compiler_params=_cparams(("parallel", "parallel")),
        name="cross",
    )(x, g, wq, gain, gmat, mk, mv, wo)


def _sample_attn_kernel(lam_ref,
                        qa_ref, kan_ref, van_ref, kac_ref, vac_ref, bac_ref, ban_ref,
                        qb_ref, kbn_ref, vbn_ref, kbc_ref, vbc_ref, ubig_ref, usmall_ref,
                        qc_ref, kcn_ref, vcn_ref, kcc_ref, vcc_ref, bcc_ref, bcn_ref,
                        oa_ref, ob_ref, oc_ref):
    lam = lam_ref[0]
    ns = qa_ref.shape[1]
    bf = lambda r: r[0].astype(BF16)

    def heads(q, width):
        for hh in range(LANES // width):
            yield hh, None, _keep_lanes(q, hh * width, (hh + 1) * width)

    def softmax2(s_c, s_n, exp=jnp.exp):
        m = jnp.maximum(jnp.max(s_c, axis=1, keepdims=True), jnp.max(s_n, axis=1, keepdims=True))
        p_c = exp(s_c - m)
        p_n = exp(s_n - m)
        inv = 1.0 / (jnp.sum(p_c, axis=1, keepdims=True) + jnp.sum(p_n, axis=1, keepdims=True))
        return p_c * inv, p_n * inv

    for h in range(qa_ref.shape[2] // LANES):
        sl = slice(h * LANES, (h + 1) * LANES)
        q = qa_ref[0, :, sl]
        k_c = kac_ref[0, :, h, :].astype(BF16)
        k_n = kan_ref[0, :, sl].astype(BF16)
        maps = []
        for _, _, qm in heads(q, HEAD_DIM):
            maps.append(softmax2(_dot_nt(qm, k_c) + bac_ref[h], _dot_nt(qm, k_n) + ban_ref[h], exp=jnp.exp2))
        a_c = (maps[0][0] - lam * maps[1][0]).astype(BF16)
        a_n = (maps[0][1] - lam * maps[1][1]).astype(BF16)
        oa_ref[0, :, sl] = (_dot(a_c, vac_ref[0, :, h, :].astype(BF16))
                            + _dot(a_n, van_ref[0, :, sl].astype(BF16)))

    past = kbc_ref.shape[1]
    tk = ubig_ref.shape[0]
    row = lax.broadcasted_iota(jnp.int32, (ns, ns), 0)
    col = lax.broadcasted_iota(jnp.int32, (ns, ns), 1)
    for pr in range(qb_ref.shape[2] // LANES):
        sl = slice(pr * LANES, (pr + 1) * LANES)
        q = qb_ref[0, :, sl]
        k_n = kbn_ref[0, :, sl].astype(BF16)
        v_n = vbn_ref[0, :, sl].astype(BF16)
        outs = []
        for _, _, qm in heads(q, HEAD_DIM):
            w, carry = _stick_block(_dot_nt(qm, k_n), col < row, usmall_ref[...], 0.0)
            acc = _dot(w.astype(BF16), v_n)
            for j in range(past // tk - 1, -1, -1):
                k_c = kbc_ref[0, j * tk:(j + 1) * tk, sl].astype(BF16)
                v_c = vbc_ref[0, j * tk:(j + 1) * tk, sl].astype(BF16)
                w, rs = _stick_block(_dot_nt(qm, k_c), None, ubig_ref[...], carry)
                acc += _dot(w.astype(BF16), v_c)
                carry = carry + rs
            outs.append(acc)
        lane = lax.broadcasted_iota(jnp.int32, q.shape, 1)
        ob_ref[0, :, sl] = jnp.where(lane < HEAD_DIM, outs[0], outs[1]).astype(ob_ref.dtype)

    for pr in range(qc_ref.shape[2] // LANES):
        sl = slice(pr * LANES, (pr + 1) * LANES)
        q = qc_ref[0, :, sl]
        k_c = kcc_ref[0, :, sl].astype(BF16)
        k_n = kcn_ref[0, :, sl].astype(BF16)
        v_c = vcc_ref[0, :, sl].astype(BF16)
        v_n = vcn_ref[0, :, sl].astype(BF16)
        outs = []
        for hh, _, qm in heads(q, HEAD_DIM):
            hd = pr * (LANES // HEAD_DIM) + hh
            p_c, p_n = softmax2(_dot_nt(qm, k_c) + bcc_ref[hd], _dot_nt(qm, k_n) + bcn_ref[hd])
            outs.append(_dot(p_c.astype(BF16), v_c) + _dot(p_n.astype(BF16), v_n))
        lane = lax.broadcasted_iota(jnp.int32, q.shape, 1)
        oc_ref[0, :, sl] = jnp.where(lane < HEAD_DIM, outs[0], outs[1]).astype(oc_ref.dtype)


def _sample_attn(layer, lam, qa, ka, va, cak, cav, bac, ban, qb, kb, vb, cbk, cbv,
                 qc, kc, vc, cck, ccv, bcc, bcn):
    b, ns, _ = qa.shape
    per_b = lambda a: pl.BlockSpec((1,) + a.shape[1:], lambda bi: (bi,) + (0,) * (a.ndim - 1))
    full = lambda a: pl.BlockSpec(a.shape, lambda bi: (0,) * a.ndim)
    layer_b = lambda a: pl.BlockSpec((None, 1) + a.shape[2:], lambda bi: (layer, bi) + (0,) * (a.ndim - 2))
    ubig = _strict_lower(ATTN_TK)
    usmall = _strict_lower(ns)
    args = [qa, ka, va, cak, cav, bac, ban, qb, kb, vb, cbk, cbv, ubig, usmall,
            qc, kc, vc, cck, ccv, bcc, bcn]
    specs = [per_b(qa), per_b(ka), per_b(va), layer_b(cak), layer_b(cav), full(bac), full(ban),
             per_b(qb), per_b(kb), per_b(vb), per_b(cbk), per_b(cbv), full(ubig), full(usmall),
             per_b(qc), per_b(kc), per_b(vc), per_b(cck), per_b(ccv), full(bcc), full(bcn)]
    return pl.pallas_call(
        _sample_attn_kernel,
        grid=(b,),
        in_specs=[pl.BlockSpec(memory_space=pltpu.SMEM)] + specs,
        out_specs=[per_b(qa), per_b(qb), per_b(qc)],
        out_shape=[jax.ShapeDtypeStruct(qa.shape, F32), jax.ShapeDtypeStruct(qb.shape, BF16),
                   jax.ShapeDtypeStruct(qc.shape, BF16)],
        compiler_params=_cparams(("parallel",)),
        name="sample_mixers",
    )(lam, *args)


def _t5_bucket_np(rel):
    half = T5_BUCKETS // 2
    max_exact = half // 2
    n = np.abs(rel)
    nf = np.maximum(n, 1).astype(np.float64)
    large = max_exact + (np.log(nf / max_exact) / math.log(T5_MAX_DIST / max_exact)
                         * (half - max_exact)).astype(np.int64)
    large = np.minimum(large, half - 1)
    return np.where(rel > 0, half, 0) + np.where(n < max_exact, n, large)


def _toeplitz(lookup, n_rows, n_cols):
    period = n_rows + n_cols
    slot = np.arange(period)
    diff = np.minimum((slot + n_rows - 1) % period - (n_rows - 1), n_cols - 1)
    vec = lookup(diff).astype(F32)
    flat = jnp.tile(vec, (1, n_rows))[:, :n_rows * (period - 1)]
    return flat.reshape(vec.shape[0], n_rows, period - 1)[:, :, :n_cols]


def _t5_bias_table(t5_bias, qpos, kpos, key_major=False):
    t5_rows = lambda rel: t5_bias[_t5_bucket_np(rel)].T
    mask = (kpos[None, :] // CHUNK) <= (qpos[:, None] // CHUNK)
    if key_major:
        table = _toeplitz(lambda dd: t5_rows(kpos[0] - qpos[0] - dd), len(kpos), len(qpos))
        mask = mask.T
    else:
        table = _toeplitz(lambda dd: t5_rows(kpos[0] - qpos[0] + dd), len(qpos), len(kpos))
    return jnp.where(jnp.asarray(mask)[None], table, NEG)


def _band_bias_table(rel_table, qpos, kpos):
    lookup = lambda dd: rel_table[:, np.clip(kpos[0] - qpos[0] + dd, -REL_CLIP, REL_CLIP) + REL_CLIP]
    table = _toeplitz(lookup, len(qpos), len(kpos))
    qc = qpos[:, None] // CHUNK
    kc = kpos[None, :] // CHUNK
    mask = (kpos[None, :] >= 0) & (kc <= qc) & (kc >= qc - C_PREV)
    return jnp.where(jnp.asarray(mask)[None], table, NEG)


def _lambda_init(layer):
    return 0.8 - 0.6 * math.exp(-0.3 * layer)


def kernel(x_prompt, x_sample, mem_prompt, cache_a_k, cache_a_v, cache_b_k, cache_b_v, cache_c_k, cache_c_v, cache_mem_k, cache_mem_v, t5_bias, ffn1_norm, ffn1_wg, ffn1_wu, ffn1_wd, mix_norm, w_in, a_qnorm, a_knorm, a_lq1, a_lk1, a_lq2, a_lk2, a_subln, c_qnorm, c_knorm, c_rel_bias, w_br_a, w_br_b, w_br_c, w_out, x_norm, mem_norm, x_wq, x_wkv, x_qnorm, x_knorm, x_wo, ffn2_norm, ffn2_wg, ffn2_wu, ffn2_wd):
    bp, sp, d = x_prompt.shape
    bs, ns, _ = x_sample.shape
    depth = w_in.shape[0]
    past = cache_a_k.shape[2]
    w_buf = cache_c_k.shape[2]
    n_mem = mem_prompt.shape[1]
    wa = cache_a_k.shape[3] * cache_a_k.shape[4]
    wb = cache_b_k.shape[3] * cache_b_k.shape[4]
    wc = cache_c_k.shape[3] * cache_c_k.shape[4]
    wx = cache_mem_k.shape[3] * cache_mem_k.shape[4]
    h_a = cache_a_k.shape[3]
    n_qkv = 3 * (wa + wb + wc)
    w_keep = min(C_PREV * CHUNK, sp)
    scale = HEAD_DIM ** -0.5
    assert sp % ATTN_TQ == 0 and ATTN_TQ == ATTN_TK and ATTN_TQ == 4 * CHUNK and sp >= w_keep
    assert past % ATTN_TK == 0
    assert sp % MIXER_A_TILE == 0 and MIXER_A_TILE % CHUNK == 0 and T5_MAX_DIST <= MIXER_A_TILE

    tile = lambda g, reps: jnp.tile(g.astype(F32), reps)
    bf = lambda a: a.astype(BF16)

    loc_a = np.arange(MIXER_A_TILE)
    loc = np.arange(ATTN_TQ)
    a_far = t5_bias[T5_BUCKETS // 2 - 1].astype(F32)
    a_bias = LOG2E * jnp.stack([
        _t5_bias_table(t5_bias, loc_a + MIXER_A_TILE, loc_a + MIXER_A_TILE, key_major=True),
        _t5_bias_table(t5_bias, loc_a + MIXER_A_TILE, loc_a, key_major=True),
        jnp.broadcast_to(a_far[:, None, None], (h_a, MIXER_A_TILE, MIXER_A_TILE))], axis=1)
    t5_log2 = LOG2E * t5_bias.astype(F32)
    a_brange = jnp.stack([LOG2E * a_far, jnp.max(t5_log2, axis=0), jnp.min(t5_log2, axis=0)], axis=1).reshape(-1)
    qpos_s = past + np.arange(ns)
    a_bias_sc = LOG2E * _t5_bias_table(t5_bias, qpos_s, np.arange(past))
    a_bias_sn = LOG2E * _t5_bias_table(t5_bias, qpos_s, qpos_s)
    kpos_sb = past - w_buf + np.arange(w_buf + ns)

    xp = x_prompt.reshape(bp * sp, d)
    xs = x_sample.reshape(bs * ns, d)
    mem = mem_prompt.reshape(bp * n_mem, d)
    outs = {k: [] for k in ("ak_p", "av_p", "bk_p", "bv_p", "ck_p", "cv_p", "mk_p", "mv_p",
                            "ak_s", "av_s", "bk_s", "bv_s", "ck_s", "cv_s")}
    flat = lambda dt: ((dt, 0),)
    dv_a = wa // h_a
    a_kv_prompt = ((F32, dv_a), (BF16, 0))
    qkv_tail = [(wb, False, True, flat(BF16)), (wb, False, False, flat(F32)), (wb, False, False, flat(F32)),
                (wc, True, True, flat(BF16)), (wc, True, True, flat(F32)), (wc, False, False, flat(F32))]
    qkv_segs_p = [(wa, True, True, flat(BF16)), (wa, True, True, a_kv_prompt),
                  (wa, False, False, a_kv_prompt)] + qkv_tail
    qkv_segs_s = [(wa, True, True, flat(BF16)), (wa, True, True, flat(F32)), (wa, False, False, flat(F32))] + qkv_tail

    wg1, wu1, wd1 = bf(ffn1_wg), bf(ffn1_wu), bf(ffn1_wd)
    wg2, wu2, wd2 = bf(ffn2_wg), bf(ffn2_wu), bf(ffn2_wd)
    w_qkv, w_gate = bf(w_in[:, :, :n_qkv]), bf(w_in[:, :, n_qkv:])
    wbr = bf(w_br_a), bf(w_br_b), bf(w_br_c)
    wout = bf(w_out)
    wq, wo, wkv = bf(x_wq), bf(x_wo), bf(x_wkv)

    for i in range(depth):
        qkv_gain = jnp.concatenate([
            tile(a_qnorm[i], wa // HEAD_DIM) * (scale * LOG2E), tile(a_knorm[i], wa // HEAD_DIM), jnp.ones((wa,), F32),
            jnp.full((wb,), scale, F32), jnp.ones((2 * wb,), F32),
            tile(c_qnorm[i], wc // HEAD_DIM) * scale, tile(c_knorm[i], wc // HEAD_DIM), jnp.ones((wc,), F32),
        ]).reshape(1, n_qkv)
        sub = (tile(a_subln[i], h_a) * (1.0 - _lambda_init(i))).reshape(1, wa)
        xq_gain = (tile(x_qnorm[i], wx // HEAD_DIM) * scale).reshape(1, wx)
        dot64 = lambda a, b: jnp.exp(jnp.sum(a.astype(F32) * b.astype(F32)))
        lam = (dot64(a_lq1[i], a_lk1[i]) - dot64(a_lq2[i], a_lk2[i]) + _lambda_init(i)).reshape(1)
        c_bias_p = _band_bias_table(c_rel_bias[i], C_PREV * CHUNK + loc, np.arange(BAND_W))
        c_bias_sc = _band_bias_table(c_rel_bias[i], qpos_s, kpos_sb[:w_buf])
        c_bias_sn = _band_bias_table(c_rel_bias[i], qpos_s, kpos_sb[w_buf:])

        xp = _ffn(xp, ffn1_norm[i], wg1, wu1, wd1, i, tm=1024, tf=1024)
        qa, ka, ka_bf, va, va_bf, qb, kb, vb, qc, kc, vc = _proj(xp, mix_norm[i], w_qkv, i, qkv_gain, qkv_segs_p, tm=1024)
        r3 = lambda a: a.reshape(bp, sp, a.shape[-1])
        oa = _attn_a(lam, a_brange, r3(qa), r3(ka_bf), r3(va_bf), a_bias)
        ob = _attn_b(r3(qb), r3(kb), r3(vb))
        oc = _attn_c(r3(qc), r3(kc), r3(vc), c_bias_p)
        xp = _merge(xp, oa.reshape(-1, wa), ob.reshape(-1, wb), oc.reshape(-1, wc), mix_norm[i], w_gate, sub,
                    *wbr, wout, i, tm=1024)
        mk, mv = _proj(mem, mem_norm[i], wkv, i,
                       jnp.concatenate([tile(x_knorm[i], wx // HEAD_DIM), jnp.ones((wx,), F32)]).reshape(1, 2 * wx),
                       [(wx, True, True, flat(F32)), (wx, False, False, flat(F32))], tm=n_mem)
        mk3, mv3 = mk.reshape(bp, n_mem, wx), mv.reshape(bp, n_mem, wx)
        xp = _cross(xp.reshape(bp, sp, d), x_norm[i], wq, xq_gain, bf(mk3), bf(mv3), wo, i, tm=1024).reshape(-1, d)
        xp = _ffn(xp, ffn2_norm[i], wg2, wu2, wd2, i, tm=1024, tf=1024)
        outs["ak_p"].append(ka.reshape(bp, sp, h_a, -1))
        outs["av_p"].append(va.reshape(bp, sp, h_a, -1))
        outs["bk_p"].append(kb.reshape(bp, sp, -1, HEAD_DIM))
        outs["bv_p"].append(vb.reshape(bp, sp, -1, HEAD_DIM))
        outs["ck_p"].append(r3(kc)[:, sp - w_keep:].reshape(bp, w_keep, -1, HEAD_DIM))
        outs["cv_p"].append(r3(vc)[:, sp - w_keep:].reshape(bp, w_keep, -1, HEAD_DIM))
        outs["mk_p"].append(mk3.reshape(bp, n_mem, -1, HEAD_DIM))
        outs["mv_p"].append(mv3.reshape(bp, n_mem, -1, HEAD_DIM))

        xs = _ffn(xs, ffn1_norm[i], wg1, wu1, wd1, i, tm=bs * ns, tf=1024)
        qa, ka, va, qb, kb, vb, qc, kc, vc = _proj(xs, mix_norm[i], w_qkv, i, qkv_gain, qkv_segs_s, tm=bs * ns)
        s3 = lambda a: a.reshape(bs, ns, a.shape[-1])
        c3 = lambda a: a.reshape(bs, a.shape[1], -1)
        oa, ob, oc = _sample_attn(
            i, lam, s3(qa), s3(ka), s3(va), cache_a_k, cache_a_v, a_bias_sc, a_bias_sn,
            s3(qb), s3(kb), s3(vb), c3(cache_b_k[i]), c3(cache_b_v[i]),
            s3(qc), s3(kc), s3(vc), c3(cache_c_k[i]), c3(cache_c_v[i]), c_bias_sc, c_bias_sn)
        xs = _merge(xs, oa.reshape(-1, wa), ob.reshape(-1, wb), oc.reshape(-1, wc), mix_norm[i], w_gate, sub,
                    *wbr, wout, i, tm=bs * ns)
        xs = _cross(xs.reshape(bs, ns, d), x_norm[i], wq, xq_gain, bf(c3(cache_mem_k[i])), bf(c3(cache_mem_v[i])),
                    wo, i, tm=ns).reshape(-1, d)
        xs = _ffn(xs, ffn2_norm[i], wg2, wu2, wd2, i, tm=bs * ns, tf=1024)
        outs["ak_s"].append(ka.reshape(bs, ns, h_a, -1))
        outs["av_s"].append(va.reshape(bs, ns, h_a, -1))
        outs["bk_s"].append(kb.reshape(bs, ns, -1, HEAD_DIM))
        outs["bv_s"].append(vb.reshape(bs, ns, -1, HEAD_DIM))
        outs["ck_s"].append(jnp.concatenate([cache_c_k[i], kc.reshape(bs, ns, -1, HEAD_DIM)], axis=1)[:, ns:])
        outs["cv_s"].append(jnp.concatenate([cache_c_v[i], vc.reshape(bs, ns, -1, HEAD_DIM)], axis=1)[:, ns:])

    st = lambda k: jnp.stack(outs[k])
    return (xp.reshape(bp, sp, d), xs.reshape(bs, ns, d),
            st("ak_p"), st("av_p"), st("bk_p"), st("bv_p"), st("ck_p"), st("cv_p"), st("mk_p"), st("mv_p"),
            st("ak_s"), st("av_s"), st("bk_s"), st("bv_s"), st("ck_s"), st("cv_s"))
```

```python
import functools
import math

import numpy as np
import jax
import jax.numpy as jnp
from jax import lax
from jax.experimental import pallas as pl
from jax.experimental.pallas import tpu as pltpu

F32 = jnp.float32
BF16 = jnp.bfloat16

EPS = 1e-6
HEAD_DIM = 64
CHUNK = 64
C_PREV = 8
REL_CLIP = 128
T5_BUCKETS = 32
T5_MAX_DIST = 128
LANES = 128
BF16_SUBLANES = 16
V7X_MXU_DIM = 256
NEG = -1e30
LOG2E = math.log2(math.e)
BOUND_SLACK = 1.001
BOUNDED_EXP2_SPAN = 100.0
STICK_SKIP = -110.0
V7X_VMEM_LIMIT_BYTES = 56 * 1024 * 1024

MIXER_A_TILE = 512
ATTN_TQ = 256
ATTN_TK = 256
BAND_W = (C_PREV + 4) * CHUNK


def _cparams(sem):
    return pltpu.CompilerParams(dimension_semantics=sem, vmem_limit_bytes=V7X_VMEM_LIMIT_BYTES)


def _rms_rows(x, g):
    return x * lax.rsqrt(jnp.mean(x * x, axis=-1, keepdims=True) + EPS) * g


def _dot(a, b):
    return jnp.dot(a, b, preferred_element_type=F32)


def _dot_nt(a, b):
    return lax.dot_general(a, b, (((1,), (1,)), ((), ())), preferred_element_type=F32)


def _keep_lanes(q, lo, hi):
    lane = lax.broadcasted_iota(jnp.int32, q.shape, 1)
    return jnp.where((lane >= lo) & (lane < hi), q.astype(F32), 0.0).astype(BF16)


def _group_rms(y, gmat):
    wide = gmat.shape[0]
    parts = []
    for c in range(y.shape[1] // wide):
        yc = y[:, c * wide:(c + 1) * wide]
        ms = _dot((yc * yc).astype(BF16), gmat)
        parts.append(yc * lax.rsqrt(ms + EPS))
    return parts[0] if len(parts) == 1 else jnp.concatenate(parts, axis=1)


def _group_mean_matrix():
    g = np.kron(np.eye(V7X_MXU_DIM // HEAD_DIM), np.ones((HEAD_DIM, HEAD_DIM))) / HEAD_DIM
    return jnp.asarray(g, BF16)


def _ffn_kernel(x_ref, g_ref, wg_ref, wu_ref, wd_ref, o_ref, h_scr):
    j = pl.program_id(1)

    @pl.when(j == 0)
    def _():
        x = x_ref[...]
        h_scr[...] = _rms_rows(x, g_ref[...]).astype(BF16)
        o_ref[...] = x

    h = h_scr[...]
    a = _dot(h, wg_ref[...])
    u = _dot(h, wu_ref[...])
    t = a * jax.nn.sigmoid(a) * u
    o_ref[...] += 0.5 * _dot(t.astype(BF16), wd_ref[...])


def _ffn(x, g, wg, wu, wd, layer, tm, tf=512):
    n, d = x.shape
    dff = wg.shape[2]
    return pl.pallas_call(
        _ffn_kernel,
        grid=(n // tm, dff // tf),
        in_specs=[
            pl.BlockSpec((tm, d), lambda i, j: (i, 0)),
            pl.BlockSpec((1, d), lambda i, j: (0, 0)),
            pl.BlockSpec((None, d, tf), lambda i, j: (layer, 0, j)),
            pl.BlockSpec((None, d, tf), lambda i, j: (layer, 0, j)),
            pl.BlockSpec((None, tf, d), lambda i, j: (layer, j, 0)),
        ],
        out_specs=pl.BlockSpec((tm, d), lambda i, j: (i, 0)),
        out_shape=jax.ShapeDtypeStruct((n, d), F32),
        scratch_shapes=[pltpu.VMEM((tm, d), BF16)],
        compiler_params=_cparams(("parallel", "arbitrary")),
        name="ffn",
    )(x, g.reshape(1, d), wg, wu, wd)


def _proj_kernel(x_ref, g_ref, w_ref, gain_ref, gmat_ref, *out_refs, segs):
    h = _rms_rows(x_ref[...], g_ref[...]).astype(BF16)
    off = 0
    out_refs = list(out_refs)
    for width, normed, gained, outs in segs:
        y = _dot(h, w_ref[:, off:off + width])
        if normed:
            y = _group_rms(y, gmat_ref[...])
        if gained:
            y = y * gain_ref[:, off:off + width]
        for _, head_width in outs:
            o_ref = out_refs.pop(0)
            if head_width:
                o_ref[...] = y.reshape(o_ref.shape).astype(o_ref.dtype)
            else:
                o_ref[...] = y.astype(o_ref.dtype)
        off += width


def _proj(x, g, w, layer, gain, segs, tm):
    n, d = x.shape
    wtot = w.shape[2]
    specs, shapes = [], []
    for width, _, _, outs in segs:
        for dt, head_width in outs:
            if head_width:
                nh = width // head_width
                specs.append(pl.BlockSpec((tm, nh, head_width), lambda i: (i, 0, 0)))
                shapes.append(jax.ShapeDtypeStruct((n, nh, head_width), dt))
            else:
                specs.append(pl.BlockSpec((tm, width), lambda i: (i, 0)))
                shapes.append(jax.ShapeDtypeStruct((n, width), dt))
    return pl.pallas_call(
        functools.partial(_proj_kernel, segs=tuple(segs)),
        grid=(n // tm,),
        in_specs=[
            pl.BlockSpec((tm, d), lambda i: (i, 0)),
            pl.BlockSpec((1, d), lambda i: (0, 0)),
            pl.BlockSpec((None, d, wtot), lambda i: (layer, 0, 0), pipeline_mode=pl.Buffered(1)),
            pl.BlockSpec((1, wtot), lambda i: (0, 0)),
            pl.BlockSpec((V7X_MXU_DIM, V7X_MXU_DIM), lambda i: (0, 0)),
        ],
        out_specs=specs,
        out_shape=shapes,
        compiler_params=_cparams(("parallel",)),
        name="proj",
    )(x, g.reshape(1, d), w, gain, _group_mean_matrix())


def _softmax_step_km(s, vt, m_ref, acc_ref):
    m_old = m_ref[...]
    m_new = jnp.maximum(m_old, jnp.max(s, axis=0, keepdims=True))
    alpha = jnp.exp2(m_old - m_new)
    p = jnp.exp2(s - m_new)
    acc_ref[...] = alpha * acc_ref[...] + _dot(vt, p.astype(BF16))
    m_ref[...] = m_new


def _cast_rows(src_ref, dst_ref, rows, dst_off=0, step=512):
    def body(i, c):
        r = pl.multiple_of(i * step, step)
        dst_ref[pl.ds(dst_off + r, step), :] = src_ref[0, pl.ds(r, step), :].astype(BF16)
        return c
    lax.fori_loop(0, rows // step, body, 0)


def _attn_a_kernel(lam_ref, brange_ref, q_ref, k_ref, v_ref, bias_ref, o_ref,
                   vtb, s_scr, knorm, m1, a1, m2, a2):
    h = pl.program_id(1)
    qi = pl.program_id(2)
    kbf = k_ref.at[0]
    seq = kbf.shape[0]
    tq = q_ref.shape[1]
    tk = vtb.shape[2]
    dv = v_ref.shape[2]

    @pl.when(qi == 0)
    def _():
        lane = lax.broadcasted_iota(jnp.int32, (tk, LANES), 1)

        def prep(j, c):
            r = pl.multiple_of(j * tk, tk)
            kb = kbf[pl.ds(r, tk), :]
            vtb[j, 0:dv, :] = v_ref[0, pl.ds(r, tk), :].astype(F32).T.astype(BF16)
            vtb[j, dv:, :] = jnp.ones((vtb.shape[1] - dv, tk), BF16)
            ksq = kb.astype(F32) * kb.astype(F32)
            n1 = jnp.max(jnp.sum(jnp.where(lane < HEAD_DIM, ksq, 0.0), axis=1, keepdims=True))
            n2 = jnp.max(jnp.sum(jnp.where(lane >= HEAD_DIM, ksq, 0.0), axis=1, keepdims=True))
            return jnp.maximum(c[0], n1), jnp.maximum(c[1], n2)
        n1, n2 = lax.fori_loop(0, seq // tk, prep, (jnp.float32(0.0), jnp.float32(0.0)))
        knorm[0] = n1
        knorm[1] = n2

    qt = q_ref[0].astype(F32).T
    sub = lax.broadcasted_iota(jnp.int32, qt.shape, 0)
    q1t = jnp.where(sub < HEAD_DIM, qt, 0.0).astype(BF16)
    q2t = jnp.where(sub >= HEAD_DIM, qt, 0.0).astype(BF16)
    states = ((m1, a1), (m2, a2))
    for m_ref, a_ref in states:
        a_ref[...] = jnp.zeros_like(a_ref)

    far_bias, bias_max, bias_min = brange_ref[3 * h], brange_ref[3 * h + 1], brange_ref[3 * h + 2]
    qsq = qt * qt
    bound1 = jnp.sqrt(jnp.sum(jnp.where(sub < HEAD_DIM, qsq, 0.0), axis=0, keepdims=True) * knorm[0])
    bound2 = jnp.sqrt(jnp.sum(jnp.where(sub >= HEAD_DIM, qsq, 0.0), axis=0, keepdims=True) * knorm[1])
    bounds = (bound1 * BOUND_SLACK + bias_max, bound2 * BOUND_SLACK + bias_max)
    spread = 2.0 * BOUND_SLACK * jnp.maximum(jnp.max(bound1), jnp.max(bound2)) + (bias_max - bias_min)
    bounded = spread <= BOUNDED_EXP2_SPAN

    @pl.when(bounded)
    def _():
        qmaps = ((q1t, bounds[0], a1), (q2t, bounds[1], a2))

        def accumulate(blocks):
            kbs = [kbf[pl.ds(pl.multiple_of(j * tk, tk), tk), :] for j, _ in blocks]
            ps = []
            for (j, tile), kb in zip(blocks, kbs):
                for qmt, shift, _ in qmaps:
                    add = (far_bias - shift) if tile is None else (tile - shift)
                    ps.append(jnp.exp2(_dot(kb, qmt) + add).astype(BF16))
            for mi, (_, _, a_ref) in enumerate(qmaps):
                total = None
                for bi, (j, _) in enumerate(blocks):
                    term = _dot(vtb[j], ps[bi * len(qmaps) + mi])
                    total = term if total is None else total + term
                a_ref[...] += total

        @pl.when(qi == 0)
        def _():
            accumulate([(qi, bias_ref[0, 0])])

        @pl.when(qi >= 1)
        def _():
            accumulate([(qi, bias_ref[0, 0]), (qi - 1, bias_ref[0, 1])])

        n_far = jnp.maximum(qi - 1, 0)
        one = n_far & 1
        two = n_far & 2

        @pl.when(one == 1)
        def _():
            accumulate([(0, None)])

        @pl.when(two == 2)
        def _():
            accumulate([(one, None), (one + 1, None)])

        def quad(i, c):
            j = one + two + 4 * i
            accumulate([(j, None), (j + 1, None), (j + 2, None), (j + 3, None)])
            return c
        lax.fori_loop(0, lax.shift_right_logical(n_far, 2), quad, 0)

    @pl.when(jnp.logical_not(bounded))
    def _():
        for m_ref, _ in states:
            m_ref[...] = jnp.full_like(m_ref, NEG)
        n_blocks = qi + 1

        def scores(t, buf):
            kb = kbf[pl.ds(pl.multiple_of((qi - t) * tk, tk), tk), :]
            s_scr[buf, 0] = _dot(kb, q1t)
            s_scr[buf, 1] = _dot(kb, q2t)

        def update(t, buf):
            bias = bias_ref[0, jnp.minimum(t, bias_ref.shape[1] - 1)]
            vt = vtb[qi - t]
            for mi, (m_ref, a_ref) in enumerate(states):
                _softmax_step_km(s_scr[buf, mi] + bias, vt, m_ref, a_ref)

        odd = n_blocks & 1

        @pl.when(odd == 1)
        def _():
            scores(0, 0)
            update(0, 0)

        n_pairs = lax.shift_right_logical(n_blocks, 1)

        @pl.when(n_pairs > 0)
        def _():
            scores(odd, 0)

        def pair(p, c):
            t0 = odd + 2 * p
            scores(t0 + 1, 1)
            update(t0, 0)
            scores(jnp.minimum(t0 + 2, qi), 0)
            update(t0 + 1, 1)
            return c
        lax.fori_loop(0, n_pairs, pair, 0)

    out_t = (a1[0:dv, :] / a1[dv:dv + 1, :]
             - lam_ref[0] * (a2[0:dv, :] / a2[dv:dv + 1, :]))
    o_ref[0] = out_t.T


def _attn_a(lam, brange, qa, ka, va, bias):
    b, s, w = qa.shape
    nh = w // LANES
    tq = tk = MIXER_A_TILE
    vrows = LANES + BF16_SUBLANES
    return pl.pallas_call(
        _attn_a_kernel,
        grid=(b, nh, s // tq),
        in_specs=[
            pl.BlockSpec(memory_space=pltpu.SMEM),
            pl.BlockSpec(memory_space=pltpu.SMEM),
            pl.BlockSpec((1, tq, LANES), lambda bi, h, qi: (bi, qi, h)),
            pl.BlockSpec((1, s, LANES), lambda bi, h, qi: (bi, 0, h)),
            pl.BlockSpec((1, s, LANES), lambda bi, h, qi: (bi, 0, h)),
            pl.BlockSpec((1,) + bias.shape[1:], lambda bi, h, qi: (h, 0, 0, 0)),
        ],
        out_specs=pl.BlockSpec((1, tq, LANES), lambda bi, h, qi: (bi, qi, h)),
        out_shape=jax.ShapeDtypeStruct((b, s, w), F32),
        scratch_shapes=[
            pltpu.VMEM((s // tk, vrows, tk), BF16),
            pltpu.VMEM((2, 2, tk, tq), F32), pltpu.SMEM((2,), F32),
            pltpu.VMEM((1, tq), F32), pltpu.VMEM((vrows, tq), F32),
            pltpu.VMEM((1, tq), F32), pltpu.VMEM((vrows, tq), F32),
        ],
        compiler_params=_cparams(("parallel", "parallel", "arbitrary")),
        name="mixer_a",
    )(lam, brange, qa, ka, va, bias)


def _stick_block(z, valid, umat, carry):
    sp = jnp.maximum(z, 0.0) + jnp.log(1.0 + jnp.exp(-jnp.abs(z)))
    log1m = -sp
    if valid is not None:
        log1m = jnp.where(valid, log1m, 0.0)
    hi = log1m.astype(BF16)
    lo = (log1m - hi.astype(F32)).astype(BF16)
    after = _dot(hi, umat) + _dot(lo, umat)
    w = jnp.exp(z - sp + after + carry)
    if valid is not None:
        w = jnp.where(valid, w, 0.0)
    return w, jnp.sum(log1m, axis=1, keepdims=True)


def _stick_block_km(z, valid, umat, carry):
    sp = jnp.maximum(z, 0.0) + jnp.log(1.0 + jnp.exp(-jnp.abs(z)))
    log1m = -sp
    if valid is not None:
        log1m = jnp.where(valid, log1m, 0.0)
    hi = log1m.astype(BF16)
    lo = (log1m - hi.astype(F32)).astype(BF16)
    after = _dot(umat, hi) + _dot(umat, lo)
    w = jnp.exp(z - sp + after + carry)
    if valid is not None:
        w = jnp.where(valid, w, 0.0)
    return w, jnp.sum(log1m, axis=0, keepdims=True)


def _attn_b_kernel(q_ref, k_ref, v_ref, u_ref, o_ref, kbf, vtb, c_scr, acc_scr, z_scr):
    step = pl.program_id(2)
    seq = kbf.shape[0]
    tk = vtb.shape[2]
    n_halves = q_ref.shape[1] // tk
    n_heads = LANES // HEAD_DIM

    @pl.when(step == 0)
    def _():
        def prep(j, c):
            r = pl.multiple_of(j * tk, tk)
            kbf[pl.ds(r, tk), :] = k_ref[0, pl.ds(r, tk), :].astype(BF16)
            vtb[j] = v_ref[0, pl.ds(r, tk), :].astype(F32).T.astype(BF16)
            return c
        lax.fori_loop(0, seq // tk, prep, 0)

    sub = lax.broadcasted_iota(jnp.int32, (LANES, tk), 0)
    qts = []
    for half in range(n_halves):
        qt = q_ref[0, half * tk:(half + 1) * tk, :].astype(F32).T
        qts.append([jnp.where((sub >= hh * HEAD_DIM) & (sub < (hh + 1) * HEAD_DIM), qt, 0.0).astype(BF16)
                    for hh in range(n_heads)])
    krow = lax.broadcasted_iota(jnp.int32, (tk, tk), 0)
    qcol = lax.broadcasted_iota(jnp.int32, (tk, tk), 1)
    umat = u_ref[...]

    def blocks(half, js, valids, first):
        kbs = [kbf[pl.ds(pl.multiple_of(j * tk, tk), tk), :] for j in js]
        cmax = None
        for hh in range(n_heads):
            carry = 0.0 if first else c_scr[half, hh]
            pv = None
            for j, kb, valid in zip(js, kbs, valids):
                w, cs = _stick_block_km(_dot(kb, qts[half][hh]), valid, umat, carry)
                term = _dot(vtb[j], w.astype(BF16))
                pv = term if pv is None else pv + term
                carry = carry + cs
            acc_scr[half, hh] = pv if first else acc_scr[half, hh] + pv
            c_scr[half, hh] = carry
            cm = jnp.max(carry)
            cmax = cm if cmax is None else jnp.maximum(cmax, cm)
        return cmax

    own_valid = krow < qcol
    everywhere = krow >= 0

    chains = []
    for half in range(n_halves):
        qb = step * n_halves + half
        prev_valid = None if half >= 1 else everywhere & (qb >= 1)
        for hh in range(n_heads):
            for j, valid in ((qb, own_valid), (jnp.maximum(qb - 1, 0), prev_valid)):
                chains.append((half, hh, j, valid))
    for c, (half, hh, j, valid) in enumerate(chains):
        z_scr[c] = _dot(kbf[pl.ds(pl.multiple_of(j * tk, tk), tk), :], qts[half][hh])
    col_sums = []
    for c, (half, hh, j, valid) in enumerate(chains):
        z = z_scr[c]
        sp = jnp.maximum(z, 0.0) + jnp.log(1.0 + jnp.exp(-jnp.abs(z)))
        log1m = -sp if valid is None else jnp.where(valid, -sp, 0.0)
        hi = log1m.astype(BF16)
        lo = (log1m - hi.astype(F32)).astype(BF16)
        z_scr[c] = z - sp + _dot(umat, hi) + _dot(umat, lo)
        col_sums.append(jnp.sum(log1m, axis=0, keepdims=True))
    cmaxes = []
    for half in range(n_halves):
        cmax = None
        for hh in range(n_heads):
            carry, pv = 0.0, None
            for c, (ch, chh, j, valid) in enumerate(chains):
                if (ch, chh) != (half, hh):
                    continue
                w = jnp.exp(z_scr[c] + carry)
                if valid is not None:
                    w = jnp.where(valid, w, 0.0)
                term = _dot(vtb[j], w.astype(BF16))
                pv = term if pv is None else pv + term
                carry = carry + col_sums[c]
            acc_scr[half, hh] = pv
            c_scr[half, hh] = carry
            cm = jnp.max(carry)
            cmax = cm if cmax is None else jnp.maximum(cmax, cm)
        cmaxes.append(cmax)

    for half in range(n_halves):
        qb = step * n_halves + half

        def cond(st):
            j, cmax = st
            return (j >= 0) & (cmax > STICK_SKIP)

        def body(st, half=half):
            j, _ = st
            return j - 2, blocks(half, [j, jnp.maximum(j - 1, 0)], [None, everywhere & (j >= 1)], False)

        lax.while_loop(cond, body, (qb - 2, cmaxes[half]))
        out_t = jnp.where(sub < HEAD_DIM, acc_scr[half, 0], acc_scr[half, 1])
        o_ref[0, half * tk:(half + 1) * tk, :] = out_t.T.astype(o_ref.dtype)


def _strict_lower(n):
    return jnp.asarray(np.tril(np.ones((n, n)), -1), BF16)


def _strict_upper(n):
    return jnp.asarray(np.triu(np.ones((n, n)), 1), BF16)


def _attn_b(qb, kb, vb):
    b, s, w = qb.shape
    n_halves = 4
    tq = n_halves * ATTN_TK
    return pl.pallas_call(
        _attn_b_kernel,
        grid=(b, w // LANES, s // tq),
        in_specs=[
            pl.BlockSpec((1, tq, LANES), lambda bi, h, qi: (bi, qi, h)),
            pl.BlockSpec((1, s, LANES), lambda bi, h, qi: (bi, 0, h)),
            pl.BlockSpec((1, s, LANES), lambda bi, h, qi: (bi, 0, h)),
            pl.BlockSpec((ATTN_TK, ATTN_TK), lambda bi, h, qi: (0, 0)),
        ],
        out_specs=pl.BlockSpec((1, tq, LANES), lambda bi, h, qi: (bi, qi, h)),
        out_shape=jax.ShapeDtypeStruct((b, s, w), BF16),
        scratch_shapes=[
            pltpu.VMEM((s, LANES), BF16), pltpu.VMEM((s // ATTN_TK, LANES, ATTN_TK), BF16),
            pltpu.VMEM((n_halves, LANES // HEAD_DIM, 1, ATTN_TK), F32),
            pltpu.VMEM((n_halves, LANES // HEAD_DIM, LANES, ATTN_TK), F32),
            pltpu.VMEM((2 * n_halves * (LANES // HEAD_DIM), ATTN_TK, ATTN_TK), F32),
        ],
        compiler_params=_cparams(("parallel", "parallel", "arbitrary")),
        name="mixer_b",
    )(qb, kb, vb, _strict_upper(ATTN_TK))


def _attn_c_kernel(q_ref, k_ref, v_ref, bias_ref, o_ref, kbf, vbf, s_scr):
    step = pl.program_id(2)
    seq = k_ref.shape[1]
    tq = ATTN_TQ
    n_blocks = q_ref.shape[1] // tq
    pad = C_PREV * CHUNK
    n_heads = LANES // HEAD_DIM

    @pl.when(step == 0)
    def _():
        kbf[0:pad, :] = jnp.zeros((pad, LANES), BF16)
        vbf[0:pad, 0:LANES] = jnp.zeros((pad, LANES), BF16)
        vbf[:, LANES:] = jnp.ones((vbf.shape[0], vbf.shape[1] - LANES), BF16)
        _cast_rows(k_ref, kbf, seq, dst_off=pad)

        def vcast(i, c):
            r = pl.multiple_of(i * tq, tq)
            vbf[pl.ds(pad + r, tq), 0:LANES] = v_ref[0, pl.ds(r, tq), :].astype(BF16)
            return c
        lax.fori_loop(0, seq // tq, vcast, 0)

    lane = lax.broadcasted_iota(jnp.int32, (tq, LANES), 1)
    col = lax.broadcasted_iota(jnp.int32, (tq, BAND_W), 1)
    for blk in range(n_blocks):
        qb = step * n_blocks + blk
        q = q_ref[0, blk * tq:(blk + 1) * tq, :]
        kw = kbf[pl.ds(pl.multiple_of(qb * tq, tq), BAND_W), :]
        in_seq = col >= pad - qb * tq
        for hh in range(n_heads):
            qm = _keep_lanes(q, hh * HEAD_DIM, (hh + 1) * HEAD_DIM)
            s_scr[blk * n_heads + hh] = jnp.where(in_seq, _dot_nt(qm, kw) + bias_ref[hh], NEG)
    for blk in range(n_blocks):
        qb = step * n_blocks + blk
        vw = vbf[pl.ds(pl.multiple_of(qb * tq, tq), BAND_W), :]
        outs = []
        for hh in range(n_heads):
            s = s_scr[blk * n_heads + hh]
            p = jnp.exp(s - jnp.max(s, axis=1, keepdims=True))
            acc = _dot(p.astype(BF16), vw)
            outs.append(acc[:, 0:LANES] / acc[:, LANES:])
        o_ref[0, blk * tq:(blk + 1) * tq, :] = jnp.where(lane < HEAD_DIM, outs[0], outs[1]).astype(o_ref.dtype)


def _attn_c(qc, kc, vc, bias):
    b, s, w = qc.shape
    n_blocks = 4
    tq = n_blocks * ATTN_TQ
    pad = C_PREV * CHUNK
    hp = LANES // HEAD_DIM
    return pl.pallas_call(
        _attn_c_kernel,
        grid=(b, w // LANES, s // tq),
        in_specs=[
            pl.BlockSpec((1, tq, LANES), lambda bi, h, qi: (bi, qi, h)),
            pl.BlockSpec((1, s, LANES), lambda bi, h, qi: (bi, 0, h)),
            pl.BlockSpec((1, s, LANES), lambda bi, h, qi: (bi, 0, h)),
            pl.BlockSpec((hp, ATTN_TQ, BAND_W), lambda bi, h, qi: (h, 0, 0)),
        ],
        out_specs=pl.BlockSpec((1, tq, LANES), lambda bi, h, qi: (bi, qi, h)),
        out_shape=jax.ShapeDtypeStruct((b, s, w), BF16),
        scratch_shapes=[pltpu.VMEM((s + pad, LANES), BF16), pltpu.VMEM((s + pad, 2 * LANES), BF16),
                        pltpu.VMEM((n_blocks * hp, ATTN_TQ, BAND_W), F32)],
        compiler_params=_cparams(("parallel", "parallel", "arbitrary")),
        name="mixer_c",
    )(qc, kc, vc, bias)


def _merge_kernel(x_ref, oa_ref, ob_ref, oc_ref, g_ref, wgate_ref, sub_ref,
                  wa_ref, wb_ref, wc_ref, wout_ref, o_ref):
    x = x_ref[...]
    d = x.shape[1]
    h = _rms_rows(x, g_ref[...]).astype(BF16)
    oa = oa_ref[...]
    parts = []
    for c in range(oa.shape[1] // LANES):
        oc_ = oa[:, c * LANES:(c + 1) * LANES]
        parts.append(oc_ * lax.rsqrt(jnp.mean(oc_ * oc_, axis=-1, keepdims=True) + EPS))
    oan = (jnp.concatenate(parts, axis=1) * sub_ref[...]).astype(BF16)
    merged = jax.nn.sigmoid(_dot(h, wgate_ref[:, 0:d])) * _dot(oan, wa_ref[...])
    merged += jax.nn.sigmoid(_dot(h, wgate_ref[:, d:2 * d])) * _dot(ob_ref[...], wb_ref[...])
    merged += jax.nn.sigmoid(_dot(h, wgate_ref[:, 2 * d:3 * d])) * _dot(oc_ref[...], wc_ref[...])
    o_ref[...] = x + _dot(merged.astype(BF16), wout_ref[...])


def _merge(x, oa, ob, oc, g, wgate, sub, wa, wb, wc, wout, layer, tm):
    n, d = x.shape
    full = lambda a: pl.BlockSpec(a.shape, lambda i: (0,) * a.ndim, pipeline_mode=pl.Buffered(1))
    of_layer = lambda a: pl.BlockSpec((None,) + a.shape[1:], lambda i: (layer,) + (0,) * (a.ndim - 1),
                                      pipeline_mode=pl.Buffered(1))
    rows = lambda a: pl.BlockSpec((tm, a.shape[1]), lambda i: (i, 0))
    g = g.reshape(1, d)
    return pl.pallas_call(
        _merge_kernel,
        grid=(n // tm,),
        in_specs=[rows(x), rows(oa), rows(ob), rows(oc), full(g), of_layer(wgate), full(sub),
                  of_layer(wa), of_layer(wb), of_layer(wc), of_layer(wout)],
        out_specs=rows(x),
        out_shape=jax.ShapeDtypeStruct((n, d), F32),
        compiler_params=_cparams(("parallel",)),
        name="merge",
    )(x, oa, ob, oc, g, wgate, sub, wa, wb, wc, wout)


def _cross_kernel(x_ref, g_ref, wq_ref, gain_ref, gmat_ref, mk_ref, mv_ref, wo_ref, o_ref):
    x = x_ref[0]
    h = _rms_rows(x, g_ref[...]).astype(BF16)
    q = (_group_rms(_dot(h, wq_ref[...]), gmat_ref[...]) * gain_ref[...]).astype(BF16)
    mk = mk_ref[0]
    mv = mv_ref[0]
    lane = lax.broadcasted_iota(jnp.int32, q.shape, 1)
    o = jnp.zeros(q.shape, F32)
    for hh in range(q.shape[1] // HEAD_DIM):
        in_head = (lane >= hh * HEAD_DIM) & (lane < (hh + 1) * HEAD_DIM)
        s = _dot_nt(_keep_lanes(q, hh * HEAD_DIM, (hh + 1) * HEAD_DIM), mk)
        p = jnp.exp(s - jnp.max(s, axis=1, keepdims=True))
        l = jnp.sum(p, axis=1, keepdims=True)
        o = jnp.where(in_head, _dot(p.astype(BF16), mv) / l, o)
    o_ref[0] = x + _dot(o.astype(BF16), wo_ref[...])


def _cross(x, g, wq, gain, mk, mv, wo, layer, tm):
    b, s, d = x.shape
    full = lambda a: pl.BlockSpec(a.shape, lambda bi, i: (0,) * a.ndim)
    of_layer = lambda a: pl.BlockSpec((None,) + a.shape[1:], lambda bi, i: (layer,) + (0,) * (a.ndim - 1))
    g = g.reshape(1, d)
    gmat = _group_mean_matrix()
    return pl.pallas_call(
        _cross_kernel,
        grid=(b, s // tm),
        in_specs=[
            pl.BlockSpec((1, tm, d), lambda bi, i: (bi, i, 0)),
            full(g), of_layer(wq), full(gain), full(gmat),
            pl.BlockSpec((1,) + mk.shape[1:], lambda bi, i: (bi, 0, 0)),
            pl.BlockSpec((1,) + mv.shape[1:], lambda bi, i: (bi, 0, 0)),
            of_layer(wo),
        ],
        out_specs=pl.BlockSpec((1, tm, d), lambda bi, i: (bi, i, 0)),
        out_shape=jax.ShapeDtypeStruct((b, s, d), F32),
        compiler_params=_cparams(("parallel", "parallel")),
        name="cross",
    )(x, g, wq, gain, gmat, mk, mv, wo)


def _sample_attn_kernel(lam_ref,
                        qa_ref, kan_ref, van_ref, kac_ref, vac_ref, bac_ref, ban_ref,
                        qb_ref, kbn_ref, vbn_ref, kbc_ref, vbc_ref, ubig_ref, usmall_ref,
                        qc_ref, kcn_ref, vcn_ref, kcc_ref, vcc_ref, bcc_ref, bcn_ref,
                        oa_ref, ob_ref, oc_ref):
    lam = lam_ref[0]
    ns = qa_ref.shape[1]
    bf = lambda r: r[0].astype(BF16)

    def heads(q, width):
        for hh in range(LANES // width):
            yield hh, None, _keep_lanes(q, hh * width, (hh + 1) * width)

    def softmax2(s_c, s_n, exp=jnp.exp):
        m = jnp.maximum(jnp.max(s_c, axis=1, keepdims=True), jnp.max(s_n, axis=1, keepdims=True))
        p_c = exp(s_c - m)
        p_n = exp(s_n - m)
        inv = 1.0 / (jnp.sum(p_c, axis=1, keepdims=True) + jnp.sum(p_n, axis=1, keepdims=True))
        return p_c * inv, p_n * inv

    for h in range(qa_ref.shape[2] // LANES):
        sl = slice(h * LANES, (h + 1) * LANES)
        q = qa_ref[0, :, sl]
        k_c = kac_ref[0, :, h, :].astype(BF16)
        k_n = kan_ref[0, :, sl].astype(BF16)
        maps = []
        for _, _, qm in heads(q, HEAD_DIM):
            maps.append(softmax2(_dot_nt(qm, k_c) + bac_ref[h], _dot_nt(qm, k_n) + ban_ref[h], exp=jnp.exp2))
        a_c = (maps[0][0] - lam * maps[1][0]).astype(BF16)
        a_n = (maps[0][1] - lam * maps[1][1]).astype(BF16)
        oa_ref[0, :, sl] = (_dot(a_c, vac_ref[0, :, h, :].astype(BF16))
                            + _dot(a_n, van_ref[0, :, sl].astype(BF16)))

    past = kbc_ref.shape[1]
    tk = ubig_ref.shape[0]
    row = lax.broadcasted_iota(jnp.int32, (ns, ns), 0)
    col = lax.broadcasted_iota(jnp.int32, (ns, ns), 1)
    for pr in range(qb_ref.shape[2] // LANES):
        sl = slice(pr * LANES, (pr + 1) * LANES)
        q = qb_ref[0, :, sl]
        k_n = kbn_ref[0, :, sl].astype(BF16)
        v_n = vbn_ref[0, :, sl].astype(BF16)
        outs = []
        for _, _, qm in heads(q, HEAD_DIM):
            w, carry = _stick_block(_dot_nt(qm, k_n), col < row, usmall_ref[...], 0.0)
            acc = _dot(w.astype(BF16), v_n)
            for j in range(past // tk - 1, -1, -1):
                k_c = kbc_ref[0, j * tk:(j + 1) * tk, sl].astype(BF16)
                v_c = vbc_ref[0, j * tk:(j + 1) * tk, sl].astype(BF16)
                w, rs = _stick_block(_dot_nt(qm, k_c), None, ubig_ref[...], carry)
                acc += _dot(w.astype(BF16), v_c)
                carry = carry + rs
            outs.append(acc)
        lane = lax.broadcasted_iota(jnp.int32, q.shape, 1)
        ob_ref[0, :, sl] = jnp.where(lane < HEAD_DIM, outs[0], outs[1]).astype(ob_ref.dtype)

    for pr in range(qc_ref.shape[2] // LANES):
        sl = slice(pr * LANES, (pr + 1) * LANES)
        q = qc_ref[0, :, sl]
        k_c = kcc_ref[0, :, sl].astype(BF16)
        k_n = kcn_ref[0, :, sl].astype(BF16)
        v_c = vcc_ref[0, :, sl].astype(BF16)
        v_n = vcn_ref[0, :, sl].astype(BF16)
        outs = []
        for hh, _, qm in heads(q, HEAD_DIM):
            hd = pr * (LANES // HEAD_DIM) + hh
            p_c, p_n = softmax2(_dot_nt(qm, k_c) + bcc_ref[hd], _dot_nt(qm, k_n) + bcn_ref[hd])
            outs.append(_dot(p_c.astype(BF16), v_c) + _dot(p_n.astype(BF16), v_n))
        lane = lax.broadcasted_iota(jnp.int32, q.shape, 1)
        oc_ref[0, :, sl] = jnp.where(lane < HEAD_DIM, outs[0], outs[1]).astype(oc_ref.dtype)


def _sample_attn(layer, lam, qa, ka, va, cak, cav, bac, ban, qb, kb, vb, cbk, cbv,
                 qc, kc, vc, cck, ccv, bcc, bcn):
    b, ns, _ = qa.shape
    per_b = lambda a: pl.BlockSpec((1,) + a.shape[1:], lambda bi: (bi,) + (0,) * (a.ndim - 1))
    full = lambda a: pl.BlockSpec(a.shape, lambda bi: (0,) * a.ndim)
    layer_b = lambda a: pl.BlockSpec((None, 1) + a.shape[2:], lambda bi: (layer, bi) + (0,) * (a.ndim - 2))
    ubig = _strict_lower(ATTN_TK)
    usmall = _strict_lower(ns)
    args = [qa, ka, va, cak, cav, bac, ban, qb, kb, vb, cbk, cbv, ubig, usmall,
            qc, kc, vc, cck, ccv, bcc, bcn]
    specs = [per_b(qa), per_b(ka), per_b(va), layer_b(cak), layer_b(cav), full(bac), full(ban),
             per_b(qb), per_b(kb), per_b(vb), per_b(cbk), per_b(cbv), full(ubig), full(usmall),
             per_b(qc), per_b(kc), per_b(vc), per_b(cck), per_b(ccv), full(bcc), full(bcn)]
    return pl.pallas_call(
        _sample_attn_kernel,
        grid=(b,),
        in_specs=[pl.BlockSpec(memory_space=pltpu.SMEM)] + specs,
        out_specs=[per_b(qa), per_b(qb), per_b(qc)],
        out_shape=[jax.ShapeDtypeStruct(qa.shape, F32), jax.ShapeDtypeStruct(qb.shape, BF16),
                   jax.ShapeDtypeStruct(qc.shape, BF16)],
        compiler_params=_cparams(("parallel",)),
        name="sample_mixers",
    )(lam, *args)


def _t5_bucket_np(rel):
    half = T5_BUCKETS // 2
    max_exact = half // 2
    n = np.abs(rel)
    nf = np.maximum(n, 1).astype(np.float64)
    large = max_exact + (np.log(nf / max_exact) / math.log(T5_MAX_DIST / max_exact)
                         * (half - max_exact)).astype(np.int64)
    large = np.minimum(large, half - 1)
    return np.where(rel > 0, half, 0) + np.where(n < max_exact, n, large)


def _toeplitz(lookup, n_rows, n_cols):
    period = n_rows + n_cols
    slot = np.arange(period)
    diff = np.minimum((slot + n_rows - 1) % period - (n_rows - 1), n_cols - 1)
    vec = lookup(diff).astype(F32)
    flat = jnp.tile(vec, (1, n_rows))[:, :n_rows * (period - 1)]
    return flat.reshape(vec.shape[0], n_rows, period - 1)[:, :, :n_cols]


def _t5_bias_table(t5_bias, qpos, kpos, key_major=False):
    t5_rows = lambda rel: t5_bias[_t5_bucket_np(rel)].T
    mask = (kpos[None, :] // CHUNK) <= (qpos[:, None] // CHUNK)
    if key_major:
        table = _toeplitz(lambda dd: t5_rows(kpos[0] - qpos[0] - dd), len(kpos), len(qpos))
        mask = mask.T
    else:
        table = _toeplitz(lambda dd: t5_rows(kpos[0] - qpos[0] + dd), len(qpos), len(kpos))
    return jnp.where(jnp.asarray(mask)[None], table, NEG)


def _band_bias_table(rel_table, qpos, kpos):
    lookup = lambda dd: rel_table[:, np.clip(kpos[0] - qpos[0] + dd, -REL_CLIP, REL_CLIP) + REL_CLIP]
    table = _toeplitz(lookup, len(qpos), len(kpos))
    qc = qpos[:, None] // CHUNK
    kc = kpos[None, :] // CHUNK
    mask = (kpos[None, :] >= 0) & (kc <= qc) & (kc >= qc - C_PREV)
    return jnp.where(jnp.asarray(mask)[None], table, NEG)


def _lambda_init(layer):
    return 0.8 - 0.6 * math.exp(-0.3 * layer)


def kernel(x_prompt, x_sample, mem_prompt, cache_a_k, cache_a_v, cache_b_k, cache_b_v, cache_c_k, cache_c_v, cache_mem_k, cache_mem_v, t5_bias, ffn1_norm, ffn1_wg, ffn1_wu, ffn1_wd, mix_norm, w_in, a_qnorm, a_knorm, a_lq1, a_lk1, a_lq2, a_lk2, a_subln, c_qnorm, c_knorm, c_rel_bias, w_br_a, w_br_b, w_br_c, w_out, x_norm, mem_norm, x_wq, x_wkv, x_qnorm, x_knorm, x_wo, ffn2_norm, ffn2_wg, ffn2_wu, ffn2_wd):
    bp, sp, d = x_prompt.shape
    bs, ns, _ = x_sample.shape
    depth = w_in.shape[0]
    past = cache_a_k.shape[2]
    w_buf = cache_c_k.shape[2]
    n_mem = mem_prompt.shape[1]
    wa = cache_a_k.shape[3] * cache_a_k.shape[4]
    wb = cache_b_k.shape[3] * cache_b_k.shape[4]
    wc = cache_c_k.shape[3] * cache_c_k.shape[4]
    wx = cache_mem_k.shape[3] * cache_mem_k.shape[4]
    h_a = cache_a_k.shape[3]
    n_qkv = 3 * (wa + wb + wc)
    w_keep = min(C_PREV * CHUNK, sp)
    scale = HEAD_DIM ** -0.5
    assert sp % ATTN_TQ == 0 and ATTN_TQ == ATTN_TK and ATTN_TQ == 4 * CHUNK and sp >= w_keep
    assert past % ATTN_TK == 0
    assert sp % MIXER_A_TILE == 0 and MIXER_A_TILE % CHUNK == 0 and T5_MAX_DIST <= MIXER_A_TILE

    tile = lambda g, reps: jnp.tile(g.astype(F32), reps)
    bf = lambda a: a.astype(BF16)

    loc_a = np.arange(MIXER_A_TILE)
    loc = np.arange(ATTN_TQ)
    a_far = t5_bias[T5_BUCKETS // 2 - 1].astype(F32)
    a_bias = LOG2E * jnp.stack([
        _t5_bias_table(t5_bias, loc_a + MIXER_A_TILE, loc_a + MIXER_A_TILE, key_major=True),
        _t5_bias_table(t5_bias, loc_a + MIXER_A_TILE, loc_a, key_major=True),
        jnp.broadcast_to(a_far[:, None, None], (h_a, MIXER_A_TILE, MIXER_A_TILE))], axis=1)
    t5_log2 = LOG2E * t5_bias.astype(F32)
    a_brange = jnp.stack([LOG2E * a_far, jnp.max(t5_log2, axis=0), jnp.min(t5_log2, axis=0)], axis=1).reshape(-1)
    qpos_s = past + np.arange(ns)
    a_bias_sc = LOG2E * _t5_bias_table(t5_bias, qpos_s, np.arange(past))
    a_bias_sn = LOG2E * _t5_bias_table(t5_bias, qpos_s, qpos_s)
    kpos_sb = past - w_buf + np.arange(w_buf + ns)

    xp = x_prompt.reshape(bp * sp, d)
    xs = x_sample.reshape(bs * ns, d)
    mem = mem_prompt.reshape(bp * n_mem, d)
    outs = {k: [] for k in ("ak_p", "av_p", "bk_p", "bv_p", "ck_p", "cv_p", "mk_p", "mv_p",
                            "ak_s", "av_s", "bk_s", "bv_s", "ck_s", "cv_s")}
    flat = lambda dt: ((dt, 0),)
    dv_a = wa // h_a
    a_kv_prompt = ((F32, dv_a), (BF16, 0))
    b_kv_prompt = ((F32, HEAD_DIM), (BF16, 0))
    c_segs = [(wc, True, True, flat(BF16)), (wc, True, True, flat(F32)), (wc, False, False, flat(F32))]
    qkv_segs_p = [(wa, True, True, flat(BF16)), (wa, True, True, a_kv_prompt), (wa, False, False, a_kv_prompt),
                  (wb, False, True, flat(BF16)), (wb, False, False, b_kv_prompt), (wb, False, False, b_kv_prompt)] + c_segs
    qkv_segs_s = [(wa, True, True, flat(BF16)), (wa, True, True, flat(F32)), (wa, False, False, flat(F32)),
                  (wb, False, True, flat(BF16)), (wb, False, False, flat(F32)), (wb, False, False, flat(F32))] + c_segs

    wg1, wu1, wd1 = bf(ffn1_wg), bf(ffn1_wu), bf(ffn1_wd)
    wg2, wu2, wd2 = bf(ffn2_wg), bf(ffn2_wu), bf(ffn2_wd)
    w_qkv, w_gate = bf(w_in[:, :, :n_qkv]), bf(w_in[:, :, n_qkv:])
    wbr = bf(w_br_a), bf(w_br_b), bf(w_br_c)
    wout = bf(w_out)
    wq, wo, wkv = bf(x_wq), bf(x_wo), bf(x_wkv)

    for i in range(depth):
        qkv_gain = jnp.concatenate([
            tile(a_qnorm[i], wa // HEAD_DIM) * (scale * LOG2E), tile(a_knorm[i], wa // HEAD_DIM), jnp.ones((wa,), F32),
            jnp.full((wb,), scale, F32), jnp.ones((2 * wb,), F32),
            tile(c_qnorm[i], wc // HEAD_DIM) * scale, tile(c_knorm[i], wc // HEAD_DIM), jnp.ones((wc,), F32),
        ]).reshape(1, n_qkv)
        sub = (tile(a_subln[i], h_a) * (1.0 - _lambda_init(i))).reshape(1, wa)
        xq_gain = (tile(x_qnorm[i], wx // HEAD_DIM) * scale).reshape(1, wx)
        dot64 = lambda a, b: jnp.exp(jnp.sum(a.astype(F32) * b.astype(F32)))
        lam = (dot64(a_lq1[i], a_lk1[i]) - dot64(a_lq2[i], a_lk2[i]) + _lambda_init(i)).reshape(1)
        c_bias_p = _band_bias_table(c_rel_bias[i], C_PREV * CHUNK + loc, np.arange(BAND_W))
        c_bias_sc = _band_bias_table(c_rel_bias[i], qpos_s, kpos_sb[:w_buf])
        c_bias_sn = _band_bias_table(c_rel_bias[i], qpos_s, kpos_sb[w_buf:])

        xp = _ffn(xp, ffn1_norm[i], wg1, wu1, wd1, i, tm=1024, tf=1024)
        qa, ka, ka_bf, va, va_bf, qb, kb, kb_bf, vb, vb_bf, qc, kc, vc = _proj(
            xp, mix_norm[i], w_qkv, i, qkv_gain, qkv_segs_p, tm=512)
        r3 = lambda a: a.reshape(bp, sp, a.shape[-1])
        oa = _attn_a(lam, a_brange, r3(qa), r3(ka_bf), r3(va_bf), a_bias)
        ob = _attn_b(r3(qb), r3(kb_bf), r3(vb_bf))
        oc = _attn_c(r3(qc), r3(kc), r3(vc), c_bias_p)
        xp = _merge(xp, oa.reshape(-1, wa), ob.reshape(-1, wb), oc.reshape(-1, wc), mix_norm[i], w_gate, sub,
                    *wbr, wout, i, tm=1024)
        mk, mv = _proj(mem, mem_norm[i], wkv, i,
                       jnp.concatenate([tile(x_knorm[i], wx // HEAD_DIM), jnp.ones((wx,), F32)]).reshape(1, 2 * wx),
                       [(wx, True, True, flat(F32)), (wx, False, False, flat(F32))], tm=n_mem)
        mk3, mv3 = mk.reshape(bp, n_mem, wx), mv.reshape(bp, n_mem, wx)
        xp = _cross(xp.reshape(bp, sp, d), x_norm[i], wq, xq_gain, bf(mk3), bf(mv3), wo, i, tm=1024).reshape(-1, d)
        xp = _ffn(xp, ffn2_norm[i], wg2, wu2, wd2, i, tm=1024, tf=1024)
        outs["ak_p"].append(ka.reshape(bp, sp, h_a, -1))
        outs["av_p"].append(va.reshape(bp, sp, h_a, -1))
        outs["bk_p"].append(kb.reshape(bp, sp, -1, HEAD_DIM))
        outs["bv_p"].append(vb.reshape(bp, sp, -1, HEAD_DIM))
        outs["ck_p"].append(r3(kc)[:, sp - w_keep:].reshape(bp, w_keep, -1, HEAD_DIM))
        outs["cv_p"].append(r3(vc)[:, sp - w_keep:].reshape(bp, w_keep, -1, HEAD_DIM))
        outs["mk_p"].append(mk3.reshape(bp, n_mem, -1, HEAD_DIM))
        outs["mv_p"].append(mv3.reshape(bp, n_mem, -1, HEAD_DIM))

        xs = _ffn(xs, ffn1_norm[i], wg1, wu1, wd1, i, tm=bs * ns, tf=1024)
        qa, ka, va, qb, kb, vb, qc, kc, vc = _proj(xs, mix_norm[i], w_qkv, i, qkv_gain, qkv_segs_s, tm=bs * ns)
        s3 = lambda a: a.reshape(bs, ns, a.shape[-1])
        c3 = lambda a: a.reshape(bs, a.shape[1], -1)
        oa, ob, oc = _sample_attn(
            i, lam, s3(qa), s3(ka), s3(va), cache_a_k, cache_a_v, a_bias_sc, a_bias_sn,
            s3(qb), s3(kb), s3(vb), c3(cache_b_k[i]), c3(cache_b_v[i]),
            s3(qc), s3(kc), s3(vc), c3(cache_c_k[i]), c3(cache_c_v[i]), c_bias_sc, c_bias_sn)
        xs = _merge(xs, oa.reshape(-1, wa), ob.reshape(-1, wb), oc.reshape(-1, wc), mix_norm[i], w_gate, sub,
                    *wbr, wout, i, tm=bs * ns)
        xs = _cross(xs.reshape(bs, ns, d), x_norm[i], wq, xq_gain, bf(c3(cache_mem_k[i])), bf(c3(cache_mem_v[i])),
                    wo, i, tm=ns).reshape(-1, d)
        xs = _ffn(xs, ffn2_norm[i], wg2, wu2, wd2, i, tm=bs * ns, tf=1024)
        outs["ak_s"].append(ka.reshape(bs, ns, h_a, -1))
        outs["av_s"].append(va.reshape(bs, ns, h_a, -1))
        outs["bk_s"].append(kb.reshape(bs, ns, -1, HEAD_DIM))
        outs["bv_s"].append(vb.reshape(bs, ns, -1, HEAD_DIM))
        outs["ck_s"].append(jnp.concatenate([cache_c_k[i], kc.reshape(bs, ns, -1, HEAD_DIM)], axis=1)[:, ns:])
        outs["cv_s"].append(jnp.concatenate([cache_c_v[i], vc.reshape(bs, ns, -1, HEAD_DIM)], axis=1)[:, ns:])

    st = lambda k: jnp.stack(outs[k])
    return (xp.reshape(bp, sp, d), xs.reshape(bs, ns, d),
            st("ak_p"), st("av_p"), st("bk_p"), st("bv_p"), st("ck_p"), st("cv_p"), st("mk_p"), st("mv_p"),
            st("ak_s"), st("av_s"), st("bk_s"), st("bv_s"), st("ck_s"), st("cv_s"))
```

```python
import functools
import math

import numpy as np
import jax
import jax.numpy as jnp
from jax import lax
from jax.experimental import pallas as pl
from jax.experimental.pallas import tpu as pltpu

F32 = jnp.float32
BF16 = jnp.bfloat16

EPS = 1e-6
HEAD_DIM = 64
CHUNK = 64
C_PREV = 8
REL_CLIP = 128
T5_BUCKETS = 32
T5_MAX_DIST = 128
LANES = 128
BF16_SUBLANES = 16
V7X_MXU_DIM = 256
NEG = -1e30
LOG2E = math.log2(math.e)
BOUND_SLACK = 1.001
BOUNDED_EXP2_SPAN = 100.0
STICK_SKIP = -110.0
V7X_VMEM_LIMIT_BYTES = 56 * 1024 * 1024

MIXER_A_TILE = 512
ATTN_TQ = 256
ATTN_TK = 256
BAND_W = (C_PREV + 4) * CHUNK


def _cparams(sem):
    return pltpu.CompilerParams(dimension_semantics=sem, vmem_limit_bytes=V7X_VMEM_LIMIT_BYTES)


def _rms_rows(x, g):
    return x * lax.rsqrt(jnp.mean(x * x, axis=-1, keepdims=True) + EPS) * g


def _dot(a, b):
    return jnp.dot(a, b, preferred_element_type=F32)


def _dot_nt(a, b):
    return lax.dot_general(a, b, (((1,), (1,)), ((), ())), preferred_element_type=F32)


def _keep_lanes(q, lo, hi):
    lane = lax.broadcasted_iota(jnp.int32, q.shape, 1)
    return jnp.where((lane >= lo) & (lane < hi), q.astype(F32), 0.0).astype(BF16)


def _group_rms(y, gmat):
    wide = gmat.shape[0]
    parts = []
    for c in range(y.shape[1] // wide):
        yc = y[:, c * wide:(c + 1) * wide]
        ms = _dot((yc * yc).astype(BF16), gmat)
        parts.append(yc * lax.rsqrt(ms + EPS))
    return parts[0] if len(parts) == 1 else jnp.concatenate(parts, axis=1)


def _group_mean_matrix():
    g = np.kron(np.eye(V7X_MXU_DIM // HEAD_DIM), np.ones((HEAD_DIM, HEAD_DIM))) / HEAD_DIM
    return jnp.asarray(g, BF16)


def _ffn_kernel(x_ref, g_ref, wg_ref, wu_ref, wd_ref, o_ref, h_scr):
    j = pl.program_id(1)

    @pl.when(j == 0)
    def _():
        x = x_ref[...]
        h_scr[...] = _rms_rows(x, g_ref[...]).astype(BF16)
        o_ref[...] = x

    h = h_scr[...]
    a = _dot(h, wg_ref[...])
    u = _dot(h, wu_ref[...])
    t = a * jax.nn.sigmoid(a) * u
    o_ref[...] += 0.5 * _dot(t.astype(BF16), wd_ref[...])


def _ffn(x, g, wg, wu, wd, layer, tm, tf=512):
    n, d = x.shape
    dff = wg.shape[2]
    return pl.pallas_call(
        _ffn_kernel,
        grid=(n // tm, dff // tf),
        in_specs=[
            pl.BlockSpec((tm, d), lambda i, j: (i, 0)),
            pl.BlockSpec((1, d), lambda i, j: (0, 0)),
            pl.BlockSpec((None, d, tf), lambda i, j: (layer, 0, j)),
            pl.BlockSpec((None, d, tf), lambda i, j: (layer, 0, j)),
            pl.BlockSpec((None, tf, d), lambda i, j: (layer, j, 0)),
        ],
        out_specs=pl.BlockSpec((tm, d), lambda i, j: (i, 0)),
        out_shape=jax.ShapeDtypeStruct((n, d), F32),
        scratch_shapes=[pltpu.VMEM((tm, d), BF16)],
        compiler_params=_cparams(("parallel", "arbitrary")),
        name="ffn",
    )(x, g.reshape(1, d), wg, wu, wd)


def _proj_kernel(x_ref, g_ref, w_ref, gain_ref, gmat_ref, *out_refs, segs):
    h = _rms_rows(x_ref[...], g_ref[...]).astype(BF16)
    off = 0
    out_refs = list(out_refs)
    for width, normed, gained, outs in segs:
        y = _dot(h, w_ref[:, off:off + width])
        if normed:
            y = _group_rms(y, gmat_ref[...])
        if gained:
            y = y * gain_ref[:, off:off + width]
        for _, head_width in outs:
            o_ref = out_refs.pop(0)
            if head_width:
                o_ref[...] = y.reshape(o_ref.shape).astype(o_ref.dtype)
            else:
                o_ref[...] = y.astype(o_ref.dtype)
        off += width


def _proj(x, g, w, layer, gain, segs, tm):
    n, d = x.shape
    wtot = w.shape[2]
    specs, shapes = [], []
    for width, _, _, outs in segs:
        for dt, head_width in outs:
            if head_width:
                nh = width // head_width
                specs.append(pl.BlockSpec((tm, nh, head_width), lambda i: (i, 0, 0)))
                shapes.append(jax.ShapeDtypeStruct((n, nh, head_width), dt))
            else:
                specs.append(pl.BlockSpec((tm, width), lambda i: (i, 0)))
                shapes.append(jax.ShapeDtypeStruct((n, width), dt))
    return pl.pallas_call(
        functools.partial(_proj_kernel, segs=tuple(segs)),
        grid=(n // tm,),
        in_specs=[
            pl.BlockSpec((tm, d), lambda i: (i, 0)),
            pl.BlockSpec((1, d), lambda i: (0, 0)),
            pl.BlockSpec((None, d, wtot), lambda i: (layer, 0, 0), pipeline_mode=pl.Buffered(1)),
            pl.BlockSpec((1, wtot), lambda i: (0, 0)),
            pl.BlockSpec((V7X_MXU_DIM, V7X_MXU_DIM), lambda i: (0, 0)),
        ],
        out_specs=specs,
        out_shape=shapes,
        compiler_params=_cparams(("parallel",)),
        name="proj",
    )(x, g.reshape(1, d), w, gain, _group_mean_matrix())


def _softmax_step_km(s, vt, m_ref, acc_ref):
    m_old = m_ref[...]
    m_new = jnp.maximum(m_old, jnp.max(s, axis=0, keepdims=True))
    alpha = jnp.exp2(m_old - m_new)
    p = jnp.exp2(s - m_new)
    acc_ref[...] = alpha * acc_ref[...] + _dot(vt, p.astype(BF16))
    m_ref[...] = m_new


def _cast_rows(src_ref, dst_ref, rows, dst_off=0, step=512):
    def body(i, c):
        r = pl.multiple_of(i * step, step)
        dst_ref[pl.ds(dst_off + r, step), :] = src_ref[0, pl.ds(r, step), :].astype(BF16)
        return c
    lax.fori_loop(0, rows // step, body, 0)


def _attn_a_kernel(lam_ref, brange_ref, q_ref, k_ref, v_ref, bias_ref, o_ref,
                   vtb, s_scr, knorm, m1, a1, m2, a2):
    h = pl.program_id(1)
    qi = pl.program_id(2)
    kbf = k_ref.at[0]
    seq = kbf.shape[0]
    tq = q_ref.shape[1]
    tk = vtb.shape[2]
    dv = v_ref.shape[2]

    @pl.when(qi == 0)
    def _():
        lane = lax.broadcasted_iota(jnp.int32, (tk, LANES), 1)

        def prep(j, c):
            r = pl.multiple_of(j * tk, tk)
            kb = kbf[pl.ds(r, tk), :]
            vtb[j, 0:dv, :] = v_ref[0, pl.ds(r, tk), :].astype(F32).T.astype(BF16)
            vtb[j, dv:, :] = jnp.ones((vtb.shape[1] - dv, tk), BF16)
            ksq = kb.astype(F32) * kb.astype(F32)
            n1 = jnp.max(jnp.sum(jnp.where(lane < HEAD_DIM, ksq, 0.0), axis=1, keepdims=True))
            n2 = jnp.max(jnp.sum(jnp.where(lane >= HEAD_DIM, ksq, 0.0), axis=1, keepdims=True))
            return jnp.maximum(c[0], n1), jnp.maximum(c[1], n2)
        n1, n2 = lax.fori_loop(0, seq // tk, prep, (jnp.float32(0.0), jnp.float32(0.0)))
        knorm[0] = n1
        knorm[1] = n2

    qt = q_ref[0].astype(F32).T
    sub = lax.broadcasted_iota(jnp.int32, qt.shape, 0)
    q1t = jnp.where(sub < HEAD_DIM, qt, 0.0).astype(BF16)
    q2t = jnp.where(sub >= HEAD_DIM, qt, 0.0).astype(BF16)
    states = ((m1, a1), (m2, a2))
    for m_ref, a_ref in states:
        a_ref[...] = jnp.zeros_like(a_ref)

    far_bias, bias_max, bias_min = brange_ref[3 * h], brange_ref[3 * h + 1], brange_ref[3 * h + 2]
    qsq = qt * qt
    bound1 = jnp.sqrt(jnp.sum(jnp.where(sub < HEAD_DIM, qsq, 0.0), axis=0, keepdims=True) * knorm[0])
    bound2 = jnp.sqrt(jnp.sum(jnp.where(sub >= HEAD_DIM, qsq, 0.0), axis=0, keepdims=True) * knorm[1])
    bounds = (bound1 * BOUND_SLACK + bias_max, bound2 * BOUND_SLACK + bias_max)
    spread = 2.0 * BOUND_SLACK * jnp.maximum(jnp.max(bound1), jnp.max(bound2)) + (bias_max - bias_min)
    bounded = spread <= BOUNDED_EXP2_SPAN

    @pl.when(bounded)
    def _():
        qmaps = ((q1t, bounds[0], a1), (q2t, bounds[1], a2))

        def accumulate(blocks):
            kbs = [kbf[pl.ds(pl.multiple_of(j * tk, tk), tk), :] for j, _ in blocks]
            ps = []
            for (j, tile), kb in zip(blocks, kbs):
                for qmt, shift, _ in qmaps:
                    add = (far_bias - shift) if tile is None else (tile - shift)
                    ps.append(jnp.exp2(_dot(kb, qmt) + add).astype(BF16))
            for mi, (_, _, a_ref) in enumerate(qmaps):
                total = None
                for bi, (j, _) in enumerate(blocks):
                    term = _dot(vtb[j], ps[bi * len(qmaps) + mi])
                    total = term if total is None else total + term
                a_ref[...] += total

        @pl.when(qi == 0)
        def _():
            accumulate([(qi, bias_ref[0, 0])])

        @pl.when(qi >= 1)
        def _():
            accumulate([(qi, bias_ref[0, 0]), (qi - 1, bias_ref[0, 1])])

        n_far = jnp.maximum(qi - 1, 0)
        one = n_far & 1
        two = n_far & 2

        @pl.when(one == 1)
        def _():
            accumulate([(0, None)])

        @pl.when(two == 2)
        def _():
            accumulate([(one, None), (one + 1, None)])

        def quad(i, c):
            j = one + two + 4 * i
            accumulate([(j, None), (j + 1, None), (j + 2, None), (j + 3, None)])
            return c
        lax.fori_loop(0, lax.shift_right_logical(n_far, 2), quad, 0)

    @pl.when(jnp.logical_not(bounded))
    def _():
        for m_ref, _ in states:
            m_ref[...] = jnp.full_like(m_ref, NEG)
        n_blocks = qi + 1

        def scores(t, buf):
            kb = kbf[pl.ds(pl.multiple_of((qi - t) * tk, tk), tk), :]
            s_scr[buf, 0] = _dot(kb, q1t)
            s_scr[buf, 1] = _dot(kb, q2t)

        def update(t, buf):
            bias = bias_ref[0, jnp.minimum(t, bias_ref.shape[1] - 1)]
            vt = vtb[qi - t]
            for mi, (m_ref, a_ref) in enumerate(states):
                _softmax_step_km(s_scr[buf, mi] + bias, vt, m_ref, a_ref)

        odd = n_blocks & 1

        @pl.when(odd == 1)
        def _():
            scores(0, 0)
            update(0, 0)

        n_pairs = lax.shift_right_logical(n_blocks, 1)

        @pl.when(n_pairs > 0)
        def _():
            scores(odd, 0)

        def pair(p, c):
            t0 = odd + 2 * p
            scores(t0 + 1, 1)
            update(t0, 0)
            scores(jnp.minimum(t0 + 2, qi), 0)
            update(t0 + 1, 1)
            return c
        lax.fori_loop(0, n_pairs, pair, 0)

    out_t = (a1[0:dv, :] / a1[dv:dv + 1, :]
             - lam_ref[0] * (a2[0:dv, :] / a2[dv:dv + 1, :]))
    o_ref[0] = out_t.T


def _attn_a(lam, brange, qa, ka, va, bias):
    b, s, w = qa.shape
    nh = w // LANES
    tq = tk = MIXER_A_TILE
    vrows = LANES + BF16_SUBLANES
    return pl.pallas_call(
        _attn_a_kernel,
        grid=(b, nh, s // tq),
        in_specs=[
            pl.BlockSpec(memory_space=pltpu.SMEM),
            pl.BlockSpec(memory_space=pltpu.SMEM),
            pl.BlockSpec((1, tq, LANES), lambda bi, h, qi: (bi, qi, h)),
            pl.BlockSpec((1, s, LANES), lambda bi, h, qi: (bi, 0, h)),
            pl.BlockSpec((1, s, LANES), lambda bi, h, qi: (bi, 0, h)),
            pl.BlockSpec((1,) + bias.shape[1:], lambda bi, h, qi: (h, 0, 0, 0)),
        ],
        out_specs=pl.BlockSpec((1, tq, LANES), lambda bi, h, qi: (bi, qi, h)),
        out_shape=jax.ShapeDtypeStruct((b, s, w), F32),
        scratch_shapes=[
            pltpu.VMEM((s // tk, vrows, tk), BF16),
            pltpu.VMEM((2, 2, tk, tq), F32), pltpu.SMEM((2,), F32),
            pltpu.VMEM((1, tq), F32), pltpu.VMEM((vrows, tq), F32),
            pltpu.VMEM((1, tq), F32), pltpu.VMEM((vrows, tq), F32),
        ],
        compiler_params=_cparams(("parallel", "parallel", "arbitrary")),
        name="mixer_a",
    )(lam, brange, qa, ka, va, bias)


def _stick_block(z, valid, umat, carry):
    sp = jnp.maximum(z, 0.0) + jnp.log(1.0 + jnp.exp(-jnp.abs(z)))
    log1m = -sp
    if valid is not None:
        log1m = jnp.where(valid, log1m, 0.0)
    hi = log1m.astype(BF16)
    lo = (log1m - hi.astype(F32)).astype(BF16)
    after = _dot(hi, umat) + _dot(lo, umat)
    w = jnp.exp(z - sp + after + carry)
    if valid is not None:
        w = jnp.where(valid, w, 0.0)
    return w, jnp.sum(log1m, axis=1, keepdims=True)


def _stick_block_km(z, valid, umat, carry):
    sp = jnp.maximum(z, 0.0) + jnp.log(1.0 + jnp.exp(-jnp.abs(z)))
    log1m = -sp
    if valid is not None:
        log1m = jnp.where(valid, log1m, 0.0)
    hi = log1m.astype(BF16)
    lo = (log1m - hi.astype(F32)).astype(BF16)
    after = _dot(umat, hi) + _dot(umat, lo)
    w = jnp.exp(z - sp + after + carry)
    if valid is not None:
        w = jnp.where(valid, w, 0.0)
    return w, jnp.sum(log1m, axis=0, keepdims=True)


def _attn_b_kernel(q_ref, k_ref, v_ref, u_ref, o_ref, vtb, c_scr, acc_scr, z_scr):
    step = pl.program_id(2)
    kbf = k_ref.at[0]
    seq = kbf.shape[0]
    tk = vtb.shape[2]
    n_halves = q_ref.shape[1] // tk
    n_heads = LANES // HEAD_DIM

    @pl.when(step == 0)
    def _():
        def prep(j, c):
            r = pl.multiple_of(j * tk, tk)
            vtb[j] = v_ref[0, pl.ds(r, tk), :].astype(F32).T.astype(BF16)
            return c
        lax.fori_loop(0, seq // tk, prep, 0)

    sub = lax.broadcasted_iota(jnp.int32, (LANES, tk), 0)
    qts = []
    for half in range(n_halves):
        qt = q_ref[0, half * tk:(half + 1) * tk, :].astype(F32).T
        qts.append([jnp.where((sub >= hh * HEAD_DIM) & (sub < (hh + 1) * HEAD_DIM), qt, 0.0).astype(BF16)
                    for hh in range(n_heads)])
    krow = lax.broadcasted_iota(jnp.int32, (tk, tk), 0)
    qcol = lax.broadcasted_iota(jnp.int32, (tk, tk), 1)
    umat = u_ref[...]

    def blocks(half, js, valids, first):
        kbs = [kbf[pl.ds(pl.multiple_of(j * tk, tk), tk), :] for j in js]
        cmax = None
        for hh in range(n_heads):
            carry = 0.0 if first else c_scr[half, hh]
            pv = None
            for j, kb, valid in zip(js, kbs, valids):
                w, cs = _stick_block_km(_dot(kb, qts[half][hh]), valid, umat, carry)
                term = _dot(vtb[j], w.astype(BF16))
                pv = term if pv is None else pv + term
                carry = carry + cs
            acc_scr[half, hh] = pv if first else acc_scr[half, hh] + pv
            c_scr[half, hh] = carry
            cm = jnp.max(carry)
            cmax = cm if cmax is None else jnp.maximum(cmax, cm)
        return cmax

    own_valid = krow < qcol
    everywhere = krow >= 0

    chains = []
    for half in range(n_halves):
        qb = step * n_halves + half
        prev_valid = None if half >= 1 else everywhere & (qb >= 1)
        for hh in range(n_heads):
            for j, valid in ((qb, own_valid), (jnp.maximum(qb - 1, 0), prev_valid)):
                chains.append((half, hh, j, valid))
    for c, (half, hh, j, valid) in enumerate(chains):
        z_scr[c] = _dot(kbf[pl.ds(pl.multiple_of(j * tk, tk), tk), :], qts[half][hh])
    col_sums = []
    for c, (half, hh, j, valid) in enumerate(chains):
        z = z_scr[c]
        sp = jnp.maximum(z, 0.0) + jnp.log(1.0 + jnp.exp(-jnp.abs(z)))
        log1m = -sp if valid is None else jnp.where(valid, -sp, 0.0)
        hi = log1m.astype(BF16)
        lo = (log1m - hi.astype(F32)).astype(BF16)
        z_scr[c] = z - sp + _dot(umat, hi) + _dot(umat, lo)
        col_sums.append(jnp.sum(log1m, axis=0, keepdims=True))
    cmaxes = []
    for half in range(n_halves):
        cmax = None
        for hh in range(n_heads):
            carry, pv = 0.0, None
            for c, (ch, chh, j, valid) in enumerate(chains):
                if (ch, chh) != (half, hh):
                    continue
                w = jnp.exp(z_scr[c] + carry)
                if valid is not None:
                    w = jnp.where(valid, w, 0.0)
                term = _dot(vtb[j], w.astype(BF16))
                pv = term if pv is None else pv + term
                carry = carry + col_sums[c]
            acc_scr[half, hh] = pv
            c_scr[half, hh] = carry
            cm = jnp.max(carry)
            cmax = cm if cmax is None else jnp.maximum(cmax, cm)
        cmaxes.append(cmax)

    for half in range(n_halves):
        qb = step * n_halves + half

        def cond(st):
            j, cmax = st
            return (j >= 0) & (cmax > STICK_SKIP)

        def body(st, half=half):
            j, _ = st
            return j - 2, blocks(half, [j, jnp.maximum(j - 1, 0)], [None, everywhere & (j >= 1)], False)

        lax.while_loop(cond, body, (qb - 2, cmaxes[half]))
        out_t = jnp.where(sub < HEAD_DIM, acc_scr[half, 0], acc_scr[half, 1])
        o_ref[0, half * tk:(half + 1) * tk, :] = out_t.T.astype(o_ref.dtype)


def _strict_lower(n):
    return jnp.asarray(np.tril(np.ones((n, n)), -1), BF16)


def _strict_upper(n):
    return jnp.asarray(np.triu(np.ones((n, n)), 1), BF16)


def _attn_b(qb, kb, vb):
    b, s, w = qb.shape
    n_halves = 4
    tq = n_halves * ATTN_TK
    return pl.pallas_call(
        _attn_b_kernel,
        grid=(b, w // LANES, s // tq),
        in_specs=[
            pl.BlockSpec((1, tq, LANES), lambda bi, h, qi: (bi, qi, h)),
            pl.BlockSpec((1, s, LANES), lambda bi, h, qi: (bi, 0, h)),
            pl.BlockSpec((1, s, LANES), lambda bi, h, qi: (bi, 0, h)),
            pl.BlockSpec((ATTN_TK, ATTN_TK), lambda bi, h, qi: (0, 0)),
        ],
        out_specs=pl.BlockSpec((1, tq, LANES), lambda bi, h, qi: (bi, qi, h)),
        out_shape=jax.ShapeDtypeStruct((b, s, w), BF16),
        scratch_shapes=[
            pltpu.VMEM((s // ATTN_TK, LANES, ATTN_TK), BF16),
            pltpu.VMEM((n_halves, LANES // HEAD_DIM, 1, ATTN_TK), F32),
            pltpu.VMEM((n_halves, LANES // HEAD_DIM, LANES, ATTN_TK), F32),
            pltpu.VMEM((2 * n_halves * (LANES // HEAD_DIM), ATTN_TK, ATTN_TK), F32),
        ],
        compiler_params=_cparams(("parallel", "parallel", "arbitrary")),
        name="mixer_b",
    )(qb, kb, vb, _strict_upper(ATTN_TK))


def _attn_c_kernel(q_ref, k_ref, v_ref, bias_ref, o_ref, kbf, vbf, s_scr):
    step = pl.program_id(2)
    seq = k_ref.shape[1]
    tq = ATTN_TQ
    n_blocks = q_ref.shape[1] // tq
    pad = C_PREV * CHUNK
    n_heads = LANES // HEAD_DIM

    @pl.when(step == 0)
    def _():
        kbf[0:pad, :] = jnp.zeros((pad, LANES), BF16)
        vbf[0:pad, 0:LANES] = jnp.zeros((pad, LANES), BF16)
        vbf[:, LANES:] = jnp.ones((vbf.shape[0], vbf.shape[1] - LANES), BF16)
        _cast_rows(k_ref, kbf, seq, dst_off=pad)

        def vcast(i, c):
            r = pl.multiple_of(i * tq, tq)
            vbf[pl.ds(pad + r, tq), 0:LANES] = v_ref[0, pl.ds(r, tq), :].astype(BF16)
            return c
        lax.fori_loop(0, seq // tq, vcast, 0)

    lane = lax.broadcasted_iota(jnp.int32, (tq, LANES), 1)
    col = lax.broadcasted_iota(jnp.int32, (tq, BAND_W), 1)
    for blk in range(n_blocks):
        qb = step * n_blocks + blk
        q = q_ref[0, blk * tq:(blk + 1) * tq, :]
        kw = kbf[pl.ds(pl.multiple_of(qb * tq, tq), BAND_W), :]
        in_seq = col >= pad - qb * tq
        for hh in range(n_heads):
            qm = _keep_lanes(q, hh * HEAD_DIM, (hh + 1) * HEAD_DIM)
            s_scr[blk * n_heads + hh] = jnp.where(in_seq, _dot_nt(qm, kw) + bias_ref[hh], NEG)
    for blk in range(n_blocks):
        qb = step * n_blocks + blk
        vw = vbf[pl.ds(pl.multiple_of(qb * tq, tq), BAND_W), :]
        outs = []
        for hh in range(n_heads):
            s = s_scr[blk * n_heads + hh]
            p = jnp.exp(s - jnp.max(s, axis=1, keepdims=True))
            acc = _dot(p.astype(BF16), vw)
            outs.append(acc[:, 0:LANES] / acc[:, LANES:])
        o_ref[0, blk * tq:(blk + 1) * tq, :] = jnp.where(lane < HEAD_DIM, outs[0], outs[1]).astype(o_ref.dtype)


def _attn_c(qc, kc, vc, bias):
    b, s, w = qc.shape
    n_blocks = 4
    tq = n_blocks * ATTN_TQ
    pad = C_PREV * CHUNK
    hp = LANES // HEAD_DIM
    return pl.pallas_call(
        _attn_c_kernel,
        grid=(b, w // LANES, s // tq),
        in_specs=[
            pl.BlockSpec((1, tq, LANES), lambda bi, h, qi: (bi, qi, h)),
            pl.BlockSpec((1, s, LANES), lambda bi, h, qi: (bi, 0, h)),
            pl.BlockSpec((1, s, LANES), lambda bi, h, qi: (bi, 0, h)),
            pl.BlockSpec((hp, ATTN_TQ, BAND_W), lambda bi, h, qi: (h, 0, 0)),
        ],
        out_specs=pl.BlockSpec((1, tq, LANES), lambda bi, h, qi: (bi, qi, h)),
        out_shape=jax.ShapeDtypeStruct((b, s, w), BF16),
        scratch_shapes=[pltpu.VMEM((s + pad, LANES), BF16), pltpu.VMEM((s + pad, 2 * LANES), BF16),
                        pltpu.VMEM((n_blocks * hp, ATTN_TQ, BAND_W), F32)],
        compiler_params=_cparams(("parallel", "parallel", "arbitrary")),
        name="mixer_c",
    )(qc, kc, vc, bias)


def _merge_kernel(x_ref, oa_ref, ob_ref, oc_ref, g_ref, wgate_ref, sub_ref,
                  wa_ref, wb_ref, wc_ref, wout_ref, o_ref):
    x = x_ref[...]
    d = x.shape[1]
    h = _rms_rows(x, g_ref[...]).astype(BF16)
    oa = oa_ref[...]
    parts = []
    for c in range(oa.shape[1] // LANES):
        oc_ = oa[:, c * LANES:(c + 1) * LANES]
        parts.append(oc_ * lax.rsqrt(jnp.mean(oc_ * oc_, axis=-1, keepdims=True) + EPS))
    oan = (jnp.concatenate(parts, axis=1) * sub_ref[...]).astype(BF16)
    merged = jax.nn.sigmoid(_dot(h, wgate_ref[:, 0:d])) * _dot(oan, wa_ref[...])
    merged += jax.nn.sigmoid(_dot(h, wgate_ref[:, d:2 * d])) * _dot(ob_ref[...], wb_ref[...])
    merged += jax.nn.sigmoid(_dot(h, wgate_ref[:, 2 * d:3 * d])) * _dot(oc_ref[...], wc_ref[...])
    o_ref[...] = x + _dot(merged.astype(BF16), wout_ref[...])


def _merge(x, oa, ob, oc, g, wgate, sub, wa, wb, wc, wout, layer, tm):
    n, d = x.shape
    full = lambda a: pl.BlockSpec(a.shape, lambda i: (0,) * a.ndim, pipeline_mode=pl.Buffered(1))
    of_layer = lambda a: pl.BlockSpec((None,) + a.shape[1:], lambda i: (layer,) + (0,) * (a.ndim - 1),
                                      pipeline_mode=pl.Buffered(1))
    rows = lambda a: pl.BlockSpec((tm, a.shape[1]), lambda i: (i, 0))
    g = g.reshape(1, d)
    return pl.pallas_call(
        _merge_kernel,
        grid=(n // tm,),
        in_specs=[rows(x), rows(oa), rows(ob), rows(oc), full(g), of_layer(wgate), full(sub),
                  of_layer(wa), of_layer(wb), of_layer(wc), of_layer(wout)],
        out_specs=rows(x),
        out_shape=jax.ShapeDtypeStruct((n, d), F32),
        compiler_params=_cparams(("parallel",)),
        name="merge",
    )(x, oa, ob, oc, g, wgate, sub, wa, wb, wc, wout)


def _cross_kernel(x_ref, g_ref, wq_ref, gain_ref, gmat_ref, mk_ref, mv_ref, wo_ref, o_ref):
    x = x_ref[0]
    h = _rms_rows(x, g_ref[...]).astype(BF16)
    q = (_group_rms(_dot(h, wq_ref[...]), gmat_ref[...]) * gain_ref[...]).astype(BF16)
    mk = mk_ref[0]
    mv = mv_ref[0]
    lane = lax.broadcasted_iota(jnp.int32, q.shape, 1)
    o = jnp.zeros(q.shape, F32)
    for hh in range(q.shape[1] // HEAD_DIM):
        in_head = (lane >= hh * HEAD_DIM) & (lane < (hh + 1) * HEAD_DIM)
        s = _dot_nt(_keep_lanes(q, hh * HEAD_DIM, (hh + 1) * HEAD_DIM), mk)
        p = jnp.exp(s - jnp.max(s, axis=1, keepdims=True))
        l = jnp.sum(p, axis=1, keepdims=True)
        o = jnp.where(in_head, _dot(p.astype(BF16), mv) / l, o)
    o_ref[0] = x + _dot(o.astype(BF16), wo_ref[...])


def _cross(x, g, wq, gain, mk, mv, wo, layer, tm):
    b, s, d = x.shape
    full = lambda a: pl.BlockSpec(a.shape, lambda bi, i: (0,) * a.ndim)
    of_layer = lambda a: pl.BlockSpec((None,) + a.shape[1:], lambda bi, i: (layer,) + (0,) * (a.ndim - 1))
    g = g.reshape(1, d)
    gmat = _group_mean_matrix()
    return pl.pallas_call(
        _cross_kernel,
        grid=(b, s // tm),
        in_specs=[
            pl.BlockSpec((1, tm, d), lambda bi, i: (bi, i, 0)),
            full(g), of_layer(wq), full(gain), full(gmat),
            pl.BlockSpec((1,) + mk.shape[1:], lambda bi, i: (bi, 0, 0)),
            pl.BlockSpec((1,) + mv.shape[1:], lambda bi, i: (bi, 0, 0)),
            of_layer(wo),
        ],
        out_specs=pl.BlockSpec((1, tm, d), lambda bi, i: (bi, i, 0)),
        out_shape=jax.ShapeDtypeStruct((b, s, d), F32),
        compiler_params=_cparams(("parallel", "parallel")),
        name="cross",
    )(x, g, wq, gain, gmat, mk, mv, wo)


def _sample_attn_kernel(lam_ref,
                        qa_ref, kan_ref, van_ref, kac_ref, vac_ref, bac_ref, ban_ref,
                        qb_ref, kbn_ref, vbn_ref, kbc_ref, vbc_ref, ubig_ref, usmall_ref,
                        qc_ref, kcn_ref, vcn_ref, kcc_ref, vcc_ref, bcc_ref, bcn_ref,
                        oa_ref, ob_ref, oc_ref):
    lam = lam_ref[0]
    ns = qa_ref.shape[1]
    bf = lambda r: r[0].astype(BF16)

    def heads(q, width):
        for hh in range(LANES // width):
            yield hh, None, _keep_lanes(q, hh * width, (hh + 1) * width)

    def softmax2(s_c, s_n, exp=jnp.exp):
        m = jnp.maximum(jnp.max(s_c, axis=1, keepdims=True), jnp.max(s_n, axis=1, keepdims=True))
        p_c = exp(s_c - m)
        p_n = exp(s_n - m)
        inv = 1.0 / (jnp.sum(p_c, axis=1, keepdims=True) + jnp.sum(p_n, axis=1, keepdims=True))
        return p_c * inv, p_n * inv

    for h in range(qa_ref.shape[2] // LANES):
        sl = slice(h * LANES, (h + 1) * LANES)
        q = qa_ref[0, :, sl]
        k_c = kac_ref[0, :, h, :].astype(BF16)
        k_n = kan_ref[0, :, sl].astype(BF16)
        maps = []
        for _, _, qm in heads(q, HEAD_DIM):
            maps.append(softmax2(_dot_nt(qm, k_c) + bac_ref[h], _dot_nt(qm, k_n) + ban_ref[h], exp=jnp.exp2))
        a_c = (maps[0][0] - lam * maps[1][0]).astype(BF16)
        a_n = (maps[0][1] - lam * maps[1][1]).astype(BF16)
        oa_ref[0, :, sl] = (_dot(a_c, vac_ref[0, :, h, :].astype(BF16))
                            + _dot(a_n, van_ref[0, :, sl].astype(BF16)))

    past = kbc_ref.shape[1]
    tk = ubig_ref.shape[0]
    row = lax.broadcasted_iota(jnp.int32, (ns, ns), 0)
    col = lax.broadcasted_iota(jnp.int32, (ns, ns), 1)
    for pr in range(qb_ref.shape[2] // LANES):
        sl = slice(pr * LANES, (pr + 1) * LANES)
        q = qb_ref[0, :, sl]
        k_n = kbn_ref[0, :, sl].astype(BF16)
        v_n = vbn_ref[0, :, sl].astype(BF16)
        outs = []
        for _, _, qm in heads(q, HEAD_DIM):
            w, carry = _stick_block(_dot_nt(qm, k_n), col < row, usmall_ref[...], 0.0)
            acc = _dot(w.astype(BF16), v_n)
            for j in range(past // tk - 1, -1, -1):
                k_c = kbc_ref[0, j * tk:(j + 1) * tk, sl].astype(BF16)
                v_c = vbc_ref[0, j * tk:(j + 1) * tk, sl].astype(BF16)
                w, rs = _stick_block(_dot_nt(qm, k_c), None, ubig_ref[...], carry)
                acc += _dot(w.astype(BF16), v_c)
                carry = carry + rs
            outs.append(acc)
        lane = lax.broadcasted_iota(jnp.int32, q.shape, 1)
        ob_ref[0, :, sl] = jnp.where(lane < HEAD_DIM, outs[0], outs[1]).astype(ob_ref.dtype)

    for pr in range(qc_ref.shape[2] // LANES):
        sl = slice(pr * LANES, (pr + 1) * LANES)
        q = qc_ref[0, :, sl]
        k_c = kcc_ref[0, :, sl].astype(BF16)
        k_n = kcn_ref[0, :, sl].astype(BF16)
        v_c = vcc_ref[0, :, sl].astype(BF16)
        v_n = vcn_ref[0, :, sl].astype(BF16)
        outs = []
        for hh, _, qm in heads(q, HEAD_DIM):
            hd = pr * (LANES // HEAD_DIM) + hh
            p_c, p_n = softmax2(_dot_nt(qm, k_c) + bcc_ref[hd], _dot_nt(qm, k_n) + bcn_ref[hd])
            outs.append(_dot(p_c.astype(BF16), v_c) + _dot(p_n.astype(BF16), v_n))
        lane = lax.broadcasted_iota(jnp.int32, q.shape, 1)
        oc_ref[0, :, sl] = jnp.where(lane < HEAD_DIM, outs[0], outs[1]).astype(oc_ref.dtype)


def _sample_attn(layer, lam, qa, ka, va, cak, cav, bac, ban, qb, kb, vb, cbk, cbv,
                 qc, kc, vc, cck, ccv, bcc, bcn):
    b, ns, _ = qa.shape
    per_b = lambda a: pl.BlockSpec((1,) + a.shape[1:], lambda bi: (bi,) + (0,) * (a.ndim - 1))
    full = lambda a: pl.BlockSpec(a.shape, lambda bi: (0,) * a.ndim)
    layer_b = lambda a: pl.BlockSpec((None, 1) + a.shape[2:], lambda bi: (layer, bi) + (0,) * (a.ndim - 2))
    ubig = _strict_lower(ATTN_TK)
    usmall = _strict_lower(ns)
    args = [qa, ka, va, cak, cav, bac, ban, qb, kb, vb, cbk, cbv, ubig, usmall,
            qc, kc, vc, cck, ccv, bcc, bcn]
    specs = [per_b(qa), per_b(ka), per_b(va), layer_b(cak), layer_b(cav), full(bac), full(ban),
             per_b(qb), per_b(kb), per_b(vb), per_b(cbk), per_b(cbv), full(ubig), full(usmall),
             per_b(qc), per_b(kc), per_b(vc), per_b(cck), per_b(ccv), full(bcc), full(bcn)]
    return pl.pallas_call(
        _sample_attn_kernel,
        grid=(b,),
        in_specs=[pl.BlockSpec(memory_space=pltpu.SMEM)] + specs,
        out_specs=[per_b(qa), per_b(qb), per_b(qc)],
        out_shape=[jax.ShapeDtypeStruct(qa.shape, F32), jax.ShapeDtypeStruct(qb.shape, BF16),
                   jax.ShapeDtypeStruct(qc.shape, BF16)],
        compiler_params=_cparams(("parallel",)),
        name="sample_mixers",
    )(lam, *args)


def _t5_bucket_np(rel):
    half = T5_BUCKETS // 2
    max_exact = half // 2
    n = np.abs(rel)
    nf = np.maximum(n, 1).astype(np.float64)
    large = max_exact + (np.log(nf / max_exact) / math.log(T5_MAX_DIST / max_exact)
                         * (half - max_exact)).astype(np.int64)
    large = np.minimum(large, half - 1)
    return np.where(rel > 0, half, 0) + np.where(n < max_exact, n, large)


def _toeplitz(lookup, n_rows, n_cols):
    period = n_rows + n_cols
    slot = np.arange(period)
    diff = np.minimum((slot + n_rows - 1) % period - (n_rows - 1), n_cols - 1)
    vec = lookup(diff).astype(F32)
    flat = jnp.tile(vec, (1, n_rows))[:, :n_rows * (period - 1)]
    return flat.reshape(vec.shape[0], n_rows, period - 1)[:, :, :n_cols]


def _t5_bias_table(t5_bias, qpos, kpos, key_major=False):
    t5_rows = lambda rel: t5_bias[_t5_bucket_np(rel)].T
    mask = (kpos[None, :] // CHUNK) <= (qpos[:, None] // CHUNK)
    if key_major:
        table = _toeplitz(lambda dd: t5_rows(kpos[0] - qpos[0] - dd), len(kpos), len(qpos))
        mask = mask.T
    else:
        table = _toeplitz(lambda dd: t5_rows(kpos[0] - qpos[0] + dd), len(qpos), len(kpos))
    return jnp.where(jnp.asarray(mask)[None], table, NEG)


def _band_bias_table(rel_table, qpos, kpos):
    lookup = lambda dd: rel_table[:, np.clip(kpos[0] - qpos[0] + dd, -REL_CLIP, REL_CLIP) + REL_CLIP]
    table = _toeplitz(lookup, len(qpos), len(kpos))
    qc = qpos[:, None] // CHUNK
    kc = kpos[None, :] // CHUNK
    mask = (kpos[None, :] >= 0) & (kc <= qc) & (kc >= qc - C_PREV)
    return jnp.where(jnp.asarray(mask)[None], table, NEG)


def _lambda_init(layer):
    return 0.8 - 0.6 * math.exp(-0.3 * layer)


def kernel(x_prompt, x_sample, mem_prompt, cache_a_k, cache_a_v, cache_b_k, cache_b_v, cache_c_k, cache_c_v, cache_mem_k, cache_mem_v, t5_bias, ffn1_norm, ffn1_wg, ffn1_wu, ffn1_wd, mix_norm, w_in, a_qnorm, a_knorm, a_lq1, a_lk1, a_lq2, a_lk2, a_subln, c_qnorm, c_knorm, c_rel_bias, w_br_a, w_br_b, w_br_c, w_out, x_norm, mem_norm, x_wq, x_wkv, x_qnorm, x_knorm, x_wo, ffn2_norm, ffn2_wg, ffn2_wu, ffn2_wd):
    bp, sp, d = x_prompt.shape
    bs, ns, _ = x_sample.shape
    depth = w_in.shape[0]
    past = cache_a_k.shape[2]
    w_buf = cache_c_k.shape[2]
    n_mem = mem_prompt.shape[1]
    wa = cache_a_k.shape[3] * cache_a_k.shape[4]
    wb = cache_b_k.shape[3] * cache_b_k.shape[4]
    wc = cache_c_k.shape[3] * cache_c_k.shape[4]
    wx = cache_mem_k.shape[3] * cache_mem_k.shape[4]
    h_a = cache_a_k.shape[3]
    n_qkv = 3 * (wa + wb + wc)
    w_keep = min(C_PREV * CHUNK, sp)
    scale = HEAD_DIM ** -0.5
    assert sp % ATTN_TQ == 0 and ATTN_TQ == ATTN_TK and ATTN_TQ == 4 * CHUNK and sp >= w_keep
    assert past % ATTN_TK == 0
    assert sp % MIXER_A_TILE == 0 and MIXER_A_TILE % CHUNK == 0 and T5_MAX_DIST <= MIXER_A_TILE

    tile = lambda g, reps: jnp.tile(g.astype(F32), reps)
    bf = lambda a: a.astype(BF16)

    loc_a = np.arange(MIXER_A_TILE)
    loc = np.arange(ATTN_TQ)
    a_far = t5_bias[T5_BUCKETS // 2 - 1].astype(F32)
    a_bias = LOG2E * jnp.stack([
        _t5_bias_table(t5_bias, loc_a + MIXER_A_TILE, loc_a + MIXER_A_TILE, key_major=True),
        _t5_bias_table(t5_bias, loc_a + MIXER_A_TILE, loc_a, key_major=True),
        jnp.broadcast_to(a_far[:, None, None], (h_a, MIXER_A_TILE, MIXER_A_TILE))], axis=1)
    t5_log2 = LOG2E * t5_bias.astype(F32)
    a_brange = jnp.stack([LOG2E * a_far, jnp.max(t5_log2, axis=0), jnp.min(t5_log2, axis=0)], axis=1).reshape(-1)
    qpos_s = past + np.arange(ns)
    a_bias_sc = LOG2E * _t5_bias_table(t5_bias, qpos_s, np.arange(past))
    a_bias_sn = LOG2E * _t5_bias_table(t5_bias, qpos_s, qpos_s)
    kpos_sb = past - w_buf + np.arange(w_buf + ns)

    xp = x_prompt.reshape(bp * sp, d)
    xs = x_sample.reshape(bs * ns, d)
    mem = mem_prompt.reshape(bp * n_mem, d)
    outs = {k: [] for k in ("ak_p", "av_p", "bk_p", "bv_p", "ck_p", "cv_p", "mk_p", "mv_p",
                            "ak_s", "av_s", "bk_s", "bv_s", "ck_s", "cv_s")}
    flat = lambda dt: ((dt, 0),)
    dv_a = wa // h_a
    a_kv_prompt = ((F32, dv_a), (BF16, 0))
    b_kv_prompt = ((F32, HEAD_DIM), (BF16, 0))
    c_segs = [(wc, True, True, flat(BF16)), (wc, True, True, flat(F32)), (wc, False, False, flat(F32))]
    qkv_segs_p = [(wa, True, True, flat(BF16)), (wa, True, True, a_kv_prompt), (wa, False, False, a_kv_prompt),
                  (wb, False, True, flat(BF16)), (wb, False, False, b_kv_prompt), (wb, False, False, b_kv_prompt)] + c_segs
    qkv_segs_s = [(wa, True, True, flat(BF16)), (wa, True, True, flat(F32)), (wa, False, False, flat(F32)),
                  (wb, False, True, flat(BF16)), (wb, False, False, flat(F32)), (wb, False, False, flat(F32))] + c_segs

    wg1, wu1, wd1 = bf(ffn1_wg), bf(ffn1_wu), bf(ffn1_wd)
    wg2, wu2, wd2 = bf(ffn2_wg), bf(ffn2_wu), bf(ffn2_wd)
    w_qkv, w_gate = bf(w_in[:, :, :n_qkv]), bf(w_in[:, :, n_qkv:])
    wbr = bf(w_br_a), bf(w_br_b), bf(w_br_c)
    wout = bf(w_out)
    wq, wo, wkv = bf(x_wq), bf(x_wo), bf(x_wkv)

    for i in range(depth):
        qkv_gain = jnp.concatenate([
            tile(a_qnorm[i], wa // HEAD_DIM) * (scale * LOG2E), tile(a_knorm[i], wa // HEAD_DIM), jnp.ones((wa,), F32),
            jnp.full((wb,), scale, F32), jnp.ones((2 * wb,), F32),
            tile(c_qnorm[i], wc // HEAD_DIM) * scale, tile(c_knorm[i], wc // HEAD_DIM), jnp.ones((wc,), F32),
        ]).reshape(1, n_qkv)
        sub = (tile(a_subln[i], h_a) * (1.0 - _lambda_init(i))).reshape(1, wa)
        xq_gain = (tile(x_qnorm[i], wx // HEAD_DIM) * scale).reshape(1, wx)
        dot64 = lambda a, b: jnp.exp(jnp.sum(a.astype(F32) * b.astype(F32)))
        lam = (dot64(a_lq1[i], a_lk1[i]) - dot64(a_lq2[i], a_lk2[i]) + _lambda_init(i)).reshape(1)
        c_bias_p = _band_bias_table(c_rel_bias[i], C_PREV * CHUNK + loc, np.arange(BAND_W))
        c_bias_sc = _band_bias_table(c_rel_bias[i], qpos_s, kpos_sb[:w_buf])
        c_bias_sn = _band_bias_table(c_rel_bias[i], qpos_s, kpos_sb[w_buf:])

        xp = _ffn(xp, ffn1_norm[i], wg1, wu1, wd1, i, tm=1024, tf=1024)
        qa, ka, ka_bf, va, va_bf, qb, kb, kb_bf, vb, vb_bf, qc, kc, vc = _proj(
            xp, mix_norm[i], w_qkv, i, qkv_gain, qkv_segs_p, tm=1024)
        r3 = lambda a: a.reshape(bp, sp, a.shape[-1])
        oa = _attn_a(lam, a_brange, r3(qa), r3(ka_bf), r3(va_bf), a_bias)
        ob = _attn_b(r3(qb), r3(kb_bf), r3(vb_bf))
        oc = _attn_c(r3(qc), r3(kc), r3(vc), c_bias_p)
        xp = _merge(xp, oa.reshape(-1, wa), ob.reshape(-1, wb), oc.reshape(-1, wc), mix_norm[i], w_gate, sub,
                    *wbr, wout, i, tm=1024)
        mk, mv = _proj(mem, mem_norm[i], wkv, i,
                       jnp.concatenate([tile(x_knorm[i], wx // HEAD_DIM), jnp.ones((wx,), F32)]).reshape(1, 2 * wx),
                       [(wx, True, True, flat(F32)), (wx, False, False, flat(F32))], tm=n_mem)
        mk3, mv3 = mk.reshape(bp, n_mem, wx), mv.reshape(bp, n_mem, wx)
        xp = _cross(xp.reshape(bp, sp, d), x_norm[i], wq, xq_gain, bf(mk3), bf(mv3), wo, i, tm=1024).reshape(-1, d)
        xp = _ffn(xp, ffn2_norm[i], wg2, wu2, wd2, i, tm=1024, tf=1024)
        outs["ak_p"].append(ka.reshape(bp, sp, h_a, -1))
        outs["av_p"].append(va.reshape(bp, sp, h_a, -1))
        outs["bk_p"].append(kb.reshape(bp, sp, -1, HEAD_DIM))
        outs["bv_p"].append(vb.reshape(bp, sp, -1, HEAD_DIM))
        outs["ck_p"].append(r3(kc)[:, sp - w_keep:].reshape(bp, w_keep, -1, HEAD_DIM))
        outs["cv_p"].append(r3(vc)[:, sp - w_keep:].reshape(bp, w_keep, -1, HEAD_DIM))
        outs["mk_p"].append(mk3.reshape(bp, n_mem, -1, HEAD_DIM))
        outs["mv_p"].append(mv3.reshape(bp, n_mem, -1, HEAD_DIM))

        xs = _ffn(xs, ffn1_norm[i], wg1, wu1, wd1, i, tm=bs * ns, tf=1024)
        qa, ka, va, qb, kb, vb, qc, kc, vc = _proj(xs, mix_norm[i], w_qkv, i, qkv_gain, qkv_segs_s, tm=bs * ns)
        s3 = lambda a: a.reshape(bs, ns, a.shape[-1])
        c3 = lambda a: a.reshape(bs, a.shape[1], -1)
        oa, ob, oc = _sample_attn(
            i, lam, s3(qa), s3(ka), s3(va), cache_a_k, cache_a_v, a_bias_sc, a_bias_sn,
            s3(qb), s3(kb), s3(vb), c3(cache_b_k[i]), c3(cache_b_v[i]),
            s3(qc), s3(kc), s3(vc), c3(cache_c_k[i]), c3(cache_c_v[i]), c_bias_sc, c_bias_sn)
        xs = _merge(xs, oa.reshape(-1, wa), ob.reshape(-1, wb), oc.reshape(-1, wc), mix_norm[i], w_gate, sub,
                    *wbr, wout, i, tm=bs * ns)
        xs = _cross(xs.reshape(bs, ns, d), x_norm[i], wq, xq_gain, bf(c3(cache_mem_k[i])), bf(c3(cache_mem_v[i])),
                    wo, i, tm=ns).reshape(-1, d)
        xs = _ffn(xs, ffn2_norm[i], wg2, wu2, wd2, i, tm=bs * ns, tf=1024)
        outs["ak_s"].append(ka.reshape(bs, ns, h_a, -1))
        outs["av_s"].append(va.reshape(bs, ns, h_a, -1))
        outs["bk_s"].append(kb.reshape(bs, ns, -1, HEAD_DIM))
        outs["bv_s"].append(vb.reshape(bs, ns, -1, HEAD_DIM))
        outs["ck_s"].append(jnp.concatenate([cache_c_k[i], kc.reshape(bs, ns, -1, HEAD_DIM)], axis=1)[:, ns:])
        outs["cv_s"].append(jnp.concatenate([cache_c_v[i], vc.reshape(bs, ns, -1, HEAD_DIM)], axis=1)[:, ns:])

    st = lambda k: jnp.stack(outs[k])
    return (xp.reshape(bp, sp, d), xs.reshape(bs, ns, d),
            st("ak_p"), st("av_p"), st("bk_p"), st("bv_p"), st("ck_p"), st("cv_p"), st("mk_p"), st("mv_p"),
            st("ak_s"), st("av_s"), st("bk_s"), st("bv_s"), st("ck_s"), st("cv_s"))
```

```python
import functools
import math

import numpy as np
import jax
import jax.numpy as jnp
from jax import lax
from jax.experimental import pallas as pl
from jax.experimental.pallas import tpu as pltpu

F32 = jnp.float32
BF16 = jnp.bfloat16

EPS = 1e-6
HEAD_DIM = 64
CHUNK = 64
C_PREV = 8
REL_CLIP = 128
T5_BUCKETS = 32
T5_MAX_DIST = 128
LANES = 128
BF16_SUBLANES = 16
V7X_MXU_DIM = 256
NEG = -1e30
LOG2E = math.log2(math.e)
BOUND_SLACK = 1.001
BOUNDED_EXP2_SPAN = 100.0
STICK_SKIP = -110.0
V7X_VMEM_LIMIT_BYTES = 56 * 1024 * 1024

MIXER_A_TILE = 512
ATTN_TQ = 256
ATTN_TK = 256
BAND_W = (C_PREV + 4) * CHUNK


def _cparams(sem):
    return pltpu.CompilerParams(dimension_semantics=sem, vmem_limit_bytes=V7X_VMEM_LIMIT_BYTES)


def _rms_rows(x, g):
    return x * lax.rsqrt(jnp.mean(x * x, axis=-1, keepdims=True) + EPS) * g


def _dot(a, b):
    return jnp.dot(a, b, preferred_element_type=F32)


def _dot_nt(a, b):
    return lax.dot_general(a, b, (((1,), (1,)), ((), ())), preferred_element_type=F32)


def _keep_lanes(q, lo, hi):
    lane = lax.broadcasted_iota(jnp.int32, q.shape, 1)
    return jnp.where((lane >= lo) & (lane < hi), q.astype(F32), 0.0).astype(BF16)


def _group_rms(y, gmat):
    wide = gmat.shape[0]
    parts = []
    for c in range(y.shape[1] // wide):
        yc = y[:, c * wide:(c + 1) * wide]
        ms = _dot((yc * yc).astype(BF16), gmat)
        parts.append(yc * lax.rsqrt(ms + EPS))
    return parts[0] if len(parts) == 1 else jnp.concatenate(parts, axis=1)


def _group_mean_matrix():
    g = np.kron(np.eye(V7X_MXU_DIM // HEAD_DIM), np.ones((HEAD_DIM, HEAD_DIM))) / HEAD_DIM
    return jnp.asarray(g, BF16)


def _ffn_kernel(x_ref, g_ref, wg_ref, wu_ref, wd_ref, o_ref, h_scr):
    j = pl.program_id(1)

    @pl.when(j == 0)
    def _():
        x = x_ref[...]
        h_scr[...] = _rms_rows(x, g_ref[...]).astype(BF16)
        o_ref[...] = x

    h = h_scr[...]
    a = _dot(h, wg_ref[...])
    u = _dot(h, wu_ref[...])
    t = a * jax.nn.sigmoid(a) * u
    o_ref[...] += 0.5 * _dot(t.astype(BF16), wd_ref[...])


def _ffn(x, g, wg, wu, wd, layer, tm, tf=512):
    n, d = x.shape
    dff = wg.shape[2]
    return pl.pallas_call(
        _ffn_kernel,
        grid=(n // tm, dff // tf),
        in_specs=[
            pl.BlockSpec((tm, d), lambda i, j: (i, 0)),
            pl.BlockSpec((1, d), lambda i, j: (0, 0)),
            pl.BlockSpec((None, d, tf), lambda i, j: (layer, 0, j)),
            pl.BlockSpec((None, d, tf), lambda i, j: (layer, 0, j)),
            pl.BlockSpec((None, tf, d), lambda i, j: (layer, j, 0)),
        ],
        out_specs=pl.BlockSpec((tm, d), lambda i, j: (i, 0)),
        out_shape=jax.ShapeDtypeStruct((n, d), F32),
        scratch_shapes=[pltpu.VMEM((tm, d), BF16)],
        compiler_params=_cparams(("parallel", "arbitrary")),
        name="ffn",
    )(x, g.reshape(1, d), wg, wu, wd)


def _proj_kernel(x_ref, g_ref, w_ref, gain_ref, gmat_ref, *out_refs, segs):
    h = _rms_rows(x_ref[...], g_ref[...]).astype(BF16)
    off = 0
    out_refs = list(out_refs)
    for width, normed, gained, outs in segs:
        y = _dot(h, w_ref[:, off:off + width])
        if normed:
            y = _group_rms(y, gmat_ref[...])
        if gained:
            y = y * gain_ref[:, off:off + width]
        for _, head_width in outs:
            o_ref = out_refs.pop(0)
            if head_width:
                o_ref[...] = y.reshape(o_ref.shape).astype(o_ref.dtype)
            else:
                o_ref[...] = y.astype(o_ref.dtype)
        off += width


def _proj(x, g, w, layer, gain, segs, tm):
    n, d = x.shape
    wtot = w.shape[2]
    specs, shapes = [], []
    for width, _, _, outs in segs:
        for dt, head_width in outs:
            if head_width:
                nh = width // head_width
                specs.append(pl.BlockSpec((tm, nh, head_width), lambda i: (i, 0, 0)))
                shapes.append(jax.ShapeDtypeStruct((n, nh, head_width), dt))
            else:
                specs.append(pl.BlockSpec((tm, width), lambda i: (i, 0)))
                shapes.append(jax.ShapeDtypeStruct((n, width), dt))
    return pl.pallas_call(
        functools.partial(_proj_kernel, segs=tuple(segs)),
        grid=(n // tm,),
        in_specs=[
            pl.BlockSpec((tm, d), lambda i: (i, 0)),
            pl.BlockSpec((1, d), lambda i: (0, 0)),
            pl.BlockSpec((None, d, wtot), lambda i: (layer, 0, 0), pipeline_mode=pl.Buffered(1)),
            pl.BlockSpec((1, wtot), lambda i: (0, 0)),
            pl.BlockSpec((V7X_MXU_DIM, V7X_MXU_DIM), lambda i: (0, 0)),
        ],
        out_specs=specs,
        out_shape=shapes,
        compiler_params=_cparams(("parallel",)),
        name="proj",
    )(x, g.reshape(1, d), w, gain, _group_mean_matrix())


def _softmax_step_km(s, vt, m_ref, acc_ref):
    m_old = m_ref[...]
    m_new = jnp.maximum(m_old, jnp.max(s, axis=0, keepdims=True))
    alpha = jnp.exp2(m_old - m_new)
    p = jnp.exp2(s - m_new)
    acc_ref[...] = alpha * acc_ref[...] + _dot(vt, p.astype(BF16))
    m_ref[...] = m_new


def _cast_rows(src_ref, dst_ref, rows, dst_off=0, step=512):
    def body(i, c):
        r = pl.multiple_of(i * step, step)
        dst_ref[pl.ds(dst_off + r, step), :] = src_ref[0, pl.ds(r, step), :].astype(BF16)
        return c
    lax.fori_loop(0, rows // step, body, 0)


def _attn_a_kernel(lam_ref, brange_ref, q_ref, k_ref, v_ref, bias_ref, o_ref,
                   vtb, s_scr, knorm, m1, a1, m2, a2):
    h = pl.program_id(1)
    qi = pl.program_id(2)
    kbf = k_ref.at[0]
    seq = kbf.shape[0]
    tq = q_ref.shape[1]
    tk = vtb.shape[2]
    dv = v_ref.shape[2]

    @pl.when(qi == 0)
    def _():
        lane = lax.broadcasted_iota(jnp.int32, (tk, LANES), 1)

        def prep(j, c):
            r = pl.multiple_of(j * tk, tk)
            kb = kbf[pl.ds(r, tk), :]
            vtb[j, 0:dv, :] = v_ref[0, pl.ds(r, tk), :].astype(F32).T.astype(BF16)
            vtb[j, dv:, :] = jnp.ones((vtb.shape[1] - dv, tk), BF16)
            ksq = kb.astype(F32) * kb.astype(F32)
            n1 = jnp.max(jnp.sum(jnp.where(lane < HEAD_DIM, ksq, 0.0), axis=1, keepdims=True))
            n2 = jnp.max(jnp.sum(jnp.where(lane >= HEAD_DIM, ksq, 0.0), axis=1, keepdims=True))
            return jnp.maximum(c[0], n1), jnp.maximum(c[1], n2)
        n1, n2 = lax.fori_loop(0, seq // tk, prep, (jnp.float32(0.0), jnp.float32(0.0)))
        knorm[0] = n1
        knorm[1] = n2

    qt = q_ref[0].astype(F32).T
    sub = lax.broadcasted_iota(jnp.int32, qt.shape, 0)
    q1t = jnp.where(sub < HEAD_DIM, qt, 0.0).astype(BF16)
    q2t = jnp.where(sub >= HEAD_DIM, qt, 0.0).astype(BF16)
    states = ((m1, a1), (m2, a2))
    for m_ref, a_ref in states:
        a_ref[...] = jnp.zeros_like(a_ref)

    far_bias, bias_max, bias_min = brange_ref[3 * h], brange_ref[3 * h + 1], brange_ref[3 * h + 2]
    qsq = qt * qt
    bound1 = jnp.sqrt(jnp.sum(jnp.where(sub < HEAD_DIM, qsq, 0.0), axis=0, keepdims=True) * knorm[0])
    bound2 = jnp.sqrt(jnp.sum(jnp.where(sub >= HEAD_DIM, qsq, 0.0), axis=0, keepdims=True) * knorm[1])
    bounds = (bound1 * BOUND_SLACK + bias_max, bound2 * BOUND_SLACK + bias_max)
    spread = 2.0 * BOUND_SLACK * jnp.maximum(jnp.max(bound1), jnp.max(bound2)) + (bias_max - bias_min)
    bounded = spread <= BOUNDED_EXP2_SPAN

    @pl.when(bounded)
    def _():
        qmaps = ((q1t, bounds[0], a1), (q2t, bounds[1], a2))

        def accumulate(blocks):
            kbs = [kbf[pl.ds(pl.multiple_of(j * tk, tk), tk), :] for j, _ in blocks]
            ps = []
            for (j, tile), kb in zip(blocks, kbs):
                for qmt, shift, _ in qmaps:
                    add = (far_bias - shift) if tile is None else (tile - shift)
                    ps.append(jnp.exp2(_dot(kb, qmt) + add).astype(BF16))
            for mi, (_, _, a_ref) in enumerate(qmaps):
                total = None
                for bi, (j, _) in enumerate(blocks):
                    term = _dot(vtb[j], ps[bi * len(qmaps) + mi])
                    total = term if total is None else total + term
                a_ref[...] += total

        @pl.when(qi == 0)
        def _():
            accumulate([(qi, bias_ref[0, 0])])

        @pl.when(qi >= 1)
        def _():
            accumulate([(qi, bias_ref[0, 0]), (qi - 1, bias_ref[0, 1])])

        n_far = jnp.maximum(qi - 1, 0)
        one = n_far & 1
        two = n_far & 2

        @pl.when(one == 1)
        def _():
            accumulate([(0, None)])

        @pl.when(two == 2)
        def _():
            accumulate([(one, None), (one + 1, None)])

        def quad(i, c):
            j = one + two + 4 * i
            accumulate([(j, None), (j + 1, None), (j + 2, None), (j + 3, None)])
            return c
        lax.fori_loop(0, lax.shift_right_logical(n_far, 2), quad, 0)

    @pl.when(jnp.logical_not(bounded))
    def _():
        for m_ref, _ in states:
            m_ref[...] = jnp.full_like(m_ref, NEG)
        n_blocks = qi + 1

        def scores(t, buf):
            kb = kbf[pl.ds(pl.multiple_of((qi - t) * tk, tk), tk), :]
            s_scr[buf, 0] = _dot(kb, q1t)
            s_scr[buf, 1] = _dot(kb, q2t)

        def update(t, buf):
            bias = bias_ref[0, jnp.minimum(t, bias_ref.shape[1] - 1)]
            vt = vtb[qi - t]
            for mi, (m_ref, a_ref) in enumerate(states):
                _softmax_step_km(s_scr[buf, mi] + bias, vt, m_ref, a_ref)

        odd = n_blocks & 1

        @pl.when(odd == 1)
        def _():
            scores(0, 0)
            update(0, 0)

        n_pairs = lax.shift_right_logical(n_blocks, 1)

        @pl.when(n_pairs > 0)
        def _():
            scores(odd, 0)

        def pair(p, c):
            t0 = odd + 2 * p
            scores(t0 + 1, 1)
            update(t0, 0)
            scores(jnp.minimum(t0 + 2, qi), 0)
            update(t0 + 1, 1)
            return c
        lax.fori_loop(0, n_pairs, pair, 0)

    out_t = (a1[0:dv, :] / a1[dv:dv + 1, :]
             - lam_ref[0] * (a2[0:dv, :] / a2[dv:dv + 1, :]))
    o_ref[0] = out_t.T


def _attn_a(lam, brange, qa, ka, va, bias):
    b, s, w = qa.shape
    nh = w // LANES
    tq = tk = MIXER_A_TILE
    vrows = LANES + BF16_SUBLANES
    return pl.pallas_call(
        _attn_a_kernel,
        grid=(b, nh, s // tq),
        in_specs=[
            pl.BlockSpec(memory_space=pltpu.SMEM),
            pl.BlockSpec(memory_space=pltpu.SMEM),
            pl.BlockSpec((1, tq, LANES), lambda bi, h, qi: (bi, qi, h)),
            pl.BlockSpec((1, s, LANES), lambda bi, h, qi: (bi, 0, h)),
            pl.BlockSpec((1, s, LANES), lambda bi, h, qi: (bi, 0, h)),
            pl.BlockSpec((1,) + bias.shape[1:], lambda bi, h, qi: (h, 0, 0, 0)),
        ],
        out_specs=pl.BlockSpec((1, tq, LANES), lambda bi, h, qi: (bi, qi, h)),
        out_shape=jax.ShapeDtypeStruct((b, s, w), F32),
        scratch_shapes=[
            pltpu.VMEM((s // tk, vrows, tk), BF16),
            pltpu.VMEM((2, 2, tk, tq), F32), pltpu.SMEM((2,), F32),
            pltpu.VMEM((1, tq), F32), pltpu.VMEM((vrows, tq), F32),
            pltpu.VMEM((1, tq), F32), pltpu.VMEM((vrows, tq), F32),
        ],
        compiler_params=_cparams(("parallel", "parallel", "arbitrary")),
        name="mixer_a",
    )(lam, brange, qa, ka, va, bias)


def _stick_block(z, valid, umat, carry):
    sp = jnp.maximum(z, 0.0) + jnp.log(1.0 + jnp.exp(-jnp.abs(z)))
    log1m = -sp
    if valid is not None:
        log1m = jnp.where(valid, log1m, 0.0)
    hi = log1m.astype(BF16)
    lo = (log1m - hi.astype(F32)).astype(BF16)
    after = _dot(hi, umat) + _dot(lo, umat)
    w = jnp.exp(z - sp + after + carry)
    if valid is not None:
        w = jnp.where(valid, w, 0.0)
    return w, jnp.sum(log1m, axis=1, keepdims=True)


def _stick_block_km(z, valid, umat, carry):
    sp = jnp.maximum(z, 0.0) + jnp.log(1.0 + jnp.exp(-jnp.abs(z)))
    log1m = -sp
    if valid is not None:
        log1m = jnp.where(valid, log1m, 0.0)
    hi = log1m.astype(BF16)
    lo = (log1m - hi.astype(F32)).astype(BF16)
    after = _dot(umat, hi) + _dot(umat, lo)
    w = jnp.exp(z - sp + after + carry)
    if valid is not None:
        w = jnp.where(valid, w, 0.0)
    return w, jnp.sum(log1m, axis=0, keepdims=True)


def _attn_b_kernel(q_ref, k_ref, v_ref, u_ref, o_ref, vtb, c_scr, acc_scr, z_scr):
    step = pl.program_id(2)
    kbf = k_ref.at[0]
    seq = kbf.shape[0]
    tk = vtb.shape[2]
    n_halves = q_ref.shape[1] // tk
    n_heads = LANES // HEAD_DIM

    @pl.when(step == 0)
    def _():
        def prep(j, c):
            r = pl.multiple_of(j * tk, tk)
            vtb[j] = v_ref[0, pl.ds(r, tk), :].astype(F32).T.astype(BF16)
            return c
        lax.fori_loop(0, seq // tk, prep, 0)

    sub = lax.broadcasted_iota(jnp.int32, (LANES, tk), 0)
    qts = []
    for half in range(n_halves):
        qt = q_ref[0, half * tk:(half + 1) * tk, :].astype(F32).T
        qts.append([jnp.where((sub >= hh * HEAD_DIM) & (sub < (hh + 1) * HEAD_DIM), qt, 0.0).astype(BF16)
                    for hh in range(n_heads)])
    krow = lax.broadcasted_iota(jnp.int32, (tk, tk), 0)
    qcol = lax.broadcasted_iota(jnp.int32, (tk, tk), 1)
    umat = u_ref[...]

    def blocks(half, js, valids, first):
        kbs = [kbf[pl.ds(pl.multiple_of(j * tk, tk), tk), :] for j in js]
        cmax = None
        for hh in range(n_heads):
            carry = 0.0 if first else c_scr[half, hh]
            pv = None
            for j, kb, valid in zip(js, kbs, valids):
                w, cs = _stick_block_km(_dot(kb, qts[half][hh]), valid, umat, carry)
                term = _dot(vtb[j], w.astype(BF16))
                pv = term if pv is None else pv + term
                carry = carry + cs
            acc_scr[half, hh] = pv if first else acc_scr[half, hh] + pv
            c_scr[half, hh] = carry
            cm = jnp.max(carry)
            cmax = cm if cmax is None else jnp.maximum(cmax, cm)
        return cmax

    own_valid = krow < qcol
    everywhere = krow >= 0

    chains = []
    for half in range(n_halves):
        qb = step * n_halves + half
        prev_valid = None if half >= 1 else everywhere & (qb >= 1)
        for hh in range(n_heads):
            for j, valid in ((qb, own_valid), (jnp.maximum(qb - 1, 0), prev_valid)):
                chains.append((half, hh, j, valid))
    for c, (half, hh, j, valid) in enumerate(chains):
        z_scr[c] = _dot(kbf[pl.ds(pl.multiple_of(j * tk, tk), tk), :], qts[half][hh])
    col_sums = []
    for c, (half, hh, j, valid) in enumerate(chains):
        z = z_scr[c]
        sp = jnp.maximum(z, 0.0) + jnp.log(1.0 + jnp.exp(-jnp.abs(z)))
        log1m = -sp if valid is None else jnp.where(valid, -sp, 0.0)
        hi = log1m.astype(BF16)
        lo = (log1m - hi.astype(F32)).astype(BF16)
        z_scr[c] = z - sp + _dot(umat, hi) + _dot(umat, lo)
        col_sums.append(jnp.sum(log1m, axis=0, keepdims=True))
    cmaxes = []
    for half in range(n_halves):
        cmax = None
        for hh in range(n_heads):
            carry, pv = 0.0, None
            for c, (ch, chh, j, valid) in enumerate(chains):
                if (ch, chh) != (half, hh):
                    continue
                w = jnp.exp(z_scr[c] + carry)
                if valid is not None:
                    w = jnp.where(valid, w, 0.0)
                term = _dot(vtb[j], w.astype(BF16))
                pv = term if pv is None else pv + term
                carry = carry + col_sums[c]
            acc_scr[half, hh] = pv
            c_scr[half, hh] = carry
            cm = jnp.max(carry)
            cmax = cm if cmax is None else jnp.maximum(cmax, cm)
        cmaxes.append(cmax)

    for half in range(n_halves):
        qb = step * n_halves + half

        def cond(st):
            j, cmax = st
            return (j >= 0) & (cmax > STICK_SKIP)

        def body(st, half=half):
            j, _ = st
            return j - 2, blocks(half, [j, jnp.maximum(j - 1, 0)], [None, everywhere & (j >= 1)], False)

        lax.while_loop(cond, body, (qb - 2, cmaxes[half]))
        out_t = jnp.where(sub < HEAD_DIM, acc_scr[half, 0], acc_scr[half, 1])
        o_ref[0, half * tk:(half + 1) * tk, :] = out_t.T.astype(o_ref.dtype)


def _strict_lower(n):
    return jnp.asarray(np.tril(np.ones((n, n)), -1), BF16)


def _strict_upper(n):
    return jnp.asarray(np.triu(np.ones((n, n)), 1), BF16)


def _attn_b(qb, kb, vb):
    b, s, w = qb.shape
    n_halves = 8
    tq = n_halves * ATTN_TK
    return pl.pallas_call(
        _attn_b_kernel,
        grid=(b, w // LANES, s // tq),
        in_specs=[
            pl.BlockSpec((1, tq, LANES), lambda bi, h, qi: (bi, qi, h)),
            pl.BlockSpec((1, s, LANES), lambda bi, h, qi: (bi, 0, h)),
            pl.BlockSpec((1, s, LANES), lambda bi, h, qi: (bi, 0, h)),
            pl.BlockSpec((ATTN_TK, ATTN_TK), lambda bi, h, qi: (0, 0)),
        ],
        out_specs=pl.BlockSpec((1, tq, LANES), lambda bi, h, qi: (bi, qi, h)),
        out_shape=jax.ShapeDtypeStruct((b, s, w), BF16),
        scratch_shapes=[
            pltpu.VMEM((s // ATTN_TK, LANES, ATTN_TK), BF16),
            pltpu.VMEM((n_halves, LANES // HEAD_DIM, 1, ATTN_TK), F32),
            pltpu.VMEM((n_halves, LANES // HEAD_DIM, LANES, ATTN_TK), F32),
            pltpu.VMEM((2 * n_halves * (LANES // HEAD_DIM), ATTN_TK, ATTN_TK), F32),
        ],
        compiler_params=_cparams(("parallel", "parallel", "arbitrary")),
        name="mixer_b",
    )(qb, kb, vb, _strict_upper(ATTN_TK))


def _attn_c_kernel(q_ref, k_ref, v_ref, bias_ref, o_ref, kbf, vbf, s_scr):
    step = pl.program_id(2)
    seq = k_ref.shape[1]
    tq = ATTN_TQ
    n_blocks = q_ref.shape[1] // tq
    pad = C_PREV * CHUNK
    n_heads = LANES // HEAD_DIM

    @pl.when(step == 0)
    def _():
        kbf[0:pad, :] = jnp.zeros((pad, LANES), BF16)
        vbf[0:pad, 0:LANES] = jnp.zeros((pad, LANES), BF16)
        vbf[:, LANES:] = jnp.ones((vbf.shape[0], vbf.shape[1] - LANES), BF16)
        _cast_rows(k_ref, kbf, seq, dst_off=pad)

        def vcast(i, c):
            r = pl.multiple_of(i * tq, tq)
            vbf[pl.ds(pad + r, tq), 0:LANES] = v_ref[0, pl.ds(r, tq), :].astype(BF16)
            return c
        lax.fori_loop(0, seq // tq, vcast, 0)

    lane = lax.broadcasted_iota(jnp.int32, (tq, LANES), 1)
    col = lax.broadcasted_iota(jnp.int32, (tq, BAND_W), 1)
    for blk in range(n_blocks):
        qb = step * n_blocks + blk
        q = q_ref[0, blk * tq:(blk + 1) * tq, :]
        kw = kbf[pl.ds(pl.multiple_of(qb * tq, tq), BAND_W), :]
        in_seq = col >= pad - qb * tq
        for hh in range(n_heads):
            qm = _keep_lanes(q, hh * HEAD_DIM, (hh + 1) * HEAD_DIM)
            s_scr[blk * n_heads + hh] = jnp.where(in_seq, _dot_nt(qm, kw) + bias_ref[hh], NEG)
    for blk in range(n_blocks):
        qb = step * n_blocks + blk
        vw = vbf[pl.ds(pl.multiple_of(qb * tq, tq), BAND_W), :]
        outs = []
        for hh in range(n_heads):
            s = s_scr[blk * n_heads + hh]
            p = jnp.exp(s - jnp.max(s, axis=1, keepdims=True))
            acc = _dot(p.astype(BF16), vw)
            outs.append(acc[:, 0:LANES] / acc[:, LANES:])
        o_ref[0, blk * tq:(blk + 1) * tq, :] = jnp.where(lane < HEAD_DIM, outs[0], outs[1]).astype(o_ref.dtype)


def _attn_c(qc, kc, vc, bias):
    b, s, w = qc.shape
    n_blocks = 8
    tq = n_blocks * ATTN_TQ
    pad = C_PREV * CHUNK
    hp = LANES // HEAD_DIM
    return pl.pallas_call(
        _attn_c_kernel,
        grid=(b, w // LANES, s // tq),
        in_specs=[
            pl.BlockSpec((1, tq, LANES), lambda bi, h, qi: (bi, qi, h)),
            pl.BlockSpec((1, s, LANES), lambda bi, h, qi: (bi, 0, h)),
            pl.BlockSpec((1, s, LANES), lambda bi, h, qi: (bi, 0, h)),
            pl.BlockSpec((hp, ATTN_TQ, BAND_W), lambda bi, h, qi: (h, 0, 0)),
        ],
        out_specs=pl.BlockSpec((1, tq, LANES), lambda bi, h, qi: (bi, qi, h)),
        out_shape=jax.ShapeDtypeStruct((b, s, w), BF16),
        scratch_shapes=[pltpu.VMEM((s + pad, LANES), BF16), pltpu.VMEM((s + pad, 2 * LANES), BF16),
                        pltpu.VMEM((n_blocks * hp, ATTN_TQ, BAND_W), F32)],
        compiler_params=_cparams(("parallel", "parallel", "arbitrary")),
        name="mixer_c",
    )(qc, kc, vc, bias)


def _merge_kernel(x_ref, oa_ref, ob_ref, oc_ref, g_ref, wgate_ref, sub_ref,
                  wa_ref, wb_ref, wc_ref, wout_ref, o_ref):
    x = x_ref[...]
    d = x.shape[1]
    h = _rms_rows(x, g_ref[...]).astype(BF16)
    oa = oa_ref[...]
    parts = []
    for c in range(oa.shape[1] // LANES):
        oc_ = oa[:, c * LANES:(c + 1) * LANES]
        parts.append(oc_ * lax.rsqrt(jnp.mean(oc_ * oc_, axis=-1, keepdims=True) + EPS))
    oan = (jnp.concatenate(parts, axis=1) * sub_ref[...]).astype(BF16)
    merged = jax.nn.sigmoid(_dot(h, wgate_ref[:, 0:d])) * _dot(oan, wa_ref[...])
    merged += jax.nn.sigmoid(_dot(h, wgate_ref[:, d:2 * d])) * _dot(ob_ref[...], wb_ref[...])
    merged += jax.nn.sigmoid(_dot(h, wgate_ref[:, 2 * d:3 * d])) * _dot(oc_ref[...], wc_ref[...])
    o_ref[...] = x + _dot(merged.astype(BF16), wout_ref[...])


def _merge(x, oa, ob, oc, g, wgate, sub, wa, wb, wc, wout, layer, tm):
    n, d = x.shape
    full = lambda a: pl.BlockSpec(a.shape, lambda i: (0,) * a.ndim, pipeline_mode=pl.Buffered(1))
    of_layer = lambda a: pl.BlockSpec((None,) + a.shape[1:], lambda i: (layer,) + (0,) * (a.ndim - 1),
                                      pipeline_mode=pl.Buffered(1))
    rows = lambda a: pl.BlockSpec((tm, a.shape[1]), lambda i: (i, 0))
    g = g.reshape(1, d)
    return pl.pallas_call(
        _merge_kernel,
        grid=(n // tm,),
        in_specs=[rows(x), rows(oa), rows(ob), rows(oc), full(g), of_layer(wgate), full(sub),
                  of_layer(wa), of_layer(wb), of_layer(wc), of_layer(wout)],
        out_specs=rows(x),
        out_shape=jax.ShapeDtypeStruct((n, d), F32),
        compiler_params=_cparams(("parallel",)),
        name="merge",
    )(x, oa, ob, oc, g, wgate, sub, wa, wb, wc, wout)


def _cross_kernel(x_ref, g_ref, wq_ref, gain_ref, gmat_ref, mk_ref, mv_ref, wo_ref, o_ref):
    x = x_ref[0]
    h = _rms_rows(x, g_ref[...]).astype(BF16)
    q = (_group_rms(_dot(h, wq_ref[...]), gmat_ref[...]) * gain_ref[...]).astype(BF16)
    mk = mk_ref[0]
    mv = mv_ref[0]
    lane = lax.broadcasted_iota(jnp.int32, q.shape, 1)
    o = jnp.zeros(q.shape, F32)
    for hh in range(q.shape[1] // HEAD_DIM):
        in_head = (lane >= hh * HEAD_DIM) & (lane < (hh + 1) * HEAD_DIM)
        s = _dot_nt(_keep_lanes(q, hh * HEAD_DIM, (hh + 1) * HEAD_DIM), mk)
        p = jnp.exp(s - jnp.max(s, axis=1, keepdims=True))
        l = jnp.sum(p, axis=1, keepdims=True)
        o = jnp.where(in_head, _dot(p.astype(BF16), mv) / l, o)
    o_ref[0] = x + _dot(o.astype(BF16), wo_ref[...])


def _cross(x, g, wq, gain, mk, mv, wo, layer, tm):
    b, s, d = x.shape
    full = lambda a: pl.BlockSpec(a.shape, lambda bi, i: (0,) * a.ndim)
    of_layer = lambda a: pl.BlockSpec((None,) + a.shape[1:], lambda bi, i: (layer,) + (0,) * (a.ndim - 1))
    g = g.reshape(1, d)
    gmat = _group_mean_matrix()
    return pl.pallas_call(
        _cross_kernel,
        grid=(b, s // tm),
        in_specs=[
            pl.BlockSpec((1, tm, d), lambda bi, i: (bi, i, 0)),
            full(g), of_layer(wq), full(gain), full(gmat),
            pl.BlockSpec((1,) + mk.shape[1:], lambda bi, i: (bi, 0, 0)),
            pl.BlockSpec((1,) + mv.shape[1:], lambda bi, i: (bi, 0, 0)),
            of_layer(wo),
        ],
        out_specs=pl.BlockSpec((1, tm, d), lambda bi, i: (bi, i, 0)),
        out_shape=jax.ShapeDtypeStruct((b, s, d), F32),
        compiler_params=_cparams(("parallel", "parallel")),
        name="cross",
    )(x, g, wq, gain, gmat, mk, mv, wo)


def _sample_attn_kernel(lam_ref,
                        qa_ref, kan_ref, van_ref, kac_ref, vac_ref, bac_ref, ban_ref,
                        qb_ref, kbn_ref, vbn_ref, kbc_ref, vbc_ref, ubig_ref, usmall_ref,
                        qc_ref, kcn_ref, vcn_ref, kcc_ref, vcc_ref, bcc_ref, bcn_ref,
                        oa_ref, ob_ref, oc_ref):
    lam = lam_ref[0]
    ns = qa_ref.shape[1]
    bf = lambda r: r[0].astype(BF16)

    def heads(q, width):
        for hh in range(LANES // width):
            yield hh, None, _keep_lanes(q, hh * width, (hh + 1) * width)

    def softmax2(s_c, s_n, exp=jnp.exp):
        m = jnp.maximum(jnp.max(s_c, axis=1, keepdims=True), jnp.max(s_n, axis=1, keepdims=True))
        p_c = exp(s_c - m)
        p_n = exp(s_n - m)
        inv = 1.0 / (jnp.sum(p_c, axis=1, keepdims=True) + jnp.sum(p_n, axis=1, keepdims=True))
        return p_c * inv, p_n * inv

    for h in range(qa_ref.shape[2] // LANES):
        sl = slice(h * LANES, (h + 1) * LANES)
        q = qa_ref[0, :, sl]
        k_c = kac_ref[0, :, h, :].astype(BF16)
        k_n = kan_ref[0, :, sl].astype(BF16)
        maps = []
        for _, _, qm in heads(q, HEAD_DIM):
            maps.append(softmax2(_dot_nt(qm, k_c) + bac_ref[h], _dot_nt(qm, k_n) + ban_ref[h], exp=jnp.exp2))
        a_c = (maps[0][0] - lam * maps[1][0]).astype(BF16)
        a_n = (maps[0][1] - lam * maps[1][1]).astype(BF16)
        oa_ref[0, :, sl] = (_dot(a_c, vac_ref[0, :, h, :].astype(BF16))
                            + _dot(a_n, van_ref[0, :, sl].astype(BF16)))

    past = kbc_ref.shape[1]
    tk = ubig_ref.shape[0]
    row = lax.broadcasted_iota(jnp.int32, (ns, ns), 0)
    col = lax.broadcasted_iota(jnp.int32, (ns, ns), 1)
    for pr in range(qb_ref.shape[2] // LANES):
        sl = slice(pr * LANES, (pr + 1) * LANES)
        q = qb_ref[0, :, sl]
        k_n = kbn_ref[0, :, sl].astype(BF16)
        v_n = vbn_ref[0, :, sl].astype(BF16)
        outs = []
        for _, _, qm in heads(q, HEAD_DIM):
            w, carry = _stick_block(_dot_nt(qm, k_n), col < row, usmall_ref[...], 0.0)
            acc = _dot(w.astype(BF16), v_n)
            for j in range(past // tk - 1, -1, -1):
                k_c = kbc_ref[0, j * tk:(j + 1) * tk, sl].astype(BF16)
                v_c = vbc_ref[0, j * tk:(j + 1) * tk, sl].astype(BF16)
                w, rs = _stick_block(_dot_nt(qm, k_c), None, ubig_ref[...], carry)
                acc += _dot(w.astype(BF16), v_c)
                carry = carry + rs
            outs.append(acc)
        lane = lax.broadcasted_iota(jnp.int32, q.shape, 1)
        ob_ref[0, :, sl] = jnp.where(lane < HEAD_DIM, outs[0], outs[1]).astype(ob_ref.dtype)

    for pr in range(qc_ref.shape[2] // LANES):
        sl = slice(pr * LANES, (pr + 1) * LANES)
        q = qc_ref[0, :, sl]
        k_c = kcc_ref[0, :, sl].astype(BF16)
        k_n = kcn_ref[0, :, sl].astype(BF16)
        v_c = vcc_ref[0, :, sl].astype(BF16)
        v_n = vcn_ref[0, :, sl].astype(BF16)
        outs = []
        for hh, _, qm in heads(q, HEAD_DIM):
            hd = pr * (LANES // HEAD_DIM) + hh
            p_c, p_n = softmax2(_dot_nt(qm, k_c) + bcc_ref[hd], _dot_nt(qm, k_n) + bcn_ref[hd])
            outs.append(_dot(p_c.astype(BF16), v_c) + _dot(p_n.astype(BF16), v_n))
        lane = lax.broadcasted_iota(jnp.int32, q.shape, 1)
        oc_ref[0, :, sl] = jnp.where(lane < HEAD_DIM, outs[0], outs[1]).astype(oc_ref.dtype)


def _sample_attn(layer, lam, qa, ka, va, cak, cav, bac, ban, qb, kb, vb, cbk, cbv,
                 qc, kc, vc, cck, ccv, bcc, bcn):
    b, ns, _ = qa.shape
    per_b = lambda a: pl.BlockSpec((1,) + a.shape[1:], lambda bi: (bi,) + (0,) * (a.ndim - 1))
    full = lambda a: pl.BlockSpec(a.shape, lambda bi: (0,) * a.ndim)
    layer_b = lambda a: pl.BlockSpec((None, 1) + a.shape[2:], lambda bi: (layer, bi) + (0,) * (a.ndim - 2))
    ubig = _strict_lower(ATTN_TK)
    usmall = _strict_lower(ns)
    args = [qa, ka, va, cak, cav, bac, ban, qb, kb, vb, cbk, cbv, ubig, usmall,
            qc, kc, vc, cck, ccv, bcc, bcn]
    specs = [per_b(qa), per_b(ka), per_b(va), layer_b(cak), layer_b(cav), full(bac), full(ban),
             per_b(qb), per_b(kb), per_b(vb), per_b(cbk), per_b(cbv), full(ubig), full(usmall),
             per_b(qc), per_b(kc), per_b(vc), per_b(cck), per_b(ccv), full(bcc), full(bcn)]
    return pl.pallas_call(
        _sample_attn_kernel,
        grid=(b,),
        in_specs=[pl.BlockSpec(memory_space=pltpu.SMEM)] + specs,
        out_specs=[per_b(qa), per_b(qb), per_b(qc)],
        out_shape=[jax.ShapeDtypeStruct(qa.shape, F32), jax.ShapeDtypeStruct(qb.shape, BF16),
                   jax.ShapeDtypeStruct(qc.shape, BF16)],
        compiler_params=_cparams(("parallel",)),
        name="sample_mixers",
    )(lam, *args)


def _t5_bucket_np(rel):
    half = T5_BUCKETS // 2
    max_exact = half // 2
    n = np.abs(rel)
    nf = np.maximum(n, 1).astype(np.float64)
    large = max_exact + (np.log(nf / max_exact) / math.log(T5_MAX_DIST / max_exact)
                         * (half - max_exact)).astype(np.int64)
    large = np.minimum(large, half - 1)
    return np.where(rel > 0, half, 0) + np.where(n < max_exact, n, large)


def _toeplitz(lookup, n_rows, n_cols):
    period = n_rows + n_cols
    slot = np.arange(period)
    diff = np.minimum((slot + n_rows - 1) % period - (n_rows - 1), n_cols - 1)
    vec = lookup(diff).astype(F32)
    flat = jnp.tile(vec, (1, n_rows))[:, :n_rows * (period - 1)]
    return flat.reshape(vec.shape[0], n_rows, period - 1)[:, :, :n_cols]


def _t5_bias_table(t5_bias, qpos, kpos, key_major=False):
    t5_rows = lambda rel: t5_bias[_t5_bucket_np(rel)].T
    mask = (kpos[None, :] // CHUNK) <= (qpos[:, None] // CHUNK)
    if key_major:
        table = _toeplitz(lambda dd: t5_rows(kpos[0] - qpos[0] - dd), len(kpos), len(qpos))
        mask = mask.T
    else:
        table = _toeplitz(lambda dd: t5_rows(kpos[0] - qpos[0] + dd), len(qpos), len(kpos))
    return jnp.where(jnp.asarray(mask)[None], table, NEG)


def _band_bias_table(rel_table, qpos, kpos):
    lookup = lambda dd: rel_table[:, np.clip(kpos[0] - qpos[0] + dd, -REL_CLIP, REL_CLIP) + REL_CLIP]
    table = _toeplitz(lookup, len(qpos), len(kpos))
    qc = qpos[:, None] // CHUNK
    kc = kpos[None, :] // CHUNK
    mask = (kpos[None, :] >= 0) & (kc <= qc) & (kc >= qc - C_PREV)
    return jnp.where(jnp.asarray(mask)[None], table, NEG)


def _lambda_init(layer):
    return 0.8 - 0.6 * math.exp(-0.3 * layer)


def kernel(x_prompt, x_sample, mem_prompt, cache_a_k, cache_a_v, cache_b_k, cache_b_v, cache_c_k, cache_c_v, cache_mem_k, cache_mem_v, t5_bias, ffn1_norm, ffn1_wg, ffn1_wu, ffn1_wd, mix_norm, w_in, a_qnorm, a_knorm, a_lq1, a_lk1, a_lq2, a_lk2, a_subln, c_qnorm, c_knorm, c_rel_bias, w_br_a, w_br_b, w_br_c, w_out, x_norm, mem_norm, x_wq, x_wkv, x_qnorm, x_knorm, x_wo, ffn2_norm, ffn2_wg, ffn2_wu, ffn2_wd):
    bp, sp, d = x_prompt.shape
    bs, ns, _ = x_sample.shape
    depth = w_in.shape[0]
    past = cache_a_k.shape[2]
    w_buf = cache_c_k.shape[2]
    n_mem = mem_prompt.shape[1]
    wa = cache_a_k.shape[3] * cache_a_k.shape[4]
    wb = cache_b_k.shape[3] * cache_b_k.shape[4]
    wc = cache_c_k.shape[3] * cache_c_k.shape[4]
    wx = cache_mem_k.shape[3] * cache_mem_k.shape[4]
    h_a = cache_a_k.shape[3]
    n_qkv = 3 * (wa + wb + wc)
    w_keep = min(C_PREV * CHUNK, sp)
    scale = HEAD_DIM ** -0.5
    assert sp % ATTN_TQ == 0 and ATTN_TQ == ATTN_TK and ATTN_TQ == 4 * CHUNK and sp >= w_keep
    assert past % ATTN_TK == 0
    assert sp % MIXER_A_TILE == 0 and MIXER_A_TILE % CHUNK == 0 and T5_MAX_DIST <= MIXER_A_TILE

    tile = lambda g, reps: jnp.tile(g.astype(F32), reps)
    bf = lambda a: a.astype(BF16)

    loc_a = np.arange(MIXER_A_TILE)
    loc = np.arange(ATTN_TQ)
    a_far = t5_bias[T5_BUCKETS // 2 - 1].astype(F32)
    a_bias = LOG2E * jnp.stack([
        _t5_bias_table(t5_bias, loc_a + MIXER_A_TILE, loc_a + MIXER_A_TILE, key_major=True),
        _t5_bias_table(t5_bias, loc_a + MIXER_A_TILE, loc_a, key_major=True),
        jnp.broadcast_to(a_far[:, None, None], (h_a, MIXER_A_TILE, MIXER_A_TILE))], axis=1)
    t5_log2 = LOG2E * t5_bias.astype(F32)
    a_brange = jnp.stack([LOG2E * a_far, jnp.max(t5_log2, axis=0), jnp.min(t5_log2, axis=0)], axis=1).reshape(-1)
    qpos_s = past + np.arange(ns)
    a_bias_sc = LOG2E * _t5_bias_table(t5_bias, qpos_s, np.arange(past))
    a_bias_sn = LOG2E * _t5_bias_table(t5_bias, qpos_s, qpos_s)
    kpos_sb = past - w_buf + np.arange(w_buf + ns)

    xp = x_prompt.reshape(bp * sp, d)
    xs = x_sample.reshape(bs * ns, d)
    mem = mem_prompt.reshape(bp * n_mem, d)
    outs = {k: [] for k in ("ak_p", "av_p", "bk_p", "bv_p", "ck_p", "cv_p", "mk_p", "mv_p",
                            "ak_s", "av_s", "bk_s", "bv_s", "ck_s", "cv_s")}
    flat = lambda dt: ((dt, 0),)
    dv_a = wa // h_a
    a_kv_prompt = ((F32, dv_a), (BF16, 0))
    b_kv_prompt = ((F32, HEAD_DIM), (BF16, 0))
    c_segs = [(wc, True, True, flat(BF16)), (wc, True, True, flat(F32)), (wc, False, False, flat(F32))]
    qkv_segs_p = [(wa, True, True, flat(BF16)), (wa, True, True, a_kv_prompt), (wa, False, False, a_kv_prompt),
                  (wb, False, True, flat(BF16)), (wb, False, False, b_kv_prompt), (wb, False, False, b_kv_prompt)] + c_segs
    qkv_segs_s = [(wa, True, True, flat(BF16)), (wa, True, True, flat(F32)), (wa, False, False, flat(F32)),
                  (wb, False, True, flat(BF16)), (wb, False, False, flat(F32)), (wb, False, False, flat(F32))] + c_segs

    wg1, wu1, wd1 = bf(ffn1_wg), bf(ffn1_wu), bf(ffn1_wd)
    wg2, wu2, wd2 = bf(ffn2_wg), bf(ffn2_wu), bf(ffn2_wd)
    w_qkv, w_gate = bf(w_in[:, :, :n_qkv]), bf(w_in[:, :, n_qkv:])
    wbr = bf(w_br_a), bf(w_br_b), bf(w_br_c)
    wout = bf(w_out)
    wq, wo, wkv = bf(x_wq), bf(x_wo), bf(x_wkv)

    for i in range(depth):
        qkv_gain = jnp.concatenate([
            tile(a_qnorm[i], wa // HEAD_DIM) * (scale * LOG2E), tile(a_knorm[i], wa // HEAD_DIM), jnp.ones((wa,), F32),
            jnp.full((wb,), scale, F32), jnp.ones((2 * wb,), F32),
            tile(c_qnorm[i], wc // HEAD_DIM) * scale, tile(c_knorm[i], wc // HEAD_DIM), jnp.ones((wc,), F32),
        ]).reshape(1, n_qkv)
        sub = (tile(a_subln[i], h_a) * (1.0 - _lambda_init(i))).reshape(1, wa)
        xq_gain = (tile(x_qnorm[i], wx // HEAD_DIM) * scale).reshape(1, wx)
        dot64 = lambda a, b: jnp.exp(jnp.sum(a.astype(F32) * b.astype(F32)))
        lam = (dot64(a_lq1[i], a_lk1[i]) - dot64(a_lq2[i], a_lk2[i]) + _lambda_init(i)).reshape(1)
        c_bias_p = _band_bias_table(c_rel_bias[i], C_PREV * CHUNK + loc, np.arange(BAND_W))
        c_bias_sc = _band_bias_table(c_rel_bias[i], qpos_s, kpos_sb[:w_buf])
        c_bias_sn = _band_bias_table(c_rel_bias[i], qpos_s, kpos_sb[w_buf:])

        xp = _ffn(xp, ffn1_norm[i], wg1, wu1, wd1, i, tm=1024, tf=1024)
        qa, ka, ka_bf, va, va_bf, qb, kb, kb_bf, vb, vb_bf, qc, kc, vc = _proj(
            xp, mix_norm[i], w_qkv, i, qkv_gain, qkv_segs_p, tm=1024)
        r3 = lambda a: a.reshape(bp, sp, a.shape[-1])
        oa = _attn_a(lam, a_brange, r3(qa), r3(ka_bf), r3(va_bf), a_bias)
        ob = _attn_b(r3(qb), r3(kb_bf), r3(vb_bf))
        oc = _attn_c(r3(qc), r3(kc), r3(vc), c_bias_p)
        xp = _merge(xp, oa.reshape(-1, wa), ob.reshape(-1, wb), oc.reshape(-1, wc), mix_norm[i], w_gate, sub,
                    *wbr, wout, i, tm=1024)
        mk, mv = _proj(mem, mem_norm[i], wkv, i,
                       jnp.concatenate([tile(x_knorm[i], wx // HEAD_DIM), jnp.ones((wx,), F32)]).reshape(1, 2 * wx),
                       [(wx, True, True, flat(F32)), (wx, False, False, flat(F32))], tm=n_mem)
        mk3, mv3 = mk.reshape(bp, n_mem, wx), mv.reshape(bp, n_mem, wx)
        xp = _cross(xp.reshape(bp, sp, d), x_norm[i], wq, xq_gain, bf(mk3), bf(mv3), wo, i, tm=1024).reshape(-1, d)
        xp = _ffn(xp, ffn2_norm[i], wg2, wu2, wd2, i, tm=1024, tf=1024)
        outs["ak_p"].append(ka.reshape(bp, sp, h_a, -1))
        outs["av_p"].append(va.reshape(bp, sp, h_a, -1))
        outs["bk_p"].append(kb.reshape(bp, sp, -1, HEAD_DIM))
        outs["bv_p"].append(vb.reshape(bp, sp, -1, HEAD_DIM))
        outs["ck_p"].append(r3(kc)[:, sp - w_keep:].reshape(bp, w_keep, -1, HEAD_DIM))
        outs["cv_p"].append(r3(vc)[:, sp - w_keep:].reshape(bp, w_keep, -1, HEAD_DIM))
        outs["mk_p"].append(mk3.reshape(bp, n_mem, -1, HEAD_DIM))
        outs["mv_p"].append(mv3.reshape(bp, n_mem, -1, HEAD_DIM))

        xs = _ffn(xs, ffn1_norm[i], wg1, wu1, wd1, i, tm=bs * ns, tf=1024)
        qa, ka, va, qb, kb, vb, qc, kc, vc = _proj(xs, mix_norm[i], w_qkv, i, qkv_gain, qkv_segs_s, tm=bs * ns)
        s3 = lambda a: a.reshape(bs, ns, a.shape[-1])
        c3 = lambda a: a.reshape(bs, a.shape[1], -1)
        oa, ob, oc = _sample_attn(
            i, lam, s3(qa), s3(ka), s3(va), cache_a_k, cache_a_v, a_bias_sc, a_bias_sn,
            s3(qb), s3(kb), s3(vb), c3(cache_b_k[i]), c3(cache_b_v[i]),
            s3(qc), s3(kc), s3(vc), c3(cache_c_k[i]), c3(cache_c_v[i]), c_bias_sc, c_bias_sn)
        xs = _merge(xs, oa.reshape(-1, wa), ob.reshape(-1, wb), oc.reshape(-1, wc), mix_norm[i], w_gate, sub,
                    *wbr, wout, i, tm=bs * ns)
        xs = _cross(xs.reshape(bs, ns, d), x_norm[i], wq, xq_gain, bf(c3(cache_mem_k[i])), bf(c3(cache_mem_v[i])),
                    wo, i, tm=ns).reshape(-1, d)
        xs = _ffn(xs, ffn2_norm[i], wg2, wu2, wd2, i, tm=bs * ns, tf=1024)
        outs["ak_s"].append(ka.reshape(bs, ns, h_a, -1))
        outs["av_s"].append(va.reshape(bs, ns, h_a, -1))
        outs["bk_s"].append(kb.reshape(bs, ns, -1, HEAD_DIM))
        outs["bv_s"].append(vb.reshape(bs, ns, -1, HEAD_DIM))
        outs["ck_s"].append(jnp.concatenate([cache_c_k[i], kc.reshape(bs, ns, -1, HEAD_DIM)], axis=1)[:, ns:])
        outs["cv_s"].append(jnp.concatenate([cache_c_v[i], vc.reshape(bs, ns, -1, HEAD_DIM)], axis=1)[:, ns:])

    st = lambda k: jnp.stack(outs[k])
    return (xp.reshape(bp, sp, d), xs.reshape(bs, ns, d),
            st("ak_p"), st("av_p"), st("bk_p"), st("bv_p"), st("ck_p"), st("cv_p"), st("mk_p"), st("mv_p"),
            st("ak_s"), st("av_s"), st("bk_s"), st("bv_s"), st("ck_s"), st("cv_s"))
```

```python
import functools
import math

import numpy as np
import jax
import jax.numpy as jnp
from jax import lax
from jax.experimental import pallas as pl
from jax.experimental.pallas import tpu as pltpu

F32 = jnp.float32
BF16 = jnp.bfloat16

EPS = 1e-6
HEAD_DIM = 64
CHUNK = 64
C_PREV = 8
REL_CLIP = 128
T5_BUCKETS = 32
T5_MAX_DIST = 128
LANES = 128
BF16_SUBLANES = 16
V7X_MXU_DIM = 256
NEG = -1e30
LOG2E = math.log2(math.e)
BOUND_SLACK = 1.001
BOUNDED_EXP2_SPAN = 100.0
STICK_SKIP = -110.0
V7X_VMEM_LIMIT_BYTES = 56 * 1024 * 1024

MIXER_A_TILE = 512
ATTN_TQ = 256
ATTN_TK = 256
BAND_W = (C_PREV + 4) * CHUNK


def _cparams(sem):
    return pltpu.CompilerParams(dimension_semantics=sem, vmem_limit_bytes=V7X_VMEM_LIMIT_BYTES)


def _rms_rows(x, g):
    return x * lax.rsqrt(jnp.mean(x * x, axis=-1, keepdims=True) + EPS) * g


def _dot(a, b):
    return jnp.dot(a, b, preferred_element_type=F32)


def _dot_nt(a, b):
    return lax.dot_general(a, b, (((1,), (1,)), ((), ())), preferred_element_type=F32)


def _keep_lanes(q, lo, hi):
    lane = lax.broadcasted_iota(jnp.int32, q.shape, 1)
    return jnp.where((lane >= lo) & (lane < hi), q.astype(F32), 0.0).astype(BF16)


def _group_rms(y, gmat):
    wide = gmat.shape[0]
    parts = []
    for c in range(y.shape[1] // wide):
        yc = y[:, c * wide:(c + 1) * wide]
        ms = _dot((yc * yc).astype(BF16), gmat)
        parts.append(yc * lax.rsqrt(ms + EPS))
    return parts[0] if len(parts) == 1 else jnp.concatenate(parts, axis=1)


def _group_mean_matrix():
    g = np.kron(np.eye(V7X_MXU_DIM // HEAD_DIM), np.ones((HEAD_DIM, HEAD_DIM))) / HEAD_DIM
    return jnp.asarray(g, BF16)


def _ffn_kernel(x_ref, g_ref, wg_ref, wu_ref, wd_ref, o_ref, h_scr):
    j = pl.program_id(1)

    @pl.when(j == 0)
    def _():
        x = x_ref[...]
        h_scr[...] = _rms_rows(x, g_ref[...]).astype(BF16)
        o_ref[...] = x

    h = h_scr[...]
    a = _dot(h, wg_ref[...])
    u = _dot(h, wu_ref[...])
    t = a * jax.nn.sigmoid(a) * u
    o_ref[...] += 0.5 * _dot(t.astype(BF16), wd_ref[...])


def _ffn(x, g, wg, wu, wd, layer, tm, tf=512):
    n, d = x.shape
    dff = wg.shape[2]
    return pl.pallas_call(
        _ffn_kernel,
        grid=(n // tm, dff // tf),
        in_specs=[
            pl.BlockSpec((tm, d), lambda i, j: (i, 0)),
            pl.BlockSpec((1, d), lambda i, j: (0, 0)),
            pl.BlockSpec((None, d, tf), lambda i, j: (layer, 0, j)),
            pl.BlockSpec((None, d, tf), lambda i, j: (layer, 0, j)),
            pl.BlockSpec((None, tf, d), lambda i, j: (layer, j, 0)),
        ],
        out_specs=pl.BlockSpec((tm, d), lambda i, j: (i, 0)),
        out_shape=jax.ShapeDtypeStruct((n, d), F32),
        scratch_shapes=[pltpu.VMEM((tm, d), BF16)],
        compiler_params=_cparams(("parallel", "arbitrary")),
        name="ffn",
    )(x, g.reshape(1, d), wg, wu, wd)


def _proj_kernel(x_ref, g_ref, w_ref, gain_ref, gmat_ref, *out_refs, segs):
    h = _rms_rows(x_ref[...], g_ref[...]).astype(BF16)
    off = 0
    out_refs = list(out_refs)
    for width, normed, gained, outs in segs:
        y = _dot(h, w_ref[:, off:off + width])
        if normed:
            y = _group_rms(y, gmat_ref[...])
        if gained:
            y = y * gain_ref[:, off:off + width]
        for _, head_width in outs:
            o_ref = out_refs.pop(0)
            if head_width:
                o_ref[...] = y.reshape(o_ref.shape).astype(o_ref.dtype)
            else:
                o_ref[...] = y.astype(o_ref.dtype)
        off += width


def _proj(x, g, w, layer, gain, segs, tm):
    n, d = x.shape
    wtot = w.shape[2]
    specs, shapes = [], []
    for width, _, _, outs in segs:
        for dt, head_width in outs:
            if head_width:
                nh = width // head_width
                specs.append(pl.BlockSpec((tm, nh, head_width), lambda i: (i, 0, 0)))
                shapes.append(jax.ShapeDtypeStruct((n, nh, head_width), dt))
            else:
                specs.append(pl.BlockSpec((tm, width), lambda i: (i, 0)))
                shapes.append(jax.ShapeDtypeStruct((n, width), dt))
    return pl.pallas_call(
        functools.partial(_proj_kernel, segs=tuple(segs)),
        grid=(n // tm,),
        in_specs=[
            pl.BlockSpec((tm, d), lambda i: (i, 0)),
            pl.BlockSpec((1, d), lambda i: (0, 0)),
            pl.BlockSpec((None, d, wtot), lambda i: (layer, 0, 0), pipeline_mode=pl.Buffered(1)),
            pl.BlockSpec((1, wtot), lambda i: (0, 0)),
            pl.BlockSpec((V7X_MXU_DIM, V7X_MXU_DIM), lambda i: (0, 0)),
        ],
        out_specs=specs,
        out_shape=shapes,
        compiler_params=_cparams(("parallel",)),
        name="proj",
    )(x, g.reshape(1, d), w, gain, _group_mean_matrix())


def _softmax_step_km(s, vt, m_ref, acc_ref):
    m_old = m_ref[...]
    m_new = jnp.maximum(m_old, jnp.max(s, axis=0, keepdims=True))
    alpha = jnp.exp2(m_old - m_new)
    p = jnp.exp2(s - m_new)
    acc_ref[...] = alpha * acc_ref[...] + _dot(vt, p.astype(BF16))
    m_ref[...] = m_new


def _cast_rows(src_ref, dst_ref, rows, dst_off=0, step=512):
    def body(i, c):
        r = pl.multiple_of(i * step, step)
        dst_ref[pl.ds(dst_off + r, step), :] = src_ref[0, pl.ds(r, step), :].astype(BF16)
        return c
    lax.fori_loop(0, rows // step, body, 0)


def _attn_a_kernel(lam_ref, brange_ref, q_ref, k_ref, v_ref, bias_ref, o_ref,
                   vtb, s_scr, knorm, m1, a1, m2, a2):
    h = pl.program_id(1)
    qi = pl.program_id(2)
    kbf = k_ref.at[0]
    seq = kbf.shape[0]
    tq = q_ref.shape[1]
    tk = vtb.shape[2]
    dv = v_ref.shape[2]

    @pl.when(qi == 0)
    def _():
        lane = lax.broadcasted_iota(jnp.int32, (tk, LANES), 1)

        def prep(j, c):
            r = pl.multiple_of(j * tk, tk)
            kb = kbf[pl.ds(r, tk), :]
            vtb[j, 0:dv, :] = v_ref[0, pl.ds(r, tk), :].astype(F32).T.astype(BF16)
            vtb[j, dv:, :] = jnp.ones((vtb.shape[1] - dv, tk), BF16)
            ksq = kb.astype(F32) * kb.astype(F32)
            n1 = jnp.max(jnp.sum(jnp.where(lane < HEAD_DIM, ksq, 0.0), axis=1, keepdims=True))
            n2 = jnp.max(jnp.sum(jnp.where(lane >= HEAD_DIM, ksq, 0.0), axis=1, keepdims=True))
            return jnp.maximum(c[0], n1), jnp.maximum(c[1], n2)
        n1, n2 = lax.fori_loop(0, seq // tk, prep, (jnp.float32(0.0), jnp.float32(0.0)))
        knorm[0] = n1
        knorm[1] = n2

    qt = q_ref[0].astype(F32).T
    sub = lax.broadcasted_iota(jnp.int32, qt.shape, 0)
    q1t = jnp.where(sub < HEAD_DIM, qt, 0.0).astype(BF16)
    q2t = jnp.where(sub >= HEAD_DIM, qt, 0.0).astype(BF16)
    states = ((m1, a1), (m2, a2))
    for m_ref, a_ref in states:
        a_ref[...] = jnp.zeros_like(a_ref)

    far_bias, bias_max, bias_min = brange_ref[3 * h], brange_ref[3 * h + 1], brange_ref[3 * h + 2]
    qsq = qt * qt
    bound1 = jnp.sqrt(jnp.sum(jnp.where(sub < HEAD_DIM, qsq, 0.0), axis=0, keepdims=True) * knorm[0])
    bound2 = jnp.sqrt(jnp.sum(jnp.where(sub >= HEAD_DIM, qsq, 0.0), axis=0, keepdims=True) * knorm[1])
    bounds = (bound1 * BOUND_SLACK + bias_max, bound2 * BOUND_SLACK + bias_max)
    spread = 2.0 * BOUND_SLACK * jnp.maximum(jnp.max(bound1), jnp.max(bound2)) + (bias_max - bias_min)
    bounded = spread <= BOUNDED_EXP2_SPAN

    @pl.when(bounded)
    def _():
        qmaps = ((q1t, bounds[0], a1), (q2t, bounds[1], a2))

        def accumulate(blocks):
            kbs = [kbf[pl.ds(pl.multiple_of(j * tk, tk), tk), :] for j, _ in blocks]
            ps = []
            for (j, tile), kb in zip(blocks, kbs):
                for qmt, shift, _ in qmaps:
                    add = (far_bias - shift) if tile is None else (tile - shift)
                    ps.append(jnp.exp2(_dot(kb, qmt) + add).astype(BF16))
            for mi, (_, _, a_ref) in enumerate(qmaps):
                total = None
                for bi, (j, _) in enumerate(blocks):
                    term = _dot(vtb[j], ps[bi * len(qmaps) + mi])
                    total = term if total is None else total + term
                a_ref[...] += total

        @pl.when(qi == 0)
        def _():
            accumulate([(qi, bias_ref[0, 0])])

        @pl.when(qi >= 1)
        def _():
            accumulate([(qi, bias_ref[0, 0]), (qi - 1, bias_ref[0, 1])])

        n_far = jnp.maximum(qi - 1, 0)
        one = n_far & 1
        two = n_far & 2
        four = n_far & 4

        @pl.when(one == 1)
        def _():
            accumulate([(0, None)])

        @pl.when(two == 2)
        def _():
            accumulate([(one, None), (one + 1, None)])

        @pl.when(four == 4)
        def _():
            accumulate([(one + two + t, None) for t in range(4)])

        def octet(i, c):
            j = one + two + four + 8 * i
            accumulate([(j + t, None) for t in range(8)])
            return c
        lax.fori_loop(0, lax.shift_right_logical(n_far, 3), octet, 0)

    @pl.when(jnp.logical_not(bounded))
    def _():
        for m_ref, _ in states:
            m_ref[...] = jnp.full_like(m_ref, NEG)
        n_blocks = qi + 1

        def scores(t, buf):
            kb = kbf[pl.ds(pl.multiple_of((qi - t) * tk, tk), tk), :]
            s_scr[buf, 0] = _dot(kb, q1t)
            s_scr[buf, 1] = _dot(kb, q2t)

        def update(t, buf):
            bias = bias_ref[0, jnp.minimum(t, bias_ref.shape[1] - 1)]
            vt = vtb[qi - t]
            for mi, (m_ref, a_ref) in enumerate(states):
                _softmax_step_km(s_scr[buf, mi] + bias, vt, m_ref, a_ref)

        odd = n_blocks & 1

        @pl.when(odd == 1)
        def _():
            scores(0, 0)
            update(0, 0)

        n_pairs = lax.shift_right_logical(n_blocks, 1)

        @pl.when(n_pairs > 0)
        def _():
            scores(odd, 0)

        def pair(p, c):
            t0 = odd + 2 * p
            scores(t0 + 1, 1)
            update(t0, 0)
            scores(jnp.minimum(t0 + 2, qi), 0)
            update(t0 + 1, 1)
            return c
        lax.fori_loop(0, n_pairs, pair, 0)

    out_t = (a1[0:dv, :] / a1[dv:dv + 1, :]
             - lam_ref[0] * (a2[0:dv, :] / a2[dv:dv + 1, :]))
    o_ref[0] = out_t.T


def _attn_a(lam, brange, qa, ka, va, bias):
    b, s, w = qa.shape
    nh = w // LANES
    tq = tk = MIXER_A_TILE
    vrows = LANES + BF16_SUBLANES
    return pl.pallas_call(
        _attn_a_kernel,
        grid=(b, nh, s // tq),
        in_specs=[
            pl.BlockSpec(memory_space=pltpu.SMEM),
            pl.BlockSpec(memory_space=pltpu.SMEM),
            pl.BlockSpec((1, tq, LANES), lambda bi, h, qi: (bi, qi, h)),
            pl.BlockSpec((1, s, LANES), lambda bi, h, qi: (bi, 0, h)),
            pl.BlockSpec((1, s, LANES), lambda bi, h, qi: (bi, 0, h)),
            pl.BlockSpec((1,) + bias.shape[1:], lambda bi, h, qi: (h, 0, 0, 0)),
        ],
        out_specs=pl.BlockSpec((1, tq, LANES), lambda bi, h, qi: (bi, qi, h)),
        out_shape=jax.ShapeDtypeStruct((b, s, w), F32),
        scratch_shapes=[
            pltpu.VMEM((s // tk, vrows, tk), BF16),
            pltpu.VMEM((2, 2, tk, tq), F32), pltpu.SMEM((2,), F32),
            pltpu.VMEM((1, tq), F32), pltpu.VMEM((vrows, tq), F32),
            pltpu.VMEM((1, tq), F32), pltpu.VMEM((vrows, tq), F32),
        ],
        compiler_params=_cparams(("parallel", "parallel", "arbitrary")),
        name="mixer_a",
    )(lam, brange, qa, ka, va, bias)


def _stick_block(z, valid, umat, carry):
    sp = jnp.maximum(z, 0.0) + jnp.log(1.0 + jnp.exp(-jnp.abs(z)))
    log1m = -sp
    if valid is not None:
        log1m = jnp.where(valid, log1m, 0.0)
    hi = log1m.astype(BF16)
    lo = (log1m - hi.astype(F32)).astype(BF16)
    after = _dot(hi, umat) + _dot(lo, umat)
    w = jnp.exp(z - sp + after + carry)
    if valid is not None:
        w = jnp.where(valid, w, 0.0)
    return w, jnp.sum(log1m, axis=1, keepdims=True)


def _stick_block_km(z, valid, umat, carry):
    sp = jnp.maximum(z, 0.0) + jnp.log(1.0 + jnp.exp(-jnp.abs(z)))
    log1m = -sp
    if valid is not None:
        log1m = jnp.where(valid, log1m, 0.0)
    hi = log1m.astype(BF16)
    lo = (log1m - hi.astype(F32)).astype(BF16)
    after = _dot(umat, hi) + _dot(umat, lo)
    w = jnp.exp(z - sp + after + carry)
    if valid is not None:
        w = jnp.where(valid, w, 0.0)
    return w, jnp.sum(log1m, axis=0, keepdims=True)


def _attn_b_kernel(q_ref, k_ref, v_ref, u_ref, o_ref, vtb, c_scr, acc_scr, z_scr):
    step = pl.program_id(2)
    kbf = k_ref.at[0]
    seq = kbf.shape[0]
    tk = vtb.shape[2]
    n_halves = q_ref.shape[1] // tk
    n_heads = LANES // HEAD_DIM

    @pl.when(step == 0)
    def _():
        def prep(j, c):
            r = pl.multiple_of(j * tk, tk)
            vtb[j] = v_ref[0, pl.ds(r, tk), :].astype(F32).T.astype(BF16)
            return c
        lax.fori_loop(0, seq // tk, prep, 0)

    sub = lax.broadcasted_iota(jnp.int32, (LANES, tk), 0)
    qts = []
    for half in range(n_halves):
        qt = q_ref[0, half * tk:(half + 1) * tk, :].astype(F32).T
        qts.append([jnp.where((sub >= hh * HEAD_DIM) & (sub < (hh + 1) * HEAD_DIM), qt, 0.0).astype(BF16)
                    for hh in range(n_heads)])
    krow = lax.broadcasted_iota(jnp.int32, (tk, tk), 0)
    qcol = lax.broadcasted_iota(jnp.int32, (tk, tk), 1)
    umat = u_ref[...]

    def blocks(half, js, valids, first):
        kbs = [kbf[pl.ds(pl.multiple_of(j * tk, tk), tk), :] for j in js]
        cmax = None
        for hh in range(n_heads):
            carry = 0.0 if first else c_scr[half, hh]
            pv = None
            for j, kb, valid in zip(js, kbs, valids):
                w, cs = _stick_block_km(_dot(kb, qts[half][hh]), valid, umat, carry)
                term = _dot(vtb[j], w.astype(BF16))
                pv = term if pv is None else pv + term
                carry = carry + cs
            acc_scr[half, hh] = pv if first else acc_scr[half, hh] + pv
            c_scr[half, hh] = carry
            cm = jnp.max(carry)
            cmax = cm if cmax is None else jnp.maximum(cmax, cm)
        return cmax

    own_valid = krow < qcol
    everywhere = krow >= 0

    chains = []
    for half in range(n_halves):
        qb = step * n_halves + half
        prev_valid = None if half >= 1 else everywhere & (qb >= 1)
        for hh in range(n_heads):
            for j, valid in ((qb, own_valid), (jnp.maximum(qb - 1, 0), prev_valid)):
                chains.append((half, hh, j, valid))
    for c, (half, hh, j, valid) in enumerate(chains):
        z_scr[c] = _dot(kbf[pl.ds(pl.multiple_of(j * tk, tk), tk), :], qts[half][hh])
    col_sums = []
    for c, (half, hh, j, valid) in enumerate(chains):
        z = z_scr[c]
        sp = jnp.maximum(z, 0.0) + jnp.log(1.0 + jnp.exp(-jnp.abs(z)))
        log1m = -sp if valid is None else jnp.where(valid, -sp, 0.0)
        hi = log1m.astype(BF16)
        lo = (log1m - hi.astype(F32)).astype(BF16)
        z_scr[c] = z - sp + _dot(umat, hi) + _dot(umat, lo)
        col_sums.append(jnp.sum(log1m, axis=0, keepdims=True))
    cmaxes = []
    for half in range(n_halves):
        cmax = None
        for hh in range(n_heads):
            carry, pv = 0.0, None
            for c, (ch, chh, j, valid) in enumerate(chains):
                if (ch, chh) != (half, hh):
                    continue
                w = jnp.exp(z_scr[c] + carry)
                if valid is not None:
                    w = jnp.where(valid, w, 0.0)
                term = _dot(vtb[j], w.astype(BF16))
                pv = term if pv is None else pv + term
                carry = carry + col_sums[c]
            acc_scr[half, hh] = pv
            c_scr[half, hh] = carry
            cm = jnp.max(carry)
            cmax = cm if cmax is None else jnp.maximum(cmax, cm)
        cmaxes.append(cmax)

    for half in range(n_halves):
        qb = step * n_halves + half

        def cond(st):
            j, cmax = st
            return (j >= 0) & (cmax > STICK_SKIP)

        def body(st, half=half):
            j, _ = st
            return j - 2, blocks(half, [j, jnp.maximum(j - 1, 0)], [None, everywhere & (j >= 1)], False)

        lax.while_loop(cond, body, (qb - 2, cmaxes[half]))
        out_t = jnp.where(sub < HEAD_DIM, acc_scr[half, 0], acc_scr[half, 1])
        o_ref[0, half * tk:(half + 1) * tk, :] = out_t.T.astype(o_ref.dtype)


def _strict_lower(n):
    return jnp.asarray(np.tril(np.ones((n, n)), -1), BF16)


def _strict_upper(n):
    return jnp.asarray(np.triu(np.ones((n, n)), 1), BF16)


def _attn_b(qb, kb, vb):
    b, s, w = qb.shape
    n_halves = 8
    tq = n_halves * ATTN_TK
    return pl.pallas_call(
        _attn_b_kernel,
        grid=(b, w // LANES, s // tq),
        in_specs=[
            pl.BlockSpec((1, tq, LANES), lambda bi, h, qi: (bi, qi, h)),
            pl.BlockSpec((1, s, LANES), lambda bi, h, qi: (bi, 0, h)),
            pl.BlockSpec((1, s, LANES), lambda bi, h, qi: (bi, 0, h)),
            pl.BlockSpec((ATTN_TK, ATTN_TK), lambda bi, h, qi: (0, 0)),
        ],
        out_specs=pl.BlockSpec((1, tq, LANES), lambda bi, h, qi: (bi, qi, h)),
        out_shape=jax.ShapeDtypeStruct((b, s, w), BF16),
        scratch_shapes=[
            pltpu.VMEM((s // ATTN_TK, LANES, ATTN_TK), BF16),
            pltpu.VMEM((n_halves, LANES // HEAD_DIM, 1, ATTN_TK), F32),
            pltpu.VMEM((n_halves, LANES // HEAD_DIM, LANES, ATTN_TK), F32),
            pltpu.VMEM((2 * n_halves * (LANES // HEAD_DIM), ATTN_TK, ATTN_TK), F32),
        ],
        compiler_params=_cparams(("parallel", "parallel", "arbitrary")),
        name="mixer_b",
    )(qb, kb, vb, _strict_upper(ATTN_TK))


def _attn_c_kernel(q_ref, k_ref, v_ref, bias_ref, o_ref, kbf, vbf, s_scr):
    step = pl.program_id(2)
    seq = k_ref.shape[1]
    tq = ATTN_TQ
    n_blocks = q_ref.shape[1] // tq
    pad = C_PREV * CHUNK
    n_heads = LANES // HEAD_DIM

    @pl.when(step == 0)
    def _():
        kbf[0:pad, :] = jnp.zeros((pad, LANES), BF16)
        vbf[0:pad, 0:LANES] = jnp.zeros((pad, LANES), BF16)
        vbf[:, LANES:] = jnp.ones((vbf.shape[0], vbf.shape[1] - LANES), BF16)
        _cast_rows(k_ref, kbf, seq, dst_off=pad)

        def vcast(i, c):
            r = pl.multiple_of(i * tq, tq)
            vbf[pl.ds(pad + r, tq), 0:LANES] = v_ref[0, pl.ds(r, tq), :].astype(BF16)
            return c
        lax.fori_loop(0, seq // tq, vcast, 0)

    lane = lax.broadcasted_iota(jnp.int32, (tq, LANES), 1)
    col = lax.broadcasted_iota(jnp.int32, (tq, BAND_W), 1)
    for blk in range(n_blocks):
        qb = step * n_blocks + blk
        q = q_ref[0, blk * tq:(blk + 1) * tq, :]
        kw = kbf[pl.ds(pl.multiple_of(qb * tq, tq), BAND_W), :]
        in_seq = col >= pad - qb * tq
        for hh in range(n_heads):
            qm = _keep_lanes(q, hh * HEAD_DIM, (hh + 1) * HEAD_DIM)
            s_scr[blk * n_heads + hh] = jnp.where(in_seq, _dot_nt(qm, kw) + bias_ref[hh], NEG)
    for blk in range(n_blocks):
        qb = step * n_blocks + blk
        vw = vbf[pl.ds(pl.multiple_of(qb * tq, tq), BAND_W), :]
        outs = []
        for hh in range(n_heads):
            s = s_scr[blk * n_heads + hh]
            p = jnp.exp(s - jnp.max(s, axis=1, keepdims=True))
            acc = _dot(p.astype(BF16), vw)
            outs.append(acc[:, 0:LANES] / acc[:, LANES:])
        o_ref[0, blk * tq:(blk + 1) * tq, :] = jnp.where(lane < HEAD_DIM, outs[0], outs[1]).astype(o_ref.dtype)


def _attn_c(qc, kc, vc, bias):
    b, s, w = qc.shape
    n_blocks = 8
    tq = n_blocks * ATTN_TQ
    pad = C_PREV * CHUNK
    hp = LANES // HEAD_DIM
    return pl.pallas_call(
        _attn_c_kernel,
        grid=(b, w // LANES, s // tq),
        in_specs=[
            pl.BlockSpec((1, tq, LANES), lambda bi, h, qi: (bi, qi, h)),
            pl.BlockSpec((1, s, LANES), lambda bi, h, qi: (bi, 0, h)),
            pl.BlockSpec((1, s, LANES), lambda bi, h, qi: (bi, 0, h)),
            pl.BlockSpec((hp, ATTN_TQ, BAND_W), lambda bi, h, qi: (h, 0, 0)),
        ],
        out_specs=pl.BlockSpec((1, tq, LANES), lambda bi, h, qi: (bi, qi, h)),
        out_shape=jax.ShapeDtypeStruct((b, s, w), BF16),
        scratch_shapes=[pltpu.VMEM((s + pad, LANES), BF16), pltpu.VMEM((s + pad, 2 * LANES), BF16),
                        pltpu.VMEM((n_blocks * hp, ATTN_TQ, BAND_W), F32)],
        compiler_params=_cparams(("parallel", "parallel", "arbitrary")),
        name="mixer_c",
    )(qc, kc, vc, bias)


def _merge_kernel(x_ref, oa_ref, ob_ref, oc_ref, g_ref, wgate_ref, sub_ref,
                  wa_ref, wb_ref, wc_ref, wout_ref, o_ref):
    x = x_ref[...]
    d = x.shape[1]
    h = _rms_rows(x, g_ref[...]).astype(BF16)
    oa = oa_ref[...]
    parts = []
    for c in range(oa.shape[1] // LANES):
        oc_ = oa[:, c * LANES:(c + 1) * LANES]
        parts.append(oc_ * lax.rsqrt(jnp.mean(oc_ * oc_, axis=-1, keepdims=True) + EPS))
    oan = (jnp.concatenate(parts, axis=1) * sub_ref[...]).astype(BF16)
    merged = jax.nn.sigmoid(_dot(h, wgate_ref[:, 0:d])) * _dot(oan, wa_ref[...])
    merged += jax.nn.sigmoid(_dot(h, wgate_ref[:, d:2 * d])) * _dot(ob_ref[...], wb_ref[...])
    merged += jax.nn.sigmoid(_dot(h, wgate_ref[:, 2 * d:3 * d])) * _dot(oc_ref[...], wc_ref[...])
    o_ref[...] = x + _dot(merged.astype(BF16), wout_ref[...])


def _merge(x, oa, ob, oc, g, wgate, sub, wa, wb, wc, wout, layer, tm):
    n, d = x.shape
    full = lambda a: pl.BlockSpec(a.shape, lambda i: (0,) * a.ndim, pipeline_mode=pl.Buffered(1))
    of_layer = lambda a: pl.BlockSpec((None,) + a.shape[1:], lambda i: (layer,) + (0,) * (a.ndim - 1),
                                      pipeline_mode=pl.Buffered(1))
    rows = lambda a: pl.BlockSpec((tm, a.shape[1]), lambda i: (i, 0))
    g = g.reshape(1, d)
    return pl.pallas_call(
        _merge_kernel,
        grid=(n // tm,),
        in_specs=[rows(x), rows(oa), rows(ob), rows(oc), full(g), of_layer(wgate), full(sub),
                  of_layer(wa), of_layer(wb), of_layer(wc), of_layer(wout)],
        out_specs=rows(x),
        out_shape=jax.ShapeDtypeStruct((n, d), F32),
        compiler_params=_cparams(("parallel",)),
        name="merge",
    )(x, oa, ob, oc, g, wgate, sub, wa, wb, wc, wout)


def _cross_kernel(x_ref, g_ref, wq_ref, gain_ref, gmat_ref, mk_ref, mv_ref, wo_ref, o_ref):
    x = x_ref[0]
    h = _rms_rows(x, g_ref[...]).astype(BF16)
    q = (_group_rms(_dot(h, wq_ref[...]), gmat_ref[...]) * gain_ref[...]).astype(BF16)
    mk = mk_ref[0]
    mv = mv_ref[0]
    lane = lax.broadcasted_iota(jnp.int32, q.shape, 1)
    o = jnp.zeros(q.shape, F32)
    for hh in range(q.shape[1] // HEAD_DIM):
        in_head = (lane >= hh * HEAD_DIM) & (lane < (hh + 1) * HEAD_DIM)
        s = _dot_nt(_keep_lanes(q, hh * HEAD_DIM, (hh + 1) * HEAD_DIM), mk)
        p = jnp.exp(s - jnp.max(s, axis=1, keepdims=True))
        l = jnp.sum(p, axis=1, keepdims=True)
        o = jnp.where(in_head, _dot(p.astype(BF16), mv) / l, o)
    o_ref[0] = x + _dot(o.astype(BF16), wo_ref[...])


def _cross(x, g, wq, gain, mk, mv, wo, layer, tm):
    b, s, d = x.shape
    full = lambda a: pl.BlockSpec(a.shape, lambda bi, i: (0,) * a.ndim)
    of_layer = lambda a: pl.BlockSpec((None,) + a.shape[1:], lambda bi, i: (layer,) + (0,) * (a.ndim - 1))
    g = g.reshape(1, d)
    gmat = _group_mean_matrix()
    return pl.pallas_call(
        _cross_kernel,
        grid=(b, s // tm),
        in_specs=[
            pl.BlockSpec((1, tm, d), lambda bi, i: (bi, i, 0)),
            full(g), of_layer(wq), full(gain), full(gmat),
            pl.BlockSpec((1,) + mk.shape[1:], lambda bi, i: (bi, 0, 0)),
            pl.BlockSpec((1,) + mv.shape[1:], lambda bi, i: (bi, 0, 0)),
            of_layer(wo),
        ],
        out_specs=pl.BlockSpec((1, tm, d), lambda bi, i: (bi, i, 0)),
        out_shape=jax.ShapeDtypeStruct((b, s, d), F32),
        compiler_params=_cparams(("parallel", "parallel")),
        name="cross",
    )(x, g, wq, gain, gmat, mk, mv, wo)


def _sample_attn_kernel(lam_ref,
                        qa_ref, kan_ref, van_ref, kac_ref, vac_ref, bac_ref, ban_ref,
                        qb_ref, kbn_ref, vbn_ref, kbc_ref, vbc_ref, ubig_ref, usmall_ref,
                        qc_ref, kcn_ref, vcn_ref, kcc_ref, vcc_ref, bcc_ref, bcn_ref,
                        oa_ref, ob_ref, oc_ref):
    lam = lam_ref[0]
    ns = qa_ref.shape[1]
    bf = lambda r: r[0].astype(BF16)

    def heads(q, width):
        for hh in range(LANES // width):
            yield hh, None, _keep_lanes(q, hh * width, (hh + 1) * width)

    def softmax2(s_c, s_n, exp=jnp.exp):
        m = jnp.maximum(jnp.max(s_c, axis=1, keepdims=True), jnp.max(s_n, axis=1, keepdims=True))
        p_c = exp(s_c - m)
        p_n = exp(s_n - m)
        inv = 1.0 / (jnp.sum(p_c, axis=1, keepdims=True) + jnp.sum(p_n, axis=1, keepdims=True))
        return p_c * inv, p_n * inv

    for h in range(qa_ref.shape[2] // LANES):
        sl = slice(h * LANES, (h + 1) * LANES)
        q = qa_ref[0, :, sl]
        k_c = kac_ref[0, :, h, :].astype(BF16)
        k_n = kan_ref[0, :, sl].astype(BF16)
        maps = []
        for _, _, qm in heads(q, HEAD_DIM):
            maps.append(softmax2(_dot_nt(qm, k_c) + bac_ref[h], _dot_nt(qm, k_n) + ban_ref[h], exp=jnp.exp2))
        a_c = (maps[0][0] - lam * maps[1][0]).astype(BF16)
        a_n = (maps[0][1] - lam * maps[1][1]).astype(BF16)
        oa_ref[0, :, sl] = (_dot(a_c, vac_ref[0, :, h, :].astype(BF16))
                            + _dot(a_n, van_ref[0, :, sl].astype(BF16)))

    past = kbc_ref.shape[1]
    tk = ubig_ref.shape[0]
    row = lax.broadcasted_iota(jnp.int32, (ns, ns), 0)
    col = lax.broadcasted_iota(jnp.int32, (ns, ns), 1)
    for pr in range(qb_ref.shape[2] // LANES):
        sl = slice(pr * LANES, (pr + 1) * LANES)
        q = qb_ref[0, :, sl]
        k_n = kbn_ref[0, :, sl].astype(BF16)
        v_n = vbn_ref[0, :, sl].astype(BF16)
        outs = []
        for _, _, qm in heads(q, HEAD_DIM):
            w, carry = _stick_block(_dot_nt(qm, k_n), col < row, usmall_ref[...], 0.0)
            acc = _dot(w.astype(BF16), v_n)
            for j in range(past // tk - 1, -1, -1):
                k_c = kbc_ref[0, j * tk:(j + 1) * tk, sl].astype(BF16)
                v_c = vbc_ref[0, j * tk:(j + 1) * tk, sl].astype(BF16)
                w, rs = _stick_block(_dot_nt(qm, k_c), None, ubig_ref[...], carry)
                acc += _dot(w.astype(BF16), v_c)
                carry = carry + rs
            outs.append(acc)
        lane = lax.broadcasted_iota(jnp.int32, q.shape, 1)
        ob_ref[0, :, sl] = jnp.where(lane < HEAD_DIM, outs[0], outs[1]).astype(ob_ref.dtype)

    for pr in range(qc_ref.shape[2] // LANES):
        sl = slice(pr * LANES, (pr + 1) * LANES)
        q = qc_ref[0, :, sl]
        k_c = kcc_ref[0, :, sl].astype(BF16)
        k_n = kcn_ref[0, :, sl].astype(BF16)
        v_c = vcc_ref[0, :, sl].astype(BF16)
        v_n = vcn_ref[0, :, sl].astype(BF16)
        outs = []
        for hh, _, qm in heads(q, HEAD_DIM):
            hd = pr * (LANES // HEAD_DIM) + hh
            p_c, p_n = softmax2(_dot_nt(qm, k_c) + bcc_ref[hd], _dot_nt(qm, k_n) + bcn_ref[hd])
            outs.append(_dot(p_c.astype(BF16), v_c) + _dot(p_n.astype(BF16), v_n))
        lane = lax.broadcasted_iota(jnp.int32, q.shape, 1)
        oc_ref[0, :, sl] = jnp.where(lane < HEAD_DIM, outs[0], outs[1]).astype(oc_ref.dtype)


def _sample_attn(layer, lam, qa, ka, va, cak, cav, bac, ban, qb, kb, vb, cbk, cbv,
                 qc, kc, vc, cck, ccv, bcc, bcn):
    b, ns, _ = qa.shape
    per_b = lambda a: pl.BlockSpec((1,) + a.shape[1:], lambda bi: (bi,) + (0,) * (a.ndim - 1))
    full = lambda a: pl.BlockSpec(a.shape, lambda bi: (0,) * a.ndim)
    layer_b = lambda a: pl.BlockSpec((None, 1) + a.shape[2:], lambda bi: (layer, bi) + (0,) * (a.ndim - 2))
    ubig = _strict_lower(ATTN_TK)
    usmall = _strict_lower(ns)
    args = [qa, ka, va, cak, cav, bac, ban, qb, kb, vb, cbk, cbv, ubig, usmall,
            qc, kc, vc, cck, ccv, bcc, bcn]
    specs = [per_b(qa), per_b(ka), per_b(va), layer_b(cak), layer_b(cav), full(bac), full(ban),
             per_b(qb), per_b(kb), per_b(vb), per_b(cbk), per_b(cbv), full(ubig), full(usmall),
             per_b(qc), per_b(kc), per_b(vc), per_b(cck), per_b(ccv), full(bcc), full(bcn)]
    return pl.pallas_call(
        _sample_attn_kernel,
        grid=(b,),
        in_specs=[pl.BlockSpec(memory_space=pltpu.SMEM)] + specs,
        out_specs=[per_b(qa), per_b(qb), per_b(qc)],
        out_shape=[jax.ShapeDtypeStruct(qa.shape, F32), jax.ShapeDtypeStruct(qb.shape, BF16),
                   jax.ShapeDtypeStruct(qc.shape, BF16)],
        compiler_params=_cparams(("parallel",)),
        name="sample_mixers",
    )(lam, *args)


def _t5_bucket_np(rel):
    half = T5_BUCKETS // 2
    max_exact = half // 2
    n = np.abs(rel)
    nf = np.maximum(n, 1).astype(np.float64)
    large = max_exact + (np.log(nf / max_exact) / math.log(T5_MAX_DIST / max_exact)
                         * (half - max_exact)).astype(np.int64)
    large = np.minimum(large, half - 1)
    return np.where(rel > 0, half, 0) + np.where(n < max_exact, n, large)


def _toeplitz(lookup, n_rows, n_cols):
    period = n_rows + n_cols
    slot = np.arange(period)
    diff = np.minimum((slot + n_rows - 1) % period - (n_rows - 1), n_cols - 1)
    vec = lookup(diff).astype(F32)
    flat = jnp.tile(vec, (1, n_rows))[:, :n_rows * (period - 1)]
    return flat.reshape(vec.shape[0], n_rows, period - 1)[:, :, :n_cols]


def _t5_bias_table(t5_bias, qpos, kpos, key_major=False):
    t5_rows = lambda rel: t5_bias[_t5_bucket_np(rel)].T
    mask = (kpos[None, :] // CHUNK) <= (qpos[:, None] // CHUNK)
    if key_major:
        table = _toeplitz(lambda dd: t5_rows(kpos[0] - qpos[0] - dd), len(kpos), len(qpos))
        mask = mask.T
    else:
        table = _toeplitz(lambda dd: t5_rows(kpos[0] - qpos[0] + dd), len(qpos), len(kpos))
    return jnp.where(jnp.asarray(mask)[None], table, NEG)


def _band_bias_table(rel_table, qpos, kpos):
    lookup = lambda dd: rel_table[:, np.clip(kpos[0] - qpos[0] + dd, -REL_CLIP, REL_CLIP) + REL_CLIP]
    table = _toeplitz(lookup, len(qpos), len(kpos))
    qc = qpos[:, None] // CHUNK
    kc = kpos[None, :] // CHUNK
    mask = (kpos[None, :] >= 0) & (kc <= qc) & (kc >= qc - C_PREV)
    return jnp.where(jnp.asarray(mask)[None], table, NEG)


def _lambda_init(layer):
    return 0.8 - 0.6 * math.exp(-0.3 * layer)


def kernel(x_prompt, x_sample, mem_prompt, cache_a_k, cache_a_v, cache_b_k, cache_b_v, cache_c_k, cache_c_v, cache_mem_k, cache_mem_v, t5_bias, ffn1_norm, ffn1_wg, ffn1_wu, ffn1_wd, mix_norm, w_in, a_qnorm, a_knorm, a_lq1, a_lk1, a_lq2, a_lk2, a_subln, c_qnorm, c_knorm, c_rel_bias, w_br_a, w_br_b, w_br_c, w_out, x_norm, mem_norm, x_wq, x_wkv, x_qnorm, x_knorm, x_wo, ffn2_norm, ffn2_wg, ffn2_wu, ffn2_wd):
    bp, sp, d = x_prompt.shape
    bs, ns, _ = x_sample.shape
    depth = w_in.shape[0]
    past = cache_a_k.shape[2]
    w_buf = cache_c_k.shape[2]
    n_mem = mem_prompt.shape[1]
    wa = cache_a_k.shape[3] * cache_a_k.shape[4]
    wb = cache_b_k.shape[3] * cache_b_k.shape[4]
    wc = cache_c_k.shape[3] * cache_c_k.shape[4]
    wx = cache_mem_k.shape[3] * cache_mem_k.shape[4]
    h_a = cache_a_k.shape[3]
    n_qkv = 3 * (wa + wb + wc)
    w_keep = min(C_PREV * CHUNK, sp)
    scale = HEAD_DIM ** -0.5
    assert sp % ATTN_TQ == 0 and ATTN_TQ == ATTN_TK and ATTN_TQ == 4 * CHUNK and sp >= w_keep
    assert past % ATTN_TK == 0
    assert sp % MIXER_A_TILE == 0 and MIXER_A_TILE % CHUNK == 0 and T5_MAX_DIST <= MIXER_A_TILE

    tile = lambda g, reps: jnp.tile(g.astype(F32), reps)
    bf = lambda a: a.astype(BF16)

    loc_a = np.arange(MIXER_A_TILE)
    loc = np.arange(ATTN_TQ)
    a_far = t5_bias[T5_BUCKETS // 2 - 1].astype(F32)
    a_bias = LOG2E * jnp.stack([
        _t5_bias_table(t5_bias, loc_a + MIXER_A_TILE, loc_a + MIXER_A_TILE, key_major=True),
        _t5_bias_table(t5_bias, loc_a + MIXER_A_TILE, loc_a, key_major=True),
        jnp.broadcast_to(a_far[:, None, None], (h_a, MIXER_A_TILE, MIXER_A_TILE))], axis=1)
    t5_log2 = LOG2E * t5_bias.astype(F32)
    a_brange = jnp.stack([LOG2E * a_far, jnp.max(t5_log2, axis=0), jnp.min(t5_log2, axis=0)], axis=1).reshape(-1)
    qpos_s = past + np.arange(ns)
    a_bias_sc = LOG2E * _t5_bias_table(t5_bias, qpos_s, np.arange(past))
    a_bias_sn = LOG2E * _t5_bias_table(t5_bias, qpos_s, qpos_s)
    kpos_sb = past - w_buf + np.arange(w_buf + ns)

    xp = x_prompt.reshape(bp * sp, d)
    xs = x_sample.reshape(bs * ns, d)
    mem = mem_prompt.reshape(bp * n_mem, d)
    outs = {k: [] for k in ("ak_p", "av_p", "bk_p", "bv_p", "ck_p", "cv_p", "mk_p", "mv_p",
                            "ak_s", "av_s", "bk_s", "bv_s", "ck_s", "cv_s")}
    flat = lambda dt: ((dt, 0),)
    dv_a = wa // h_a
    a_kv_prompt = ((F32, dv_a), (BF16, 0))
    b_kv_prompt = ((F32, HEAD_DIM), (BF16, 0))
    c_segs = [(wc, True, True, flat(BF16)), (wc, True, True, flat(F32)), (wc, False, False, flat(F32))]
    qkv_segs_p = [(wa, True, True, flat(BF16)), (wa, True, True, a_kv_prompt), (wa, False, False, a_kv_prompt),
                  (wb, False, True, flat(BF16)), (wb, False, False, b_kv_prompt), (wb, False, False, b_kv_prompt)] + c_segs
    qkv_segs_s = [(wa, True, True, flat(BF16)), (wa, True, True, flat(F32)), (wa, False, False, flat(F32)),
                  (wb, False, True, flat(BF16)), (wb, False, False, flat(F32)), (wb, False, False, flat(F32))] + c_segs

    wg1, wu1, wd1 = bf(ffn1_wg), bf(ffn1_wu), bf(ffn1_wd)
    wg2, wu2, wd2 = bf(ffn2_wg), bf(ffn2_wu), bf(ffn2_wd)
    w_qkv, w_gate = bf(w_in[:, :, :n_qkv]), bf(w_in[:, :, n_qkv:])
    wbr = bf(w_br_a), bf(w_br_b), bf(w_br_c)
    wout = bf(w_out)
    wq, wo, wkv = bf(x_wq), bf(x_wo), bf(x_wkv)

    for i in range(depth):
        qkv_gain = jnp.concatenate([
            tile(a_qnorm[i], wa // HEAD_DIM) * (scale * LOG2E), tile(a_knorm[i], wa // HEAD_DIM), jnp.ones((wa,), F32),
            jnp.full((wb,), scale, F32), jnp.ones((2 * wb,), F32),
            tile(c_qnorm[i], wc // HEAD_DIM) * scale, tile(c_knorm[i], wc // HEAD_DIM), jnp.ones((wc,), F32),
        ]).reshape(1, n_qkv)
        sub = (tile(a_subln[i], h_a) * (1.0 - _lambda_init(i))).reshape(1, wa)
        xq_gain = (tile(x_qnorm[i], wx // HEAD_DIM) * scale).reshape(1, wx)
        dot64 = lambda a, b: jnp.exp(jnp.sum(a.astype(F32) * b.astype(F32)))
        lam = (dot64(a_lq1[i], a_lk1[i]) - dot64(a_lq2[i], a_lk2[i]) + _lambda_init(i)).reshape(1)
        c_bias_p = _band_bias_table(c_rel_bias[i], C_PREV * CHUNK + loc, np.arange(BAND_W))
        c_bias_sc = _band_bias_table(c_rel_bias[i], qpos_s, kpos_sb[:w_buf])
        c_bias_sn = _band_bias_table(c_rel_bias[i], qpos_s, kpos_sb[w_buf:])

        xp = _ffn(xp, ffn1_norm[i], wg1, wu1, wd1, i, tm=1024, tf=1024)
        qa, ka, ka_bf, va, va_bf, qb, kb, kb_bf, vb, vb_bf, qc, kc, vc = _proj(
            xp, mix_norm[i], w_qkv, i, qkv_gain, qkv_segs_p, tm=1024)
        r3 = lambda a: a.reshape(bp, sp, a.shape[-1])
        oa = _attn_a(lam, a_brange, r3(qa), r3(ka_bf), r3(va_bf), a_bias)
        ob = _attn_b(r3(qb), r3(kb_bf), r3(vb_bf))
        oc = _attn_c(r3(qc), r3(kc), r3(vc), c_bias_p)
        xp = _merge(xp, oa.reshape(-1, wa), ob.reshape(-1, wb), oc.reshape(-1, wc), mix_norm[i], w_gate, sub,
                    *wbr, wout, i, tm=1024)
        mk, mv = _proj(mem, mem_norm[i], wkv, i,
                       jnp.concatenate([tile(x_knorm[i], wx // HEAD_DIM), jnp.ones((wx,), F32)]).reshape(1, 2 * wx),
                       [(wx, True, True, flat(F32)), (wx, False, False, flat(F32))], tm=n_mem)
        mk3, mv3 = mk.reshape(bp, n_mem, wx), mv.reshape(bp, n_mem, wx)
        xp = _cross(xp.reshape(bp, sp, d), x_norm[i], wq, xq_gain, bf(mk3), bf(mv3), wo, i, tm=1024).reshape(-1, d)
        xp = _ffn(xp, ffn2_norm[i], wg2, wu2, wd2, i, tm=1024, tf=1024)
        outs["ak_p"].append(ka.reshape(bp, sp, h_a, -1))
        outs["av_p"].append(va.reshape(bp, sp, h_a, -1))
        outs["bk_p"].append(kb.reshape(bp, sp, -1, HEAD_DIM))
        outs["bv_p"].append(vb.reshape(bp, sp, -1, HEAD_DIM))
        outs["ck_p"].append(r3(kc)[:, sp - w_keep:].reshape(bp, w_keep, -1, HEAD_DIM))
        outs["cv_p"].append(r3(vc)[:, sp - w_keep:].reshape(bp, w_keep, -1, HEAD_DIM))
        outs["mk_p"].append(mk3.reshape(bp, n_mem, -1, HEAD_DIM))
        outs["mv_p"].append(mv3.reshape(bp, n_mem, -1, HEAD_DIM))

        xs = _ffn(xs, ffn1_norm[i], wg1, wu1, wd1, i, tm=bs * ns, tf=1024)
        qa, ka, va, qb, kb, vb, qc, kc, vc = _proj(xs, mix_norm[i], w_qkv, i, qkv_gain, qkv_segs_s, tm=bs * ns)
        s3 = lambda a: a.reshape(bs, ns, a.shape[-1])
        c3 = lambda a: a.reshape(bs, a.shape[1], -1)
        oa, ob, oc = _sample_attn(
            i, lam, s3(qa), s3(ka), s3(va), cache_a_k, cache_a_v, a_bias_sc, a_bias_sn,
            s3(qb), s3(kb), s3(vb), c3(cache_b_k[i]), c3(cache_b_v[i]),
            s3(qc), s3(kc), s3(vc), c3(cache_c_k[i]), c3(cache_c_v[i]), c_bias_sc, c_bias_sn)
        xs = _merge(xs, oa.reshape(-1, wa), ob.reshape(-1, wb), oc.reshape(-1, wc), mix_norm[i], w_gate, sub,
                    *wbr, wout, i, tm=bs * ns)
        xs = _cross(xs.reshape(bs, ns, d), x_norm[i], wq, xq_gain, bf(c3(cache_mem_k[i])), bf(c3(cache_mem_v[i])),
                    wo, i, tm=ns).reshape(-1, d)
        xs = _ffn(xs, ffn2_norm[i], wg2, wu2, wd2, i, tm=bs * ns, tf=1024)
        outs["ak_s"].append(ka.reshape(bs, ns, h_a, -1))
        outs["av_s"].append(va.reshape(bs, ns, h_a, -1))
        outs["bk_s"].append(kb.reshape(bs, ns, -1, HEAD_DIM))
        outs["bv_s"].append(vb.reshape(bs, ns, -1, HEAD_DIM))
        outs["ck_s"].append(jnp.concatenate([cache_c_k[i], kc.reshape(bs, ns, -1, HEAD_DIM)], axis=1)[:, ns:])
        outs["cv_s"].append(jnp.concatenate([cache_c_v[i], vc.reshape(bs, ns, -1, HEAD_DIM)], axis=1)[:, ns:])

    st = lambda k: jnp.stack(outs[k])
    return (xp.reshape(bp, sp, d), xs.reshape(bs, ns, d),
            st("ak_p"), st("av_p"), st("bk_p"), st("bv_p"), st("ck_p"), st("cv_p"), st("mk_p"), st("mv_p"),
            st("ak_s"), st("av_s"), st("bk_s"), st("bv_s"), st("ck_s"), st("cv_s"))
```

```python
import functools
import math

import numpy as np
import jax
import jax.numpy as jnp
from jax import lax
from jax.experimental import pallas as pl
from jax.experimental.pallas import tpu as pltpu

F32 = jnp.float32
BF16 = jnp.bfloat16

EPS = 1e-6
HEAD_DIM = 64
CHUNK = 64
C_PREV = 8
REL_CLIP = 128
T5_BUCKETS = 32
T5_MAX_DIST = 128
LANES = 128
BF16_SUBLANES = 16
V7X_MXU_DIM = 256
NEG = -1e30
LOG2E = math.log2(math.e)
BOUND_SLACK = 1.001
BOUNDED_EXP2_SPAN = 100.0
STICK_SKIP = -110.0
V7X_VMEM_LIMIT_BYTES = 56 * 1024 * 1024

MIXER_A_TILE = 512
ATTN_TQ = 256
ATTN_TK = 256
MIXER_BC_BLOCKS = 8
BAND_W = (C_PREV + 4) * CHUNK


def _cparams(sem):
    return pltpu.CompilerParams(dimension_semantics=sem, vmem_limit_bytes=V7X_VMEM_LIMIT_BYTES)


def _rms_rows(x, g):
    return x * lax.rsqrt(jnp.mean(x * x, axis=-1, keepdims=True) + EPS) * g


def _dot(a, b):
    return jnp.dot(a, b, preferred_element_type=F32)


def _dot_nt(a, b):
    return lax.dot_general(a, b, (((1,), (1,)), ((), ())), preferred_element_type=F32)


def _keep_lanes(q, lo, hi):
    lane = lax.broadcasted_iota(jnp.int32, q.shape, 1)
    return jnp.where((lane >= lo) & (lane < hi), q.astype(F32), 0.0).astype(BF16)


def _group_rms(y, gmat):
    wide = gmat.shape[0]
    parts = []
    for c in range(y.shape[1] // wide):
        yc = y[:, c * wide:(c + 1) * wide]
        ms = _dot((yc * yc).astype(BF16), gmat)
        parts.append(yc * lax.rsqrt(ms + EPS))
    return parts[0] if len(parts) == 1 else jnp.concatenate(parts, axis=1)


def _group_mean_matrix():
    g = np.kron(np.eye(V7X_MXU_DIM // HEAD_DIM), np.ones((HEAD_DIM, HEAD_DIM))) / HEAD_DIM
    return jnp.asarray(g, BF16)


def _ffn_kernel(x_ref, g_ref, wg_ref, wu_ref, wd_ref, o_ref, h_scr):
    j = pl.program_id(1)

    @pl.when(j == 0)
    def _():
        x = x_ref[...]
        h_scr[...] = _rms_rows(x, g_ref[...]).astype(BF16)
        o_ref[...] = x

    h = h_scr[...]
    a = _dot(h, wg_ref[...])
    u = _dot(h, wu_ref[...])
    t = a * jax.nn.sigmoid(a) * u
    o_ref[...] += 0.5 * _dot(t.astype(BF16), wd_ref[...])


def _ffn(x, g, wg, wu, wd, layer, tm, tf=512):
    n, d = x.shape
    dff = wg.shape[2]
    return pl.pallas_call(
        _ffn_kernel,
        grid=(n // tm, dff // tf),
        in_specs=[
            pl.BlockSpec((tm, d), lambda i, j: (i, 0)),
            pl.BlockSpec((1, d), lambda i, j: (0, 0)),
            pl.BlockSpec((None, d, tf), lambda i, j: (layer, 0, j)),
            pl.BlockSpec((None, d, tf), lambda i, j: (layer, 0, j)),
            pl.BlockSpec((None, tf, d), lambda i, j: (layer, j, 0)),
        ],
        out_specs=pl.BlockSpec((tm, d), lambda i, j: (i, 0)),
        out_shape=jax.ShapeDtypeStruct((n, d), F32),
        scratch_shapes=[pltpu.VMEM((tm, d), BF16)],
        compiler_params=_cparams(("parallel", "arbitrary")),
        name="ffn",
    )(x, g.reshape(1, d), wg, wu, wd)


def _proj_kernel(x_ref, g_ref, w_ref, gain_ref, gmat_ref, *out_refs, segs):
    h = _rms_rows(x_ref[...], g_ref[...]).astype(BF16)
    off = 0
    out_refs = list(out_refs)
    for width, normed, gained, outs in segs:
        y = _dot(h, w_ref[:, off:off + width])
        if normed:
            y = _group_rms(y, gmat_ref[...])
        if gained:
            y = y * gain_ref[:, off:off + width]
        for _, head_width in outs:
            o_ref = out_refs.pop(0)
            if head_width:
                o_ref[...] = y.reshape(o_ref.shape).astype(o_ref.dtype)
            else:
                o_ref[...] = y.astype(o_ref.dtype)
        off += width


def _proj(x, g, w, layer, gain, segs, tm):
    n, d = x.shape
    wtot = w.shape[2]
    specs, shapes = [], []
    for width, _, _, outs in segs:
        for dt, head_width in outs:
            if head_width:
                nh = width // head_width
                specs.append(pl.BlockSpec((tm, nh, head_width), lambda i: (i, 0, 0)))
                shapes.append(jax.ShapeDtypeStruct((n, nh, head_width), dt))
            else:
                specs.append(pl.BlockSpec((tm, width), lambda i: (i, 0)))
                shapes.append(jax.ShapeDtypeStruct((n, width), dt))
    return pl.pallas_call(
        functools.partial(_proj_kernel, segs=tuple(segs)),
        grid=(n // tm,),
        in_specs=[
            pl.BlockSpec((tm, d), lambda i: (i, 0)),
            pl.BlockSpec((1, d), lambda i: (0, 0)),
            pl.BlockSpec((None, d, wtot), lambda i: (layer, 0, 0), pipeline_mode=pl.Buffered(1)),
            pl.BlockSpec((1, wtot), lambda i: (0, 0)),
            pl.BlockSpec((V7X_MXU_DIM, V7X_MXU_DIM), lambda i: (0, 0)),
        ],
        out_specs=specs,
        out_shape=shapes,
        compiler_params=_cparams(("parallel",)),
        name="proj",
    )(x, g.reshape(1, d), w, gain, _group_mean_matrix())


def _softmax_step_km(s, vt, m_ref, acc_ref):
    m_old = m_ref[...]
    m_new = jnp.maximum(m_old, jnp.max(s, axis=0, keepdims=True))
    alpha = jnp.exp2(m_old - m_new)
    p = jnp.exp2(s - m_new)
    acc_ref[...] = alpha * acc_ref[...] + _dot(vt, p.astype(BF16))
    m_ref[...] = m_new


def _cast_rows(src_ref, dst_ref, rows, dst_off=0, step=512):
    def body(i, c):
        r = pl.multiple_of(i * step, step)
        dst_ref[pl.ds(dst_off + r, step), :] = src_ref[0, pl.ds(r, step), :].astype(BF16)
        return c
    lax.fori_loop(0, rows // step, body, 0)


def _attn_a_kernel(lam_ref, brange_ref, q_ref, k_ref, v_ref, bias_ref, o_ref,
                   vtb, s_scr, knorm, m1, a1, m2, a2):
    h = pl.program_id(1)
    qi = pl.program_id(2)
    kbf = k_ref.at[0]
    seq = kbf.shape[0]
    tq = q_ref.shape[1]
    tk = vtb.shape[2]
    dv = v_ref.shape[2]

    @pl.when(qi == 0)
    def _():
        lane = lax.broadcasted_iota(jnp.int32, (tk, LANES), 1)

        def prep(j, c):
            r = pl.multiple_of(j * tk, tk)
            kb = kbf[pl.ds(r, tk), :]
            vtb[j, 0:dv, :] = v_ref[0, pl.ds(r, tk), :].astype(F32).T.astype(BF16)
            vtb[j, dv:, :] = jnp.ones((vtb.shape[1] - dv, tk), BF16)
            ksq = kb.astype(F32) * kb.astype(F32)
            n1 = jnp.max(jnp.sum(jnp.where(lane < HEAD_DIM, ksq, 0.0), axis=1, keepdims=True))
            n2 = jnp.max(jnp.sum(jnp.where(lane >= HEAD_DIM, ksq, 0.0), axis=1, keepdims=True))
            return jnp.maximum(c[0], n1), jnp.maximum(c[1], n2)
        n1, n2 = lax.fori_loop(0, seq // tk, prep, (jnp.float32(0.0), jnp.float32(0.0)))
        knorm[0] = n1
        knorm[1] = n2

    qt = q_ref[0].astype(F32).T
    sub = lax.broadcasted_iota(jnp.int32, qt.shape, 0)
    q1t = jnp.where(sub < HEAD_DIM, qt, 0.0).astype(BF16)
    q2t = jnp.where(sub >= HEAD_DIM, qt, 0.0).astype(BF16)
    states = ((m1, a1), (m2, a2))
    for m_ref, a_ref in states:
        a_ref[...] = jnp.zeros_like(a_ref)

    far_bias, bias_max, bias_min = brange_ref[3 * h], brange_ref[3 * h + 1], brange_ref[3 * h + 2]
    qsq = qt * qt
    bound1 = jnp.sqrt(jnp.sum(jnp.where(sub < HEAD_DIM, qsq, 0.0), axis=0, keepdims=True) * knorm[0])
    bound2 = jnp.sqrt(jnp.sum(jnp.where(sub >= HEAD_DIM, qsq, 0.0), axis=0, keepdims=True) * knorm[1])
    bounds = (bound1 * BOUND_SLACK + bias_max, bound2 * BOUND_SLACK + bias_max)
    spread = 2.0 * BOUND_SLACK * jnp.maximum(jnp.max(bound1), jnp.max(bound2)) + (bias_max - bias_min)
    bounded = spread <= BOUNDED_EXP2_SPAN

    @pl.when(bounded)
    def _():
        qmaps = ((q1t, bounds[0], a1), (q2t, bounds[1], a2))

        def accumulate(blocks):
            kbs = [kbf[pl.ds(pl.multiple_of(j * tk, tk), tk), :] for j, _ in blocks]
            ps = []
            for (j, tile), kb in zip(blocks, kbs):
                for qmt, shift, _ in qmaps:
                    add = (far_bias - shift) if tile is None else (tile - shift)
                    ps.append(jnp.exp2(_dot(kb, qmt) + add).astype(BF16))
            for mi, (_, _, a_ref) in enumerate(qmaps):
                total = None
                for bi, (j, _) in enumerate(blocks):
                    term = _dot(vtb[j], ps[bi * len(qmaps) + mi])
                    total = term if total is None else total + term
                a_ref[...] += total

        @pl.when(qi == 0)
        def _():
            accumulate([(qi, bias_ref[0, 0])])

        @pl.when(qi >= 1)
        def _():
            accumulate([(qi, bias_ref[0, 0]), (qi - 1, bias_ref[0, 1])])

        n_far = jnp.maximum(qi - 1, 0)
        one = n_far & 1
        two = n_far & 2
        four = n_far & 4

        @pl.when(one == 1)
        def _():
            accumulate([(0, None)])

        @pl.when(two == 2)
        def _():
            accumulate([(one, None), (one + 1, None)])

        @pl.when(four == 4)
        def _():
            accumulate([(one + two + t, None) for t in range(4)])

        def octet(i, c):
            j = one + two + four + 8 * i
            accumulate([(j + t, None) for t in range(8)])
            return c
        lax.fori_loop(0, lax.shift_right_logical(n_far, 3), octet, 0)

    @pl.when(jnp.logical_not(bounded))
    def _():
        for m_ref, _ in states:
            m_ref[...] = jnp.full_like(m_ref, NEG)
        n_blocks = qi + 1

        def scores(t, buf):
            kb = kbf[pl.ds(pl.multiple_of((qi - t) * tk, tk), tk), :]
            s_scr[buf, 0] = _dot(kb, q1t)
            s_scr[buf, 1] = _dot(kb, q2t)

        def update(t, buf):
            bias = bias_ref[0, jnp.minimum(t, bias_ref.shape[1] - 1)]
            vt = vtb[qi - t]
            for mi, (m_ref, a_ref) in enumerate(states):
                _softmax_step_km(s_scr[buf, mi] + bias, vt, m_ref, a_ref)

        odd = n_blocks & 1

        @pl.when(odd == 1)
        def _():
            scores(0, 0)
            update(0, 0)

        n_pairs = lax.shift_right_logical(n_blocks, 1)

        @pl.when(n_pairs > 0)
        def _():
            scores(odd, 0)

        def pair(p, c):
            t0 = odd + 2 * p
            scores(t0 + 1, 1)
            update(t0, 0)
            scores(jnp.minimum(t0 + 2, qi), 0)
            update(t0 + 1, 1)
            return c
        lax.fori_loop(0, n_pairs, pair, 0)

    out_t = (a1[0:dv, :] / a1[dv:dv + 1, :]
             - lam_ref[0] * (a2[0:dv, :] / a2[dv:dv + 1, :]))
    o_ref[0] = out_t.T


def _attn_a(lam, brange, qa, ka, va, bias):
    b, s, w = qa.shape
    nh = w // LANES
    tq = tk = MIXER_A_TILE
    vrows = LANES + BF16_SUBLANES
    return pl.pallas_call(
        _attn_a_kernel,
        grid=(b, nh, s // tq),
        in_specs=[
            pl.BlockSpec(memory_space=pltpu.SMEM),
            pl.BlockSpec(memory_space=pltpu.SMEM),
            pl.BlockSpec((1, tq, LANES), lambda bi, h, qi: (bi, qi, h)),
            pl.BlockSpec((1, s, LANES), lambda bi, h, qi: (bi, 0, h)),
            pl.BlockSpec((1, s, LANES), lambda bi, h, qi: (bi, 0, h)),
            pl.BlockSpec((1,) + bias.shape[1:], lambda bi, h, qi: (h, 0, 0, 0)),
        ],
        out_specs=pl.BlockSpec((1, tq, LANES), lambda bi, h, qi: (bi, qi, h)),
        out_shape=jax.ShapeDtypeStruct((b, s, w), F32),
        scratch_shapes=[
            pltpu.VMEM((s // tk, vrows, tk), BF16),
            pltpu.VMEM((2, 2, tk, tq), F32), pltpu.SMEM((2,), F32),
            pltpu.VMEM((1, tq), F32), pltpu.VMEM((vrows, tq), F32),
            pltpu.VMEM((1, tq), F32), pltpu.VMEM((vrows, tq), F32),
        ],
        compiler_params=_cparams(("parallel", "parallel", "arbitrary")),
        name="mixer_a",
    )(lam, brange, qa, ka, va, bias)


def _stick_block(z, valid, umat, carry):
    sp = jnp.maximum(z, 0.0) + jnp.log(1.0 + jnp.exp(-jnp.abs(z)))
    log1m = -sp
    if valid is not None:
        log1m = jnp.where(valid, log1m, 0.0)
    hi = log1m.astype(BF16)
    lo = (log1m - hi.astype(F32)).astype(BF16)
    after = _dot(hi, umat) + _dot(lo, umat)
    w = jnp.exp(z - sp + after + carry)
    if valid is not None:
        w = jnp.where(valid, w, 0.0)
    return w, jnp.sum(log1m, axis=1, keepdims=True)


def _stick_block_km(z, valid, umat, carry):
    sp = jnp.maximum(z, 0.0) + jnp.log(1.0 + jnp.exp(-jnp.abs(z)))
    log1m = -sp
    if valid is not None:
        log1m = jnp.where(valid, log1m, 0.0)
    hi = log1m.astype(BF16)
    lo = (log1m - hi.astype(F32)).astype(BF16)
    after = _dot(umat, hi) + _dot(umat, lo)
    w = jnp.exp(z - sp + after + carry)
    if valid is not None:
        w = jnp.where(valid, w, 0.0)
    return w, jnp.sum(log1m, axis=0, keepdims=True)


def _attn_b_kernel(q_ref, k_ref, v_ref, u_ref, o_ref, vtb, c_scr, acc_scr, z_scr):
    step = pl.program_id(2)
    kbf = k_ref.at[0]
    seq = kbf.shape[0]
    tk = vtb.shape[2]
    n_halves = q_ref.shape[1] // tk
    n_heads = LANES // HEAD_DIM

    @pl.when(step == 0)
    def _():
        def prep(j, c):
            r = pl.multiple_of(j * tk, tk)
            vtb[j] = v_ref[0, pl.ds(r, tk), :].astype(F32).T.astype(BF16)
            return c
        lax.fori_loop(0, seq // tk, prep, 0)

    sub = lax.broadcasted_iota(jnp.int32, (LANES, tk), 0)
    qts = []
    for half in range(n_halves):
        qt = q_ref[0, half * tk:(half + 1) * tk, :].astype(F32).T
        qts.append([jnp.where((sub >= hh * HEAD_DIM) & (sub < (hh + 1) * HEAD_DIM), qt, 0.0).astype(BF16)
                    for hh in range(n_heads)])
    krow = lax.broadcasted_iota(jnp.int32, (tk, tk), 0)
    qcol = lax.broadcasted_iota(jnp.int32, (tk, tk), 1)
    umat = u_ref[...]

    def blocks(half, js, valids, first):
        kbs = [kbf[pl.ds(pl.multiple_of(j * tk, tk), tk), :] for j in js]
        cmax = None
        for hh in range(n_heads):
            carry = 0.0 if first else c_scr[half, hh]
            pv = None
            for j, kb, valid in zip(js, kbs, valids):
                w, cs = _stick_block_km(_dot(kb, qts[half][hh]), valid, umat, carry)
                term = _dot(vtb[j], w.astype(BF16))
                pv = term if pv is None else pv + term
                carry = carry + cs
            acc_scr[half, hh] = pv if first else acc_scr[half, hh] + pv
            c_scr[half, hh] = carry
            cm = jnp.max(carry)
            cmax = cm if cmax is None else jnp.maximum(cmax, cm)
        return cmax

    own_valid = krow < qcol
    everywhere = krow >= 0

    chains = []
    for half in range(n_halves):
        qb = step * n_halves + half
        prev_valid = None if half >= 1 else everywhere & (qb >= 1)
        for hh in range(n_heads):
            for j, valid in ((qb, own_valid), (jnp.maximum(qb - 1, 0), prev_valid)):
                chains.append((half, hh, j, valid))
    for c, (half, hh, j, valid) in enumerate(chains):
        z_scr[c] = _dot(kbf[pl.ds(pl.multiple_of(j * tk, tk), tk), :], qts[half][hh])
    col_sums = []
    for c, (half, hh, j, valid) in enumerate(chains):
        z = z_scr[c]
        sp = jnp.maximum(z, 0.0) + jnp.log(1.0 + jnp.exp(-jnp.abs(z)))
        log1m = -sp if valid is None else jnp.where(valid, -sp, 0.0)
        hi = log1m.astype(BF16)
        lo = (log1m - hi.astype(F32)).astype(BF16)
        z_scr[c] = z - sp + _dot(umat, hi) + _dot(umat, lo)
        col_sums.append(jnp.sum(log1m, axis=0, keepdims=True))
    cmaxes = []
    for half in range(n_halves):
        cmax = None
        for hh in range(n_heads):
            carry, pv = 0.0, None
            for c, (ch, chh, j, valid) in enumerate(chains):
                if (ch, chh) != (half, hh):
                    continue
                w = jnp.exp(z_scr[c] + carry)
                if valid is not None:
                    w = jnp.where(valid, w, 0.0)
                term = _dot(vtb[j], w.astype(BF16))
                pv = term if pv is None else pv + term
                carry = carry + col_sums[c]
            acc_scr[half, hh] = pv
            c_scr[half, hh] = carry
            cm = jnp.max(carry)
            cmax = cm if cmax is None else jnp.maximum(cmax, cm)
        cmaxes.append(cmax)

    for half in range(n_halves):
        qb = step * n_halves + half

        def cond(st):
            j, cmax = st
            return (j >= 0) & (cmax > STICK_SKIP)

        def body(st, half=half):
            j, _ = st
            return j - 2, blocks(half, [j, jnp.maximum(j - 1, 0)], [None, everywhere & (j >= 1)], False)

        lax.while_loop(cond, body, (qb - 2, cmaxes[half]))
        out_t = jnp.where(sub < HEAD_DIM, acc_scr[half, 0], acc_scr[half, 1])
        o_ref[0, half * tk:(half + 1) * tk, :] = out_t.T.astype(o_ref.dtype)


def _strict_lower(n):
    return jnp.asarray(np.tril(np.ones((n, n)), -1), BF16)


def _strict_upper(n):
    return jnp.asarray(np.triu(np.ones((n, n)), 1), BF16)


def _attn_b(qb, kb, vb):
    b, s, w = qb.shape
    n_halves = MIXER_BC_BLOCKS
    tq = n_halves * ATTN_TK
    return pl.pallas_call(
        _attn_b_kernel,
        grid=(b, w // LANES, s // tq),
        in_specs=[
            pl.BlockSpec((1, tq, LANES), lambda bi, h, qi: (bi, qi, h)),
            pl.BlockSpec((1, s, LANES), lambda bi, h, qi: (bi, 0, h)),
            pl.BlockSpec((1, s, LANES), lambda bi, h, qi: (bi, 0, h)),
            pl.BlockSpec((ATTN_TK, ATTN_TK), lambda bi, h, qi: (0, 0)),
        ],
        out_specs=pl.BlockSpec((1, tq, LANES), lambda bi, h, qi: (bi, qi, h)),
        out_shape=jax.ShapeDtypeStruct((b, s, w), BF16),
        scratch_shapes=[
            pltpu.VMEM((s // ATTN_TK, LANES, ATTN_TK), BF16),
            pltpu.VMEM((n_halves, LANES // HEAD_DIM, 1, ATTN_TK), F32),
            pltpu.VMEM((n_halves, LANES // HEAD_DIM, LANES, ATTN_TK), F32),
            pltpu.VMEM((2 * n_halves * (LANES // HEAD_DIM), ATTN_TK, ATTN_TK), F32),
        ],
        compiler_params=_cparams(("parallel", "parallel", "arbitrary")),
        name="mixer_b",
    )(qb, kb, vb, _strict_upper(ATTN_TK))


def _attn_c_kernel(q_ref, k_ref, v_ref, bias_ref, o_ref, kbf, vbf, s_scr):
    step = pl.program_id(2)
    seq = k_ref.shape[1]
    tq = ATTN_TQ
    n_blocks = q_ref.shape[1] // tq
    pad = C_PREV * CHUNK
    n_heads = LANES // HEAD_DIM

    @pl.when(step == 0)
    def _():
        kbf[0:pad, :] = jnp.zeros((pad, LANES), BF16)
        vbf[0:pad, 0:LANES] = jnp.zeros((pad, LANES), BF16)
        vbf[:, LANES:] = jnp.ones((vbf.shape[0], vbf.shape[1] - LANES), BF16)
        _cast_rows(k_ref, kbf, seq, dst_off=pad)

        def vcast(i, c):
            r = pl.multiple_of(i * tq, tq)
            vbf[pl.ds(pad + r, tq), 0:LANES] = v_ref[0, pl.ds(r, tq), :].astype(BF16)
            return c
        lax.fori_loop(0, seq // tq, vcast, 0)

    lane = lax.broadcasted_iota(jnp.int32, (tq, LANES), 1)
    col = lax.broadcasted_iota(jnp.int32, (tq, BAND_W), 1)
    for blk in range(n_blocks):
        qb = step * n_blocks + blk
        q = q_ref[0, blk * tq:(blk + 1) * tq, :]
        kw = kbf[pl.ds(pl.multiple_of(qb * tq, tq), BAND_W), :]
        in_seq = col >= pad - qb * tq
        for hh in range(n_heads):
            qm = _keep_lanes(q, hh * HEAD_DIM, (hh + 1) * HEAD_DIM)
            s_scr[blk * n_heads + hh] = jnp.where(in_seq, _dot_nt(qm, kw) + bias_ref[hh], NEG)
    for blk in range(n_blocks):
        qb = step * n_blocks + blk
        vw = vbf[pl.ds(pl.multiple_of(qb * tq, tq), BAND_W), :]
        outs = []
        for hh in range(n_heads):
            s = s_scr[blk * n_heads + hh]
            p = jnp.exp(s - jnp.max(s, axis=1, keepdims=True))
            acc = _dot(p.astype(BF16), vw)
            outs.append(acc[:, 0:LANES] / acc[:, LANES:])
        o_ref[0, blk * tq:(blk + 1) * tq, :] = jnp.where(lane < HEAD_DIM, outs[0], outs[1]).astype(o_ref.dtype)


def _attn_c(qc, kc, vc, bias):
    b, s, w = qc.shape
    n_blocks = MIXER_BC_BLOCKS
    tq = n_blocks * ATTN_TQ
    pad = C_PREV * CHUNK
    hp = LANES // HEAD_DIM
    return pl.pallas_call(
        _attn_c_kernel,
        grid=(b, w // LANES, s // tq),
        in_specs=[
            pl.BlockSpec((1, tq, LANES), lambda bi, h, qi: (bi, qi, h)),
            pl.BlockSpec((1, s, LANES), lambda bi, h, qi: (bi, 0, h)),
            pl.BlockSpec((1, s, LANES), lambda bi, h, qi: (bi, 0, h)),
            pl.BlockSpec((hp, ATTN_TQ, BAND_W), lambda bi, h, qi: (h, 0, 0)),
        ],
        out_specs=pl.BlockSpec((1, tq, LANES), lambda bi, h, qi: (bi, qi, h)),
        out_shape=jax.ShapeDtypeStruct((b, s, w), BF16),
        scratch_shapes=[pltpu.VMEM((s + pad, LANES), BF16), pltpu.VMEM((s + pad, 2 * LANES), BF16),
                        pltpu.VMEM((n_blocks * hp, ATTN_TQ, BAND_W), F32)],
        compiler_params=_cparams(("parallel", "parallel", "arbitrary")),
        name="mixer_c",
    )(qc, kc, vc, bias)


def _merge_kernel(x_ref, oa_ref, ob_ref, oc_ref, g_ref, wgate_ref, sub_ref,
                  wa_ref, wb_ref, wc_ref, wout_ref, o_ref):
    x = x_ref[...]
    d = x.shape[1]
    h = _rms_rows(x, g_ref[...]).astype(BF16)
    oa = oa_ref[...]
    parts = []
    for c in range(oa.shape[1] // LANES):
        oc_ = oa[:, c * LANES:(c + 1) * LANES]
        parts.append(oc_ * lax.rsqrt(jnp.mean(oc_ * oc_, axis=-1, keepdims=True) + EPS))
    oan = (jnp.concatenate(parts, axis=1) * sub_ref[...]).astype(BF16)
    merged = jax.nn.sigmoid(_dot(h, wgate_ref[:, 0:d])) * _dot(oan, wa_ref[...])
    merged += jax.nn.sigmoid(_dot(h, wgate_ref[:, d:2 * d])) * _dot(ob_ref[...], wb_ref[...])
    merged += jax.nn.sigmoid(_dot(h, wgate_ref[:, 2 * d:3 * d])) * _dot(oc_ref[...], wc_ref[...])
    o_ref[...] = x + _dot(merged.astype(BF16), wout_ref[...])


def _merge(x, oa, ob, oc, g, wgate, sub, wa, wb, wc, wout, layer, tm):
    n, d = x.shape
    full = lambda a: pl.BlockSpec(a.shape, lambda i: (0,) * a.ndim, pipeline_mode=pl.Buffered(1))
    of_layer = lambda a: pl.BlockSpec((None,) + a.shape[1:], lambda i: (layer,) + (0,) * (a.ndim - 1),
                                      pipeline_mode=pl.Buffered(1))
    rows = lambda a: pl.BlockSpec((tm, a.shape[1]), lambda i: (i, 0))
    g = g.reshape(1, d)
    return pl.pallas_call(
        _merge_kernel,
        grid=(n // tm,),
        in_specs=[rows(x), rows(oa), rows(ob), rows(oc), full(g), of_layer(wgate), full(sub),
                  of_layer(wa), of_layer(wb), of_layer(wc), of_layer(wout)],
        out_specs=rows(x),
        out_shape=jax.ShapeDtypeStruct((n, d), F32),
        compiler_params=_cparams(("parallel",)),
        name="merge",
    )(x, oa, ob, oc, g, wgate, sub, wa, wb, wc, wout)


def _cross_kernel(x_ref, g_ref, wq_ref, gain_ref, gmat_ref, mk_ref, mv_ref, wo_ref, o_ref):
    x = x_ref[0]
    h = _rms_rows(x, g_ref[...]).astype(BF16)
    q = (_group_rms(_dot(h, wq_ref[...]), gmat_ref[...]) * gain_ref[...]).astype(BF16)
    mk = mk_ref[0]
    mv = mv_ref[0]
    lane = lax.broadcasted_iota(jnp.int32, q.shape, 1)
    o = jnp.zeros(q.shape, F32)
    for hh in range(q.shape[1] // HEAD_DIM):
        in_head = (lane >= hh * HEAD_DIM) & (lane < (hh + 1) * HEAD_DIM)
        s = _dot_nt(_keep_lanes(q, hh * HEAD_DIM, (hh + 1) * HEAD_DIM), mk)
        p = jnp.exp(s - jnp.max(s, axis=1, keepdims=True))
        l = jnp.sum(p, axis=1, keepdims=True)
        o = jnp.where(in_head, _dot(p.astype(BF16), mv) / l, o)
    o_ref[0] = x + _dot(o.astype(BF16), wo_ref[...])


def _cross(x, g, wq, gain, mk, mv, wo, layer, tm):
    b, s, d = x.shape
    full = lambda a: pl.BlockSpec(a.shape, lambda bi, i: (0,) * a.ndim)
    of_layer = lambda a: pl.BlockSpec((None,) + a.shape[1:], lambda bi, i: (layer,) + (0,) * (a.ndim - 1))
    g = g.reshape(1, d)
    gmat = _group_mean_matrix()
    return pl.pallas_call(
        _cross_kernel,
        grid=(b, s // tm),
        in_specs=[
            pl.BlockSpec((1, tm, d), lambda bi, i: (bi, i, 0)),
            full(g), of_layer(wq), full(gain), full(gmat),
            pl.BlockSpec((1,) + mk.shape[1:], lambda bi, i: (bi, 0, 0)),
            pl.BlockSpec((1,) + mv.shape[1:], lambda bi, i: (bi, 0, 0)),
            of_layer(wo),
        ],
        out_specs=pl.BlockSpec((1, tm, d), lambda bi, i: (bi, i, 0)),
        out_shape=jax.ShapeDtypeStruct((b, s, d), F32),
        compiler_params=_cparams(("parallel", "parallel")),
        name="cross",
    )(x, g, wq, gain, gmat, mk, mv, wo)


def _sample_attn_kernel(lam_ref,
                        qa_ref, kan_ref, van_ref, kac_ref, vac_ref, bac_ref, ban_ref,
                        qb_ref, kbn_ref, vbn_ref, kbc_ref, vbc_ref, ubig_ref, usmall_ref,
                        qc_ref, kcn_ref, vcn_ref, kcc_ref, vcc_ref, bcc_ref, bcn_ref,
                        oa_ref, ob_ref, oc_ref):
    lam = lam_ref[0]
    ns = qa_ref.shape[1]
    bf = lambda r: r[0].astype(BF16)

    def heads(q, width):
        for hh in range(LANES // width):
            yield hh, None, _keep_lanes(q, hh * width, (hh + 1) * width)

    def softmax2(s_c, s_n, exp=jnp.exp):
        m = jnp.maximum(jnp.max(s_c, axis=1, keepdims=True), jnp.max(s_n, axis=1, keepdims=True))
        p_c = exp(s_c - m)
        p_n = exp(s_n - m)
        inv = 1.0 / (jnp.sum(p_c, axis=1, keepdims=True) + jnp.sum(p_n, axis=1, keepdims=True))
        return p_c * inv, p_n * inv

    for h in range(qa_ref.shape[2] // LANES):
        sl = slice(h * LANES, (h + 1) * LANES)
        q = qa_ref[0, :, sl]
        k_c = kac_ref[0, :, h, :].astype(BF16)
        k_n = kan_ref[0, :, sl].astype(BF16)
        maps = []
        for _, _, qm in heads(q, HEAD_DIM):
            maps.append(softmax2(_dot_nt(qm, k_c) + bac_ref[h], _dot_nt(qm, k_n) + ban_ref[h], exp=jnp.exp2))
        a_c = (maps[0][0] - lam * maps[1][0]).astype(BF16)
        a_n = (maps[0][1] - lam * maps[1][1]).astype(BF16)
        oa_ref[0, :, sl] = (_dot(a_c, vac_ref[0, :, h, :].astype(BF16))
                            + _dot(a_n, van_ref[0, :, sl].astype(BF16)))

    past = kbc_ref.shape[1]
    tk = ubig_ref.shape[0]
    row = lax.broadcasted_iota(jnp.int32, (ns, ns), 0)
    col = lax.broadcasted_iota(jnp.int32, (ns, ns), 1)
    for pr in range(qb_ref.shape[2] // LANES):
        sl = slice(pr * LANES, (pr + 1) * LANES)
        q = qb_ref[0, :, sl]
        k_n = kbn_ref[0, :, sl].astype(BF16)
        v_n = vbn_ref[0, :, sl].astype(BF16)
        outs = []
        for _, _, qm in heads(q, HEAD_DIM):
            w, carry = _stick_block(_dot_nt(qm, k_n), col < row, usmall_ref[...], 0.0)
            acc = _dot(w.astype(BF16), v_n)
            for j in range(past // tk - 1, -1, -1):
                k_c = kbc_ref[0, j * tk:(j + 1) * tk, sl].astype(BF16)
                v_c = vbc_ref[0, j * tk:(j + 1) * tk, sl].astype(BF16)
                w, rs = _stick_block(_dot_nt(qm, k_c), None, ubig_ref[...], carry)
                acc += _dot(w.astype(BF16), v_c)
                carry = carry + rs
            outs.append(acc)
        lane = lax.broadcasted_iota(jnp.int32, q.shape, 1)
        ob_ref[0, :, sl] = jnp.where(lane < HEAD_DIM, outs[0], outs[1]).astype(ob_ref.dtype)

    for pr in range(qc_ref.shape[2] // LANES):
        sl = slice(pr * LANES, (pr + 1) * LANES)
        q = qc_ref[0, :, sl]
        k_c = kcc_ref[0, :, sl].astype(BF16)
        k_n = kcn_ref[0, :, sl].astype(BF16)
        v_c = vcc_ref[0, :, sl].astype(BF16)
        v_n = vcn_ref[0, :, sl].astype(BF16)
        outs = []
        for hh, _, qm in heads(q, HEAD_DIM):
            hd = pr * (LANES // HEAD_DIM) + hh
            p_c, p_n = softmax2(_dot_nt(qm, k_c) + bcc_ref[hd], _dot_nt(qm, k_n) + bcn_ref[hd])
            outs.append(_dot(p_c.astype(BF16), v_c) + _dot(p_n.astype(BF16), v_n))
        lane = lax.broadcasted_iota(jnp.int32, q.shape, 1)
        oc_ref[0, :, sl] = jnp.where(lane < HEAD_DIM, outs[0], outs[1]).astype(oc_ref.dtype)


def _sample_attn(layer, lam, qa, ka, va, cak, cav, bac, ban, qb, kb, vb, cbk, cbv,
                 qc, kc, vc, cck, ccv, bcc, bcn):
    b, ns, _ = qa.shape
    per_b = lambda a: pl.BlockSpec((1,) + a.shape[1:], lambda bi: (bi,) + (0,) * (a.ndim - 1))
    full = lambda a: pl.BlockSpec(a.shape, lambda bi: (0,) * a.ndim)
    layer_b = lambda a: pl.BlockSpec((None, 1) + a.shape[2:], lambda bi: (layer, bi) + (0,) * (a.ndim - 2))
    ubig = _strict_lower(ATTN_TK)
    usmall = _strict_lower(ns)
    args = [qa, ka, va, cak, cav, bac, ban, qb, kb, vb, cbk, cbv, ubig, usmall,
            qc, kc, vc, cck, ccv, bcc, bcn]
    specs = [per_b(qa), per_b(ka), per_b(va), layer_b(cak), layer_b(cav), full(bac), full(ban),
             per_b(qb), per_b(kb), per_b(vb), per_b(cbk), per_b(cbv), full(ubig), full(usmall),
             per_b(qc), per_b(kc), per_b(vc), per_b(cck), per_b(ccv), full(bcc), full(bcn)]
    return pl.pallas_call(
        _sample_attn_kernel,
        grid=(b,),
        in_specs=[pl.BlockSpec(memory_space=pltpu.SMEM)] + specs,
        out_specs=[per_b(qa), per_b(qb), per_b(qc)],
        out_shape=[jax.ShapeDtypeStruct(qa.shape, F32), jax.ShapeDtypeStruct(qb.shape, BF16),
                   jax.ShapeDtypeStruct(qc.shape, BF16)],
        compiler_params=_cparams(("parallel",)),
        name="sample_mixers",
    )(lam, *args)


def _t5_bucket_np(rel):
    half = T5_BUCKETS // 2
    max_exact = half // 2
    n = np.abs(rel)
    nf = np.maximum(n, 1).astype(np.float64)
    large = max_exact + (np.log(nf / max_exact) / math.log(T5_MAX_DIST / max_exact)
                         * (half - max_exact)).astype(np.int64)
    large = np.minimum(large, half - 1)
    return np.where(rel > 0, half, 0) + np.where(n < max_exact, n, large)


def _toeplitz(lookup, n_rows, n_cols):
    period = n_rows + n_cols
    slot = np.arange(period)
    diff = np.minimum((slot + n_rows - 1) % period - (n_rows - 1), n_cols - 1)
    vec = lookup(diff).astype(F32)
    flat = jnp.tile(vec, (1, n_rows))[:, :n_rows * (period - 1)]
    return flat.reshape(vec.shape[0], n_rows, period - 1)[:, :, :n_cols]


def _t5_bias_table(t5_bias, qpos, kpos, key_major=False):
    t5_rows = lambda rel: t5_bias[_t5_bucket_np(rel)].T
    mask = (kpos[None, :] // CHUNK) <= (qpos[:, None] // CHUNK)
    if key_major:
        table = _toeplitz(lambda dd: t5_rows(kpos[0] - qpos[0] - dd), len(kpos), len(qpos))
        mask = mask.T
    else:
        table = _toeplitz(lambda dd: t5_rows(kpos[0] - qpos[0] + dd), len(qpos), len(kpos))
    return jnp.where(jnp.asarray(mask)[None], table, NEG)


def _band_bias_table(rel_table, qpos, kpos):
    lookup = lambda dd: rel_table[:, np.clip(kpos[0] - qpos[0] + dd, -REL_CLIP, REL_CLIP) + REL_CLIP]
    table = _toeplitz(lookup, len(qpos), len(kpos))
    qc = qpos[:, None] // CHUNK
    kc = kpos[None, :] // CHUNK
    mask = (kpos[None, :] >= 0) & (kc <= qc) & (kc >= qc - C_PREV)
    return jnp.where(jnp.asarray(mask)[None], table, NEG)


def _lambda_init(layer):
    return 0.8 - 0.6 * math.exp(-0.3 * layer)


def kernel(x_prompt, x_sample, mem_prompt, cache_a_k, cache_a_v, cache_b_k, cache_b_v, cache_c_k, cache_c_v, cache_mem_k, cache_mem_v, t5_bias, ffn1_norm, ffn1_wg, ffn1_wu, ffn1_wd, mix_norm, w_in, a_qnorm, a_knorm, a_lq1, a_lk1, a_lq2, a_lk2, a_subln, c_qnorm, c_knorm, c_rel_bias, w_br_a, w_br_b, w_br_c, w_out, x_norm, mem_norm, x_wq, x_wkv, x_qnorm, x_knorm, x_wo, ffn2_norm, ffn2_wg, ffn2_wu, ffn2_wd):
    bp, sp, d = x_prompt.shape
    bs, ns, _ = x_sample.shape
    depth = w_in.shape[0]
    past = cache_a_k.shape[2]
    w_buf = cache_c_k.shape[2]
    n_mem = mem_prompt.shape[1]
    wa = cache_a_k.shape[3] * cache_a_k.shape[4]
    wb = cache_b_k.shape[3] * cache_b_k.shape[4]
    wc = cache_c_k.shape[3] * cache_c_k.shape[4]
    wx = cache_mem_k.shape[3] * cache_mem_k.shape[4]
    h_a = cache_a_k.shape[3]
    n_qkv = 3 * (wa + wb + wc)
    w_keep = min(C_PREV * CHUNK, sp)
    scale = HEAD_DIM ** -0.5
    assert sp % ATTN_TQ == 0 and ATTN_TQ == ATTN_TK and ATTN_TQ == 4 * CHUNK and sp >= w_keep
    assert past % ATTN_TK == 0 and sp % (MIXER_BC_BLOCKS * ATTN_TQ) == 0
    assert sp % MIXER_A_TILE == 0 and MIXER_A_TILE % CHUNK == 0 and T5_MAX_DIST <= MIXER_A_TILE

    tile = lambda g, reps: jnp.tile(g.astype(F32), reps)
    bf = lambda a: a.astype(BF16)

    loc_a = np.arange(MIXER_A_TILE)
    loc = np.arange(ATTN_TQ)
    a_far = t5_bias[T5_BUCKETS // 2 - 1].astype(F32)
    a_bias = LOG2E * jnp.stack([
        _t5_bias_table(t5_bias, loc_a + MIXER_A_TILE, loc_a + MIXER_A_TILE, key_major=True),
        _t5_bias_table(t5_bias, loc_a + MIXER_A_TILE, loc_a, key_major=True),
        jnp.broadcast_to(a_far[:, None, None], (h_a, MIXER_A_TILE, MIXER_A_TILE))], axis=1)
    t5_log2 = LOG2E * t5_bias.astype(F32)
    a_brange = jnp.stack([LOG2E * a_far, jnp.max(t5_log2, axis=0), jnp.min(t5_log2, axis=0)], axis=1).reshape(-1)
    qpos_s = past + np.arange(ns)
    a_bias_sc = LOG2E * _t5_bias_table(t5_bias, qpos_s, np.arange(past))
    a_bias_sn = LOG2E * _t5_bias_table(t5_bias, qpos_s, qpos_s)
    kpos_sb = past - w_buf + np.arange(w_buf + ns)

    xp = x_prompt.reshape(bp * sp, d)
    xs = x_sample.reshape(bs * ns, d)
    mem = mem_prompt.reshape(bp * n_mem, d)
    outs = {k: [] for k in ("ak_p", "av_p", "bk_p", "bv_p", "ck_p", "cv_p", "mk_p", "mv_p",
                            "ak_s", "av_s", "bk_s", "bv_s", "ck_s", "cv_s")}
    flat = lambda dt: ((dt, 0),)
    dv_a = wa // h_a
    a_kv_prompt = ((F32, dv_a), (BF16, 0))
    b_kv_prompt = ((F32, HEAD_DIM), (BF16, 0))
    c_segs = [(wc, True, True, flat(BF16)), (wc, True, True, flat(F32)), (wc, False, False, flat(F32))]
    qkv_segs_p = [(wa, True, True, flat(BF16)), (wa, True, True, a_kv_prompt), (wa, False, False, a_kv_prompt),
                  (wb, False, True, flat(BF16)), (wb, False, False, b_kv_prompt), (wb, False, False, b_kv_prompt)] + c_segs
    qkv_segs_s = [(wa, True, True, flat(BF16)), (wa, True, True, flat(F32)), (wa, False, False, flat(F32)),
                  (wb, False, True, flat(BF16)), (wb, False, False, flat(F32)), (wb, False, False, flat(F32))] + c_segs

    wg1, wu1, wd1 = bf(ffn1_wg), bf(ffn1_wu), bf(ffn1_wd)
    wg2, wu2, wd2 = bf(ffn2_wg), bf(ffn2_wu), bf(ffn2_wd)
    w_qkv, w_gate = bf(w_in[:, :, :n_qkv]), bf(w_in[:, :, n_qkv:])
    wbr = bf(w_br_a), bf(w_br_b), bf(w_br_c)
    wout = bf(w_out)
    wq, wo, wkv = bf(x_wq), bf(x_wo), bf(x_wkv)

    for i in range(depth):
        qkv_gain = jnp.concatenate([
            tile(a_qnorm[i], wa // HEAD_DIM) * (scale * LOG2E), tile(a_knorm[i], wa // HEAD_DIM), jnp.ones((wa,), F32),
            jnp.full((wb,), scale, F32), jnp.ones((2 * wb,), F32),
            tile(c_qnorm[i], wc // HEAD_DIM) * scale, tile(c_knorm[i], wc // HEAD_DIM), jnp.ones((wc,), F32),
        ]).reshape(1, n_qkv)
        sub = (tile(a_subln[i], h_a) * (1.0 - _lambda_init(i))).reshape(1, wa)
        xq_gain = (tile(x_qnorm[i], wx // HEAD_DIM) * scale).reshape(1, wx)
        dot64 = lambda a, b: jnp.exp(jnp.sum(a.astype(F32) * b.astype(F32)))
        lam = (dot64(a_lq1[i], a_lk1[i]) - dot64(a_lq2[i], a_lk2[i]) + _lambda_init(i)).reshape(1)
        c_bias_p = _band_bias_table(c_rel_bias[i], C_PREV * CHUNK + loc, np.arange(BAND_W))
        c_bias_sc = _band_bias_table(c_rel_bias[i], qpos_s, kpos_sb[:w_buf])
        c_bias_sn = _band_bias_table(c_rel_bias[i], qpos_s, kpos_sb[w_buf:])

        xp = _ffn(xp, ffn1_norm[i], wg1, wu1, wd1, i, tm=1024, tf=1024)
        qa, ka, ka_bf, va, va_bf, qb, kb, kb_bf, vb, vb_bf, qc, kc, vc = _proj(
            xp, mix_norm[i], w_qkv, i, qkv_gain, qkv_segs_p, tm=1024)
        r3 = lambda a: a.reshape(bp, sp, a.shape[-1])
        oa = _attn_a(lam, a_brange, r3(qa), r3(ka_bf), r3(va_bf), a_bias)
        ob = _attn_b(r3(qb), r3(kb_bf), r3(vb_bf))
        oc = _attn_c(r3(qc), r3(kc), r3(vc), c_bias_p)
        xp = _merge(xp, oa.reshape(-1, wa), ob.reshape(-1, wb), oc.reshape(-1, wc), mix_norm[i], w_gate, sub,
                    *wbr, wout, i, tm=1024)
        mk, mv = _proj(mem, mem_norm[i], wkv, i,
                       jnp.concatenate([tile(x_knorm[i], wx // HEAD_DIM), jnp.ones((wx,), F32)]).reshape(1, 2 * wx),
                       [(wx, True, True, flat(F32)), (wx, False, False, flat(F32))], tm=n_mem)
        mk3, mv3 = mk.reshape(bp, n_mem, wx), mv.reshape(bp, n_mem, wx)
        xp = _cross(xp.reshape(bp, sp, d), x_norm[i], wq, xq_gain, bf(mk3), bf(mv3), wo, i, tm=1024).reshape(-1, d)
        xp = _ffn(xp, ffn2_norm[i], wg2, wu2, wd2, i, tm=1024, tf=1024)
        outs["ak_p"].append(ka.reshape(bp, sp, h_a, -1))
        outs["av_p"].append(va.reshape(bp, sp, h_a, -1))
        outs["bk_p"].append(kb.reshape(bp, sp, -1, HEAD_DIM))
        outs["bv_p"].append(vb.reshape(bp, sp, -1, HEAD_DIM))
        outs["ck_p"].append(r3(kc)[:, sp - w_keep:].reshape(bp, w_keep, -1, HEAD_DIM))
        outs["cv_p"].append(r3(vc)[:, sp - w_keep:].reshape(bp, w_keep, -1, HEAD_DIM))
        outs["mk_p"].append(mk3.reshape(bp, n_mem, -1, HEAD_DIM))
        outs["mv_p"].append(mv3.reshape(bp, n_mem, -1, HEAD_DIM))

        xs = _ffn(xs, ffn1_norm[i], wg1, wu1, wd1, i, tm=bs * ns, tf=1024)
        qa, ka, va, qb, kb, vb, qc, kc, vc = _proj(xs, mix_norm[i], w_qkv, i, qkv_gain, qkv_segs_s, tm=bs * ns)
        s3 = lambda a: a.reshape(bs, ns, a.shape[-1])
        c3 = lambda a: a.reshape(bs, a.shape[1], -1)
        oa, ob, oc = _sample_attn(
            i, lam, s3(qa), s3(ka), s3(va), cache_a_k, cache_a_v, a_bias_sc, a_bias_sn,
            s3(qb), s3(kb), s3(vb), c3(cache_b_k[i]), c3(cache_b_v[i]),
            s3(qc), s3(kc), s3(vc), c3(cache_c_k[i]), c3(cache_c_v[i]), c_bias_sc, c_bias_sn)
        xs = _merge(xs, oa.reshape(-1, wa), ob.reshape(-1, wb), oc.reshape(-1, wc), mix_norm[i], w_gate, sub,
                    *wbr, wout, i, tm=bs * ns)
        xs = _cross(xs.reshape(bs, ns, d), x_norm[i], wq, xq_gain, bf(c3(cache_mem_k[i])), bf(c3(cache_mem_v[i])),
                    wo, i, tm=ns).reshape(-1, d)
        xs = _ffn(xs, ffn2_norm[i], wg2, wu2, wd2, i, tm=bs * ns, tf=1024)
        outs["ak_s"].append(ka.reshape(bs, ns, h_a, -1))
        outs["av_s"].append(va.reshape(bs, ns, h_a, -1))
        outs["bk_s"].append(kb.reshape(bs, ns, -1, HEAD_DIM))
        outs["bv_s"].append(vb.reshape(bs, ns, -1, HEAD_DIM))
        outs["ck_s"].append(jnp.concatenate([cache_c_k[i], kc.reshape(bs, ns, -1, HEAD_DIM)], axis=1)[:, ns:])
        outs["cv_s"].append(jnp.concatenate([cache_c_v[i], vc.reshape(bs, ns, -1, HEAD_DIM)], axis=1)[:, ns:])

    st = lambda k: jnp.stack(outs[k])
    return (xp.reshape(bp, sp, d), xs.reshape(bs, ns, d),
            st("ak_p"), st("av_p"), st("bk_p"), st("bv_p"), st("ck_p"), st("cv_p"), st("mk_p"), st("mv_p"),
            st("ak_s"), st("av_s"), st("bk_s"), st("bv_s"), st("ck_s"), st("cv_s"))
```
